```python
import math
import jax
import jax.numpy as jnp
from jax import lax
import numpy as np

D_MODEL = 2048
BATCH = 4
SEQ = 4096
DEPTH = 2

GRID_W = 64
CTX_LEN = 256
MIX_WIDTH = D_MODEL
S5_WIDTH = MIX_WIDTH // 2
S5_GROUP = 16
S5_GROUPS = S5_WIDTH // S5_GROUP
S5_STATE = 64
HGRN_WIDTH = MIX_WIDTH - S5_WIDTH
HGRN_HEAD_DIM = 128
HGRN_HEADS = HGRN_WIDTH // HGRN_HEAD_DIM
HGRN_CHUNK = 64
IN_COLS = S5_WIDTH + 5 * HGRN_WIDTH
D_FF = ((8 * D_MODEL // 3 + 127) // 128) * 128
N_MOD = 9
EPS = 1e-6
F_MIN = 1e-6
DT_MIN = 0.001
DT_MAX = 0.1
LAMBDA_RE_MAX = -1e-4

kernel_name = 'hymba_s5_hgrn2_macaron_dit'


def rmsnorm(x, w):
    xf = x.astype(jnp.float32)
    y = xf * lax.rsqrt(jnp.mean(xf * xf, axis=-1, keepdims=True) + EPS)
    return y * w.astype(jnp.float32)


def modulate(h, mods, i_shift, i_scale):
    return h * (1.0 + mods[..., i_scale, :]) + mods[..., i_shift, :]


def swiglu(h, w_gate, w_up, w_down):
    return (jax.nn.silu(h @ w_gate) * (h @ w_up)) @ w_down


def ffn_half_step(stream, mods, norm_w_i, w_gate, w_up, w_down, base):
    h = modulate(rmsnorm(stream, norm_w_i), mods, base, base + 1)
    return stream + 0.5 * mods[..., base + 2, :] * swiglu(h, w_gate, w_up, w_down)


def raster_to_colmajor(t, rows):
    b, l, ch = t.shape
    return t.reshape(b, rows, GRID_W, ch).transpose(0, 2, 1, 3).reshape(b, l, ch)


def colmajor_to_raster(t, rows):
    b, l, ch = t.shape
    return t.reshape(b, GRID_W, rows, ch).transpose(0, 2, 1, 3).reshape(b, l, ch)


def s5_discretise(lam_re, lam_im, log_step, b_re, b_im):
    lam_re = jnp.minimum(lam_re.astype(jnp.float32), LAMBDA_RE_MAX)
    lam_im = lam_im.astype(jnp.float32)
    dt = jnp.exp(log_step.astype(jnp.float32))[:, None]
    mag = jnp.exp(lam_re * dt)
    lb_re = mag * jnp.cos(lam_im * dt)
    lb_im = mag * jnp.sin(lam_im * dt)
    den = lam_re * lam_re + lam_im * lam_im
    nr = lb_re - 1.0
    ni = lb_im
    cf_re = (nr * lam_re + ni * lam_im) / den
    cf_im = (ni * lam_re - nr * lam_im) / den
    b_re = b_re.astype(jnp.float32)
    b_im = b_im.astype(jnp.float32)
    br = cf_re[..., None] * b_re - cf_im[..., None] * b_im
    bi = cf_re[..., None] * b_im + cf_im[..., None] * b_re
    return lb_re, lb_im, br, bi


def s5_combine(e_i, e_j):
    ar_i, ai_i, br_i, bi_i = e_i
    ar_j, ai_j, br_j, bi_j = e_j
    return (ar_j * ar_i - ai_j * ai_i,
            ar_j * ai_i + ai_j * ar_i,
            ar_j * br_i - ai_j * bi_i + br_j,
            ar_j * bi_i + ai_j * br_i + bi_j)


def s5_scan(u, lb_re, lb_im, br, bi, x0_re, x0_im):
    bu_re = jnp.einsum('blgh,gph->blgp', u, br)
    bu_im = jnp.einsum('blgh,gph->blgp', u, bi)
    bu_re = bu_re.at[:, 0].add(lb_re * x0_re - lb_im * x0_im)
    bu_im = bu_im.at[:, 0].add(lb_re * x0_im + lb_im * x0_re)
    seq_len = u.shape[1]
    a_re = jnp.broadcast_to(lb_re, (1, seq_len) + lb_re.shape)
    a_im = jnp.broadcast_to(lb_im, (1, seq_len) + lb_im.shape)
    _, _, xr, xi = lax.associative_scan(s5_combine, (a_re, a_im, bu_re, bu_im), axis=1)
    return xr, xi


def s5_readout(xr, xi, c_re, c_im):
    return jnp.einsum('blgp,ghp->blgh', xr, c_re) - jnp.einsum('blgp,ghp->blgh', xi, c_im)


def s5_glu(y, w_glu, b_glu):
    g = jax.nn.gelu(y)
    return g * jax.nn.sigmoid(g @ w_glu + b_glu)


def s5_mixer(u_ctx, u_lat, lam_re, lam_im, log_step, b_re, b_im, c_re, c_im,
             d_skip, w_glu, b_glu, need_ctx_out):
    def groups(u):
        b, l, _ = u.shape
        return u.astype(jnp.float32).reshape(b, l, S5_GROUPS, S5_GROUP)
    uc = groups(u_ctx)
    ul = groups(u_lat)
    dg = d_skip.astype(jnp.float32).reshape(S5_GROUPS, S5_GROUP)
    y_lat = dg * ul
    y_ctx = dg * uc
    zero = jnp.zeros((uc.shape[0], S5_GROUPS, S5_STATE), jnp.float32)
    for d in range(2):
        reverse = d == 1
        lb_re, lb_im, br, bi = s5_discretise(lam_re[d], lam_im[d], log_step[d], b_re[d], b_im[d])
        cr = c_re[d].astype(jnp.float32)
        ci = c_im[d].astype(jnp.float32)
        ucd = jnp.flip(uc, axis=1) if reverse else uc
        uld = jnp.flip(ul, axis=1) if reverse else ul
        xr_c, xi_c = s5_scan(ucd, lb_re, lb_im, br, bi, zero, zero)
        xr_l, xi_l = s5_scan(uld, lb_re, lb_im, br, bi, xr_c[:, -1], xi_c[:, -1])
        yl = s5_readout(xr_l, xi_l, cr, ci)
        y_lat = y_lat + (jnp.flip(yl, axis=1) if reverse else yl)
        if need_ctx_out:
            yc = s5_readout(xr_c, xi_c, cr, ci)
            y_ctx = y_ctx + (jnp.flip(yc, axis=1) if reverse else yc)
    b, l = ul.shape[:2]
    out_lat = s5_glu(y_lat.reshape(b, l, S5_WIDTH), w_glu, b_glu)
    out_ctx = s5_glu(y_ctx.reshape(b, uc.shape[1], S5_WIDTH), w_glu, b_glu) if need_ctx_out else None
    return out_ctx, out_lat


def hgrn_forget(z, lb):
    f = lb + (1.0 - lb) * jax.nn.sigmoid(z)
    log_f = jnp.log(jnp.maximum(f, F_MIN))
    return 1.0 - f, log_f


def hgrn2_chunk_scan(q, k, v, log_f, s0):
    b, h, l, _ = q.shape
    n_chunks = l // HGRN_CHUNK

    def chunks(t):
        return jnp.moveaxis(t.reshape(b, h, n_chunks, HGRN_CHUNK, t.shape[-1]), 2, 0)

    tri = jnp.tril(jnp.ones((HGRN_CHUNK, HGRN_CHUNK), dtype=bool))[:, :, None]

    def step(s, inp):
        qc, kc, vc, gc = inp
        gcum = jnp.cumsum(gc, axis=2)
        diff = gcum[:, :, :, None, :] - gcum[:, :, None, :, :]
        decay = jnp.where(tri, jnp.exp(jnp.minimum(diff, 0.0)), 0.0)
        scores = jnp.einsum('bhtd,bhsd,bhtsd->bhts', qc, kc, decay)
        o = (jnp.einsum('bhts,bhsv->bhtv', scores, vc)
             + jnp.einsum('bhtd,bhdv->bhtv', qc * jnp.exp(gcum), s))
        g_last = gcum[:, :, -1]
        s_new = (jnp.exp(g_last)[..., None] * s
                 + jnp.einsum('bhsd,bhsv->bhdv', kc * jnp.exp(g_last[:, :, None] - gcum), vc))
        return s_new, o

    s_fin, o = lax.scan(step, s0, (chunks(q), chunks(k), chunks(v), chunks(log_f)))
    o = jnp.moveaxis(o, 0, 2).reshape(b, h, l, v.shape[-1])
    return o, s_fin


def hgrn2_direction(q, z, v, lb, s0, reverse):
    k, log_f = hgrn_forget(z, lb)
    if reverse:
        q, k, v, log_f = [jnp.flip(t, axis=2) for t in (q, k, v, log_f)]
    o, s_fin = hgrn2_chunk_scan(q, k, v, log_f, s0)
    if reverse:
        o = jnp.flip(o, axis=2)
    return o, s_fin


def to_heads(t):
    b, l, _ = t.shape
    return t.reshape(b, l, HGRN_HEADS, HGRN_HEAD_DIM).transpose(0, 2, 1, 3)


def hgrn2_mixer(p_ctx, p_lat, lb_l, norm_w_l, rows, need_ctx_out):
    def prep(p, reorder):
        p = p.astype(jnp.float32)
        q, zf, zb, v, gate = [p[..., i * HGRN_WIDTH:(i + 1) * HGRN_WIDTH] for i in range(5)]
        if reorder:
            q, zf, zb, v = [raster_to_colmajor(t, rows) for t in (q, zf, zb, v)]
        return to_heads(jax.nn.silu(q)), (to_heads(zf), to_heads(zb)), to_heads(v), gate

    qc, zc, vc, gate_c = prep(p_ctx, False)
    ql, zl, vl, gate_l = prep(p_lat, True)
    s_zero = jnp.zeros((qc.shape[0], HGRN_HEADS, HGRN_HEAD_DIM, HGRN_HEAD_DIM), jnp.float32)
    o_lat_dirs = []
    o_ctx_dirs = []
    for d in range(2):
        reverse = d == 1
        lb = lb_l[d].reshape(HGRN_HEADS, 1, HGRN_HEAD_DIM)
        oc, s_ctx = hgrn2_direction(qc, zc[d], vc, lb, s_zero, reverse)
        ol, _ = hgrn2_direction(ql, zl[d], vl, lb, s_ctx, reverse)
        o_lat_dirs.append(ol)
        o_ctx_dirs.append(oc)

    def readout(o, gate, reorder):
        b, _, l, _ = o.shape
        o = rmsnorm(o.transpose(0, 2, 1, 3), norm_w_l).reshape(b, l, HGRN_WIDTH)
        if reorder:
            o = colmajor_to_raster(o, rows)
        return o * jax.nn.silu(gate)

    out_lat = readout(o_lat_dirs[0] + o_lat_dirs[1], gate_l, True)
    out_ctx = readout(o_ctx_dirs[0] + o_ctx_dirs[1], gate_c, False) if need_ctx_out else None
    return out_ctx, out_lat


def setup_inputs(seed: int = 0) -> dict:
    key = jax.random.key(seed)
    ks = jax.random.split(key, 25)
    f32 = jnp.float32

    def nrm(k, shape, scale):
        return scale * jax.random.normal(k, shape, f32)

    G, P, H = S5_GROUPS, S5_STATE, S5_GROUP
    lam_im_init = jnp.pi * jnp.arange(P, dtype=f32)
    return {
        'x': nrm(ks[0], (BATCH, SEQ, D_MODEL), 1.0),
        'c': nrm(ks[1], (BATCH, D_MODEL), 1.0),
        'ctx': nrm(ks[2], (BATCH, CTX_LEN, D_MODEL), 1.0),
        'c_ctx': nrm(ks[3], (D_MODEL,), 1.0),
        'w_ada': nrm(ks[4], (DEPTH, D_MODEL, N_MOD * D_MODEL), 0.5 * D_MODEL ** -0.5),
        'b_ada': nrm(ks[5], (DEPTH, N_MOD * D_MODEL), 0.01),
        'norm_w': 1.0 + nrm(ks[6], (DEPTH, 3, D_MODEL), 0.05),
        'ffn_w_gate': nrm(ks[7], (DEPTH, 2, D_MODEL, D_FF), D_MODEL ** -0.5),
        'ffn_w_up': nrm(ks[8], (DEPTH, 2, D_MODEL, D_FF), D_MODEL ** -0.5),
        'ffn_w_down': nrm(ks[9], (DEPTH, 2, D_FF, D_MODEL), D_FF ** -0.5),
        'w_in': nrm(ks[10], (DEPTH, D_MODEL, IN_COLS), D_MODEL ** -0.5),
        'w_out': nrm(ks[11], (DEPTH, MIX_WIDTH, D_MODEL), MIX_WIDTH ** -0.5),
        's5_lambda_re': -0.5 + nrm(ks[12], (DEPTH, 2, G, P), 0.01),
        's5_lambda_im': lam_im_init + nrm(ks[13], (DEPTH, 2, G, P), 0.01),
        's5_log_step': jax.random.uniform(ks[14], (DEPTH, 2, G), f32, math.log(DT_MIN), math.log(DT_MAX)),
        's5_b_re': nrm(ks[15], (DEPTH, 2, G, P, H), (2 * H) ** -0.5),
        's5_b_im': nrm(ks[16], (DEPTH, 2, G, P, H), (2 * H) ** -0.5),
        's5_c_re': nrm(ks[17], (DEPTH, 2, G, H, P), P ** -0.5),
        's5_c_im': nrm(ks[18], (DEPTH, 2, G, H, P), P ** -0.5),
        's5_d': nrm(ks[19], (DEPTH, S5_WIDTH), 1.0),
        's5_w_glu': nrm(ks[20], (DEPTH, S5_WIDTH, S5_WIDTH), S5_WIDTH ** -0.5),
        's5_b_glu': nrm(ks[21], (DEPTH, S5_WIDTH), 0.01),
        'hgrn_lower_bounds': nrm(ks[22], (DEPTH, 2, HGRN_WIDTH), 0.1),
        'hgrn_norm_w': 1.0 + nrm(ks[23], (DEPTH, HGRN_HEAD_DIM), 0.05),
        'final_norm_w': 1.0 + nrm(ks[24], (D_MODEL,), 0.05),
    }


def reference(x, c, ctx, c_ctx, w_ada, b_ada, norm_w, ffn_w_gate, ffn_w_up, ffn_w_down,
              w_in, w_out, s5_lambda_re, s5_lambda_im, s5_log_step, s5_b_re, s5_b_im,
              s5_c_re, s5_c_im, s5_d, s5_w_glu, s5_b_glu, hgrn_lower_bounds, hgrn_norm_w,
              final_norm_w):
    batch, seq_len, _ = x.shape
    rows = seq_len // GRID_W
    lb_soft = jax.nn.softmax(hgrn_lower_bounds.astype(jnp.float32), axis=0)
    lb_all = jnp.cumsum(lb_soft, axis=0) - lb_soft[0]
    for l in range(DEPTH):
        last = l == DEPTH - 1
        mod_lat = (jax.nn.silu(c) @ w_ada[l] + b_ada[l]).reshape(batch, 1, N_MOD, D_MODEL)
        mod_ctx = (jax.nn.silu(c_ctx) @ w_ada[l] + b_ada[l]).reshape(N_MOD, D_MODEL)

        x = ffn_half_step(x, mod_lat, norm_w[l, 0], ffn_w_gate[l, 0], ffn_w_up[l, 0], ffn_w_down[l, 0], 0)
        ctx = ffn_half_step(ctx, mod_ctx, norm_w[l, 0], ffn_w_gate[l, 0], ffn_w_up[l, 0], ffn_w_down[l, 0], 0)

        h_lat = modulate(rmsnorm(x, norm_w[l, 1]), mod_lat, 3, 4)
        h_ctx = modulate(rmsnorm(ctx, norm_w[l, 1]), mod_ctx, 3, 4)
        p_lat = h_lat @ w_in[l]
        p_ctx = h_ctx @ w_in[l]
        s5_ctx, s5_lat = s5_mixer(p_ctx[..., :S5_WIDTH], p_lat[..., :S5_WIDTH],
                                  s5_lambda_re[l], s5_lambda_im[l], s5_log_step[l],
                                  s5_b_re[l], s5_b_im[l], s5_c_re[l], s5_c_im[l],
                                  s5_d[l], s5_w_glu[l], s5_b_glu[l], not last)
        hg_ctx, hg_lat = hgrn2_mixer(p_ctx[..., S5_WIDTH:], p_lat[..., S5_WIDTH:],
                                     lb_all[l], hgrn_norm_w[l], rows, not last)
        y_lat = jnp.concatenate([s5_lat, hg_lat], axis=-1) @ w_out[l]
        x = x + mod_lat[..., 5, :] * y_lat
        if not last:
            y_ctx = jnp.concatenate([s5_ctx, hg_ctx], axis=-1) @ w_out[l]
            ctx = ctx + mod_ctx[..., 5, :] * y_ctx

        x = ffn_half_step(x, mod_lat, norm_w[l, 2], ffn_w_gate[l, 1], ffn_w_up[l, 1], ffn_w_down[l, 1], 6)
        if not last:
            ctx = ffn_half_step(ctx, mod_ctx, norm_w[l, 2], ffn_w_gate[l, 1], ffn_w_up[l, 1], ffn_w_down[l, 1], 6)
    return rmsnorm(x, final_norm_w)
```

```python
import functools

import numpy as np
import jax
import jax.numpy as jnp
from jax import lax
from jax.experimental import pallas as pl
from jax.experimental.pallas import tpu as pltpu

F32 = jnp.float32
BF16 = jnp.bfloat16

EPS = 1e-6
F_MIN = 1e-6
LAMBDA_RE_MAX = -1e-4
GRID_W = 64
N_MOD = 9
S5_GROUP = 16
S5_STATE = 64
S5_BLOCK = 8
S5_OCT = 8
HEAD_DIM = 128
CHUNK = 64
SUB = 8
LANE = 128
MOD_ROWS = 8
VMEM_LIMIT = 56 * 1024 * 1024


def _params(sem):
    return pltpu.CompilerParams(dimension_semantics=sem, vmem_limit_bytes=VMEM_LIMIT)


def _tile(n, pref, mult=8):
    t = min(n, pref)
    while t > 0:
        if n % t == 0 and t % mult == 0:
            return t
        t -= 1
    return n


def _norm_mod(x, nw, shift, scale):
    ms = jnp.mean(x * x, axis=-1, keepdims=True)
    y = x * lax.rsqrt(ms + EPS) * nw
    return y * (1.0 + scale) + shift


def _silu(x):
    return x * jax.nn.sigmoid(x)


def _gelu_tanh(x):
    return 0.5 * x * (1.0 + jnp.tanh(0.7978845608028654 * (x + 0.044715 * (x * x * x))))


def _split3(x):
    hi = x.astype(BF16)
    r1 = x - hi.astype(F32)
    mid = r1.astype(BF16)
    lo = (r1 - mid.astype(F32)).astype(BF16)
    return hi, mid, lo


def _ada_kernel(c_ref, w_ref, b_ref, o_ref):
    a = _silu(c_ref[...])
    a_hi = a.astype(BF16)
    a_lo = (a - a_hi.astype(F32)).astype(BF16)
    w = w_ref[...]
    w_hi = w.astype(BF16)
    w_lo = (w - w_hi.astype(F32)).astype(BF16)
    acc = jnp.dot(a_hi, w_hi, preferred_element_type=F32)
    acc += jnp.dot(a_lo, w_hi, preferred_element_type=F32)
    acc += jnp.dot(a_hi, w_lo, preferred_element_type=F32)
    o_ref[...] = acc + b_ref[...]


def _ada(cvec, w_ada, b_ada):
    depth, d, n = w_ada.shape
    tn = _tile(n, 1024, LANE)
    return pl.pallas_call(
        _ada_kernel,
        grid=(depth, n // tn),
        in_specs=[
            pl.BlockSpec((MOD_ROWS, d), lambda l, j: (0, 0)),
            pl.BlockSpec((None, d, tn), lambda l, j: (l, 0, j)),
            pl.BlockSpec((None, 1, tn), lambda l, j: (l, 0, j)),
        ],
        out_specs=pl.BlockSpec((None, MOD_ROWS, tn), lambda l, j: (l, 0, j)),
        out_shape=jax.ShapeDtypeStruct((depth, MOD_ROWS, n), F32),
        compiler_params=_params(("arbitrary", "arbitrary")),
        name="ada",
    )(cvec, w_ada, b_ada.reshape(depth, 1, n))


def _ffn_kernel(*refs, base, n_f, final):
    if final:
        x_ref, m_ref, nw_ref, wg_ref, wu_ref, wd_ref, fnw_ref, o_ref, h_ref, acc_ref = refs
    else:
        x_ref, m_ref, nw_ref, wg_ref, wu_ref, wd_ref, o_ref, h_ref, acc_ref = refs
    j = pl.program_id(1)

    @pl.when(j == 0)
    def _():
        h = _norm_mod(x_ref[...], nw_ref[...], m_ref[0, base:base + 1, :], m_ref[0, base + 1:base + 2, :])
        h_ref[...] = h.astype(BF16)
        acc_ref[...] = jnp.zeros_like(acc_ref)

    h = h_ref[...]
    g = jnp.dot(h, wg_ref[...], preferred_element_type=F32)
    u = jnp.dot(h, wu_ref[...], preferred_element_type=F32)
    a = (_silu(g) * u).astype(BF16)
    acc_ref[...] += jnp.dot(a, wd_ref[...], preferred_element_type=F32)

    @pl.when(j == n_f - 1)
    def _():
        y = x_ref[...] + (0.5 * m_ref[0, base + 2:base + 3, :]) * acc_ref[...]
        if final:
            ms = jnp.mean(y * y, axis=-1, keepdims=True)
            y = y * lax.rsqrt(ms + EPS) * fnw_ref[...]
        o_ref[...] = y


def _ffn(x2d, mods, nw, wg, wu, wd, *, base, rows_per_mod, mod_off, final_nw=None, tm=512, tf=512):
    n, d = x2d.shape
    fp = wg.shape[1]
    tm = _tile(n, tm)
    tf = _tile(fp, tf, LANE)
    n_f = fp // tf
    final = final_nw is not None
    in_specs = [
        pl.BlockSpec((tm, d), lambda i, j: (i, 0)),
        pl.BlockSpec((1, N_MOD, d), lambda i, j: ((i * tm) // rows_per_mod + mod_off, 0, 0)),
        pl.BlockSpec((1, d), lambda i, j: (0, 0)),
        pl.BlockSpec((d, tf), lambda i, j: (0, j)),
        pl.BlockSpec((d, tf), lambda i, j: (0, j)),
        pl.BlockSpec((tf, d), lambda i, j: (j, 0)),
    ]
    args = [x2d, mods, nw.reshape(1, d), wg, wu, wd]
    if final:
        in_specs.append(pl.BlockSpec((1, d), lambda i, j: (0, 0)))
        args.append(final_nw.reshape(1, d))
    return pl.pallas_call(
        functools.partial(_ffn_kernel, base=base, n_f=n_f, final=final),
        grid=(n // tm, n_f),
        in_specs=in_specs,
        out_specs=pl.BlockSpec((tm, d), lambda i, j: (i, 0)),
        out_shape=jax.ShapeDtypeStruct((n, d), F32),
        scratch_shapes=[pltpu.VMEM((tm, d), BF16), pltpu.VMEM((tm, d), F32)],
        compiler_params=_params(("arbitrary", "arbitrary")),
        name="ffn_final" if final else "ffn",
    )(*args)


def _uproj_kernel(x_ref, m_ref, nw_ref, w_ref, uf_ref, ub_ref):
    h = _norm_mod(x_ref[...], nw_ref[...], m_ref[0, 3:4, :], m_ref[0, 4:5, :])
    u = jnp.dot(h.astype(BF16), w_ref[...], preferred_element_type=F32)
    uf_ref[...] = u
    ub_ref[...] = u.astype(BF16)


def _uproj(x2d, mods, nw, w_u, *, rows_per_mod, mod_off, tm=512):
    n, d = x2d.shape
    s5w = w_u.shape[1]
    tm = _tile(n, tm, 16)
    return pl.pallas_call(
        _uproj_kernel,
        grid=(n // tm,),
        in_specs=[
            pl.BlockSpec((tm, d), lambda i: (i, 0)),
            pl.BlockSpec((1, N_MOD, d), lambda i: ((i * tm) // rows_per_mod + mod_off, 0, 0)),
            pl.BlockSpec((1, d), lambda i: (0, 0)),
            pl.BlockSpec((d, s5w), lambda i: (0, 0)),
        ],
        out_specs=[pl.BlockSpec((tm, s5w), lambda i: (i, 0)), pl.BlockSpec((tm, s5w), lambda i: (i, 0))],
        out_shape=[jax.ShapeDtypeStruct((n, s5w), F32), jax.ShapeDtypeStruct((n, s5w), BF16)],
        compiler_params=_params(("arbitrary",)),
        name="uproj",
    )(x2d, mods, nw.reshape(1, d), w_u)


def _hproj_kernel(x_ref, m_ref, nw_ref, w_ref, o_ref, h_ref, *, rows, cb, d):
    n = pl.program_id(2)

    @pl.when(n == 0)
    def _():
        for cl in range(cb):
            xc = x_ref[:, cl * d:(cl + 1) * d]
            h = _norm_mod(xc, nw_ref[...], m_ref[0, 3:4, :], m_ref[0, 4:5, :])
            h_ref[cl * rows:(cl + 1) * rows, :] = h.astype(BF16)

    p = jnp.dot(h_ref[...], w_ref[...], preferred_element_type=F32)

    @pl.when(n == 0)
    def _():
        o_ref[...] = _silu(p)

    @pl.when(n != 0)
    def _():
        o_ref[...] = p


def _hproj(x3d, mods, nw, w_h, *, rows, cols, d, mod_of_batch, cb, tn=1024):
    b = x3d.shape[0]
    nh = w_h.shape[1]
    cb = _tile(cols, cb, 1)
    tn = _tile(nh, tn, LANE)
    mod_fn = (lambda bi: bi) if mod_of_batch else (lambda bi: b)
    return pl.pallas_call(
        functools.partial(_hproj_kernel, rows=rows, cb=cb, d=d),
        grid=(b, cols // cb, nh // tn),
        in_specs=[
            pl.BlockSpec((None, rows, cb * d), lambda bi, ci, n: (bi, 0, ci)),
            pl.BlockSpec((1, N_MOD, d), lambda bi, ci, n: (mod_fn(bi), 0, 0)),
            pl.BlockSpec((1, d), lambda bi, ci, n: (0, 0)),
            pl.BlockSpec((d, tn), lambda bi, ci, n: (0, n)),
        ],
        out_specs=pl.BlockSpec((None, cb * rows, tn), lambda bi, ci, n: (bi, ci, n)),
        out_shape=jax.ShapeDtypeStruct((b, cols * rows, nh), F32),
        scratch_shapes=[pltpu.VMEM((cb * rows, d), BF16)],
        compiler_params=_params(("arbitrary", "arbitrary", "arbitrary")),
        name="hproj",
    )(x3d, mods, nw.reshape(1, d), w_h)


def _s5_prep(lam_re, lam_im, log_step, b_re, b_im, c_re, c_im):
    hp = lax.Precision.HIGHEST
    t = S5_BLOCK
    g, p = lam_re.shape[1:]
    h = b_re.shape[-1]
    n_oct = g // S5_OCT
    lam_re = jnp.minimum(lam_re.astype(F32), LAMBDA_RE_MAX)
    lam_im = lam_im.astype(F32)
    dt = jnp.exp(log_step.astype(F32))[..., None]
    mag = jnp.exp(lam_re * dt)
    lb_re = mag * jnp.cos(lam_im * dt)
    lb_im = mag * jnp.sin(lam_im * dt)
    den = lam_re * lam_re + lam_im * lam_im
    nr = lb_re - 1.0
    ni = lb_im
    cf_re = (nr * lam_re + ni * lam_im) / den
    cf_im = (ni * lam_re - nr * lam_im) / den
    b_re = b_re.astype(F32)
    b_im = b_im.astype(F32)
    br = cf_re[..., None] * b_re - cf_im[..., None] * b_im
    bi = cf_re[..., None] * b_im + cf_im[..., None] * b_re
    cr = c_re.astype(F32)
    ci = c_im.astype(F32)
    j = jnp.arange(t + 1, dtype=F32)[None, :, None, None]
    pmag = jnp.exp(j * (lam_re * dt)[:, None])
    pw_re = pmag * jnp.cos(j * (lam_im * dt)[:, None])
    pw_im = pmag * jnp.sin(j * (lam_im * dt)[:, None])
    pb_re = pw_re[..., None] * br[:, None] - pw_im[..., None] * bi[:, None]
    pb_im = pw_re[..., None] * bi[:, None] + pw_im[..., None] * br[:, None]
    kj = (jnp.einsum('dgep,djgph->djgeh', cr, pb_re[:, :t], precision=hp)
          - jnp.einsum('dgep,djgph->djgeh', ci, pb_im[:, :t], precision=hp))
    r = np.arange(t)
    lag = r[None, :] - r[:, None]
    sel_f = np.stack([(lag == jj) for jj in range(t)]).astype(np.float32)
    sel_b = np.stack([(-lag == jj) for jj in range(t)]).astype(np.float32)
    kin = (jnp.einsum('jab,jgeh->gahbe', sel_f, kj[0], precision=hp)
           + jnp.einsum('jab,jgeh->gahbe', sel_b, kj[1], precision=hp))
    eye = jnp.eye(S5_OCT, dtype=F32)
    kin = kin.reshape(n_oct, S5_OCT, t, h, t, h)
    kmat = jnp.einsum('ogahbe,gk->oaghbke', kin, eye).reshape(n_oct, t * S5_OCT * h, t * S5_OCT * h)
    idx_f = (t - 1 - r)
    idx_b = r

    def we_dir(d_, idx):
        re = jnp.transpose(pb_re[d_][idx], (1, 0, 3, 2))
        im = jnp.transpose(pb_im[d_][idx], (1, 0, 3, 2))
        re = re.reshape(n_oct, S5_OCT, t, h, p)
        im = im.reshape(n_oct, S5_OCT, t, h, p)
        re = jnp.einsum('ogrhp,gk->orghkp', re, eye)
        im = jnp.einsum('ogrhp,gk->orghkp', im, eye)
        rows_ = t * S5_OCT * h
        return re.reshape(n_oct, rows_, S5_OCT * p), im.reshape(n_oct, rows_, S5_OCT * p)

    wef_re, wef_im = we_dir(0, idx_f)
    web_re, web_im = we_dir(1, idx_b)
    we = jnp.concatenate([wef_re, wef_im, web_re, web_im], axis=-1)
    def wy_dir(d_, idx):
        pr = pw_re[d_][idx]
        pi = pw_im[d_][idx]
        a_re = cr[d_][None] * pr[:, :, None, :] - ci[d_][None] * pi[:, :, None, :]
        a_im = cr[d_][None] * pi[:, :, None, :] + ci[d_][None] * pr[:, :, None, :]
        def pack(a):
            a = jnp.transpose(a, (1, 3, 0, 2)).reshape(n_oct, S5_OCT, p, t, h)
            a = jnp.einsum('ogprh,gk->ogprkh', a, eye)
            return a.reshape(n_oct, S5_OCT * p, t * S5_OCT * h)
        return pack(a_re), pack(-a_im)

    wyf_re, wyf_im = wy_dir(0, r + 1)
    wyb_re, wyb_im = wy_dir(1, t - r)
    wy = jnp.concatenate([wyf_re, wyf_im, wyb_re, wyb_im], axis=1)
    dec_re = pw_re[:, t].reshape(2, n_oct, S5_OCT * p)
    dec_im = pw_im[:, t].reshape(2, n_oct, S5_OCT * p)
    dec = jnp.concatenate([dec_re, dec_im], axis=-1)
    dec = jnp.transpose(dec, (1, 0, 2)).reshape(n_oct * 2, 1, 2 * S5_OCT * p)
    return kmat.astype(BF16), we.astype(BF16), wy.astype(BF16), dec


def _gather_octet(u_ref, lhs_ref, o, n_oct):
    width = n_oct * LANE
    for oo in range(n_oct):
        @pl.when(o == oo)
        def _(oo=oo):
            for r in range(S5_BLOCK):
                lhs_ref[:, r * LANE:(r + 1) * LANE] = u_ref[:, r * width + oo * LANE: r * width + (oo + 1) * LANE]


def _s5e_kernel(u_ref, we_ref, e_ref, lhs_ref, *, n_oct):
    o = pl.program_id(1)
    _gather_octet(u_ref, lhs_ref, o, n_oct)
    e_ref[...] = jnp.dot(lhs_ref[...], we_ref[...], preferred_element_type=F32)


def _s5e(u2, we, *, tmr=256):
    m, wid = u2.shape
    n_oct = we.shape[0]
    kdim = we.shape[1]
    ncol = we.shape[2]
    tmr = _tile(m, tmr, 16)
    return pl.pallas_call(
        functools.partial(_s5e_kernel, n_oct=n_oct),
        grid=(m // tmr, n_oct),
        in_specs=[
            pl.BlockSpec((tmr, wid), lambda i, o: (i, 0)),
            pl.BlockSpec((None, kdim, ncol), lambda i, o: (o, 0, 0)),
        ],
        out_specs=pl.BlockSpec((tmr, ncol), lambda i, o: (i, o)),
        out_shape=jax.ShapeDtypeStruct((m, n_oct * ncol), F32),
        scratch_shapes=[pltpu.VMEM((tmr, kdim), BF16)],
        compiler_params=_params(("arbitrary", "arbitrary")),
        name="s5e",
    )(u2, we)


def _s5scan_kernel(ec_ref, el_ref, a_ref, sc_ref, sl_ref, *, nb, n_c, n_l, half):
    rev = pl.program_id(1) == 1
    a_re = a_ref[:, :half]
    a_im = a_ref[:, half:]

    def run(e_ref, s_ref, n, carry):
        def body(s, carry):
            i = jnp.where(rev, n - 1 - s, s)
            new = []
            for b in range(nb):
                x_re, x_im = carry[b]
                row = b * n + i
                s_ref[pl.ds(row, 1), :half] = x_re
                s_ref[pl.ds(row, 1), half:] = x_im
                e_re = e_ref[pl.ds(row, 1), :half]
                e_im = e_ref[pl.ds(row, 1), half:]
                new.append((a_re * x_re - a_im * x_im + e_re, a_re * x_im + a_im * x_re + e_im))
            return tuple(new)
        return lax.fori_loop(0, n, body, carry)

    zero = jnp.zeros((1, half), F32)
    carry = tuple((zero, zero) for _ in range(nb))
    carry = run(ec_ref, sc_ref, n_c, carry)
    run(el_ref, sl_ref, n_l, carry)


def _s5scan(e_ctx, e_lat, dec, *, nb):
    mc, wid = e_ctx.shape
    ml = e_lat.shape[0]
    ncol = dec.shape[-1]
    nblk = wid // ncol
    return pl.pallas_call(
        functools.partial(_s5scan_kernel, nb=nb, n_c=mc // nb, n_l=ml // nb, half=ncol // 2),
        grid=(nblk // 2, 2),
        in_specs=[
            pl.BlockSpec((mc, ncol), lambda o, d: (0, o * 2 + d)),
            pl.BlockSpec((ml, ncol), lambda o, d: (0, o * 2 + d)),
            pl.BlockSpec((None, 1, ncol), lambda o, d: (o * 2 + d, 0, 0)),
        ],
        out_specs=[
            pl.BlockSpec((mc, ncol), lambda o, d: (0, o * 2 + d)),
            pl.BlockSpec((ml, ncol), lambda o, d: (0, o * 2 + d)),
        ],
        out_shape=[jax.ShapeDtypeStruct((mc, wid), F32), jax.ShapeDtypeStruct((ml, wid), F32)],
        compiler_params=_params(("arbitrary", "arbitrary")),
        name="s5scan",
    )(e_ctx, e_lat, dec)


def _s5y_kernel(u_ref, s_ref, k_ref, wy_ref, y_ref, lhs_ref, *, n_oct):
    o = pl.program_id(1)
    _gather_octet(u_ref, lhs_ref, o, n_oct)
    res = jnp.dot(lhs_ref[...], k_ref[...], preferred_element_type=F32)
    res += jnp.dot(s_ref[...].astype(BF16), wy_ref[...], preferred_element_type=F32)
    width = n_oct * LANE
    for oo in range(n_oct):
        @pl.when(o == oo)
        def _(oo=oo):
            for r in range(S5_BLOCK):
                y_ref[:, r * width + oo * LANE: r * width + (oo + 1) * LANE] = res[:, r * LANE:(r + 1) * LANE]


def _s5y(u2, s, kmat, wy, *, tmr=256):
    m, wid = u2.shape
    n_oct, kdim, _ = kmat.shape
    sdim = wy.shape[1]
    tmr = _tile(m, tmr, 16)
    return pl.pallas_call(
        functools.partial(_s5y_kernel, n_oct=n_oct),
        grid=(m // tmr, n_oct),
        in_specs=[
            pl.BlockSpec((tmr, wid), lambda i, o: (i, 0)),
            pl.BlockSpec((tmr, sdim), lambda i, o: (i, o)),
            pl.BlockSpec((None, kdim, kdim), lambda i, o: (o, 0, 0)),
            pl.BlockSpec((None, sdim, kdim), lambda i, o: (o, 0, 0)),
        ],
        out_specs=pl.BlockSpec((tmr, wid), lambda i, o: (i, 0)),
        out_shape=jax.ShapeDtypeStruct((m, wid), F32),
        scratch_shapes=[pltpu.VMEM((tmr, kdim), BF16)],
        compiler_params=_params(("arbitrary", "arbitrary")),
        name="s5y",
    )(u2, s, kmat, wy)


def _hgrn_consts():
    c = CHUNK
    t = np.arange(c)
    u = np.arange(c)
    mats = [(u[None, :] <= t[:, None]).astype(np.float32), (u[None, :] > t[:, None]).astype(np.float32)]
    masks = []
    half = c // 2
    while half >= SUB:
        par = 2 * half
        mid = (t // par) * par + half
        second = t >= mid
        m_q = (u[None, :] >= mid[:, None]) & (u[None, :] <= t[:, None])
        m_k = (u[None, :] > t[:, None]) & (u[None, :] <= mid[:, None] - 1)
        mats.append(np.where(second[:, None], m_q, m_k).astype(np.float32))
        same_parent = (t[:, None] // par) == (t[None, :] // par)
        masks.append((same_parent & second[:, None] & (~second)[None, :]).astype(np.float32))
        half //= 2
    ref = (t // SUB) * SUB + SUB // 2 - 1
    pos = (u[None, :] > ref[:, None]) & (u[None, :] <= t[:, None])
    neg = (u[None, :] > t[:, None]) & (u[None, :] <= ref[:, None])
    mats.append(pos.astype(np.float32) - neg.astype(np.float32))
    masks.append((((t[:, None] // SUB) == (t[None, :] // SUB)) & (t[None, :] <= t[:, None])).astype(np.float32))
    return np.stack(mats), np.stack(masks)


def _hgrn_kernel(*refs, nsub, heads, reverse, final, n_lvl):
    if final:
        (q_ref, z_ref, v_ref, lb_ref, mats_ref, masks_ref, s0_ref, of_ref, gate_ref, hnw_ref,
         o_ref, sfin_ref, st_ref) = refs
    else:
        q_ref, z_ref, v_ref, lb_ref, mats_ref, masks_ref, s0_ref, o_ref, sfin_ref, st_ref = refs
    j = pl.program_id(1)
    nj = pl.num_programs(1)
    c = CHUNK
    hd = HEAD_DIM

    @pl.when(j == 0)
    def _():
        st_ref[...] = s0_ref[...]

    mats = mats_ref[...]
    lb = lb_ref[...]
    last_row = 0 if reverse else c - 1

    def chunk(s, _):
        cl = (nsub - 1 - s) if reverse else s
        r0 = pl.multiple_of(cl * c, c)
        rows = pl.ds(r0, c)
        z = z_ref[rows, :]
        q = q_ref[rows, :]
        v = v_ref[rows, :].astype(BF16)
        f = lb + (1.0 - lb) * jax.nn.sigmoid(z)
        k = 1.0 - f
        g = jnp.log(jnp.maximum(f, F_MIN))
        g_hi, g_mid, g_lo = _split3(g)
        ex = jnp.dot(mats, g_hi, preferred_element_type=F32)
        ex += jnp.dot(mats, g_mid, preferred_element_type=F32)
        ex += jnp.dot(mats, g_lo, preferred_element_type=F32)
        q_dec = (q * jnp.exp(ex[0:c])).astype(BF16)
        k_dec = (k * jnp.exp(ex[c:2 * c])).astype(BF16)
        dec_row = jnp.exp(ex[last_row:last_row + 1])
        ql, kl = [], []
        for lv in range(n_lvl - 1):
            e = jnp.exp(ex[(2 + lv) * c:(3 + lv) * c])
            ql.append((q * e).astype(BF16))
            kl.append((k * e).astype(BF16))
        ed = ex[(1 + n_lvl) * c:(2 + n_lvl) * c]
        ql.append((q * jnp.exp(ed)).astype(BF16))
        kl.append((k * jnp.exp(-ed)).astype(BF16))
        nt = (((1,), (1,)), ((), ()))
        tn = (((0,), (0,)), ((), ()))
        outs = []
        for h in range(heads):
            sl = slice(h * hd, (h + 1) * hd)
            a = jnp.zeros((c, c), F32)
            for lv in range(n_lvl):
                sc = lax.dot_general(ql[lv][:, sl], kl[lv][:, sl], nt, preferred_element_type=F32)
                a += jnp.where(masks_ref[lv] > 0.0, sc, 0.0)
            st = st_ref[h]
            o_h = jnp.dot(a.astype(BF16), v[:, sl], preferred_element_type=F32)
            o_h += lax.dot_general(q_dec[:, sl], st.astype(BF16), nt, preferred_element_type=F32)
            st_ref[h] = st * dec_row[:, sl] + lax.dot_general(v[:, sl], k_dec[:, sl], tn,
                                                              preferred_element_type=F32)
            outs.append(o_h)
        if final:
            of = of_ref[rows, :]
            gate = gate_ref[rows, :]
            res = []
            for h in range(heads):
                sl = slice(h * hd, (h + 1) * hd)
                o_h = outs[h] + of[:, sl]
                ms = jnp.mean(o_h * o_h, axis=-1, keepdims=True)
                res.append(o_h * lax.rsqrt(ms + EPS) * hnw_ref[...])
            o = jnp.concatenate(res, axis=-1) * _silu(gate)
            o_ref[rows, :] = o.astype(o_ref.dtype)
        else:
            o_ref[rows, :] = jnp.concatenate(outs, axis=-1)
        return 0

    lax.fori_loop(0, nsub, chunk, 0)

    @pl.when(j == nj - 1)
    def _():
        sfin_ref[...] = st_ref[...]


def _hgrn(p, lb, mats, masks, s0, *, direction, width, o_fwd=None, hnw=None, nsub=4):
    b, t, _ = p.shape
    heads = width // HEAD_DIM
    nsub = _tile(t // CHUNK, nsub, 1)
    rows = nsub * CHUNK
    nj = t // rows
    reverse = direction == 1
    final = o_fwd is not None
    n_lvl = masks.shape[0]
    blk = (lambda j: nj - 1 - j) if reverse else (lambda j: j)
    zcol = 2 if reverse else 1
    in_specs = [
        pl.BlockSpec((None, rows, width), lambda bi, j: (bi, blk(j), 0)),
        pl.BlockSpec((None, rows, width), lambda bi, j: (bi, blk(j), zcol)),
        pl.BlockSpec((None, rows, width), lambda bi, j: (bi, blk(j), 3)),
        pl.BlockSpec((1, width), lambda bi, j: (0, 0)),
        pl.BlockSpec(mats.shape, lambda bi, j: (0, 0)),
        pl.BlockSpec(masks.shape, lambda bi, j: (0, 0, 0)),
        pl.BlockSpec((None, heads, HEAD_DIM, HEAD_DIM), lambda bi, j: (bi, 0, 0, 0)),
    ]
    args = [p, p, p, lb.reshape(1, width), mats, masks, s0]
    if final:
        in_specs += [
            pl.BlockSpec((None, rows, width), lambda bi, j: (bi, blk(j), 0)),
            pl.BlockSpec((None, rows, width), lambda bi, j: (bi, blk(j), 4)),
            pl.BlockSpec((1, HEAD_DIM), lambda bi, j: (0, 0)),
        ]
        args += [o_fwd, p, hnw.reshape(1, HEAD_DIM)]
    return pl.pallas_call(
        functools.partial(_hgrn_kernel, nsub=nsub, heads=heads, reverse=reverse, final=final, n_lvl=n_lvl),
        grid=(b, nj),
        in_specs=in_specs,
        out_specs=[
            pl.BlockSpec((None, rows, width), lambda bi, j: (bi, blk(j), 0)),
            pl.BlockSpec((None, heads, HEAD_DIM, HEAD_DIM), lambda bi, j: (bi, 0, 0, 0)),
        ],
        out_shape=[
            jax.ShapeDtypeStruct((b, t, width), BF16 if final else F32),
            jax.ShapeDtypeStruct((b, heads, HEAD_DIM, HEAD_DIM), F32),
        ],
        scratch_shapes=[pltpu.VMEM((heads, HEAD_DIM, HEAD_DIM), F32)],
        compiler_params=_params(("arbitrary", "arbitrary")),
        name="hgrn_bwd" if final else "hgrn_fwd",
    )(*args)


def _mixout_kernel(x_ref, y_ref, u_ref, hg_ref, m_ref, dsk_ref, wglu_ref, bglu_ref, wo_ref, o_ref,
                   *, s5w, rows, rb):
    yy = y_ref[...] + dsk_ref[...] * u_ref[...]
    g = _gelu_tanh(yy)
    zz = jnp.dot(g.astype(BF16), wglu_ref[...], preferred_element_type=F32) + bglu_ref[...]
    s5 = (g * jax.nn.sigmoid(zz)).astype(BF16)
    hw = hg_ref.shape[-1] // rb
    if rb == 1:
        hg = hg_ref[...]
    else:
        hg = jnp.concatenate([hg_ref[:, r * hw:(r + 1) * hw] for r in range(rb)], axis=0)
    acc = jnp.dot(s5, wo_ref[:s5w, :], preferred_element_type=F32)
    acc += jnp.dot(hg, wo_ref[s5w:, :], preferred_element_type=F32)
    o_ref[...] = x_ref[...] + m_ref[0, 5:6, :] * acc


def _mixout(x3d, y3d, u3d, hg, mods, d_skip, w_glu, b_glu, w_o, *, mod_of_batch, colmajor, rb=8):
    b, t, d = x3d.shape
    s5w = y3d.shape[-1]
    hw = hg.shape[-1]
    if colmajor:
        cols = GRID_W
        rows = t // cols
        rb = _tile(rows, rb, 1)
        tm = rb * cols
        hg_v = hg.reshape(b, cols, rows * hw)
        hg_spec = pl.BlockSpec((None, cols, rb * hw), lambda bi, i: (bi, 0, i))
    else:
        rb = 1
        tm = _tile(t, 512)
        hg_v = hg
        hg_spec = pl.BlockSpec((None, tm, hw), lambda bi, i: (bi, i, 0))
    mod_fn = (lambda bi: bi) if mod_of_batch else (lambda bi: b)
    return pl.pallas_call(
        functools.partial(_mixout_kernel, s5w=s5w, rows=tm, rb=rb),
        grid=(b, t // tm),
        in_specs=[
            pl.BlockSpec((None, tm, d), lambda bi, i: (bi, i, 0)),
            pl.BlockSpec((None, tm, s5w), lambda bi, i: (bi, i, 0)),
            pl.BlockSpec((None, tm, s5w), lambda bi, i: (bi, i, 0)),
            hg_spec,
            pl.BlockSpec((1, N_MOD, d), lambda bi, i: (mod_fn(bi), 0, 0)),
            pl.BlockSpec((1, s5w), lambda bi, i: (0, 0)),
            pl.BlockSpec((s5w, s5w), lambda bi, i: (0, 0)),
            pl.BlockSpec((1, s5w), lambda bi, i: (0, 0)),
            pl.BlockSpec((s5w + hw, d), lambda bi, i: (0, 0)),
        ],
        out_specs=pl.BlockSpec((None, tm, d), lambda bi, i: (bi, i, 0)),
        out_shape=jax.ShapeDtypeStruct((b, t, d), F32),
        compiler_params=_params(("arbitrary", "arbitrary")),
        name="mixout",
    )(x3d, y3d, u3d, hg_v, mods, d_skip.reshape(1, s5w), w_glu, b_glu.reshape(1, s5w), w_o)


def _pad_cast(w, axis, mult):
    n = w.shape[axis]
    padn = (-n) % mult
    if padn:
        pad = [(0, 0)] * w.ndim
        pad[axis] = (0, padn)
        w = jnp.pad(w, pad)
    return w.astype(BF16)


def kernel(x, c, ctx, c_ctx, w_ada, b_ada, norm_w, ffn_w_gate, ffn_w_up, ffn_w_down, w_in, w_out,
           s5_lambda_re, s5_lambda_im, s5_log_step, s5_b_re, s5_b_im, s5_c_re, s5_c_im, s5_d,
           s5_w_glu, s5_b_glu, hgrn_lower_bounds, hgrn_norm_w, final_norm_w):
    batch, seq, d = x.shape
    n_ctx = ctx.shape[1]
    depth = w_ada.shape[0]
    s5w = s5_d.shape[-1]
    hw = hgrn_lower_bounds.shape[-1]
    rows = seq // GRID_W
    assert batch < MOD_ROWS and rows % CHUNK == 0 and n_ctx % CHUNK == 0
    assert seq % S5_BLOCK == 0 and n_ctx % S5_BLOCK == 0 and (s5w // S5_GROUP) % S5_OCT == 0

    cvec = jnp.zeros((MOD_ROWS, d), F32).at[:batch].set(c.astype(F32)).at[batch].set(c_ctx.astype(F32))
    mods_all = _ada(cvec, w_ada, b_ada).reshape(depth, MOD_ROWS, N_MOD, d)

    lb_soft = jax.nn.softmax(hgrn_lower_bounds.astype(F32), axis=0)
    lb_all = jnp.cumsum(lb_soft, axis=0) - lb_soft[0]

    mats_np, masks_np = _hgrn_consts()
    n_m = mats_np.shape[0]
    mats_f = jnp.asarray(mats_np.reshape(n_m * CHUNK, CHUNK), BF16)
    mats_b = jnp.asarray(mats_np[:, ::-1, ::-1].reshape(n_m * CHUNK, CHUNK), BF16)
    masks_f = jnp.asarray(masks_np, F32)
    masks_b = jnp.asarray(masks_np[:, ::-1, ::-1].copy(), F32)
    heads = hw // HEAD_DIM
    s_zero = jnp.zeros((batch, heads, HEAD_DIM, HEAD_DIM), F32)

    xl = x.astype(F32)
    xc = ctx.astype(F32)
    nblk_l = seq // S5_BLOCK
    nblk_c = n_ctx // S5_BLOCK
    for l in range(depth):
        last = l == depth - 1
        mods = mods_all[l]
        wg = [_pad_cast(ffn_w_gate[l, i], 1, 512) for i in range(2)]
        wu = [_pad_cast(ffn_w_up[l, i], 1, 512) for i in range(2)]
        wd = [_pad_cast(ffn_w_down[l, i], 0, 512) for i in range(2)]
        w_u = w_in[l][:, :s5w].astype(BF16)
        w_h = w_in[l][:, s5w:].astype(BF16)
        w_o = w_out[l].astype(BF16)
        w_glu = s5_w_glu[l].astype(BF16)

        xl = _ffn(xl.reshape(batch * seq, d), mods, norm_w[l, 0], wg[0], wu[0], wd[0],
                  base=0, rows_per_mod=seq, mod_off=0).reshape(batch, seq, d)
        xc = _ffn(xc.reshape(batch * n_ctx, d), mods, norm_w[l, 0], wg[0], wu[0], wd[0],
                  base=0, rows_per_mod=batch * n_ctx, mod_off=batch).reshape(batch, n_ctx, d)

        ul_f, ul_b = _uproj(xl.reshape(batch * seq, d), mods, norm_w[l, 1], w_u, rows_per_mod=seq, mod_off=0)
        uc_f, uc_b = _uproj(xc.reshape(batch * n_ctx, d), mods, norm_w[l, 1], w_u,
                            rows_per_mod=batch * n_ctx, mod_off=batch)
        p_lat = _hproj(xl.reshape(batch, rows, GRID_W * d), mods, norm_w[l, 1], w_h,
                       rows=rows, cols=GRID_W, d=d, mod_of_batch=True, cb=16, tn=hw)
        p_ctx = _hproj(xc, mods, norm_w[l, 1], w_h, rows=n_ctx, cols=1, d=d, mod_of_batch=False, cb=1, tn=hw)

        kmat, we, wy, dec = _s5_prep(s5_lambda_re[l], s5_lambda_im[l], s5_log_step[l], s5_b_re[l],
                                     s5_b_im[l], s5_c_re[l], s5_c_im[l])
        u2_l = ul_b.reshape(batch * nblk_l, S5_BLOCK * s5w)
        u2_c = uc_b.reshape(batch * nblk_c, S5_BLOCK * s5w)
        e_l = _s5e(u2_l, we)
        e_c = _s5e(u2_c, we)
        st_c, st_l = _s5scan(e_c, e_l, dec, nb=batch)
        y_l = _s5y(u2_l, st_l, kmat, wy).reshape(batch, seq, s5w)

        lb_f = lb_all[l, 0]
        lb_b = lb_all[l, 1]
        oc_f, sc_f = _hgrn(p_ctx, lb_f, mats_f, masks_f, s_zero, direction=0, width=hw)
        ol_f, _ = _hgrn(p_lat, lb_f, mats_f, masks_f, sc_f, direction=0, width=hw)
        hg_c, sc_b = _hgrn(p_ctx, lb_b, mats_b, masks_b, s_zero, direction=1, width=hw,
                           o_fwd=oc_f, hnw=hgrn_norm_w[l])
        hg_l, _ = _hgrn(p_lat, lb_b, mats_b, masks_b, sc_b, direction=1, width=hw,
                        o_fwd=ol_f, hnw=hgrn_norm_w[l])

        xl = _mixout(xl, y_l, ul_f.reshape(batch, seq, s5w), hg_l, mods, s5_d[l], w_glu, s5_b_glu[l], w_o,
                     mod_of_batch=True, colmajor=True)
        if not last:
            y_c = _s5y(u2_c, st_c, kmat, wy).reshape(batch, n_ctx, s5w)
            xc = _mixout(xc, y_c, uc_f.reshape(batch, n_ctx, s5w), hg_c, mods, s5_d[l], w_glu, s5_b_glu[l],
                         w_o, mod_of_batch=False, colmajor=False)

        xl = _ffn(xl.reshape(batch * seq, d), mods, norm_w[l, 2], wg[1], wu[1], wd[1],
                  base=6, rows_per_mod=seq, mod_off=0,
                  final_nw=final_norm_w if last else None).reshape(batch, seq, d)
        if not last:
            xc = _ffn(xc.reshape(batch * n_ctx, d), mods, norm_w[l, 2], wg[1], wu[1], wd[1],
                      base=6, rows_per_mod=batch * n_ctx, mod_off=batch).reshape(batch, n_ctx, d)
    return xl
```

```python
import functools

import numpy as np
import jax
import jax.numpy as jnp
from jax import lax
from jax.experimental import pallas as pl
from jax.experimental.pallas import tpu as pltpu

F32 = jnp.float32
BF16 = jnp.bfloat16

EPS = 1e-6
F_MIN = 1e-6
LAMBDA_RE_MAX = -1e-4
GRID_W = 64
N_MOD = 9
S5_GROUP = 16
S5_STATE = 64
S5_BLOCK = 8
S5_OCT = 8
HEAD_DIM = 128
CHUNK = 64
SUB = 8
LANE = 128
MOD_ROWS = 8
VMEM_LIMIT = 56 * 1024 * 1024


def _params(sem):
    return pltpu.CompilerParams(dimension_semantics=sem, vmem_limit_bytes=VMEM_LIMIT)


def _tile(n, pref, mult=8):
    t = min(n, pref)
    while t > 0:
        if n % t == 0 and t % mult == 0:
            return t
        t -= 1
    return n


def _norm_mod(x, nw, shift, scale):
    ms = jnp.mean(x * x, axis=-1, keepdims=True)
    y = x * lax.rsqrt(ms + EPS) * nw
    return y * (1.0 + scale) + shift


def _silu(x):
    return x * jax.nn.sigmoid(x)


def _gelu_tanh(x):
    return 0.5 * x * (1.0 + jnp.tanh(0.7978845608028654 * (x + 0.044715 * (x * x * x))))


def _ada_kernel(c_ref, w_ref, b_ref, o_ref):
    a = _silu(c_ref[...])
    a_hi = a.astype(BF16)
    a_lo = (a - a_hi.astype(F32)).astype(BF16)
    w = w_ref[...]
    w_hi = w.astype(BF16)
    w_lo = (w - w_hi.astype(F32)).astype(BF16)
    acc = jnp.dot(a_hi, w_hi, preferred_element_type=F32)
    acc += jnp.dot(a_lo, w_hi, preferred_element_type=F32)
    acc += jnp.dot(a_hi, w_lo, preferred_element_type=F32)
    o_ref[...] = acc + b_ref[...]


def _ada(cvec, w_ada, b_ada):
    depth, d, n = w_ada.shape
    tn = _tile(n, 1024, LANE)
    return pl.pallas_call(
        _ada_kernel,
        grid=(depth, n // tn),
        in_specs=[
            pl.BlockSpec((MOD_ROWS, d), lambda l, j: (0, 0)),
            pl.BlockSpec((None, d, tn), lambda l, j: (l, 0, j)),
            pl.BlockSpec((None, 1, tn), lambda l, j: (l, 0, j)),
        ],
        out_specs=pl.BlockSpec((None, MOD_ROWS, tn), lambda l, j: (l, 0, j)),
        out_shape=jax.ShapeDtypeStruct((depth, MOD_ROWS, n), F32),
        compiler_params=_params(("arbitrary", "arbitrary")),
        name="ada",
    )(cvec, w_ada, b_ada.reshape(depth, 1, n))


def _ffn_kernel(*refs, base, n_f, final):
    if final:
        x_ref, m_ref, nw_ref, wg_ref, wu_ref, wd_ref, fnw_ref, o_ref, h_ref, acc_ref = refs
    else:
        x_ref, m_ref, nw_ref, wg_ref, wu_ref, wd_ref, o_ref, h_ref, acc_ref = refs
    j = pl.program_id(1)

    @pl.when(j == 0)
    def _():
        h = _norm_mod(x_ref[...], nw_ref[...], m_ref[0, base:base + 1, :], m_ref[0, base + 1:base + 2, :])
        h_ref[...] = h.astype(BF16)
        acc_ref[...] = jnp.zeros_like(acc_ref)

    h = h_ref[...]
    g = jnp.dot(h, wg_ref[...], preferred_element_type=F32)
    u = jnp.dot(h, wu_ref[...], preferred_element_type=F32)
    a = (_silu(g) * u).astype(BF16)
    acc_ref[...] += jnp.dot(a, wd_ref[...], preferred_element_type=F32)

    @pl.when(j == n_f - 1)
    def _():
        y = x_ref[...] + (0.5 * m_ref[0, base + 2:base + 3, :]) * acc_ref[...]
        if final:
            ms = jnp.mean(y * y, axis=-1, keepdims=True)
            y = y * lax.rsqrt(ms + EPS) * fnw_ref[...]
        o_ref[...] = y


def _ffn(x2d, mods, nw, wg, wu, wd, *, base, rows_per_mod, mod_off, final_nw=None, tm=512, tf=512):
    n, d = x2d.shape
    fp = wg.shape[1]
    tm = _tile(n, tm)
    tf = _tile(fp, tf, LANE)
    n_f = fp // tf
    final = final_nw is not None
    in_specs = [
        pl.BlockSpec((tm, d), lambda i, j: (i, 0)),
        pl.BlockSpec((1, N_MOD, d), lambda i, j: ((i * tm) // rows_per_mod + mod_off, 0, 0)),
        pl.BlockSpec((1, d), lambda i, j: (0, 0)),
        pl.BlockSpec((d, tf), lambda i, j: (0, j)),
        pl.BlockSpec((d, tf), lambda i, j: (0, j)),
        pl.BlockSpec((tf, d), lambda i, j: (j, 0)),
    ]
    args = [x2d, mods, nw.reshape(1, d), wg, wu, wd]
    if final:
        in_specs.append(pl.BlockSpec((1, d), lambda i, j: (0, 0)))
        args.append(final_nw.reshape(1, d))
    return pl.pallas_call(
        functools.partial(_ffn_kernel, base=base, n_f=n_f, final=final),
        grid=(n // tm, n_f),
        in_specs=in_specs,
        out_specs=pl.BlockSpec((tm, d), lambda i, j: (i, 0)),
        out_shape=jax.ShapeDtypeStruct((n, d), F32),
        scratch_shapes=[pltpu.VMEM((tm, d), BF16), pltpu.VMEM((tm, d), F32)],
        compiler_params=_params(("arbitrary", "arbitrary")),
        name="ffn_final" if final else "ffn",
    )(*args)


def _uproj_kernel(x_ref, m_ref, nw_ref, w_ref, u_ref):
    h = _norm_mod(x_ref[...], nw_ref[...], m_ref[0, 3:4, :], m_ref[0, 4:5, :])
    u_ref[...] = jnp.dot(h.astype(BF16), w_ref[...], preferred_element_type=F32)


def _uproj(x2d, mods, nw, w_u, *, rows_per_mod, mod_off, tm=512):
    n, d = x2d.shape
    s5w = w_u.shape[1]
    tm = _tile(n, tm)
    return pl.pallas_call(
        _uproj_kernel,
        grid=(n // tm,),
        in_specs=[
            pl.BlockSpec((tm, d), lambda i: (i, 0)),
            pl.BlockSpec((1, N_MOD, d), lambda i: ((i * tm) // rows_per_mod + mod_off, 0, 0)),
            pl.BlockSpec((1, d), lambda i: (0, 0)),
            pl.BlockSpec((d, s5w), lambda i: (0, 0)),
        ],
        out_specs=pl.BlockSpec((tm, s5w), lambda i: (i, 0)),
        out_shape=jax.ShapeDtypeStruct((n, s5w), F32),
        compiler_params=_params(("arbitrary",)),
        name="uproj",
    )(x2d, mods, nw.reshape(1, d), w_u)


def _hproj_kernel(x_ref, m_ref, nw_ref, w_ref, o_ref, h_ref, *, rows, cb, colmajor):
    n = pl.program_id(2)

    @pl.when(n == 0)
    def _():
        for cl in range(cb):
            xc = x_ref[:, cl, :] if colmajor else x_ref[...]
            h = _norm_mod(xc, nw_ref[...], m_ref[0, 3:4, :], m_ref[0, 4:5, :])
            h_ref[cl * rows:(cl + 1) * rows, :] = h.astype(BF16)

    p = jnp.dot(h_ref[...], w_ref[...], preferred_element_type=F32)

    @pl.when(n == 0)
    def _():
        o_ref[...] = _silu(p)

    @pl.when(n != 0)
    def _():
        o_ref[...] = p


def _hproj(x3d, mods, nw, w_h, *, cols, mod_of_batch, cb, tn):
    b, t, d = x3d.shape
    rows = t // cols
    nh = w_h.shape[1]
    colmajor = cols > 1
    cb = _tile(cols, cb, 8) if colmajor else 1
    mod_fn = (lambda bi: bi) if mod_of_batch else (lambda bi: b)
    if colmajor:
        x_v = x3d.reshape(b, rows, cols, d)
        x_spec = pl.BlockSpec((None, rows, cb, d), lambda bi, ci, n: (bi, 0, ci, 0))
    else:
        x_v = x3d
        x_spec = pl.BlockSpec((None, rows, d), lambda bi, ci, n: (bi, 0, 0))
    return pl.pallas_call(
        functools.partial(_hproj_kernel, rows=rows, cb=cb, colmajor=colmajor),
        grid=(b, cols // cb, nh // tn),
        in_specs=[
            x_spec,
            pl.BlockSpec((1, N_MOD, d), lambda bi, ci, n: (mod_fn(bi), 0, 0)),
            pl.BlockSpec((1, d), lambda bi, ci, n: (0, 0)),
            pl.BlockSpec((d, tn), lambda bi, ci, n: (0, n)),
        ],
        out_specs=pl.BlockSpec((None, cb * rows, tn), lambda bi, ci, n: (bi, ci, n)),
        out_shape=jax.ShapeDtypeStruct((b, t, nh), F32),
        scratch_shapes=[pltpu.VMEM((cb * rows, d), BF16)],
        compiler_params=_params(("arbitrary", "arbitrary", "arbitrary")),
        name="hproj",
    )(x_v, mods, nw.reshape(1, d), w_h)


def _s5_sel_consts():
    t, h, p, o8 = S5_BLOCK, S5_GROUP, S5_STATE, S5_OCT
    colsel = np.zeros((t, 2, t, h, t, o8, h), np.float32)
    for r in range(t):
        for r2 in range(t):
            for hh in range(h):
                if r2 >= r:
                    colsel[r, 0, r2 - r, hh, r2, :, hh] = 1.0
                if r >= r2:
                    colsel[r, 1, r - r2, hh, r2, :, hh] = 1.0
    colsel = colsel.reshape(t, 2 * t * h, t * o8 * h)
    tile_e = np.zeros((4, p, 4, o8, p), np.float32)
    tile_y = np.zeros((4, h, 4, o8, h), np.float32)
    for dp in range(4):
        for i in range(p):
            tile_e[dp, i, dp, :, i] = 1.0
        for i in range(h):
            tile_y[dp, i, dp, :, i] = 1.0
    return colsel, tile_e.reshape(4 * p, 4 * o8 * p), tile_y.reshape(4 * h, 4 * o8 * h)


def _s5w_kernel(uk_ref, pb_ref, ca_ref, colsel_ref, tile_e_ref, tile_y_ref, k_ref, we_ref, wy_ref):
    t, h, p, o8 = S5_BLOCK, S5_GROUP, S5_STATE, S5_OCT
    gh = o8 * h
    gp = o8 * p

    def diag_mask(shape, row_div, col_mod, col_div):
        rg = lax.broadcasted_iota(jnp.int32, shape, 0) // row_div
        cg = (lax.broadcasted_iota(jnp.int32, shape, 1) % col_mod) // col_div
        return rg == cg

    mk = diag_mask((gh, t * gh), h, gh, h)
    me = diag_mask((gh, 4 * gp), h, gp, p)
    my = diag_mask((gp, 4 * gh), p, gh, h)
    uk = uk_ref[...].astype(BF16)
    for r in range(t):
        blk = jnp.dot(uk, colsel_ref[r], preferred_element_type=F32)
        k_ref[r * gh:(r + 1) * gh, :] = jnp.where(mk, blk, 0.0).astype(BF16)
        blk = jnp.dot(pb_ref[r].astype(BF16), tile_e_ref[...], preferred_element_type=F32)
        we_ref[r * gh:(r + 1) * gh, :] = jnp.where(me, blk, 0.0).astype(BF16)
        blk = jnp.where(my, jnp.dot(ca_ref[r].astype(BF16), tile_y_ref[...], preferred_element_type=F32), 0.0)
        for dp in range(4):
            wy_ref[dp * gp:(dp + 1) * gp, r * gh:(r + 1) * gh] = blk[:, dp * gh:(dp + 1) * gh].astype(BF16)


def _s5_prep(lam_re, lam_im, log_step, b_re, b_im, c_re, c_im):
    hp = lax.Precision.HIGHEST
    t = S5_BLOCK
    g, p = lam_re.shape[1:]
    h = b_re.shape[-1]
    n_oct = g // S5_OCT
    lam_re = jnp.minimum(lam_re.astype(F32), LAMBDA_RE_MAX)
    lam_im = lam_im.astype(F32)
    dt = jnp.exp(log_step.astype(F32))[..., None]
    mag = jnp.exp(lam_re * dt)
    lb_re = mag * jnp.cos(lam_im * dt)
    lb_im = mag * jnp.sin(lam_im * dt)
    den = lam_re * lam_re + lam_im * lam_im
    nr = lb_re - 1.0
    ni = lb_im
    cf_re = (nr * lam_re + ni * lam_im) / den
    cf_im = (ni * lam_re - nr * lam_im) / den
    b_re = b_re.astype(F32)
    b_im = b_im.astype(F32)
    br = cf_re[..., None] * b_re - cf_im[..., None] * b_im
    bi = cf_re[..., None] * b_im + cf_im[..., None] * b_re
    cr = c_re.astype(F32)
    ci = c_im.astype(F32)
    j = jnp.arange(t + 1, dtype=F32)[None, :, None, None]
    pmag = jnp.exp(j * (lam_re * dt)[:, None])
    pw_re = pmag * jnp.cos(j * (lam_im * dt)[:, None])
    pw_im = pmag * jnp.sin(j * (lam_im * dt)[:, None])
    pb_re = pw_re[..., None] * br[:, None] - pw_im[..., None] * bi[:, None]
    pb_im = pw_re[..., None] * bi[:, None] + pw_im[..., None] * br[:, None]
    kj = (jnp.einsum('dgep,djgph->djgeh', cr, pb_re[:, :t], precision=hp)
          - jnp.einsum('dgep,djgph->djgeh', ci, pb_im[:, :t], precision=hp))
    kj = kj.at[0, 0].add(kj[1, 0]).at[1, 0].set(0.0)
    uk = jnp.transpose(kj, (2, 4, 0, 1, 3)).reshape(n_oct, S5_OCT * h, 2 * t * h)
    r = np.arange(t)
    pb4 = jnp.stack([pb_re[0][t - 1 - r], pb_im[0][t - 1 - r], pb_re[1][r], pb_im[1][r]])
    pb3 = jnp.transpose(pb4.reshape(4, t, n_oct, S5_OCT, p, h), (2, 1, 3, 5, 0, 4))
    pb3 = pb3.reshape(n_oct, t, S5_OCT * h, 4 * p)

    def c_lam(d_, idx):
        pr = pw_re[d_][idx][:, :, None, :]
        pi = pw_im[d_][idx][:, :, None, :]
        return cr[d_][None] * pr - ci[d_][None] * pi, -(cr[d_][None] * pi + ci[d_][None] * pr)

    ca4 = jnp.stack(c_lam(0, r + 1) + c_lam(1, t - r))
    ca3 = jnp.transpose(ca4.reshape(4, t, n_oct, S5_OCT, h, p), (2, 1, 3, 5, 0, 4))
    ca3 = ca3.reshape(n_oct, t, S5_OCT * p, 4 * h)
    colsel, tile_e, tile_y = (jnp.asarray(a, BF16) for a in _s5_sel_consts())
    kdim = t * S5_OCT * h
    sdim = 4 * S5_OCT * p
    kmat, we, wy = pl.pallas_call(
        _s5w_kernel,
        grid=(n_oct,),
        in_specs=[
            pl.BlockSpec((None,) + uk.shape[1:], lambda o: (o, 0, 0)),
            pl.BlockSpec((None,) + pb3.shape[1:], lambda o: (o, 0, 0, 0)),
            pl.BlockSpec((None,) + ca3.shape[1:], lambda o: (o, 0, 0, 0)),
            pl.BlockSpec(colsel.shape, lambda o: (0, 0, 0)),
            pl.BlockSpec(tile_e.shape, lambda o: (0, 0)),
            pl.BlockSpec(tile_y.shape, lambda o: (0, 0)),
        ],
        out_specs=[
            pl.BlockSpec((None, kdim, kdim), lambda o: (o, 0, 0)),
            pl.BlockSpec((None, kdim, sdim), lambda o: (o, 0, 0)),
            pl.BlockSpec((None, sdim, kdim), lambda o: (o, 0, 0)),
        ],
        out_shape=[
            jax.ShapeDtypeStruct((n_oct, kdim, kdim), BF16),
            jax.ShapeDtypeStruct((n_oct, kdim, sdim), BF16),
            jax.ShapeDtypeStruct((n_oct, sdim, kdim), BF16),
        ],
        compiler_params=_params(("arbitrary",)),
        name="s5w",
    )(uk, pb3, ca3, colsel, tile_e, tile_y)
    dec_re = pw_re[:, t].reshape(2, n_oct, S5_OCT * p)
    dec_im = pw_im[:, t].reshape(2, n_oct, S5_OCT * p)
    dec = jnp.concatenate([dec_re, dec_im], axis=-1)
    dec = jnp.transpose(dec, (1, 0, 2)).reshape(n_oct * 2, 1, 2 * S5_OCT * p)
    return kmat, we, wy, dec


def _block_rows(u_ref, tmr):
    return jnp.concatenate(
        [u_ref[pl.ds(r, tmr, stride=S5_BLOCK), :].astype(BF16) for r in range(S5_BLOCK)], axis=1)


def _s5e_kernel(u_ref, we_ref, e_ref, *, tmr):
    e_ref[...] = jnp.dot(_block_rows(u_ref, tmr), we_ref[...], preferred_element_type=F32)


def _s5e(u, we, *, tmr=256):
    n, wid = u.shape
    m = n // S5_BLOCK
    n_oct, kdim, ncol = we.shape
    tmr = _tile(m, tmr)
    return pl.pallas_call(
        functools.partial(_s5e_kernel, tmr=tmr),
        grid=(n_oct, m // tmr),
        in_specs=[
            pl.BlockSpec((tmr * S5_BLOCK, LANE), lambda o, i: (i, o)),
            pl.BlockSpec((None, kdim, ncol), lambda o, i: (o, 0, 0)),
        ],
        out_specs=pl.BlockSpec((tmr, ncol), lambda o, i: (i, o)),
        out_shape=jax.ShapeDtypeStruct((m, n_oct * ncol), F32),
        compiler_params=_params(("arbitrary", "arbitrary")),
        name="s5e",
    )(u, we)


def _s5scan_kernel(ec_ref, el_ref, a_ref, sc_ref, sl_ref, *, nb, n_c, n_l, half):
    rev = pl.program_id(1) == 1
    a_re = a_ref[:, :half]
    a_im = a_ref[:, half:]

    def run(e_ref, s_ref, n, carry):
        def body(s, carry):
            i = jnp.where(rev, n - 1 - s, s)
            new = []
            for b in range(nb):
                x_re, x_im = carry[b]
                row = b * n + i
                s_ref[pl.ds(row, 1), :half] = x_re
                s_ref[pl.ds(row, 1), half:] = x_im
                e_re = e_ref[pl.ds(row, 1), :half]
                e_im = e_ref[pl.ds(row, 1), half:]
                new.append((a_re * x_re - a_im * x_im + e_re, a_re * x_im + a_im * x_re + e_im))
            return tuple(new)
        return lax.fori_loop(0, n, body, carry)

    zero = jnp.zeros((1, half), F32)
    carry = tuple((zero, zero) for _ in range(nb))
    carry = run(ec_ref, sc_ref, n_c, carry)
    run(el_ref, sl_ref, n_l, carry)


def _s5scan(e_ctx, e_lat, dec, *, nb):
    mc, wid = e_ctx.shape
    ml = e_lat.shape[0]
    ncol = dec.shape[-1]
    nblk = wid // ncol
    return pl.pallas_call(
        functools.partial(_s5scan_kernel, nb=nb, n_c=mc // nb, n_l=ml // nb, half=ncol // 2),
        grid=(nblk // 2, 2),
        in_specs=[
            pl.BlockSpec((mc, ncol), lambda o, d: (0, o * 2 + d)),
            pl.BlockSpec((ml, ncol), lambda o, d: (0, o * 2 + d)),
            pl.BlockSpec((None, 1, ncol), lambda o, d: (o * 2 + d, 0, 0)),
        ],
        out_specs=[
            pl.BlockSpec((mc, ncol), lambda o, d: (0, o * 2 + d)),
            pl.BlockSpec((ml, ncol), lambda o, d: (0, o * 2 + d)),
        ],
        out_shape=[jax.ShapeDtypeStruct((mc, wid), F32), jax.ShapeDtypeStruct((ml, wid), F32)],
        compiler_params=_params(("arbitrary", "arbitrary")),
        name="s5scan",
    )(e_ctx, e_lat, dec)


def _s5y_kernel(u_ref, s_ref, k_ref, wy_ref, y_ref, *, tmr):
    res = jnp.dot(_block_rows(u_ref, tmr), k_ref[...], preferred_element_type=F32)
    res += jnp.dot(s_ref[...].astype(BF16), wy_ref[...], preferred_element_type=F32)
    for r in range(S5_BLOCK):
        y_ref[pl.ds(r, tmr, stride=S5_BLOCK), :] = res[:, r * LANE:(r + 1) * LANE]


def _s5y(u, s, kmat, wy, *, tmr=256):
    n, wid = u.shape
    m = n // S5_BLOCK
    n_oct, kdim, _ = kmat.shape
    sdim = wy.shape[1]
    tmr = _tile(m, tmr)
    return pl.pallas_call(
        functools.partial(_s5y_kernel, tmr=tmr),
        grid=(n_oct, m // tmr),
        in_specs=[
            pl.BlockSpec((tmr * S5_BLOCK, LANE), lambda o, i: (i, o)),
            pl.BlockSpec((tmr, sdim), lambda o, i: (i, o)),
            pl.BlockSpec((None, kdim, kdim), lambda o, i: (o, 0, 0)),
            pl.BlockSpec((None, sdim, kdim), lambda o, i: (o, 0, 0)),
        ],
        out_specs=pl.BlockSpec((tmr * S5_BLOCK, LANE), lambda o, i: (i, o)),
        out_shape=jax.ShapeDtypeStruct((n, wid), F32),
        compiler_params=_params(("arbitrary", "arbitrary")),
        name="s5y",
    )(u, s, kmat, wy)


def _hgrn_masks():
    c = CHUNK
    t = np.arange(c)
    masks = []
    half = c // 2
    while half >= SUB:
        par = 2 * half
        second = (t % par) >= half
        same_parent = (t[:, None] // par) == (t[None, :] // par)
        masks.append((same_parent & second[:, None] & (~second)[None, :]).astype(np.float32))
        half //= 2
    masks.append((((t[:, None] // SUB) == (t[None, :] // SUB)) & (t[None, :] <= t[:, None])).astype(np.float32))
    return np.stack(masks)


def _hgrn_decays(g, reverse):
    nb = CHUNK // SUB
    g3 = g.reshape(nb, SUB, HEAD_DIM)
    r = lax.broadcasted_iota(jnp.int32, g3.shape, 1)
    p = g3
    for k in (1, 2, 4):
        if reverse:
            p = p + jnp.where(r <= SUB - 1 - k, pltpu.roll(p, SUB - k, 1), 0.0)
        else:
            p = p + jnp.where(r >= k, pltpu.roll(p, k, 1), 0.0)
    last = 0 if reverse else SUB - 1
    ref = SUB // 2 if reverse else SUB // 2 - 1
    tot = jnp.broadcast_to(p[:, last:last + 1, :], p.shape)
    ep = jnp.exp(p)
    eq = jnp.exp(tot - p)
    ed = p - jnp.broadcast_to(p[:, ref:ref + 1, :], p.shape)
    edp = jnp.exp(ed)
    edn = jnp.exp(-ed)
    et = jnp.broadcast_to(ep[:, last:last + 1, :], p.shape)
    mem = (lambda i: nb - 1 - i) if reverse else (lambda i: i)
    epb = [ep[mem(i)] for i in range(nb)]
    eqb = [eq[mem(i)] for i in range(nb)]
    etb = [et[mem(i)] for i in range(nb)]

    def assemble(blocks):
        return jnp.concatenate([blocks[mem(i)] for i in range(nb)], axis=0)

    cq = [None] * nb
    acc = None
    for i in range(nb):
        cq[i] = epb[i] if acc is None else epb[i] * acc
        acc = etb[i] if acc is None else acc * etb[i]
    total = acc[0:1, :]
    ck = [None] * nb
    acc = None
    for i in range(nb - 1, -1, -1):
        ck[i] = eqb[i] if acc is None else eqb[i] * acc
        acc = etb[i] if acc is None else acc * etb[i]
    levels = []
    half = nb // 2
    while half >= 1:
        par = 2 * half
        blocks = []
        for i in range(nb):
            j = i % par
            if j >= half:
                f = epb[i]
                for m in range(i - j + half, i):
                    f = f * etb[m]
            else:
                f = eqb[i]
                for m in range(i + 1, i - j + half):
                    f = f * etb[m]
            blocks.append(f)
        levels.append(assemble(blocks))
        half //= 2
    return levels, edp.reshape(g.shape), edn.reshape(g.shape), assemble(cq), assemble(ck), total


def _hgrn_kernel(*refs, nsub, heads, reverse, final, n_lvl):
    if final:
        (q_ref, z_ref, v_ref, lb_ref, masks_ref, s0_ref, of_ref, gate_ref, hnw_ref,
         o_ref, sfin_ref, st_ref) = refs
    else:
        q_ref, z_ref, v_ref, lb_ref, masks_ref, s0_ref, o_ref, sfin_ref, st_ref = refs
    j = pl.program_id(1)
    nj = pl.num_programs(1)
    c = CHUNK
    hd = HEAD_DIM
    nt = (((1,), (1,)), ((), ()))
    tn = (((0,), (0,)), ((), ()))

    @pl.when(j == 0)
    def _():
        st_ref[...] = s0_ref[...]

    def chunk(s, _):
        cl = (nsub - 1 - s) if reverse else s
        rows = pl.ds(pl.multiple_of(cl * c, c), c)
        for h in range(heads):
            sl = slice(h * hd, (h + 1) * hd)
            z = z_ref[rows, sl]
            q = q_ref[rows, sl]
            v = v_ref[rows, sl].astype(BF16)
            lb = lb_ref[:, sl]
            f = lb + (1.0 - lb) * jax.nn.sigmoid(z)
            k = 1.0 - f
            g = jnp.log(jnp.maximum(f, F_MIN))
            levels, edp, edn, cq, ck, total = _hgrn_decays(g, reverse)
            a = jnp.zeros((c, c), F32)
            for lv in range(n_lvl):
                if lv < n_lvl - 1:
                    ql = (q * levels[lv]).astype(BF16)
                    kl = (k * levels[lv]).astype(BF16)
                else:
                    ql = (q * edp).astype(BF16)
                    kl = (k * edn).astype(BF16)
                sc = lax.dot_general(ql, kl, nt, preferred_element_type=F32)
                a += jnp.where(masks_ref[lv] > 0.0, sc, 0.0)
            st = st_ref[h]
            o_h = jnp.dot(a.astype(BF16), v, preferred_element_type=F32)
            o_h += lax.dot_general((q * cq).astype(BF16), st.astype(BF16), nt, preferred_element_type=F32)
            st_ref[h] = st * total + lax.dot_general(v, (k * ck).astype(BF16), tn, preferred_element_type=F32)
            if final:
                o_h = o_h + of_ref[rows, sl]
                ms = jnp.mean(o_h * o_h, axis=-1, keepdims=True)
                o_h = o_h * lax.rsqrt(ms + EPS) * hnw_ref[...] * _silu(gate_ref[rows, sl])
            o_ref[rows, sl] = o_h.astype(o_ref.dtype)
        return 0

    lax.fori_loop(0, nsub, chunk, 0)

    @pl.when(j == nj - 1)
    def _():
        sfin_ref[...] = st_ref[...]


def _hgrn(p, lb, masks, s0, *, direction, width, o_fwd=None, hnw=None, nsub=4):
    b, t, _ = p.shape
    heads = width // HEAD_DIM
    nsub = _tile(t // CHUNK, nsub, 1)
    rows = nsub * CHUNK
    nj = t // rows
    reverse = direction == 1
    final = o_fwd is not None
    n_lvl = masks.shape[0]
    blk = (lambda j: nj - 1 - j) if reverse else (lambda j: j)
    zcol = 2 if reverse else 1
    in_specs = [
        pl.BlockSpec((None, rows, width), lambda bi, j: (bi, blk(j), 0)),
        pl.BlockSpec((None, rows, width), lambda bi, j: (bi, blk(j), zcol)),
        pl.BlockSpec((None, rows, width), lambda bi, j: (bi, blk(j), 3)),
        pl.BlockSpec((1, width), lambda bi, j: (0, 0)),
        pl.BlockSpec(masks.shape, lambda bi, j: (0, 0, 0)),
        pl.BlockSpec((None, heads, HEAD_DIM, HEAD_DIM), lambda bi, j: (bi, 0, 0, 0)),
    ]
    args = [p, p, p, lb.reshape(1, width), masks, s0]
    if final:
        in_specs += [
            pl.BlockSpec((None, rows, width), lambda bi, j: (bi, blk(j), 0)),
            pl.BlockSpec((None, rows, width), lambda bi, j: (bi, blk(j), 4)),
            pl.BlockSpec((1, HEAD_DIM), lambda bi, j: (0, 0)),
        ]
        args += [o_fwd, p, hnw.reshape(1, HEAD_DIM)]
    return pl.pallas_call(
        functools.partial(_hgrn_kernel, nsub=nsub, heads=heads, reverse=reverse, final=final, n_lvl=n_lvl),
        grid=(b, nj),
        in_specs=in_specs,
        out_specs=[
            pl.BlockSpec((None, rows, width), lambda bi, j: (bi, blk(j), 0)),
            pl.BlockSpec((None, heads, HEAD_DIM, HEAD_DIM), lambda bi, j: (bi, 0, 0, 0)),
        ],
        out_shape=[
            jax.ShapeDtypeStruct((b, t, width), F32),
            jax.ShapeDtypeStruct((b, heads, HEAD_DIM, HEAD_DIM), F32),
        ],
        scratch_shapes=[pltpu.VMEM((heads, HEAD_DIM, HEAD_DIM), F32)],
        compiler_params=_params(("arbitrary", "arbitrary")),
        name="hgrn_bwd" if final else "hgrn_fwd",
    )(*args)


def _mixout_kernel(x_ref, y_ref, u_ref, hg_ref, m_ref, dsk_ref, wglu_ref, bglu_ref, wo_ref, o_ref,
                   *, s5w, rb, colmajor):
    yy = y_ref[...] + dsk_ref[...] * u_ref[...]
    g = _gelu_tanh(yy)
    zz = jnp.dot(g.astype(BF16), wglu_ref[...], preferred_element_type=F32) + bglu_ref[...]
    s5 = (g * jax.nn.sigmoid(zz)).astype(BF16)
    if colmajor:
        hg = jnp.concatenate([hg_ref[:, r, :] for r in range(rb)], axis=0).astype(BF16)
    else:
        hg = hg_ref[...].astype(BF16)
    acc = jnp.dot(s5, wo_ref[:s5w, :], preferred_element_type=F32)
    acc += jnp.dot(hg, wo_ref[s5w:, :], preferred_element_type=F32)
    o_ref[...] = x_ref[...] + m_ref[0, 5:6, :] * acc


def _mixout(x3d, y3d, u3d, hg, mods, d_skip, w_glu, b_glu, w_o, *, mod_of_batch, colmajor, rb=8):
    b, t, d = x3d.shape
    s5w = y3d.shape[-1]
    hw = hg.shape[-1]
    if colmajor:
        cols = GRID_W
        rows = t // cols
        rb = _tile(rows, rb, 8)
        tm = rb * cols
        hg_v = hg.reshape(b, cols, rows, hw)
        hg_spec = pl.BlockSpec((None, cols, rb, hw), lambda bi, i: (bi, 0, i, 0))
    else:
        rb = 1
        tm = _tile(t, 512)
        hg_v = hg
        hg_spec = pl.BlockSpec((None, tm, hw), lambda bi, i: (bi, i, 0))
    mod_fn = (lambda bi: bi) if mod_of_batch else (lambda bi: b)
    return pl.pallas_call(
        functools.partial(_mixout_kernel, s5w=s5w, rb=rb, colmajor=colmajor),
        grid=(b, t // tm),
        in_specs=[
            pl.BlockSpec((None, tm, d), lambda bi, i: (bi, i, 0)),
            pl.BlockSpec((None, tm, s5w), lambda bi, i: (bi, i, 0)),
            pl.BlockSpec((None, tm, s5w), lambda bi, i: (bi, i, 0)),
            hg_spec,
            pl.BlockSpec((1, N_MOD, d), lambda bi, i: (mod_fn(bi), 0, 0)),
            pl.BlockSpec((1, s5w), lambda bi, i: (0, 0)),
            pl.BlockSpec((s5w, s5w), lambda bi, i: (0, 0)),
            pl.BlockSpec((1, s5w), lambda bi, i: (0, 0)),
            pl.BlockSpec((s5w + hw, d), lambda bi, i: (0, 0)),
        ],
        out_specs=pl.BlockSpec((None, tm, d), lambda bi, i: (bi, i, 0)),
        out_shape=jax.ShapeDtypeStruct((b, t, d), F32),
        compiler_params=_params(("arbitrary", "arbitrary")),
        name="mixout",
    )(x3d, y3d, u3d, hg_v, mods, d_skip.reshape(1, s5w), w_glu, b_glu.reshape(1, s5w), w_o)


def _pad_cast(w, axis, mult):
    n = w.shape[axis]
    padn = (-n) % mult
    if padn:
        pad = [(0, 0)] * w.ndim
        pad[axis] = (0, padn)
        w = jnp.pad(w, pad)
    return w.astype(BF16)


def kernel(x, c, ctx, c_ctx, w_ada, b_ada, norm_w, ffn_w_gate, ffn_w_up, ffn_w_down, w_in, w_out,
           s5_lambda_re, s5_lambda_im, s5_log_step, s5_b_re, s5_b_im, s5_c_re, s5_c_im, s5_d,
           s5_w_glu, s5_b_glu, hgrn_lower_bounds, hgrn_norm_w, final_norm_w):
    batch, seq, d = x.shape
    n_ctx = ctx.shape[1]
    depth = w_ada.shape[0]
    s5w = s5_d.shape[-1]
    hw = hgrn_lower_bounds.shape[-1]
    rows = seq // GRID_W
    assert batch < MOD_ROWS and rows % CHUNK == 0 and n_ctx % CHUNK == 0
    assert seq % S5_BLOCK == 0 and n_ctx % S5_BLOCK == 0 and (s5w // S5_GROUP) % S5_OCT == 0

    cvec = jnp.zeros((MOD_ROWS, d), F32).at[:batch].set(c.astype(F32)).at[batch].set(c_ctx.astype(F32))
    mods_all = _ada(cvec, w_ada, b_ada).reshape(depth, MOD_ROWS, N_MOD, d)

    lb_soft = jax.nn.softmax(hgrn_lower_bounds.astype(F32), axis=0)
    lb_all = jnp.cumsum(lb_soft, axis=0) - lb_soft[0]

    masks_np = _hgrn_masks()
    masks_f = jnp.asarray(masks_np, F32)
    masks_b = jnp.asarray(masks_np[:, ::-1, ::-1].copy(), F32)
    heads = hw // HEAD_DIM
    s_zero = jnp.zeros((batch, heads, HEAD_DIM, HEAD_DIM), F32)

    xl = x.astype(F32)
    xc = ctx.astype(F32)
    for l in range(depth):
        last = l == depth - 1
        mods = mods_all[l]
        wg = [_pad_cast(ffn_w_gate[l, i], 1, 512) for i in range(2)]
        wu = [_pad_cast(ffn_w_up[l, i], 1, 512) for i in range(2)]
        wd = [_pad_cast(ffn_w_down[l, i], 0, 512) for i in range(2)]
        w_u = w_in[l][:, :s5w].astype(BF16)
        w_h = w_in[l][:, s5w:].astype(BF16)
        w_o = w_out[l].astype(BF16)
        w_glu = s5_w_glu[l].astype(BF16)

        xl = _ffn(xl.reshape(batch * seq, d), mods, norm_w[l, 0], wg[0], wu[0], wd[0],
                  base=0, rows_per_mod=seq, mod_off=0).reshape(batch, seq, d)
        xc = _ffn(xc.reshape(batch * n_ctx, d), mods, norm_w[l, 0], wg[0], wu[0], wd[0],
                  base=0, rows_per_mod=batch * n_ctx, mod_off=batch).reshape(batch, n_ctx, d)

        u_l = _uproj(xl.reshape(batch * seq, d), mods, norm_w[l, 1], w_u, rows_per_mod=seq, mod_off=0)
        u_c = _uproj(xc.reshape(batch * n_ctx, d), mods, norm_w[l, 1], w_u,
                     rows_per_mod=batch * n_ctx, mod_off=batch)
        p_lat = _hproj(xl, mods, norm_w[l, 1], w_h, cols=GRID_W, mod_of_batch=True, cb=16, tn=hw)
        p_ctx = _hproj(xc, mods, norm_w[l, 1], w_h, cols=1, mod_of_batch=False, cb=1, tn=hw)

        kmat, we, wy, dec = _s5_prep(s5_lambda_re[l], s5_lambda_im[l], s5_log_step[l], s5_b_re[l],
                                     s5_b_im[l], s5_c_re[l], s5_c_im[l])
        e_l = _s5e(u_l, we)
        e_c = _s5e(u_c, we)
        st_c, st_l = _s5scan(e_c, e_l, dec, nb=batch)
        y_l = _s5y(u_l, st_l, kmat, wy).reshape(batch, seq, s5w)

        lb_f = lb_all[l, 0]
        lb_b = lb_all[l, 1]
        oc_f, sc_f = _hgrn(p_ctx, lb_f, masks_f, s_zero, direction=0, width=hw)
        ol_f, _ = _hgrn(p_lat, lb_f, masks_f, sc_f, direction=0, width=hw)
        hg_c, sc_b = _hgrn(p_ctx, lb_b, masks_b, s_zero, direction=1, width=hw,
                           o_fwd=oc_f, hnw=hgrn_norm_w[l])
        hg_l, _ = _hgrn(p_lat, lb_b, masks_b, sc_b, direction=1, width=hw,
                        o_fwd=ol_f, hnw=hgrn_norm_w[l])

        xl = _mixout(xl, y_l, u_l.reshape(batch, seq, s5w), hg_l, mods, s5_d[l], w_glu, s5_b_glu[l], w_o,
                     mod_of_batch=True, colmajor=True)
        if not last:
            y_c = _s5y(u_c, st_c, kmat, wy).reshape(batch, n_ctx, s5w)
            xc = _mixout(xc, y_c, u_c.reshape(batch, n_ctx, s5w), hg_c, mods, s5_d[l], w_glu, s5_b_glu[l],
                         w_o, mod_of_batch=False, colmajor=False)

        xl = _ffn(xl.reshape(batch * seq, d), mods, norm_w[l, 2], wg[1], wu[1], wd[1],
                  base=6, rows_per_mod=seq, mod_off=0,
                  final_nw=final_norm_w if last else None).reshape(batch, seq, d)
        if not last:
            xc = _ffn(xc.reshape(batch * n_ctx, d), mods, norm_w[l, 2], wg[1], wu[1], wd[1],
                      base=6, rows_per_mod=batch * n_ctx, mod_off=batch).reshape(batch, n_ctx, d)
    return xl
```

```python
import functools

import numpy as np
import jax
import jax.numpy as jnp
from jax import lax
from jax.experimental import pallas as pl
from jax.experimental.pallas import tpu as pltpu

F32 = jnp.float32
BF16 = jnp.bfloat16

EPS = 1e-6
F_MIN = 1e-6
LAMBDA_RE_MAX = -1e-4
GRID_W = 64
N_MOD = 9
S5_GROUP = 16
S5_STATE = 64
S5_BLOCK = 8
S5_OCT = 8
HEAD_DIM = 128
CHUNK = 64
SUB = 8
LANE = 128
MOD_ROWS = 8
VMEM_LIMIT = 56 * 1024 * 1024


def _params(sem):
    return pltpu.CompilerParams(dimension_semantics=sem, vmem_limit_bytes=VMEM_LIMIT)


def _tile(n, pref, mult=8):
    t = min(n, pref)
    while t > 0:
        if n % t == 0 and t % mult == 0:
            return t
        t -= 1
    return n


def _norm_mod(x, nw, shift, scale):
    ms = jnp.mean(x * x, axis=-1, keepdims=True)
    y = x * lax.rsqrt(ms + EPS) * nw
    return y * (1.0 + scale) + shift


def _silu(x):
    return x * jax.nn.sigmoid(x)


def _gelu_tanh(x):
    return 0.5 * x * (1.0 + jnp.tanh(0.7978845608028654 * (x + 0.044715 * (x * x * x))))


def _ada_kernel(c_ref, w_ref, b_ref, o_ref):
    a = _silu(c_ref[...])
    a_hi = a.astype(BF16)
    a_lo = (a - a_hi.astype(F32)).astype(BF16)
    w = w_ref[...]
    w_hi = w.astype(BF16)
    w_lo = (w - w_hi.astype(F32)).astype(BF16)
    acc = jnp.dot(a_hi, w_hi, preferred_element_type=F32)
    acc += jnp.dot(a_lo, w_hi, preferred_element_type=F32)
    acc += jnp.dot(a_hi, w_lo, preferred_element_type=F32)
    o_ref[...] = acc + b_ref[...]


def _ada(cvec, w_ada, b_ada):
    depth, d, n = w_ada.shape
    tn = _tile(n, 1024, LANE)
    return pl.pallas_call(
        _ada_kernel,
        grid=(depth, n // tn),
        in_specs=[
            pl.BlockSpec((MOD_ROWS, d), lambda l, j: (0, 0)),
            pl.BlockSpec((None, d, tn), lambda l, j: (l, 0, j)),
            pl.BlockSpec((None, 1, tn), lambda l, j: (l, 0, j)),
        ],
        out_specs=pl.BlockSpec((None, MOD_ROWS, tn), lambda l, j: (l, 0, j)),
        out_shape=jax.ShapeDtypeStruct((depth, MOD_ROWS, n), F32),
        compiler_params=_params(("arbitrary", "arbitrary")),
        name="ada",
    )(cvec, w_ada, b_ada.reshape(depth, 1, n))


def _ffn_kernel(*refs, base, n_f, final):
    if final:
        x_ref, m_ref, mn_ref, nw_ref, wg_ref, wu_ref, wd_ref, fnw_ref, o_ref, h_ref, xk_ref, acc_ref = refs
    else:
        x_ref, m_ref, mn_ref, nw_ref, wg_ref, wu_ref, wd_ref, o_ref, h_ref, xk_ref, acc_ref = refs
    i = pl.program_id(0)
    j = pl.program_id(1)
    cur = i % 2

    def prep(mod_ref, slot):
        x = x_ref[...]
        h = _norm_mod(x, nw_ref[...], mod_ref[0, base:base + 1, :], mod_ref[0, base + 1:base + 2, :])
        h_ref[slot] = h.astype(BF16)
        xk_ref[...] = x

    def partial_sum():
        h = h_ref[cur]
        g = jnp.dot(h, wg_ref[...], preferred_element_type=F32)
        u = jnp.dot(h, wu_ref[...], preferred_element_type=F32)
        a = (_silu(g) * u).astype(BF16)
        return jnp.dot(a, wd_ref[...], preferred_element_type=F32)

    @pl.when((i == 0) & (j == 0))
    def _():
        prep(m_ref, 0)

    @pl.when(j == 0)
    def _():
        acc_ref[...] = jnp.zeros_like(acc_ref)

    @pl.when(j < n_f - 1)
    def _():
        acc_ref[...] += partial_sum()

    @pl.when(j == n_f - 1)
    def _():
        y = xk_ref[...] + (0.5 * m_ref[0, base + 2:base + 3, :]) * (acc_ref[...] + partial_sum())
        if final:
            ms = jnp.mean(y * y, axis=-1, keepdims=True)
            y = y * lax.rsqrt(ms + EPS) * fnw_ref[...]
        o_ref[...] = y
        prep(mn_ref, 1 - cur)


def _ffn(x2d, mods, nw, wg, wu, wd, *, layer, half, base, rows_per_mod, mod_off, final_nw=None,
         tm=512, tf=512):
    n, d = x2d.shape
    fp = wg.shape[-1]
    tm = _tile(n, tm)
    tf = _tile(fp, tf, LANE)
    n_f = fp // tf
    n_t = n // tm
    final = final_nw is not None

    def nxt(i, j):
        return jnp.minimum(i + (j == n_f - 1).astype(jnp.int32), n_t - 1)

    in_specs = [
        pl.BlockSpec((tm, d), lambda i, j: (nxt(i, j), 0)),
        pl.BlockSpec((1, N_MOD, d), lambda i, j: ((i * tm) // rows_per_mod + mod_off, 0, 0)),
        pl.BlockSpec((1, N_MOD, d),
                     lambda i, j: ((jnp.minimum(i + 1, n_t - 1) * tm) // rows_per_mod + mod_off, 0, 0)),
        pl.BlockSpec((1, d), lambda i, j: (0, 0)),
        pl.BlockSpec((None, None, d, tf), lambda i, j: (layer, half, 0, j)),
        pl.BlockSpec((None, None, d, tf), lambda i, j: (layer, half, 0, j)),
        pl.BlockSpec((None, None, tf, d), lambda i, j: (layer, half, j, 0)),
    ]
    args = [x2d, mods, mods, nw.reshape(1, d), wg, wu, wd]
    if final:
        in_specs.append(pl.BlockSpec((1, d), lambda i, j: (0, 0)))
        args.append(final_nw.reshape(1, d))
    return pl.pallas_call(
        functools.partial(_ffn_kernel, base=base, n_f=n_f, final=final),
        grid=(n_t, n_f),
        in_specs=in_specs,
        out_specs=pl.BlockSpec((tm, d), lambda i, j: (i, 0)),
        out_shape=jax.ShapeDtypeStruct((n, d), F32),
        scratch_shapes=[pltpu.VMEM((2, tm, d), BF16), pltpu.VMEM((tm, d), F32), pltpu.VMEM((tm, d), F32)],
        compiler_params=_params(("arbitrary", "arbitrary")),
        name="ffn_final" if final else "ffn",
    )(*args)


def _uproj_kernel(x_ref, m_ref, nw_ref, w_ref, u_ref):
    h = _norm_mod(x_ref[...], nw_ref[...], m_ref[0, 3:4, :], m_ref[0, 4:5, :])
    u_ref[...] = jnp.dot(h.astype(BF16), w_ref[...], preferred_element_type=F32)


def _uproj(x2d, mods, nw, w_in, *, layer, s5w, rows_per_mod, mod_off, tm=512):
    n, d = x2d.shape
    tm = _tile(n, tm)
    return pl.pallas_call(
        _uproj_kernel,
        grid=(n // tm,),
        in_specs=[
            pl.BlockSpec((tm, d), lambda i: (i, 0)),
            pl.BlockSpec((1, N_MOD, d), lambda i: ((i * tm) // rows_per_mod + mod_off, 0, 0)),
            pl.BlockSpec((1, d), lambda i: (0, 0)),
            pl.BlockSpec((None, d, s5w), lambda i: (layer, 0, 0)),
        ],
        out_specs=pl.BlockSpec((tm, s5w), lambda i: (i, 0)),
        out_shape=jax.ShapeDtypeStruct((n, s5w), F32),
        compiler_params=_params(("arbitrary",)),
        name="uproj",
    )(x2d, mods, nw.reshape(1, d), w_in)


def _hproj_kernel(*refs, colmajor):
    if colmajor:
        x_ref, m_ref, nw_ref, w_ref, perm_ref, o_ref, h_ref = refs
    else:
        x_ref, m_ref, nw_ref, w_ref, o_ref, h_ref = refs
    n = pl.program_id(2)

    @pl.when(n == 0)
    def _():
        h = _norm_mod(x_ref[...], nw_ref[...], m_ref[0, 3:4, :], m_ref[0, 4:5, :]).astype(BF16)
        if colmajor:
            h = h.reshape(h_ref.shape)
            h = jnp.dot(perm_ref[...], h, preferred_element_type=F32).astype(BF16)
        h_ref[...] = h

    p = jnp.dot(h_ref[...], w_ref[...], preferred_element_type=F32)

    @pl.when(n == 0)
    def _():
        o_ref[...] = _silu(p)

    @pl.when(n != 0)
    def _():
        o_ref[...] = p


def _hproj(x3d, mods, nw, w_in, *, layer, s5w, cols, mod_of_batch, cb, tn):
    b, t, d = x3d.shape
    rows = t // cols
    nh = w_in.shape[-1] - s5w
    assert s5w % tn == 0
    col0 = s5w // tn
    colmajor = cols > 1
    cb = _tile(cols, cb, 8) if colmajor else 1
    mod_fn = (lambda bi: bi) if mod_of_batch else (lambda bi: b)
    in_specs = [
        None,
        pl.BlockSpec((1, N_MOD, d), lambda bi, ci, n: (mod_fn(bi), 0, 0)),
        pl.BlockSpec((1, d), lambda bi, ci, n: (0, 0)),
        pl.BlockSpec((None, d, tn), lambda bi, ci, n: (layer, 0, col0 + n)),
    ]
    args = [None, mods, nw.reshape(1, d), w_in]
    if colmajor:
        args[0] = x3d.reshape(b, rows, cols, d)
        in_specs[0] = pl.BlockSpec((None, rows, cb, d), lambda bi, ci, n: (bi, 0, ci, 0))
        src = np.arange(rows * cb).reshape(rows, cb).T.reshape(-1)
        perm = np.zeros((rows * cb, rows * cb), np.float32)
        perm[np.arange(rows * cb), src] = 1.0
        args.append(jnp.asarray(perm, BF16))
        in_specs.append(pl.BlockSpec(perm.shape, lambda bi, ci, n: (0, 0)))
    else:
        args[0] = x3d
        in_specs[0] = pl.BlockSpec((None, rows, d), lambda bi, ci, n: (bi, 0, 0))
    return pl.pallas_call(
        functools.partial(_hproj_kernel, colmajor=colmajor),
        grid=(b, cols // cb, nh // tn),
        in_specs=in_specs,
        out_specs=pl.BlockSpec((None, cb * rows, tn), lambda bi, ci, n: (bi, ci, n)),
        out_shape=jax.ShapeDtypeStruct((b, t, nh), F32),
        scratch_shapes=[pltpu.VMEM((cb * rows, d), BF16)],
        compiler_params=_params(("arbitrary", "arbitrary", "arbitrary")),
        name="hproj",
    )(*args)


def _s5_sel_consts():
    t, h, p, o8 = S5_BLOCK, S5_GROUP, S5_STATE, S5_OCT
    colsel = np.zeros((t, 2, t, h, t, o8, h), np.float32)
    for r in range(t):
        for r2 in range(t):
            for hh in range(h):
                if r2 >= r:
                    colsel[r, 0, r2 - r, hh, r2, :, hh] = 1.0
                if r >= r2:
                    colsel[r, 1, r - r2, hh, r2, :, hh] = 1.0
    colsel = colsel.reshape(t, 2 * t * h, t * o8 * h)
    tile_e = np.zeros((4, p, 4, o8, p), np.float32)
    tile_y = np.zeros((4, h, 4, o8, h), np.float32)
    for dp in range(4):
        for i in range(p):
            tile_e[dp, i, dp, :, i] = 1.0
        for i in range(h):
            tile_y[dp, i, dp, :, i] = 1.0
    return colsel, tile_e.reshape(4 * p, 4 * o8 * p), tile_y.reshape(4 * h, 4 * o8 * h)


def _s5w_kernel(uk_ref, pb_ref, ca_ref, colsel_ref, tile_e_ref, tile_y_ref, k_ref, we_ref, wy_ref):
    t, h, p, o8 = S5_BLOCK, S5_GROUP, S5_STATE, S5_OCT
    gh = o8 * h
    gp = o8 * p

    def diag_mask(shape, row_div, col_mod, col_div):
        rg = lax.broadcasted_iota(jnp.int32, shape, 0) // row_div
        cg = (lax.broadcasted_iota(jnp.int32, shape, 1) % col_mod) // col_div
        return rg == cg

    mk = diag_mask((gh, t * gh), h, gh, h)
    me = diag_mask((gh, 4 * gp), h, gp, p)
    my = diag_mask((gp, 4 * gh), p, gh, h)
    uk = uk_ref[...].astype(BF16)
    for r in range(t):
        blk = jnp.dot(uk, colsel_ref[r], preferred_element_type=F32)
        k_ref[r * gh:(r + 1) * gh, :] = jnp.where(mk, blk, 0.0).astype(BF16)
        blk = jnp.dot(pb_ref[r].astype(BF16), tile_e_ref[...], preferred_element_type=F32)
        we_ref[r * gh:(r + 1) * gh, :] = jnp.where(me, blk, 0.0).astype(BF16)
        blk = jnp.where(my, jnp.dot(ca_ref[r].astype(BF16), tile_y_ref[...], preferred_element_type=F32), 0.0)
        for dp in range(4):
            wy_ref[dp * gp:(dp + 1) * gp, r * gh:(r + 1) * gh] = blk[:, dp * gh:(dp + 1) * gh].astype(BF16)


def _s5_prep(lam_re, lam_im, log_step, b_re, b_im, c_re, c_im):
    hp = lax.Precision.HIGHEST
    t = S5_BLOCK
    g, p = lam_re.shape[1:]
    h = b_re.shape[-1]
    n_oct = g // S5_OCT
    lam_re = jnp.minimum(lam_re.astype(F32), LAMBDA_RE_MAX)
    lam_im = lam_im.astype(F32)
    dt = jnp.exp(log_step.astype(F32))[..., None]
    mag = jnp.exp(lam_re * dt)
    lb_re = mag * jnp.cos(lam_im * dt)
    lb_im = mag * jnp.sin(lam_im * dt)
    den = lam_re * lam_re + lam_im * lam_im
    nr = lb_re - 1.0
    ni = lb_im
    cf_re = (nr * lam_re + ni * lam_im) / den
    cf_im = (ni * lam_re - nr * lam_im) / den
    b_re = b_re.astype(F32)
    b_im = b_im.astype(F32)
    br = cf_re[..., None] * b_re - cf_im[..., None] * b_im
    bi = cf_re[..., None] * b_im + cf_im[..., None] * b_re
    cr = c_re.astype(F32)
    ci = c_im.astype(F32)
    j = jnp.arange(t + 1, dtype=F32)[None, :, None, None]
    pmag = jnp.exp(j * (lam_re * dt)[:, None])
    pw_re = pmag * jnp.cos(j * (lam_im * dt)[:, None])
    pw_im = pmag * jnp.sin(j * (lam_im * dt)[:, None])
    pb_re = pw_re[..., None] * br[:, None] - pw_im[..., None] * bi[:, None]
    pb_im = pw_re[..., None] * bi[:, None] + pw_im[..., None] * br[:, None]
    kj = (jnp.einsum('dgep,djgph->djgeh', cr, pb_re[:, :t], precision=hp)
          - jnp.einsum('dgep,djgph->djgeh', ci, pb_im[:, :t], precision=hp))
    kj = kj.at[0, 0].add(kj[1, 0]).at[1, 0].set(0.0)
    uk = jnp.transpose(kj, (2, 4, 0, 1, 3)).reshape(n_oct, S5_OCT * h, 2 * t * h)
    r = np.arange(t)
    pb4 = jnp.stack([pb_re[0][t - 1 - r], pb_im[0][t - 1 - r], pb_re[1][r], pb_im[1][r]])
    pb3 = jnp.transpose(pb4.reshape(4, t, n_oct, S5_OCT, p, h), (2, 1, 3, 5, 0, 4))
    pb3 = pb3.reshape(n_oct, t, S5_OCT * h, 4 * p)

    def c_lam(d_, idx):
        pr = pw_re[d_][idx][:, :, None, :]
        pi = pw_im[d_][idx][:, :, None, :]
        return cr[d_][None] * pr - ci[d_][None] * pi, -(cr[d_][None] * pi + ci[d_][None] * pr)

    ca4 = jnp.stack(c_lam(0, r + 1) + c_lam(1, t - r))
    ca3 = jnp.transpose(ca4.reshape(4, t, n_oct, S5_OCT, h, p), (2, 1, 3, 5, 0, 4))
    ca3 = ca3.reshape(n_oct, t, S5_OCT * p, 4 * h)
    colsel, tile_e, tile_y = (jnp.asarray(a, BF16) for a in _s5_sel_consts())
    kdim = t * S5_OCT * h
    sdim = 4 * S5_OCT * p
    kmat, we, wy = pl.pallas_call(
        _s5w_kernel,
        grid=(n_oct,),
        in_specs=[
            pl.BlockSpec((None,) + uk.shape[1:], lambda o: (o, 0, 0)),
            pl.BlockSpec((None,) + pb3.shape[1:], lambda o: (o, 0, 0, 0)),
            pl.BlockSpec((None,) + ca3.shape[1:], lambda o: (o, 0, 0, 0)),
            pl.BlockSpec(colsel.shape, lambda o: (0, 0, 0)),
            pl.BlockSpec(tile_e.shape, lambda o: (0, 0)),
            pl.BlockSpec(tile_y.shape, lambda o: (0, 0)),
        ],
        out_specs=[
            pl.BlockSpec((None, kdim, kdim), lambda o: (o, 0, 0)),
            pl.BlockSpec((None, kdim, sdim), lambda o: (o, 0, 0)),
            pl.BlockSpec((None, sdim, kdim), lambda o: (o, 0, 0)),
        ],
        out_shape=[
            jax.ShapeDtypeStruct((n_oct, kdim, kdim), BF16),
            jax.ShapeDtypeStruct((n_oct, kdim, sdim), BF16),
            jax.ShapeDtypeStruct((n_oct, sdim, kdim), BF16),
        ],
        compiler_params=_params(("arbitrary",)),
        name="s5w",
    )(uk, pb3, ca3, colsel, tile_e, tile_y)
    dec_re = pw_re[:, t].reshape(2, n_oct, S5_OCT * p)
    dec_im = pw_im[:, t].reshape(2, n_oct, S5_OCT * p)
    dec = jnp.concatenate([dec_re, dec_im], axis=-1)
    dec = jnp.transpose(dec, (1, 0, 2)).reshape(n_oct * 2, 1, 2 * S5_OCT * p)
    return kmat, we, wy, dec


def _block_rows(u_ref, tmr):
    return jnp.concatenate(
        [u_ref[pl.ds(r, tmr, stride=S5_BLOCK), :].astype(BF16) for r in range(S5_BLOCK)], axis=1)


def _s5e_kernel(u_ref, we_ref, e_ref, *, tmr):
    e_ref[...] = jnp.dot(_block_rows(u_ref, tmr), we_ref[...], preferred_element_type=F32)


def _s5e(u, we, *, tmr=256):
    n, wid = u.shape
    m = n // S5_BLOCK
    n_oct, kdim, ncol = we.shape
    tmr = _tile(m, tmr)
    return pl.pallas_call(
        functools.partial(_s5e_kernel, tmr=tmr),
        grid=(n_oct, m // tmr),
        in_specs=[
            pl.BlockSpec((tmr * S5_BLOCK, LANE), lambda o, i: (i, o)),
            pl.BlockSpec((None, kdim, ncol), lambda o, i: (o, 0, 0)),
        ],
        out_specs=pl.BlockSpec((tmr, ncol), lambda o, i: (i, o)),
        out_shape=jax.ShapeDtypeStruct((m, n_oct * ncol), F32),
        compiler_params=_params(("arbitrary", "arbitrary")),
        name="s5e",
    )(u, we)


def _s5scan_kernel(ec_ref, el_ref, a_ref, sc_ref, sl_ref, *, nb, n_c, n_l, half):
    rev = pl.program_id(1) == 1
    a_re = a_ref[:, :half]
    a_im = a_ref[:, half:]

    def run(e_ref, s_ref, n, carry):
        def body(s, carry):
            i = jnp.where(rev, n - 1 - s, s)
            new = []
            for b in range(nb):
                x_re, x_im = carry[b]
                row = b * n + i
                s_ref[pl.ds(row, 1), :half] = x_re
                s_ref[pl.ds(row, 1), half:] = x_im
                e_re = e_ref[pl.ds(row, 1), :half]
                e_im = e_ref[pl.ds(row, 1), half:]
                new.append((a_re * x_re - a_im * x_im + e_re, a_re * x_im + a_im * x_re + e_im))
            return tuple(new)
        return lax.fori_loop(0, n, body, carry)

    zero = jnp.zeros((1, half), F32)
    carry = tuple((zero, zero) for _ in range(nb))
    carry = run(ec_ref, sc_ref, n_c, carry)
    run(el_ref, sl_ref, n_l, carry)


def _s5scan(e_ctx, e_lat, dec, *, nb):
    mc, wid = e_ctx.shape
    ml = e_lat.shape[0]
    ncol = dec.shape[-1]
    nblk = wid // ncol
    return pl.pallas_call(
        functools.partial(_s5scan_kernel, nb=nb, n_c=mc // nb, n_l=ml // nb, half=ncol // 2),
        grid=(nblk // 2, 2),
        in_specs=[
            pl.BlockSpec((mc, ncol), lambda o, d: (0, o * 2 + d)),
            pl.BlockSpec((ml, ncol), lambda o, d: (0, o * 2 + d)),
            pl.BlockSpec((None, 1, ncol), lambda o, d: (o * 2 + d, 0, 0)),
        ],
        out_specs=[
            pl.BlockSpec((mc, ncol), lambda o, d: (0, o * 2 + d)),
            pl.BlockSpec((ml, ncol), lambda o, d: (0, o * 2 + d)),
        ],
        out_shape=[jax.ShapeDtypeStruct((mc, wid), F32), jax.ShapeDtypeStruct((ml, wid), F32)],
        compiler_params=_params(("arbitrary", "arbitrary")),
        name="s5scan",
    )(e_ctx, e_lat, dec)


def _s5y_kernel(u_ref, s_ref, k_ref, wy_ref, y_ref, *, tmr):
    res = jnp.dot(_block_rows(u_ref, tmr), k_ref[...], preferred_element_type=F32)
    res += jnp.dot(s_ref[...].astype(BF16), wy_ref[...], preferred_element_type=F32)
    for r in range(S5_BLOCK):
        y_ref[pl.ds(r, tmr, stride=S5_BLOCK), :] = res[:, r * LANE:(r + 1) * LANE]


def _s5y(u, s, kmat, wy, *, tmr=256):
    n, wid = u.shape
    m = n // S5_BLOCK
    n_oct, kdim, _ = kmat.shape
    sdim = wy.shape[1]
    tmr = _tile(m, tmr)
    return pl.pallas_call(
        functools.partial(_s5y_kernel, tmr=tmr),
        grid=(n_oct, m // tmr),
        in_specs=[
            pl.BlockSpec((tmr * S5_BLOCK, LANE), lambda o, i: (i, o)),
            pl.BlockSpec((tmr, sdim), lambda o, i: (i, o)),
            pl.BlockSpec((None, kdim, kdim), lambda o, i: (o, 0, 0)),
            pl.BlockSpec((None, sdim, kdim), lambda o, i: (o, 0, 0)),
        ],
        out_specs=pl.BlockSpec((tmr * S5_BLOCK, LANE), lambda o, i: (i, o)),
        out_shape=jax.ShapeDtypeStruct((n, wid), F32),
        compiler_params=_params(("arbitrary", "arbitrary")),
        name="s5y",
    )(u, s, kmat, wy)


def _hgrn_masks():
    c = CHUNK
    t = np.arange(c)
    masks = []
    half = c // 2
    while half >= SUB:
        par = 2 * half
        second = (t % par) >= half
        same_parent = (t[:, None] // par) == (t[None, :] // par)
        masks.append((same_parent & second[:, None] & (~second)[None, :]).astype(np.float32))
        half //= 2
    masks.append((((t[:, None] // SUB) == (t[None, :] // SUB)) & (t[None, :] <= t[:, None])).astype(np.float32))
    return np.stack(masks)


def _hgrn_decays(g, reverse):
    nb = CHUNK // SUB
    g3 = g.reshape(nb, SUB, HEAD_DIM)
    r = lax.broadcasted_iota(jnp.int32, g3.shape, 1)
    p = g3
    for k in (1, 2, 4):
        if reverse:
            p = p + jnp.where(r <= SUB - 1 - k, pltpu.roll(p, SUB - k, 1), 0.0)
        else:
            p = p + jnp.where(r >= k, pltpu.roll(p, k, 1), 0.0)
    last = 0 if reverse else SUB - 1
    ref = SUB // 2 if reverse else SUB // 2 - 1
    tot = jnp.broadcast_to(p[:, last:last + 1, :], p.shape)
    ep = jnp.exp(p)
    eq = jnp.exp(tot - p)
    ed = p - jnp.broadcast_to(p[:, ref:ref + 1, :], p.shape)
    edp = jnp.exp(ed)
    edn = jnp.exp(-ed)
    et = jnp.broadcast_to(ep[:, last:last + 1, :], p.shape)
    mem = (lambda i: nb - 1 - i) if reverse else (lambda i: i)
    epb = [ep[mem(i)] for i in range(nb)]
    eqb = [eq[mem(i)] for i in range(nb)]
    etb = [et[mem(i)] for i in range(nb)]

    def assemble(blocks):
        return jnp.concatenate([blocks[mem(i)] for i in range(nb)], axis=0)

    cq = [None] * nb
    acc = None
    for i in range(nb):
        cq[i] = epb[i] if acc is None else epb[i] * acc
        acc = etb[i] if acc is None else acc * etb[i]
    total = acc[0:1, :]
    ck = [None] * nb
    acc = None
    for i in range(nb - 1, -1, -1):
        ck[i] = eqb[i] if acc is None else eqb[i] * acc
        acc = etb[i] if acc is None else acc * etb[i]
    levels = []
    half = nb // 2
    while half >= 1:
        par = 2 * half
        blocks = []
        for i in range(nb):
            j = i % par
            if j >= half:
                f = epb[i]
                for m in range(i - j + half, i):
                    f = f * etb[m]
            else:
                f = eqb[i]
                for m in range(i + 1, i - j + half):
                    f = f * etb[m]
            blocks.append(f)
        levels.append(assemble(blocks))
        half //= 2
    return levels, edp.reshape(g.shape), edn.reshape(g.shape), assemble(cq), assemble(ck), total


def _hgrn_kernel(*refs, nsub, heads, reverse, final, n_lvl):
    if final:
        (q_ref, z_ref, v_ref, lb_ref, masks_ref, s0_ref, of_ref, gate_ref, hnw_ref,
         o_ref, sfin_ref, st_ref) = refs
    else:
        q_ref, z_ref, v_ref, lb_ref, masks_ref, s0_ref, o_ref, sfin_ref, st_ref = refs
    j = pl.program_id(1)
    nj = pl.num_programs(1)
    c = CHUNK
    hd = HEAD_DIM
    nt = (((1,), (1,)), ((), ()))
    tn = (((0,), (0,)), ((), ()))

    @pl.when(j == 0)
    def _():
        st_ref[...] = s0_ref[...]

    mbool = [masks_ref[lv] > 0.0 for lv in range(n_lvl)]

    def chunk(s, _):
        cl = (nsub - 1 - s) if reverse else s
        rows = pl.ds(pl.multiple_of(cl * c, c), c)
        for h in range(heads):
            sl = slice(h * hd, (h + 1) * hd)
            z = z_ref[rows, sl]
            q = q_ref[rows, sl]
            v = v_ref[rows, sl].astype(BF16)
            lb = lb_ref[:, sl]
            f = lb + (1.0 - lb) * jax.nn.sigmoid(z)
            k = 1.0 - f
            g = jnp.log(jnp.maximum(f, F_MIN))
            levels, edp, edn, cq, ck, total = _hgrn_decays(g, reverse)
            a = jnp.zeros((c, c), F32)
            for lv in range(n_lvl - 1, -1, -1):
                if lv < n_lvl - 1:
                    ql = (q * levels[lv]).astype(BF16)
                    kl = (k * levels[lv]).astype(BF16)
                else:
                    ql = (q * edp).astype(BF16)
                    kl = (k * edn).astype(BF16)
                sc = lax.dot_general(ql, kl, nt, preferred_element_type=F32)
                a = jnp.where(mbool[lv], sc, a)
            st = st_ref[h]
            o_h = jnp.dot(a.astype(BF16), v, preferred_element_type=F32)
            o_h += lax.dot_general((q * cq).astype(BF16), st.astype(BF16), nt, preferred_element_type=F32)
            st_ref[h] = st * total + lax.dot_general(v, (k * ck).astype(BF16), tn, preferred_element_type=F32)
            if final:
                o_h = o_h + of_ref[rows, sl]
                ms = jnp.mean(o_h * o_h, axis=-1, keepdims=True)
                o_h = o_h * lax.rsqrt(ms + EPS) * hnw_ref[...] * _silu(gate_ref[rows, sl])
            o_ref[rows, sl] = o_h.astype(o_ref.dtype)
        return 0

    lax.fori_loop(0, nsub, chunk, 0)

    @pl.when(j == nj - 1)
    def _():
        sfin_ref[...] = st_ref[...]


def _hgrn(p, lb, masks, s0, *, direction, width, o_fwd=None, hnw=None, nsub=4):
    b, t, _ = p.shape
    heads = width // HEAD_DIM
    nsub = _tile(t // CHUNK, nsub, 1)
    rows = nsub * CHUNK
    nj = t // rows
    reverse = direction == 1
    final = o_fwd is not None
    n_lvl = masks.shape[0]
    blk = (lambda j: nj - 1 - j) if reverse else (lambda j: j)
    zcol = 2 if reverse else 1
    in_specs = [
        pl.BlockSpec((None, rows, width), lambda bi, j: (bi, blk(j), 0)),
        pl.BlockSpec((None, rows, width), lambda bi, j: (bi, blk(j), zcol)),
        pl.BlockSpec((None, rows, width), lambda bi, j: (bi, blk(j), 3)),
        pl.BlockSpec((1, width), lambda bi, j: (0, 0)),
        pl.BlockSpec(masks.shape, lambda bi, j: (0, 0, 0)),
        pl.BlockSpec((None, heads, HEAD_DIM, HEAD_DIM), lambda bi, j: (bi, 0, 0, 0)),
    ]
    args = [p, p, p, lb.reshape(1, width), masks, s0]
    if final:
        in_specs += [
            pl.BlockSpec((None, rows, width), lambda bi, j: (bi, blk(j), 0)),
            pl.BlockSpec((None, rows, width), lambda bi, j: (bi, blk(j), 4)),
            pl.BlockSpec((1, HEAD_DIM), lambda bi, j: (0, 0)),
        ]
        args += [o_fwd, p, hnw.reshape(1, HEAD_DIM)]
    return pl.pallas_call(
        functools.partial(_hgrn_kernel, nsub=nsub, heads=heads, reverse=reverse, final=final, n_lvl=n_lvl),
        grid=(b, nj),
        in_specs=in_specs,
        out_specs=[
            pl.BlockSpec((None, rows, width), lambda bi, j: (bi, blk(j), 0)),
            pl.BlockSpec((None, heads, HEAD_DIM, HEAD_DIM), lambda bi, j: (bi, 0, 0, 0)),
        ],
        out_shape=[
            jax.ShapeDtypeStruct((b, t, width), F32),
            jax.ShapeDtypeStruct((b, heads, HEAD_DIM, HEAD_DIM), F32),
        ],
        scratch_shapes=[pltpu.VMEM((heads, HEAD_DIM, HEAD_DIM), F32)],
        compiler_params=_params(("arbitrary", "arbitrary")),
        name="hgrn_bwd" if final else "hgrn_fwd",
    )(*args)


def _mixout_kernel(x_ref, y_ref, u_ref, hg_ref, m_ref, dsk_ref, wglu_ref, bglu_ref, wo_ref, o_ref,
                   *, s5w, rb, colmajor):
    yy = y_ref[...] + dsk_ref[...] * u_ref[...]
    g = _gelu_tanh(yy)
    zz = jnp.dot(g.astype(BF16), wglu_ref[...], preferred_element_type=F32) + bglu_ref[...]
    s5 = (g * jax.nn.sigmoid(zz)).astype(BF16)
    if colmajor:
        hg = jnp.concatenate([hg_ref[:, r, :] for r in range(rb)], axis=0).astype(BF16)
    else:
        hg = hg_ref[...].astype(BF16)
    acc = jnp.dot(s5, wo_ref[:s5w, :], preferred_element_type=F32)
    acc += jnp.dot(hg, wo_ref[s5w:, :], preferred_element_type=F32)
    o_ref[...] = x_ref[...] + m_ref[0, 5:6, :] * acc


def _mixout(x3d, y3d, u3d, hg, mods, d_skip, w_glu, b_glu, w_o, *, layer, mod_of_batch, colmajor, rb=8):
    b, t, d = x3d.shape
    s5w = y3d.shape[-1]
    hw = hg.shape[-1]
    if colmajor:
        cols = GRID_W
        rows = t // cols
        rb = _tile(rows, rb, 8)
        tm = rb * cols
        hg_v = hg.reshape(b, cols, rows, hw)
        hg_spec = pl.BlockSpec((None, cols, rb, hw), lambda bi, i: (bi, 0, i, 0))
    else:
        rb = 1
        tm = _tile(t, 512)
        hg_v = hg
        hg_spec = pl.BlockSpec((None, tm, hw), lambda bi, i: (bi, i, 0))
    mod_fn = (lambda bi: bi) if mod_of_batch else (lambda bi: b)
    return pl.pallas_call(
        functools.partial(_mixout_kernel, s5w=s5w, rb=rb, colmajor=colmajor),
        grid=(b, t // tm),
        in_specs=[
            pl.BlockSpec((None, tm, d), lambda bi, i: (bi, i, 0)),
            pl.BlockSpec((None, tm, s5w), lambda bi, i: (bi, i, 0)),
            pl.BlockSpec((None, tm, s5w), lambda bi, i: (bi, i, 0)),
            hg_spec,
            pl.BlockSpec((1, N_MOD, d), lambda bi, i: (mod_fn(bi), 0, 0)),
            pl.BlockSpec((1, s5w), lambda bi, i: (0, 0)),
            pl.BlockSpec((None, s5w, s5w), lambda bi, i: (layer, 0, 0)),
            pl.BlockSpec((1, s5w), lambda bi, i: (0, 0)),
            pl.BlockSpec((None, s5w + hw, d), lambda bi, i: (layer, 0, 0)),
        ],
        out_specs=pl.BlockSpec((None, tm, d), lambda bi, i: (bi, i, 0)),
        out_shape=jax.ShapeDtypeStruct((b, t, d), F32),
        compiler_params=_params(("arbitrary", "arbitrary")),
        name="mixout",
    )(x3d, y3d, u3d, hg_v, mods, d_skip.reshape(1, s5w), w_glu, b_glu.reshape(1, s5w), w_o)


def kernel(x, c, ctx, c_ctx, w_ada, b_ada, norm_w, ffn_w_gate, ffn_w_up, ffn_w_down, w_in, w_out,
           s5_lambda_re, s5_lambda_im, s5_log_step, s5_b_re, s5_b_im, s5_c_re, s5_c_im, s5_d,
           s5_w_glu, s5_b_glu, hgrn_lower_bounds, hgrn_norm_w, final_norm_w):
    batch, seq, d = x.shape
    n_ctx = ctx.shape[1]
    depth = w_ada.shape[0]
    s5w = s5_d.shape[-1]
    hw = hgrn_lower_bounds.shape[-1]
    rows = seq // GRID_W
    assert batch < MOD_ROWS and rows % CHUNK == 0 and n_ctx % CHUNK == 0
    assert seq % S5_BLOCK == 0 and n_ctx % S5_BLOCK == 0 and (s5w // S5_GROUP) % S5_OCT == 0

    cvec = jnp.zeros((MOD_ROWS, d), F32).at[:batch].set(c.astype(F32)).at[batch].set(c_ctx.astype(F32))
    mods_all = _ada(cvec, w_ada, b_ada).reshape(depth, MOD_ROWS, N_MOD, d)

    lb_soft = jax.nn.softmax(hgrn_lower_bounds.astype(F32), axis=0)
    lb_all = jnp.cumsum(lb_soft, axis=0) - lb_soft[0]

    masks_np = _hgrn_masks()
    masks_f = jnp.asarray(masks_np, F32)
    masks_b = jnp.asarray(masks_np[:, ::-1, ::-1].copy(), F32)
    heads = hw // HEAD_DIM
    s_zero = jnp.zeros((batch, heads, HEAD_DIM, HEAD_DIM), F32)

    f_pad = (-ffn_w_gate.shape[-1]) % 512
    wg = jnp.pad(ffn_w_gate, ((0, 0), (0, 0), (0, 0), (0, f_pad))).astype(BF16)
    wu = jnp.pad(ffn_w_up, ((0, 0), (0, 0), (0, 0), (0, f_pad))).astype(BF16)
    wd = jnp.pad(ffn_w_down, ((0, 0), (0, 0), (0, f_pad), (0, 0))).astype(BF16)
    w_in_b = w_in.astype(BF16)
    w_o = w_out.astype(BF16)
    w_glu = s5_w_glu.astype(BF16)

    xl = x.astype(F32)
    xc = ctx.astype(F32)
    lat = dict(rows_per_mod=seq, mod_off=0)
    cx = dict(rows_per_mod=batch * n_ctx, mod_off=batch)
    for l in range(depth):
        last = l == depth - 1
        mods = mods_all[l]

        xl = _ffn(xl.reshape(batch * seq, d), mods, norm_w[l, 0], wg, wu, wd, layer=l, half=0, base=0,
                  **lat).reshape(batch, seq, d)
        xc = _ffn(xc.reshape(batch * n_ctx, d), mods, norm_w[l, 0], wg, wu, wd, layer=l, half=0, base=0,
                  **cx).reshape(batch, n_ctx, d)

        u_l = _uproj(xl.reshape(batch * seq, d), mods, norm_w[l, 1], w_in_b, layer=l, s5w=s5w, **lat)
        u_c = _uproj(xc.reshape(batch * n_ctx, d), mods, norm_w[l, 1], w_in_b, layer=l, s5w=s5w, **cx)
        p_lat = _hproj(xl, mods, norm_w[l, 1], w_in_b, layer=l, s5w=s5w, cols=GRID_W, mod_of_batch=True,
                       cb=16, tn=hw)
        p_ctx = _hproj(xc, mods, norm_w[l, 1], w_in_b, layer=l, s5w=s5w, cols=1, mod_of_batch=False,
                       cb=1, tn=hw)

        kmat, we, wy, dec = _s5_prep(s5_lambda_re[l], s5_lambda_im[l], s5_log_step[l], s5_b_re[l],
                                     s5_b_im[l], s5_c_re[l], s5_c_im[l])
        e_l = _s5e(u_l, we)
        e_c = _s5e(u_c, we)
        st_c, st_l = _s5scan(e_c, e_l, dec, nb=batch)
        y_l = _s5y(u_l, st_l, kmat, wy).reshape(batch, seq, s5w)

        lb_f = lb_all[l, 0]
        lb_b = lb_all[l, 1]
        oc_f, sc_f = _hgrn(p_ctx, lb_f, masks_f, s_zero, direction=0, width=hw)
        ol_f, _ = _hgrn(p_lat, lb_f, masks_f, sc_f, direction=0, width=hw)
        hg_c, sc_b = _hgrn(p_ctx, lb_b, masks_b, s_zero, direction=1, width=hw,
                           o_fwd=oc_f, hnw=hgrn_norm_w[l])
        hg_l, _ = _hgrn(p_lat, lb_b, masks_b, sc_b, direction=1, width=hw,
                        o_fwd=ol_f, hnw=hgrn_norm_w[l])

        xl = _mixout(xl, y_l, u_l.reshape(batch, seq, s5w), hg_l, mods, s5_d[l], w_glu, s5_b_glu[l], w_o,
                     layer=l, mod_of_batch=True, colmajor=True)
        if not last:
            y_c = _s5y(u_c, st_c, kmat, wy).reshape(batch, n_ctx, s5w)
            xc = _mixout(xc, y_c, u_c.reshape(batch, n_ctx, s5w), hg_c, mods, s5_d[l], w_glu, s5_b_glu[l],
                         w_o, layer=l, mod_of_batch=False, colmajor=False)

        xl = _ffn(xl.reshape(batch * seq, d), mods, norm_w[l, 2], wg, wu, wd, layer=l, half=1, base=6,
                  final_nw=final_norm_w if last else None, **lat).reshape(batch, seq, d)
        if not last:
            xc = _ffn(xc.reshape(batch * n_ctx, d), mods, norm_w[l, 2], wg, wu, wd, layer=l, half=1, base=6,
                      **cx).reshape(batch, n_ctx, d)
    return xl
```

```python
import functools

import numpy as np
import jax
import jax.numpy as jnp
from jax import lax
from jax.experimental import pallas as pl
from jax.experimental.pallas import tpu as pltpu

F32 = jnp.float32
BF16 = jnp.bfloat16

EPS = 1e-6
F_MIN = 1e-6
LAMBDA_RE_MAX = -1e-4
GRID_W = 64
N_MOD = 9
S5_GROUP = 16
S5_STATE = 64
S5_BLOCK = 8
S5_OCT = 8
HEAD_DIM = 128
CHUNK = 64
SUB = 8
LANE = 128
MOD_ROWS = 8
VMEM_LIMIT = 60 * 1024 * 1024


def _params(sem):
    return pltpu.CompilerParams(dimension_semantics=sem, vmem_limit_bytes=VMEM_LIMIT)


def _tile(n, pref, mult=8):
    t = min(n, pref)
    while t > 0:
        if n % t == 0 and t % mult == 0:
            return t
        t -= 1
    return n


def _norm_mod(x, nw, shift, scale):
    ms = jnp.mean(x * x, axis=-1, keepdims=True)
    y = x * lax.rsqrt(ms + EPS) * nw
    return y * (1.0 + scale) + shift


def _silu(x):
    return x * jax.nn.sigmoid(x)


def _gelu_tanh(x):
    return 0.5 * x * (1.0 + jnp.tanh(0.7978845608028654 * (x + 0.044715 * (x * x * x))))


def _ada_kernel(c_ref, w_ref, b_ref, o_ref):
    a = _silu(c_ref[...])
    a_hi = a.astype(BF16)
    a_lo = (a - a_hi.astype(F32)).astype(BF16)
    w = w_ref[...]
    w_hi = w.astype(BF16)
    w_lo = (w - w_hi.astype(F32)).astype(BF16)
    acc = jnp.dot(a_hi, w_hi, preferred_element_type=F32)
    acc += jnp.dot(a_lo, w_hi, preferred_element_type=F32)
    acc += jnp.dot(a_hi, w_lo, preferred_element_type=F32)
    o_ref[...] = acc + b_ref[...]


def _ada(cvec, w_ada, b_ada):
    depth, d, n = w_ada.shape
    tn = _tile(n, 1024, LANE)
    return pl.pallas_call(
        _ada_kernel,
        grid=(depth, n // tn),
        in_specs=[
            pl.BlockSpec((MOD_ROWS, d), lambda l, j: (0, 0)),
            pl.BlockSpec((None, d, tn), lambda l, j: (l, 0, j)),
            pl.BlockSpec((None, 1, tn), lambda l, j: (l, 0, j)),
        ],
        out_specs=pl.BlockSpec((None, MOD_ROWS, tn), lambda l, j: (l, 0, j)),
        out_shape=jax.ShapeDtypeStruct((depth, MOD_ROWS, n), F32),
        compiler_params=_params(("arbitrary", "arbitrary")),
        name="ada",
    )(cvec, w_ada, b_ada.reshape(depth, 1, n))


def _ffn_kernel(*refs, base, n_f, final):
    if final:
        x_ref, m_ref, nw_ref, wg_ref, wu_ref, wd_ref, fnw_ref, o_ref, h_ref = refs
    else:
        x_ref, m_ref, nw_ref, wg_ref, wu_ref, wd_ref, o_ref, h_ref = refs
    j = pl.program_id(1)

    @pl.when(j == 0)
    def _():
        h = _norm_mod(x_ref[...], nw_ref[...], m_ref[0, base:base + 1, :], m_ref[0, base + 1:base + 2, :])
        h_ref[...] = h.astype(BF16)

    def partial_sum():
        h = h_ref[...]
        g = jnp.dot(h, wg_ref[...], preferred_element_type=F32)
        u = jnp.dot(h, wu_ref[...], preferred_element_type=F32)
        a = (_silu(g) * u).astype(BF16)
        return jnp.dot(a, wd_ref[...], preferred_element_type=F32)

    @pl.when(j == 0)
    def _():
        o_ref[...] = partial_sum()

    @pl.when((j > 0) & (j < n_f - 1))
    def _():
        o_ref[...] += partial_sum()

    @pl.when(j == n_f - 1)
    def _():
        y = x_ref[...] + (0.5 * m_ref[0, base + 2:base + 3, :]) * (o_ref[...] + partial_sum())
        if final:
            ms = jnp.mean(y * y, axis=-1, keepdims=True)
            y = y * lax.rsqrt(ms + EPS) * fnw_ref[...]
        o_ref[...] = y


def _ffn(x2d, mods, nw, wg, wu, wd, *, layer, half, base, rows_per_mod, mod_off, final_nw=None,
         tm=1024, tf=512):
    n, d = x2d.shape
    fp = wg.shape[-1]
    tm = _tile(min(n, rows_per_mod), tm)
    tf = _tile(fp, tf, LANE)
    n_f = fp // tf
    final = final_nw is not None
    in_specs = [
        pl.BlockSpec((tm, d), lambda i, j: (i, 0)),
        pl.BlockSpec((1, N_MOD, d), lambda i, j: ((i * tm) // rows_per_mod + mod_off, 0, 0)),
        pl.BlockSpec((1, d), lambda i, j: (0, 0)),
        pl.BlockSpec((None, None, d, tf), lambda i, j: (layer, half, 0, j)),
        pl.BlockSpec((None, None, d, tf), lambda i, j: (layer, half, 0, j)),
        pl.BlockSpec((None, None, tf, d), lambda i, j: (layer, half, j, 0)),
    ]
    args = [x2d, mods, nw.reshape(1, d), wg, wu, wd]
    if final:
        in_specs.append(pl.BlockSpec((1, d), lambda i, j: (0, 0)))
        args.append(final_nw.reshape(1, d))
    return pl.pallas_call(
        functools.partial(_ffn_kernel, base=base, n_f=n_f, final=final),
        grid=(n // tm, n_f),
        in_specs=in_specs,
        out_specs=pl.BlockSpec((tm, d), lambda i, j: (i, 0)),
        out_shape=jax.ShapeDtypeStruct((n, d), F32),
        scratch_shapes=[pltpu.VMEM((tm, d), BF16)],
        compiler_params=_params(("arbitrary", "arbitrary")),
        name="ffn_final" if final else "ffn",
    )(*args)


def _uproj_kernel(x_ref, m_ref, nw_ref, w_ref, u_ref):
    h = _norm_mod(x_ref[...], nw_ref[...], m_ref[0, 3:4, :], m_ref[0, 4:5, :])
    u_ref[...] = jnp.dot(h.astype(BF16), w_ref[...], preferred_element_type=F32)


def _uproj(x2d, mods, nw, w_in, *, layer, s5w, rows_per_mod, mod_off, tm=512):
    n, d = x2d.shape
    tm = _tile(min(n, rows_per_mod), tm)
    return pl.pallas_call(
        _uproj_kernel,
        grid=(n // tm,),
        in_specs=[
            pl.BlockSpec((tm, d), lambda i: (i, 0)),
            pl.BlockSpec((1, N_MOD, d), lambda i: ((i * tm) // rows_per_mod + mod_off, 0, 0)),
            pl.BlockSpec((1, d), lambda i: (0, 0)),
            pl.BlockSpec((None, d, s5w), lambda i: (layer, 0, 0)),
        ],
        out_specs=pl.BlockSpec((tm, s5w), lambda i: (i, 0)),
        out_shape=jax.ShapeDtypeStruct((n, s5w), F32),
        compiler_params=_params(("arbitrary",)),
        name="uproj",
    )(x2d, mods, nw.reshape(1, d), w_in)


def _hproj_kernel(*refs, colmajor):
    if colmajor:
        x_ref, m_ref, nw_ref, w_ref, perm_ref, o_ref, h_ref = refs
    else:
        x_ref, m_ref, nw_ref, w_ref, o_ref, h_ref = refs
    n = pl.program_id(2)

    @pl.when(n == 0)
    def _():
        h = _norm_mod(x_ref[...], nw_ref[...], m_ref[0, 3:4, :], m_ref[0, 4:5, :]).astype(BF16)
        if colmajor:
            h = h.reshape(h_ref.shape)
            h = jnp.dot(perm_ref[...], h, preferred_element_type=F32).astype(BF16)
        h_ref[...] = h

    p = jnp.dot(h_ref[...], w_ref[...], preferred_element_type=F32)

    @pl.when(n == 0)
    def _():
        o_ref[...] = _silu(p)

    @pl.when(n != 0)
    def _():
        o_ref[...] = p


def _hproj(x3d, mods, nw, w_in, *, layer, s5w, cols, mod_of_batch, cb, tn):
    b, t, d = x3d.shape
    rows = t // cols
    nh = w_in.shape[-1] - s5w
    assert s5w % tn == 0
    col0 = s5w // tn
    colmajor = cols > 1
    cb = _tile(cols, cb, 8) if colmajor else 1
    mod_fn = (lambda bi: bi) if mod_of_batch else (lambda bi: b)
    in_specs = [
        None,
        pl.BlockSpec((1, N_MOD, d), lambda bi, ci, n: (mod_fn(bi), 0, 0)),
        pl.BlockSpec((1, d), lambda bi, ci, n: (0, 0)),
        pl.BlockSpec((None, d, tn), lambda bi, ci, n: (layer, 0, col0 + n)),
    ]
    args = [None, mods, nw.reshape(1, d), w_in]
    if colmajor:
        args[0] = x3d.reshape(b, rows, cols, d)
        in_specs[0] = pl.BlockSpec((None, rows, cb, d), lambda bi, ci, n: (bi, 0, ci, 0))
        src = np.arange(rows * cb).reshape(rows, cb).T.reshape(-1)
        perm = np.zeros((rows * cb, rows * cb), np.float32)
        perm[np.arange(rows * cb), src] = 1.0
        args.append(jnp.asarray(perm, BF16))
        in_specs.append(pl.BlockSpec(perm.shape, lambda bi, ci, n: (0, 0)))
    else:
        args[0] = x3d
        in_specs[0] = pl.BlockSpec((None, rows, d), lambda bi, ci, n: (bi, 0, 0))
    return pl.pallas_call(
        functools.partial(_hproj_kernel, colmajor=colmajor),
        grid=(b, cols // cb, nh // tn),
        in_specs=in_specs,
        out_specs=pl.BlockSpec((None, cb * rows, tn), lambda bi, ci, n: (bi, ci, n)),
        out_shape=jax.ShapeDtypeStruct((b, t, nh), F32),
        scratch_shapes=[pltpu.VMEM((cb * rows, d), BF16)],
        compiler_params=_params(("arbitrary", "arbitrary", "arbitrary")),
        name="hproj",
    )(*args)


def _s5_sel_consts():
    t, h, p, o8 = S5_BLOCK, S5_GROUP, S5_STATE, S5_OCT
    colsel = np.zeros((t, 2, t, h, t, o8, h), np.float32)
    for r in range(t):
        for r2 in range(t):
            for hh in range(h):
                if r2 >= r:
                    colsel[r, 0, r2 - r, hh, r2, :, hh] = 1.0
                if r >= r2:
                    colsel[r, 1, r - r2, hh, r2, :, hh] = 1.0
    colsel = colsel.reshape(t, 2 * t * h, t * o8 * h)
    tile_e = np.zeros((4, p, 4, o8, p), np.float32)
    tile_y = np.zeros((4, h, 4, o8, h), np.float32)
    for dp in range(4):
        for i in range(p):
            tile_e[dp, i, dp, :, i] = 1.0
        for i in range(h):
            tile_y[dp, i, dp, :, i] = 1.0
    return colsel, tile_e.reshape(4 * p, 4 * o8 * p), tile_y.reshape(4 * h, 4 * o8 * h)


def _s5w_kernel(uk_ref, pb_ref, ca_ref, colsel_ref, tile_e_ref, tile_y_ref, k_ref, we_ref, wy_ref):
    t, h, p, o8 = S5_BLOCK, S5_GROUP, S5_STATE, S5_OCT
    gh = o8 * h
    gp = o8 * p

    def diag_mask(shape, row_div, col_mod, col_div):
        rg = lax.broadcasted_iota(jnp.int32, shape, 0) // row_div
        cg = (lax.broadcasted_iota(jnp.int32, shape, 1) % col_mod) // col_div
        return rg == cg

    mk = diag_mask((gh, t * gh), h, gh, h)
    me = diag_mask((gh, 4 * gp), h, gp, p)
    my = diag_mask((gp, 4 * gh), p, gh, h)
    uk = uk_ref[...].astype(BF16)
    for r in range(t):
        blk = jnp.dot(uk, colsel_ref[r], preferred_element_type=F32)
        k_ref[r * gh:(r + 1) * gh, :] = jnp.where(mk, blk, 0.0).astype(BF16)
        blk = jnp.dot(pb_ref[r].astype(BF16), tile_e_ref[...], preferred_element_type=F32)
        we_ref[r * gh:(r + 1) * gh, :] = jnp.where(me, blk, 0.0).astype(BF16)
        blk = jnp.where(my, jnp.dot(ca_ref[r].astype(BF16), tile_y_ref[...], preferred_element_type=F32), 0.0)
        for dp in range(4):
            wy_ref[dp * gp:(dp + 1) * gp, r * gh:(r + 1) * gh] = blk[:, dp * gh:(dp + 1) * gh].astype(BF16)


def _s5_prep(lam_re, lam_im, log_step, b_re, b_im, c_re, c_im):
    hp = lax.Precision.HIGHEST
    t = S5_BLOCK
    g, p = lam_re.shape[1:]
    h = b_re.shape[-1]
    n_oct = g // S5_OCT
    lam_re = jnp.minimum(lam_re.astype(F32), LAMBDA_RE_MAX)
    lam_im = lam_im.astype(F32)
    dt = jnp.exp(log_step.astype(F32))[..., None]
    mag = jnp.exp(lam_re * dt)
    lb_re = mag * jnp.cos(lam_im * dt)
    lb_im = mag * jnp.sin(lam_im * dt)
    den = lam_re * lam_re + lam_im * lam_im
    nr = lb_re - 1.0
    ni = lb_im
    cf_re = (nr * lam_re + ni * lam_im) / den
    cf_im = (ni * lam_re - nr * lam_im) / den
    b_re = b_re.astype(F32)
    b_im = b_im.astype(F32)
    br = cf_re[..., None] * b_re - cf_im[..., None] * b_im
    bi = cf_re[..., None] * b_im + cf_im[..., None] * b_re
    cr = c_re.astype(F32)
    ci = c_im.astype(F32)
    j = jnp.arange(t + 1, dtype=F32)[None, :, None, None]
    pmag = jnp.exp(j * (lam_re * dt)[:, None])
    pw_re = pmag * jnp.cos(j * (lam_im * dt)[:, None])
    pw_im = pmag * jnp.sin(j * (lam_im * dt)[:, None])
    pb_re = pw_re[..., None] * br[:, None] - pw_im[..., None] * bi[:, None]
    pb_im = pw_re[..., None] * bi[:, None] + pw_im[..., None] * br[:, None]
    kj = (jnp.einsum('dgep,djgph->djgeh', cr, pb_re[:, :t], precision=hp)
          - jnp.einsum('dgep,djgph->djgeh', ci, pb_im[:, :t], precision=hp))
    kj = kj.at[0, 0].add(kj[1, 0]).at[1, 0].set(0.0)
    uk = jnp.transpose(kj, (2, 4, 0, 1, 3)).reshape(n_oct, S5_OCT * h, 2 * t * h)
    r = np.arange(t)
    pb4 = jnp.stack([pb_re[0][t - 1 - r], pb_im[0][t - 1 - r], pb_re[1][r], pb_im[1][r]])
    pb3 = jnp.transpose(pb4.reshape(4, t, n_oct, S5_OCT, p, h), (2, 1, 3, 5, 0, 4))
    pb3 = pb3.reshape(n_oct, t, S5_OCT * h, 4 * p)

    def c_lam(d_, idx):
        pr = pw_re[d_][idx][:, :, None, :]
        pi = pw_im[d_][idx][:, :, None, :]
        return cr[d_][None] * pr - ci[d_][None] * pi, -(cr[d_][None] * pi + ci[d_][None] * pr)

    ca4 = jnp.stack(c_lam(0, r + 1) + c_lam(1, t - r))
    ca3 = jnp.transpose(ca4.reshape(4, t, n_oct, S5_OCT, h, p), (2, 1, 3, 5, 0, 4))
    ca3 = ca3.reshape(n_oct, t, S5_OCT * p, 4 * h)
    colsel, tile_e, tile_y = (jnp.asarray(a, BF16) for a in _s5_sel_consts())
    kdim = t * S5_OCT * h
    sdim = 4 * S5_OCT * p
    kmat, we, wy = pl.pallas_call(
        _s5w_kernel,
        grid=(n_oct,),
        in_specs=[
            pl.BlockSpec((None,) + uk.shape[1:], lambda o: (o, 0, 0)),
            pl.BlockSpec((None,) + pb3.shape[1:], lambda o: (o, 0, 0, 0)),
            pl.BlockSpec((None,) + ca3.shape[1:], lambda o: (o, 0, 0, 0)),
            pl.BlockSpec(colsel.shape, lambda o: (0, 0, 0)),
            pl.BlockSpec(tile_e.shape, lambda o: (0, 0)),
            pl.BlockSpec(tile_y.shape, lambda o: (0, 0)),
        ],
        out_specs=[
            pl.BlockSpec((None, kdim, kdim), lambda o: (o, 0, 0)),
            pl.BlockSpec((None, kdim, sdim), lambda o: (o, 0, 0)),
            pl.BlockSpec((None, sdim, kdim), lambda o: (o, 0, 0)),
        ],
        out_shape=[
            jax.ShapeDtypeStruct((n_oct, kdim, kdim), BF16),
            jax.ShapeDtypeStruct((n_oct, kdim, sdim), BF16),
            jax.ShapeDtypeStruct((n_oct, sdim, kdim), BF16),
        ],
        compiler_params=_params(("arbitrary",)),
        name="s5w",
    )(uk, pb3, ca3, colsel, tile_e, tile_y)
    dec_re = pw_re[:, t].reshape(2, n_oct, S5_OCT * p)
    dec_im = pw_im[:, t].reshape(2, n_oct, S5_OCT * p)
    dec = jnp.concatenate([dec_re, dec_im], axis=-1)
    dec = jnp.transpose(dec, (1, 0, 2)).reshape(n_oct * 2, 1, 2 * S5_OCT * p)
    return kmat, we, wy, dec


def _block_rows(u_ref, tmr):
    return jnp.concatenate(
        [u_ref[pl.ds(r, tmr, stride=S5_BLOCK), :].astype(BF16) for r in range(S5_BLOCK)], axis=1)


def _s5e_kernel(u_ref, we_ref, e_ref, *, tmr):
    e_ref[...] = jnp.dot(_block_rows(u_ref, tmr), we_ref[...], preferred_element_type=F32)


def _s5e(u, we, *, tmr=256):
    n, wid = u.shape
    m = n // S5_BLOCK
    n_oct, kdim, ncol = we.shape
    tmr = _tile(m, tmr)
    return pl.pallas_call(
        functools.partial(_s5e_kernel, tmr=tmr),
        grid=(n_oct, m // tmr),
        in_specs=[
            pl.BlockSpec((tmr * S5_BLOCK, LANE), lambda o, i: (i, o)),
            pl.BlockSpec((None, kdim, ncol), lambda o, i: (o, 0, 0)),
        ],
        out_specs=pl.BlockSpec((tmr, ncol), lambda o, i: (i, o)),
        out_shape=jax.ShapeDtypeStruct((m, n_oct * ncol), F32),
        compiler_params=_params(("arbitrary", "arbitrary")),
        name="s5e",
    )(u, we)


def _s5scan_kernel(ec_ref, el_ref, a_ref, sc_ref, sl_ref, *, nb, n_c, n_l, half):
    rev = pl.program_id(1) == 1
    a_re = a_ref[:, :half]
    a_im = a_ref[:, half:]

    def run(e_ref, s_ref, n, carry):
        def body(s, carry):
            i = jnp.where(rev, n - 1 - s, s)
            new = []
            for b in range(nb):
                x_re, x_im = carry[b]
                row = b * n + i
                s_ref[pl.ds(row, 1), :half] = x_re
                s_ref[pl.ds(row, 1), half:] = x_im
                e_re = e_ref[pl.ds(row, 1), :half]
                e_im = e_ref[pl.ds(row, 1), half:]
                new.append((a_re * x_re - a_im * x_im + e_re, a_re * x_im + a_im * x_re + e_im))
            return tuple(new)
        return lax.fori_loop(0, n, body, carry)

    zero = jnp.zeros((1, half), F32)
    carry = tuple((zero, zero) for _ in range(nb))
    carry = run(ec_ref, sc_ref, n_c, carry)
    run(el_ref, sl_ref, n_l, carry)


def _s5scan(e_ctx, e_lat, dec, *, nb):
    mc, wid = e_ctx.shape
    ml = e_lat.shape[0]
    ncol = dec.shape[-1]
    nblk = wid // ncol
    return pl.pallas_call(
        functools.partial(_s5scan_kernel, nb=nb, n_c=mc // nb, n_l=ml // nb, half=ncol // 2),
        grid=(nblk // 2, 2),
        in_specs=[
            pl.BlockSpec((mc, ncol), lambda o, d: (0, o * 2 + d)),
            pl.BlockSpec((ml, ncol), lambda o, d: (0, o * 2 + d)),
            pl.BlockSpec((None, 1, ncol), lambda o, d: (o * 2 + d, 0, 0)),
        ],
        out_specs=[
            pl.BlockSpec((mc, ncol), lambda o, d: (0, o * 2 + d)),
            pl.BlockSpec((ml, ncol), lambda o, d: (0, o * 2 + d)),
        ],
        out_shape=[jax.ShapeDtypeStruct((mc, wid), F32), jax.ShapeDtypeStruct((ml, wid), F32)],
        compiler_params=_params(("arbitrary", "arbitrary")),
        name="s5scan",
    )(e_ctx, e_lat, dec)


def _s5y_kernel(u_ref, s_ref, k_ref, wy_ref, y_ref, *, tmr):
    res = jnp.dot(_block_rows(u_ref, tmr), k_ref[...], preferred_element_type=F32)
    res += jnp.dot(s_ref[...].astype(BF16), wy_ref[...], preferred_element_type=F32)
    for r in range(S5_BLOCK):
        y_ref[pl.ds(r, tmr, stride=S5_BLOCK), :] = res[:, r * LANE:(r + 1) * LANE]


def _s5y(u, s, kmat, wy, *, tmr=256):
    n, wid = u.shape
    m = n // S5_BLOCK
    n_oct, kdim, _ = kmat.shape
    sdim = wy.shape[1]
    tmr = _tile(m, tmr)
    return pl.pallas_call(
        functools.partial(_s5y_kernel, tmr=tmr),
        grid=(n_oct, m // tmr),
        in_specs=[
            pl.BlockSpec((tmr * S5_BLOCK, LANE), lambda o, i: (i, o)),
            pl.BlockSpec((tmr, sdim), lambda o, i: (i, o)),
            pl.BlockSpec((None, kdim, kdim), lambda o, i: (o, 0, 0)),
            pl.BlockSpec((None, sdim, kdim), lambda o, i: (o, 0, 0)),
        ],
        out_specs=pl.BlockSpec((tmr * S5_BLOCK, LANE), lambda o, i: (i, o)),
        out_shape=jax.ShapeDtypeStruct((n, wid), F32),
        compiler_params=_params(("arbitrary", "arbitrary")),
        name="s5y",
    )(u, s, kmat, wy)


def _hgrn_masks():
    c = CHUNK
    t = np.arange(c)
    masks = []
    half = c // 2
    while half >= SUB:
        par = 2 * half
        second = (t % par) >= half
        same_parent = (t[:, None] // par) == (t[None, :] // par)
        masks.append((same_parent & second[:, None] & (~second)[None, :]).astype(np.float32))
        half //= 2
    masks.append((((t[:, None] // SUB) == (t[None, :] // SUB)) & (t[None, :] <= t[:, None])).astype(np.float32))
    return np.stack(masks)


def _hgrn_decays(g, reverse):
    nb = CHUNK // SUB
    g3 = g.reshape(nb, SUB, HEAD_DIM)
    r = lax.broadcasted_iota(jnp.int32, g3.shape, 1)
    p = g3
    for k in (1, 2, 4):
        if reverse:
            p = p + jnp.where(r <= SUB - 1 - k, pltpu.roll(p, SUB - k, 1), 0.0)
        else:
            p = p + jnp.where(r >= k, pltpu.roll(p, k, 1), 0.0)
    last = 0 if reverse else SUB - 1
    ref = SUB // 2 if reverse else SUB // 2 - 1
    tot = jnp.broadcast_to(p[:, last:last + 1, :], p.shape)
    ep = jnp.exp(p)
    eq = jnp.exp(tot - p)
    ed = p - jnp.broadcast_to(p[:, ref:ref + 1, :], p.shape)
    edp = jnp.exp(ed)
    edn = jnp.exp(-ed)
    et = jnp.broadcast_to(ep[:, last:last + 1, :], p.shape)
    mem = (lambda i: nb - 1 - i) if reverse else (lambda i: i)
    epb = [ep[mem(i)] for i in range(nb)]
    eqb = [eq[mem(i)] for i in range(nb)]
    etb = [et[mem(i)] for i in range(nb)]

    def assemble(blocks):
        return jnp.concatenate([blocks[mem(i)] for i in range(nb)], axis=0)

    cq = [None] * nb
    acc = None
    for i in range(nb):
        cq[i] = epb[i] if acc is None else epb[i] * acc
        acc = etb[i] if acc is None else acc * etb[i]
    total = acc[0:1, :]
    ck = [None] * nb
    acc = None
    for i in range(nb - 1, -1, -1):
        ck[i] = eqb[i] if acc is None else eqb[i] * acc
        acc = etb[i] if acc is None else acc * etb[i]
    levels = []
    half = nb // 2
    while half >= 1:
        par = 2 * half
        blocks = []
        for i in range(nb):
            j = i % par
            if j >= half:
                f = epb[i]
                for m in range(i - j + half, i):
                    f = f * etb[m]
            else:
                f = eqb[i]
                for m in range(i + 1, i - j + half):
                    f = f * etb[m]
            blocks.append(f)
        levels.append(assemble(blocks))
        half //= 2
    return levels, edp.reshape(g.shape), edn.reshape(g.shape), assemble(cq), assemble(ck), total


def _hgrn_kernel(*refs, nsub, heads, reverse, final, n_lvl):
    if final:
        (q_ref, z_ref, v_ref, lb_ref, masks_ref, s0_ref, of_ref, gate_ref, hnw_ref,
         o_ref, sfin_ref, st_ref) = refs
    else:
        q_ref, z_ref, v_ref, lb_ref, masks_ref, s0_ref, o_ref, sfin_ref, st_ref = refs
    j = pl.program_id(1)
    nj = pl.num_programs(1)
    c = CHUNK
    hd = HEAD_DIM
    nt = (((1,), (1,)), ((), ()))
    tn = (((0,), (0,)), ((), ()))

    @pl.when(j == 0)
    def _():
        st_ref[...] = s0_ref[...]

    mbool = [masks_ref[lv] > 0.0 for lv in range(n_lvl)]

    def chunk(s, _):
        cl = (nsub - 1 - s) if reverse else s
        rows = pl.ds(pl.multiple_of(cl * c, c), c)
        for h in range(heads):
            sl = slice(h * hd, (h + 1) * hd)
            z = z_ref[rows, sl]
            q = q_ref[rows, sl]
            v = v_ref[rows, sl].astype(BF16)
            lb = lb_ref[:, sl]
            f = lb + (1.0 - lb) * jax.nn.sigmoid(z)
            k = 1.0 - f
            g = jnp.log(jnp.maximum(f, F_MIN))
            levels, edp, edn, cq, ck, total = _hgrn_decays(g, reverse)
            qb = q.astype(BF16)
            kb = k.astype(BF16)
            a = jnp.zeros((c, c), F32)
            for lv in range(n_lvl - 1, -1, -1):
                if lv < n_lvl - 1:
                    e = levels[lv].astype(BF16)
                    ql = qb * e
                    kl = kb * e
                else:
                    ql = qb * edp.astype(BF16)
                    kl = kb * edn.astype(BF16)
                sc = lax.dot_general(ql, kl, nt, preferred_element_type=F32)
                a = jnp.where(mbool[lv], sc, a)
            st = st_ref[h]
            o_h = jnp.dot(a.astype(BF16), v, preferred_element_type=F32)
            o_h += lax.dot_general(qb * cq.astype(BF16), st.astype(BF16), nt, preferred_element_type=F32)
            st_ref[h] = st * total + lax.dot_general(v, kb * ck.astype(BF16), tn, preferred_element_type=F32)
            if final:
                o_h = o_h + of_ref[rows, sl]
                ms = jnp.mean(o_h * o_h, axis=-1, keepdims=True)
                o_h = o_h * lax.rsqrt(ms + EPS) * hnw_ref[...] * _silu(gate_ref[rows, sl])
            o_ref[rows, sl] = o_h.astype(o_ref.dtype)
        return 0

    lax.fori_loop(0, nsub, chunk, 0, unroll=2)

    @pl.when(j == nj - 1)
    def _():
        sfin_ref[...] = st_ref[...]


def _hgrn(p, lb, masks, s0, *, direction, width, o_fwd=None, hnw=None, nsub=4):
    b, t, _ = p.shape
    heads = width // HEAD_DIM
    nsub = _tile(t // CHUNK, nsub, 1)
    rows = nsub * CHUNK
    nj = t // rows
    reverse = direction == 1
    final = o_fwd is not None
    n_lvl = masks.shape[0]
    blk = (lambda j: nj - 1 - j) if reverse else (lambda j: j)
    zcol = 2 if reverse else 1
    in_specs = [
        pl.BlockSpec((None, rows, width), lambda bi, j: (bi, blk(j), 0)),
        pl.BlockSpec((None, rows, width), lambda bi, j: (bi, blk(j), zcol)),
        pl.BlockSpec((None, rows, width), lambda bi, j: (bi, blk(j), 3)),
        pl.BlockSpec((1, width), lambda bi, j: (0, 0)),
        pl.BlockSpec(masks.shape, lambda bi, j: (0, 0, 0)),
        pl.BlockSpec((None, heads, HEAD_DIM, HEAD_DIM), lambda bi, j: (bi, 0, 0, 0)),
    ]
    args = [p, p, p, lb.reshape(1, width), masks, s0]
    if final:
        in_specs += [
            pl.BlockSpec((None, rows, width), lambda bi, j: (bi, blk(j), 0)),
            pl.BlockSpec((None, rows, width), lambda bi, j: (bi, blk(j), 4)),
            pl.BlockSpec((1, HEAD_DIM), lambda bi, j: (0, 0)),
        ]
        args += [o_fwd, p, hnw.reshape(1, HEAD_DIM)]
    return pl.pallas_call(
        functools.partial(_hgrn_kernel, nsub=nsub, heads=heads, reverse=reverse, final=final, n_lvl=n_lvl),
        grid=(b, nj),
        in_specs=in_specs,
        out_specs=[
            pl.BlockSpec((None, rows, width), lambda bi, j: (bi, blk(j), 0)),
            pl.BlockSpec((None, heads, HEAD_DIM, HEAD_DIM), lambda bi, j: (bi, 0, 0, 0)),
        ],
        out_shape=[
            jax.ShapeDtypeStruct((b, t, width), F32),
            jax.ShapeDtypeStruct((b, heads, HEAD_DIM, HEAD_DIM), F32),
        ],
        scratch_shapes=[pltpu.VMEM((heads, HEAD_DIM, HEAD_DIM), F32)],
        compiler_params=_params(("arbitrary", "arbitrary")),
        name="hgrn_bwd" if final else "hgrn_fwd",
    )(*args)


def _mixout_kernel(x_ref, y_ref, u_ref, hg_ref, m_ref, dsk_ref, wglu_ref, bglu_ref, wo_ref, o_ref,
                   *, s5w, rb, colmajor):
    yy = y_ref[...] + dsk_ref[...] * u_ref[...]
    g = _gelu_tanh(yy)
    zz = jnp.dot(g.astype(BF16), wglu_ref[...], preferred_element_type=F32) + bglu_ref[...]
    s5 = (g * jax.nn.sigmoid(zz)).astype(BF16)
    if colmajor:
        hg = jnp.concatenate([hg_ref[:, r, :] for r in range(rb)], axis=0).astype(BF16)
    else:
        hg = hg_ref[...].astype(BF16)
    acc = jnp.dot(s5, wo_ref[:s5w, :], preferred_element_type=F32)
    acc += jnp.dot(hg, wo_ref[s5w:, :], preferred_element_type=F32)
    o_ref[...] = x_ref[...] + m_ref[0, 5:6, :] * acc


def _mixout(x3d, y3d, u3d, hg, mods, d_skip, w_glu, b_glu, w_o, *, layer, mod_of_batch, colmajor, rb=8):
    b, t, d = x3d.shape
    s5w = y3d.shape[-1]
    hw = hg.shape[-1]
    if colmajor:
        cols = GRID_W
        rows = t // cols
        rb = _tile(rows, rb, 8)
        tm = rb * cols
        hg_v = hg.reshape(b, cols, rows, hw)
        hg_spec = pl.BlockSpec((None, cols, rb, hw), lambda bi, i: (bi, 0, i, 0))
    else:
        rb = 1
        tm = _tile(t, 512)
        hg_v = hg
        hg_spec = pl.BlockSpec((None, tm, hw), lambda bi, i: (bi, i, 0))
    mod_fn = (lambda bi: bi) if mod_of_batch else (lambda bi: b)
    return pl.pallas_call(
        functools.partial(_mixout_kernel, s5w=s5w, rb=rb, colmajor=colmajor),
        grid=(b, t // tm),
        in_specs=[
            pl.BlockSpec((None, tm, d), lambda bi, i: (bi, i, 0)),
            pl.BlockSpec((None, tm, s5w), lambda bi, i: (bi, i, 0)),
            pl.BlockSpec((None, tm, s5w), lambda bi, i: (bi, i, 0)),
            hg_spec,
            pl.BlockSpec((1, N_MOD, d), lambda bi, i: (mod_fn(bi), 0, 0)),
            pl.BlockSpec((1, s5w), lambda bi, i: (0, 0)),
            pl.BlockSpec((None, s5w, s5w), lambda bi, i: (layer, 0, 0)),
            pl.BlockSpec((1, s5w), lambda bi, i: (0, 0)),
            pl.BlockSpec((None, s5w + hw, d), lambda bi, i: (layer, 0, 0)),
        ],
        out_specs=pl.BlockSpec((None, tm, d), lambda bi, i: (bi, i, 0)),
        out_shape=jax.ShapeDtypeStruct((b, t, d), F32),
        compiler_params=_params(("arbitrary", "arbitrary")),
        name="mixout",
    )(x3d, y3d, u3d, hg_v, mods, d_skip.reshape(1, s5w), w_glu, b_glu.reshape(1, s5w), w_o)


def kernel(x, c, ctx, c_ctx, w_ada, b_ada, norm_w, ffn_w_gate, ffn_w_up, ffn_w_down, w_in, w_out,
           s5_lambda_re, s5_lambda_im, s5_log_step, s5_b_re, s5_b_im, s5_c_re, s5_c_im, s5_d,
           s5_w_glu, s5_b_glu, hgrn_lower_bounds, hgrn_norm_w, final_norm_w):
    batch, seq, d = x.shape
    n_ctx = ctx.shape[1]
    depth = w_ada.shape[0]
    s5w = s5_d.shape[-1]
    hw = hgrn_lower_bounds.shape[-1]
    rows = seq // GRID_W
    assert batch < MOD_ROWS and rows % CHUNK == 0 and n_ctx % CHUNK == 0
    assert seq % S5_BLOCK == 0 and n_ctx % S5_BLOCK == 0 and (s5w // S5_GROUP) % S5_OCT == 0

    cvec = jnp.zeros((MOD_ROWS, d), F32).at[:batch].set(c.astype(F32)).at[batch].set(c_ctx.astype(F32))
    mods_all = _ada(cvec, w_ada, b_ada).reshape(depth, MOD_ROWS, N_MOD, d)

    lb_soft = jax.nn.softmax(hgrn_lower_bounds.astype(F32), axis=0)
    lb_all = jnp.cumsum(lb_soft, axis=0) - lb_soft[0]

    masks_np = _hgrn_masks()
    masks_f = jnp.asarray(masks_np, F32)
    masks_b = jnp.asarray(masks_np[:, ::-1, ::-1].copy(), F32)
    heads = hw // HEAD_DIM
    s_zero = jnp.zeros((batch, heads, HEAD_DIM, HEAD_DIM), F32)

    f_pad = (-ffn_w_gate.shape[-1]) % 512
    def cast_pad(w, axis):
        shape = list(w.shape)
        shape[axis] = f_pad
        return jnp.concatenate([w.astype(BF16), jnp.zeros(shape, BF16)], axis=axis)

    wg = cast_pad(ffn_w_gate, 3)
    wu = cast_pad(ffn_w_up, 3)
    wd = cast_pad(ffn_w_down, 2)
    w_in_b = w_in.astype(BF16)
    w_o = w_out.astype(BF16)
    w_glu = s5_w_glu.astype(BF16)

    xl = x.astype(F32)
    xc = ctx.astype(F32)
    lat = dict(rows_per_mod=seq, mod_off=0)
    cx = dict(rows_per_mod=batch * n_ctx, mod_off=batch)
    for l in range(depth):
        last = l == depth - 1
        mods = mods_all[l]

        xl = _ffn(xl.reshape(batch * seq, d), mods, norm_w[l, 0], wg, wu, wd, layer=l, half=0, base=0,
                  **lat).reshape(batch, seq, d)
        xc = _ffn(xc.reshape(batch * n_ctx, d), mods, norm_w[l, 0], wg, wu, wd, layer=l, half=0, base=0,
                  **cx).reshape(batch, n_ctx, d)

        u_l = _uproj(xl.reshape(batch * seq, d), mods, norm_w[l, 1], w_in_b, layer=l, s5w=s5w, **lat)
        u_c = _uproj(xc.reshape(batch * n_ctx, d), mods, norm_w[l, 1], w_in_b, layer=l, s5w=s5w, **cx)
        p_lat = _hproj(xl, mods, norm_w[l, 1], w_in_b, layer=l, s5w=s5w, cols=GRID_W, mod_of_batch=True,
                       cb=16, tn=hw)
        p_ctx = _hproj(xc, mods, norm_w[l, 1], w_in_b, layer=l, s5w=s5w, cols=1, mod_of_batch=False,
                       cb=1, tn=hw)

        kmat, we, wy, dec = _s5_prep(s5_lambda_re[l], s5_lambda_im[l], s5_log_step[l], s5_b_re[l],
                                     s5_b_im[l], s5_c_re[l], s5_c_im[l])
        e_l = _s5e(u_l, we)
        e_c = _s5e(u_c, we)
        st_c, st_l = _s5scan(e_c, e_l, dec, nb=batch)
        y_l = _s5y(u_l, st_l, kmat, wy).reshape(batch, seq, s5w)

        lb_f = lb_all[l, 0]
        lb_b = lb_all[l, 1]
        oc_f, sc_f = _hgrn(p_ctx, lb_f, masks_f, s_zero, direction=0, width=hw)
        ol_f, _ = _hgrn(p_lat, lb_f, masks_f, sc_f, direction=0, width=hw)
        hg_c, sc_b = _hgrn(p_ctx, lb_b, masks_b, s_zero, direction=1, width=hw,
                           o_fwd=oc_f, hnw=hgrn_norm_w[l])
        hg_l, _ = _hgrn(p_lat, lb_b, masks_b, sc_b, direction=1, width=hw,
                        o_fwd=ol_f, hnw=hgrn_norm_w[l])

        xl = _mixout(xl, y_l, u_l.reshape(batch, seq, s5w), hg_l, mods, s5_d[l], w_glu, s5_b_glu[l], w_o,
                     layer=l, mod_of_batch=True, colmajor=True)
        if not last:
            y_c = _s5y(u_c, st_c, kmat, wy).reshape(batch, n_ctx, s5w)
            xc = _mixout(xc, y_c, u_c.reshape(batch, n_ctx, s5w), hg_c, mods, s5_d[l], w_glu, s5_b_glu[l],
                         w_o, layer=l, mod_of_batch=False, colmajor=False)

        xl = _ffn(xl.reshape(batch * seq, d), mods, norm_w[l, 2], wg, wu, wd, layer=l, half=1, base=6,
                  final_nw=final_norm_w if last else None, **lat).reshape(batch, seq, d)
        if not last:
            xc = _ffn(xc.reshape(batch * n_ctx, d), mods, norm_w[l, 2], wg, wu, wd, layer=l, half=1, base=6,
                      **cx).reshape(batch, n_ctx, d)
    return xl
```

```python
import functools

import numpy as np
import jax
import jax.numpy as jnp
from jax import lax
from jax.experimental import pallas as pl
from jax.experimental.pallas import tpu as pltpu

F32 = jnp.float32
BF16 = jnp.bfloat16

EPS = 1e-6
F_MIN = 1e-6
LAMBDA_RE_MAX = -1e-4
GRID_W = 64
N_MOD = 9
S5_GROUP = 16
S5_STATE = 64
S5_BLOCK = 8
S5_OCT = 8
HEAD_DIM = 128
CHUNK = 64
SUB = 8
LANE = 128
MOD_ROWS = 8
VMEM_LIMIT = 60 * 1024 * 1024


def _params(sem):
    return pltpu.CompilerParams(dimension_semantics=sem, vmem_limit_bytes=VMEM_LIMIT)


def _tile(n, pref, mult=8):
    t = min(n, pref)
    while t > 0:
        if n % t == 0 and t % mult == 0:
            return t
        t -= 1
    return n


def _norm_mod(x, nw, shift, scale):
    ms = jnp.mean(x * x, axis=-1, keepdims=True)
    y = x * lax.rsqrt(ms + EPS) * nw
    return y * (1.0 + scale) + shift


def _silu(x):
    return x * jax.nn.sigmoid(x)


def _gelu_tanh(x):
    return 0.5 * x * (1.0 + jnp.tanh(0.7978845608028654 * (x + 0.044715 * (x * x * x))))


def _ada_kernel(c_ref, w_ref, b_ref, o_ref):
    a = _silu(c_ref[...])
    a_hi = a.astype(BF16)
    a_lo = (a - a_hi.astype(F32)).astype(BF16)
    w = w_ref[...]
    w_hi = w.astype(BF16)
    w_lo = (w - w_hi.astype(F32)).astype(BF16)
    acc = jnp.dot(a_hi, w_hi, preferred_element_type=F32)
    acc += jnp.dot(a_lo, w_hi, preferred_element_type=F32)
    acc += jnp.dot(a_hi, w_lo, preferred_element_type=F32)
    o_ref[...] = acc + b_ref[...]


def _ada(cvec, w_ada, b_ada):
    depth, d, n = w_ada.shape
    tn = _tile(n, 1024, LANE)
    return pl.pallas_call(
        _ada_kernel,
        grid=(depth, n // tn),
        in_specs=[
            pl.BlockSpec((MOD_ROWS, d), lambda l, j: (0, 0)),
            pl.BlockSpec((None, d, tn), lambda l, j: (l, 0, j)),
            pl.BlockSpec((None, 1, tn), lambda l, j: (l, 0, j)),
        ],
        out_specs=pl.BlockSpec((None, MOD_ROWS, tn), lambda l, j: (l, 0, j)),
        out_shape=jax.ShapeDtypeStruct((depth, MOD_ROWS, n), F32),
        compiler_params=_params(("arbitrary", "arbitrary")),
        name="ada",
    )(cvec, w_ada, b_ada.reshape(depth, 1, n))


def _ffn_kernel(*refs, base, n_f, final, tail):
    if final:
        x_ref, m_ref, nw_ref, wg_ref, wu_ref, wd_ref, fnw_ref, o_ref, h_ref = refs
    else:
        x_ref, m_ref, nw_ref, wg_ref, wu_ref, wd_ref, o_ref, h_ref = refs
    j = pl.program_id(1)

    @pl.when(j == 0)
    def _():
        h = _norm_mod(x_ref[...], nw_ref[...], m_ref[0, base:base + 1, :], m_ref[0, base + 1:base + 2, :])
        h_ref[...] = h.astype(BF16)

    def partial_sum(valid=None):
        h = h_ref[...]
        g = jnp.dot(h, wg_ref[...], preferred_element_type=F32)
        u = jnp.dot(h, wu_ref[...], preferred_element_type=F32)
        a = (_silu(g) * u).astype(BF16)
        wd = wd_ref[...]
        if valid is not None:
            a = jnp.where(lax.broadcasted_iota(jnp.int32, a.shape, 1) < valid, a, jnp.zeros_like(a))
            wd = jnp.where(lax.broadcasted_iota(jnp.int32, wd.shape, 0) < valid, wd, jnp.zeros_like(wd))
        return jnp.dot(a, wd, preferred_element_type=F32)

    @pl.when(j == 0)
    def _():
        o_ref[...] = partial_sum()

    @pl.when((j > 0) & (j < n_f - 1))
    def _():
        o_ref[...] += partial_sum()

    @pl.when(j == n_f - 1)
    def _():
        y = x_ref[...] + (0.5 * m_ref[0, base + 2:base + 3, :]) * (o_ref[...] + partial_sum(tail))
        if final:
            ms = jnp.mean(y * y, axis=-1, keepdims=True)
            y = y * lax.rsqrt(ms + EPS) * fnw_ref[...]
        o_ref[...] = y


def _ffn(x2d, mods, nw, wg, wu, wd, *, layer, half, base, rows_per_mod, mod_off, final_nw=None,
         tm=1024, tf=512):
    n, d = x2d.shape
    fp = wg.shape[-1]
    tm = _tile(min(n, rows_per_mod), tm)
    n_f = pl.cdiv(fp, tf)
    tail = fp - (n_f - 1) * tf if fp % tf else None
    assert n_f >= 2
    final = final_nw is not None
    in_specs = [
        pl.BlockSpec((tm, d), lambda i, j: (i, 0)),
        pl.BlockSpec((1, N_MOD, d), lambda i, j: ((i * tm) // rows_per_mod + mod_off, 0, 0)),
        pl.BlockSpec((1, d), lambda i, j: (0, 0)),
        pl.BlockSpec((None, None, d, tf), lambda i, j: (layer, half, 0, j)),
        pl.BlockSpec((None, None, d, tf), lambda i, j: (layer, half, 0, j)),
        pl.BlockSpec((None, None, tf, d), lambda i, j: (layer, half, j, 0)),
    ]
    args = [x2d, mods, nw.reshape(1, d), wg, wu, wd]
    if final:
        in_specs.append(pl.BlockSpec((1, d), lambda i, j: (0, 0)))
        args.append(final_nw.reshape(1, d))
    return pl.pallas_call(
        functools.partial(_ffn_kernel, base=base, n_f=n_f, final=final, tail=tail),
        grid=(n // tm, n_f),
        in_specs=in_specs,
        out_specs=pl.BlockSpec((tm, d), lambda i, j: (i, 0)),
        out_shape=jax.ShapeDtypeStruct((n, d), F32),
        scratch_shapes=[pltpu.VMEM((tm, d), BF16)],
        compiler_params=_params(("arbitrary", "arbitrary")),
        name="ffn_final" if final else "ffn",
    )(*args)


def _uproj_kernel(x_ref, m_ref, nw_ref, w_ref, u_ref):
    h = _norm_mod(x_ref[...], nw_ref[...], m_ref[0, 3:4, :], m_ref[0, 4:5, :])
    u_ref[...] = jnp.dot(h.astype(BF16), w_ref[...], preferred_element_type=F32)


def _uproj(x2d, mods, nw, w_in, *, layer, s5w, rows_per_mod, mod_off, tm=512):
    n, d = x2d.shape
    tm = _tile(min(n, rows_per_mod), tm)
    return pl.pallas_call(
        _uproj_kernel,
        grid=(n // tm,),
        in_specs=[
            pl.BlockSpec((tm, d), lambda i: (i, 0)),
            pl.BlockSpec((1, N_MOD, d), lambda i: ((i * tm) // rows_per_mod + mod_off, 0, 0)),
            pl.BlockSpec((1, d), lambda i: (0, 0)),
            pl.BlockSpec((None, d, s5w), lambda i: (layer, 0, 0)),
        ],
        out_specs=pl.BlockSpec((tm, s5w), lambda i: (i, 0)),
        out_shape=jax.ShapeDtypeStruct((n, s5w), F32),
        compiler_params=_params(("arbitrary",)),
        name="uproj",
    )(x2d, mods, nw.reshape(1, d), w_in)


def _hproj_kernel(*refs, colmajor):
    if colmajor:
        x_ref, m_ref, nw_ref, w_ref, perm_ref, o_ref, h_ref = refs
    else:
        x_ref, m_ref, nw_ref, w_ref, o_ref, h_ref = refs
    n = pl.program_id(2)

    @pl.when(n == 0)
    def _():
        h = _norm_mod(x_ref[...], nw_ref[...], m_ref[0, 3:4, :], m_ref[0, 4:5, :]).astype(BF16)
        if colmajor:
            h = h.reshape(h_ref.shape)
            h = jnp.dot(perm_ref[...], h, preferred_element_type=F32).astype(BF16)
        h_ref[...] = h

    p = jnp.dot(h_ref[...], w_ref[...], preferred_element_type=F32)

    @pl.when(n == 0)
    def _():
        o_ref[...] = _silu(p)

    @pl.when(n != 0)
    def _():
        o_ref[...] = p


def _hproj(x3d, mods, nw, w_in, *, layer, s5w, cols, mod_of_batch, cb, tn):
    b, t, d = x3d.shape
    rows = t // cols
    nh = w_in.shape[-1] - s5w
    assert s5w % tn == 0
    col0 = s5w // tn
    colmajor = cols > 1
    cb = _tile(cols, cb, 8) if colmajor else 1
    mod_fn = (lambda bi: bi) if mod_of_batch else (lambda bi: b)
    in_specs = [
        None,
        pl.BlockSpec((1, N_MOD, d), lambda bi, ci, n: (mod_fn(bi), 0, 0)),
        pl.BlockSpec((1, d), lambda bi, ci, n: (0, 0)),
        pl.BlockSpec((None, d, tn), lambda bi, ci, n: (layer, 0, col0 + n)),
    ]
    args = [None, mods, nw.reshape(1, d), w_in]
    if colmajor:
        args[0] = x3d.reshape(b, rows, cols, d)
        in_specs[0] = pl.BlockSpec((None, rows, cb, d), lambda bi, ci, n: (bi, 0, ci, 0))
        src = np.arange(rows * cb).reshape(rows, cb).T.reshape(-1)
        perm = np.zeros((rows * cb, rows * cb), np.float32)
        perm[np.arange(rows * cb), src] = 1.0
        args.append(jnp.asarray(perm, BF16))
        in_specs.append(pl.BlockSpec(perm.shape, lambda bi, ci, n: (0, 0)))
    else:
        args[0] = x3d
        in_specs[0] = pl.BlockSpec((None, rows, d), lambda bi, ci, n: (bi, 0, 0))
    return pl.pallas_call(
        functools.partial(_hproj_kernel, colmajor=colmajor),
        grid=(b, cols // cb, nh // tn),
        in_specs=in_specs,
        out_specs=pl.BlockSpec((None, cb * rows, tn), lambda bi, ci, n: (bi, ci, n)),
        out_shape=jax.ShapeDtypeStruct((b, t, nh), F32),
        scratch_shapes=[pltpu.VMEM((cb * rows, d), BF16)],
        compiler_params=_params(("arbitrary", "arbitrary", "arbitrary")),
        name="hproj",
    )(*args)


def _s5_sel_consts():
    t, h, p, o8 = S5_BLOCK, S5_GROUP, S5_STATE, S5_OCT
    colsel = np.zeros((t, 2, t, h, t, o8, h), np.float32)
    for r in range(t):
        for r2 in range(t):
            for hh in range(h):
                if r2 >= r:
                    colsel[r, 0, r2 - r, hh, r2, :, hh] = 1.0
                if r >= r2:
                    colsel[r, 1, r - r2, hh, r2, :, hh] = 1.0
    colsel = colsel.reshape(t, 2 * t * h, t * o8 * h)
    tile_e = np.zeros((4, p, 4, o8, p), np.float32)
    tile_y = np.zeros((4, h, 4, o8, h), np.float32)
    for dp in range(4):
        for i in range(p):
            tile_e[dp, i, dp, :, i] = 1.0
        for i in range(h):
            tile_y[dp, i, dp, :, i] = 1.0
    return colsel, tile_e.reshape(4 * p, 4 * o8 * p), tile_y.reshape(4 * h, 4 * o8 * h)


def _s5w_kernel(uk_ref, pb_ref, ca_ref, colsel_ref, tile_e_ref, tile_y_ref, k_ref, we_ref, wy_ref):
    t, h, p, o8 = S5_BLOCK, S5_GROUP, S5_STATE, S5_OCT
    gh = o8 * h
    gp = o8 * p

    def diag_mask(shape, row_div, col_mod, col_div):
        rg = lax.broadcasted_iota(jnp.int32, shape, 0) // row_div
        cg = (lax.broadcasted_iota(jnp.int32, shape, 1) % col_mod) // col_div
        return rg == cg

    mk = diag_mask((gh, t * gh), h, gh, h)
    me = diag_mask((gh, 4 * gp), h, gp, p)
    my = diag_mask((gp, 4 * gh), p, gh, h)
    uk = uk_ref[...].astype(BF16)
    for r in range(t):
        blk = jnp.dot(uk, colsel_ref[r], preferred_element_type=F32)
        k_ref[r * gh:(r + 1) * gh, :] = jnp.where(mk, blk, 0.0).astype(BF16)
        blk = jnp.dot(pb_ref[r].astype(BF16), tile_e_ref[...], preferred_element_type=F32)
        we_ref[r * gh:(r + 1) * gh, :] = jnp.where(me, blk, 0.0).astype(BF16)
        blk = jnp.where(my, jnp.dot(ca_ref[r].astype(BF16), tile_y_ref[...], preferred_element_type=F32), 0.0)
        for dp in range(4):
            wy_ref[dp * gp:(dp + 1) * gp, r * gh:(r + 1) * gh] = blk[:, dp * gh:(dp + 1) * gh].astype(BF16)


def _s5_prep(lam_re, lam_im, log_step, b_re, b_im, c_re, c_im):
    hp = lax.Precision.HIGHEST
    t = S5_BLOCK
    g, p = lam_re.shape[1:]
    h = b_re.shape[-1]
    n_oct = g // S5_OCT
    lam_re = jnp.minimum(lam_re.astype(F32), LAMBDA_RE_MAX)
    lam_im = lam_im.astype(F32)
    dt = jnp.exp(log_step.astype(F32))[..., None]
    mag = jnp.exp(lam_re * dt)
    lb_re = mag * jnp.cos(lam_im * dt)
    lb_im = mag * jnp.sin(lam_im * dt)
    den = lam_re * lam_re + lam_im * lam_im
    nr = lb_re - 1.0
    ni = lb_im
    cf_re = (nr * lam_re + ni * lam_im) / den
    cf_im = (ni * lam_re - nr * lam_im) / den
    b_re = b_re.astype(F32)
    b_im = b_im.astype(F32)
    br = cf_re[..., None] * b_re - cf_im[..., None] * b_im
    bi = cf_re[..., None] * b_im + cf_im[..., None] * b_re
    cr = c_re.astype(F32)
    ci = c_im.astype(F32)
    j = jnp.arange(t + 1, dtype=F32)[None, :, None, None]
    pmag = jnp.exp(j * (lam_re * dt)[:, None])
    pw_re = pmag * jnp.cos(j * (lam_im * dt)[:, None])
    pw_im = pmag * jnp.sin(j * (lam_im * dt)[:, None])
    pb_re = pw_re[..., None] * br[:, None] - pw_im[..., None] * bi[:, None]
    pb_im = pw_re[..., None] * bi[:, None] + pw_im[..., None] * br[:, None]
    kj = (jnp.einsum('dgep,djgph->djgeh', cr, pb_re[:, :t], precision=hp)
          - jnp.einsum('dgep,djgph->djgeh', ci, pb_im[:, :t], precision=hp))
    kj = kj.at[0, 0].add(kj[1, 0]).at[1, 0].set(0.0)
    uk = jnp.transpose(kj, (2, 4, 0, 1, 3)).reshape(n_oct, S5_OCT * h, 2 * t * h)
    r = np.arange(t)
    pb4 = jnp.stack([pb_re[0][t - 1 - r], pb_im[0][t - 1 - r], pb_re[1][r], pb_im[1][r]])
    pb3 = jnp.transpose(pb4.reshape(4, t, n_oct, S5_OCT, p, h), (2, 1, 3, 5, 0, 4))
    pb3 = pb3.reshape(n_oct, t, S5_OCT * h, 4 * p)

    def c_lam(d_, idx):
        pr = pw_re[d_][idx][:, :, None, :]
        pi = pw_im[d_][idx][:, :, None, :]
        return cr[d_][None] * pr - ci[d_][None] * pi, -(cr[d_][None] * pi + ci[d_][None] * pr)

    ca4 = jnp.stack(c_lam(0, r + 1) + c_lam(1, t - r))
    ca3 = jnp.transpose(ca4.reshape(4, t, n_oct, S5_OCT, h, p), (2, 1, 3, 5, 0, 4))
    ca3 = ca3.reshape(n_oct, t, S5_OCT * p, 4 * h)
    colsel, tile_e, tile_y = (jnp.asarray(a, BF16) for a in _s5_sel_consts())
    kdim = t * S5_OCT * h
    sdim = 4 * S5_OCT * p
    kmat, we, wy = pl.pallas_call(
        _s5w_kernel,
        grid=(n_oct,),
        in_specs=[
            pl.BlockSpec((None,) + uk.shape[1:], lambda o: (o, 0, 0)),
            pl.BlockSpec((None,) + pb3.shape[1:], lambda o: (o, 0, 0, 0)),
            pl.BlockSpec((None,) + ca3.shape[1:], lambda o: (o, 0, 0, 0)),
            pl.BlockSpec(colsel.shape, lambda o: (0, 0, 0)),
            pl.BlockSpec(tile_e.shape, lambda o: (0, 0)),
            pl.BlockSpec(tile_y.shape, lambda o: (0, 0)),
        ],
        out_specs=[
            pl.BlockSpec((None, kdim, kdim), lambda o: (o, 0, 0)),
            pl.BlockSpec((None, kdim, sdim), lambda o: (o, 0, 0)),
            pl.BlockSpec((None, sdim, kdim), lambda o: (o, 0, 0)),
        ],
        out_shape=[
            jax.ShapeDtypeStruct((n_oct, kdim, kdim), BF16),
            jax.ShapeDtypeStruct((n_oct, kdim, sdim), BF16),
            jax.ShapeDtypeStruct((n_oct, sdim, kdim), BF16),
        ],
        compiler_params=_params(("arbitrary",)),
        name="s5w",
    )(uk, pb3, ca3, colsel, tile_e, tile_y)
    nk2 = S5_OCT * p // LANE
    a_re = jnp.transpose(pw_re[:, t].reshape(2, n_oct, nk2, LANE), (1, 0, 2, 3))
    a_im = jnp.transpose(pw_im[:, t].reshape(2, n_oct, nk2, LANE), (1, 0, 2, 3))
    dec = jnp.stack([jnp.concatenate([a_re, a_re], axis=2), jnp.concatenate([-a_im, a_im], axis=2)], axis=2)
    return kmat, we, wy, dec


def _block_rows(u_ref, tmr):
    return jnp.concatenate(
        [u_ref[pl.ds(r, tmr, stride=S5_BLOCK), :].astype(BF16) for r in range(S5_BLOCK)], axis=1)


def _s5e_kernel(u_ref, we_ref, e_ref, *, tmr):
    res = jnp.dot(_block_rows(u_ref, tmr), we_ref[...], preferred_element_type=F32)
    nk = res.shape[1] // (2 * LANE)
    for d in range(2):
        for k in range(nk):
            c0 = (d * nk + k) * LANE
            e_ref[d, pl.ds(k, tmr, stride=nk), :] = res[:, c0:c0 + LANE]


def _s5e(u, we, *, tmr=1024):
    n, wid = u.shape
    m = n // S5_BLOCK
    n_oct, kdim, ncol = we.shape
    nk = ncol // (2 * LANE)
    tmr = _tile(m, tmr)
    return pl.pallas_call(
        functools.partial(_s5e_kernel, tmr=tmr),
        grid=(n_oct, m // tmr),
        in_specs=[
            pl.BlockSpec((tmr * S5_BLOCK, LANE), lambda o, i: (i, o)),
            pl.BlockSpec((None, kdim, ncol), lambda o, i: (o, 0, 0)),
        ],
        out_specs=pl.BlockSpec((None, 2, tmr * nk, LANE), lambda o, i: (o, 0, i, 0)),
        out_shape=jax.ShapeDtypeStruct((n_oct, 2, m * nk, LANE), F32),
        compiler_params=_params(("arbitrary", "arbitrary")),
        name="s5e",
    )(u, we)


def _s5scan_kernel(ec_ref, el_ref, a_ref, sc_ref, sl_ref, *, nb, n_c, n_l, nk):
    rev = pl.program_id(1) == 1
    a1 = a_ref[0]
    a2 = a_ref[1]

    def run(e_ref, s_ref, n, carry):
        def body(s, carry):
            i = jnp.where(rev, n - 1 - s, s)
            new = []
            for b in range(nb):
                x = carry[b]
                rows = pl.ds(pl.multiple_of((b * n + i) * nk, nk), nk)
                s_ref[rows, :] = x
                new.append(a1 * x + a2 * pltpu.roll(x, nk // 2, 0) + e_ref[rows, :])
            return tuple(new)
        return lax.fori_loop(0, n, body, carry)

    carry = tuple(jnp.zeros((nk, LANE), F32) for _ in range(nb))
    carry = run(ec_ref, sc_ref, n_c, carry)
    run(el_ref, sl_ref, n_l, carry)


def _s5scan(e_ctx, e_lat, dec, *, nb):
    n_oct, _, rc, _ = e_ctx.shape
    rl = e_lat.shape[2]
    nk = dec.shape[-2]
    return pl.pallas_call(
        functools.partial(_s5scan_kernel, nb=nb, n_c=rc // (nk * nb), n_l=rl // (nk * nb), nk=nk),
        grid=(n_oct, 2),
        in_specs=[
            pl.BlockSpec((None, None, rc, LANE), lambda o, d: (o, d, 0, 0)),
            pl.BlockSpec((None, None, rl, LANE), lambda o, d: (o, d, 0, 0)),
            pl.BlockSpec((None, None, 2, nk, LANE), lambda o, d: (o, d, 0, 0, 0)),
        ],
        out_specs=[
            pl.BlockSpec((None, None, rc, LANE), lambda o, d: (o, d, 0, 0)),
            pl.BlockSpec((None, None, rl, LANE), lambda o, d: (o, d, 0, 0)),
        ],
        out_shape=[jax.ShapeDtypeStruct(e_ctx.shape, F32), jax.ShapeDtypeStruct(e_lat.shape, F32)],
        compiler_params=_params(("arbitrary", "arbitrary")),
        name="s5scan",
    )(e_ctx, e_lat, dec)


def _s5y_kernel(u_ref, s_ref, k_ref, wy_ref, y_ref, *, tmr):
    nk = s_ref.shape[1] // tmr
    st = jnp.concatenate(
        [s_ref[d, pl.ds(k, tmr, stride=nk), :].astype(BF16) for d in range(2) for k in range(nk)], axis=1)
    res = jnp.dot(_block_rows(u_ref, tmr), k_ref[...], preferred_element_type=F32)
    res += jnp.dot(st, wy_ref[...], preferred_element_type=F32)
    for r in range(S5_BLOCK):
        y_ref[pl.ds(r, tmr, stride=S5_BLOCK), :] = res[:, r * LANE:(r + 1) * LANE]


def _s5y(u, s, kmat, wy, *, tmr=512):
    n, wid = u.shape
    m = n // S5_BLOCK
    n_oct, kdim, _ = kmat.shape
    sdim = wy.shape[1]
    nk = s.shape[2] // m
    tmr = _tile(m, tmr)
    return pl.pallas_call(
        functools.partial(_s5y_kernel, tmr=tmr),
        grid=(n_oct, m // tmr),
        in_specs=[
            pl.BlockSpec((tmr * S5_BLOCK, LANE), lambda o, i: (i, o)),
            pl.BlockSpec((None, 2, tmr * nk, LANE), lambda o, i: (o, 0, i, 0)),
            pl.BlockSpec((None, kdim, kdim), lambda o, i: (o, 0, 0)),
            pl.BlockSpec((None, sdim, kdim), lambda o, i: (o, 0, 0)),
        ],
        out_specs=pl.BlockSpec((tmr * S5_BLOCK, LANE), lambda o, i: (i, o)),
        out_shape=jax.ShapeDtypeStruct((n, wid), F32),
        compiler_params=_params(("arbitrary", "arbitrary")),
        name="s5y",
    )(u, s, kmat, wy)


def _hgrn_masks():
    c = CHUNK
    t = np.arange(c)
    masks = []
    half = c // 2
    while half >= SUB:
        par = 2 * half
        second = (t % par) >= half
        same_parent = (t[:, None] // par) == (t[None, :] // par)
        masks.append((same_parent & second[:, None] & (~second)[None, :]).astype(np.float32))
        half //= 2
    masks.append((((t[:, None] // SUB) == (t[None, :] // SUB)) & (t[None, :] <= t[:, None])).astype(np.float32))
    return np.stack(masks)


def _hgrn_decays(g, reverse):
    nb = CHUNK // SUB
    g3 = g.reshape(nb, SUB, HEAD_DIM)
    r = lax.broadcasted_iota(jnp.int32, g3.shape, 1)
    p = g3
    for k in (1, 2, 4):
        if reverse:
            p = p + jnp.where(r <= SUB - 1 - k, pltpu.roll(p, SUB - k, 1), 0.0)
        else:
            p = p + jnp.where(r >= k, pltpu.roll(p, k, 1), 0.0)
    last = 0 if reverse else SUB - 1
    ref = SUB // 2 if reverse else SUB // 2 - 1
    tot = jnp.broadcast_to(p[:, last:last + 1, :], p.shape)
    ep = jnp.exp(p)
    eq = jnp.exp(tot - p)
    ed = p - jnp.broadcast_to(p[:, ref:ref + 1, :], p.shape)
    edp = jnp.exp(ed)
    edn = jnp.exp(-ed)
    et = jnp.broadcast_to(ep[:, last:last + 1, :], p.shape)
    mem = (lambda i: nb - 1 - i) if reverse else (lambda i: i)
    epb = [ep[mem(i)] for i in range(nb)]
    eqb = [eq[mem(i)] for i in range(nb)]
    etb = [et[mem(i)] for i in range(nb)]

    def assemble(blocks):
        return jnp.concatenate([blocks[mem(i)] for i in range(nb)], axis=0)

    cq = [None] * nb
    acc = None
    for i in range(nb):
        cq[i] = epb[i] if acc is None else epb[i] * acc
        acc = etb[i] if acc is None else acc * etb[i]
    total = acc[0:1, :]
    ck = [None] * nb
    acc = None
    for i in range(nb - 1, -1, -1):
        ck[i] = eqb[i] if acc is None else eqb[i] * acc
        acc = etb[i] if acc is None else acc * etb[i]
    levels = []
    half = nb // 2
    while half >= 1:
        par = 2 * half
        blocks = []
        for i in range(nb):
            j = i % par
            if j >= half:
                f = epb[i]
                for m in range(i - j + half, i):
                    f = f * etb[m]
            else:
                f = eqb[i]
                for m in range(i + 1, i - j + half):
                    f = f * etb[m]
            blocks.append(f)
        levels.append(assemble(blocks))
        half //= 2
    return levels, edp.reshape(g.shape), edn.reshape(g.shape), assemble(cq), assemble(ck), total


def _hgrn_kernel(*refs, nsub, heads, reverse, final, n_lvl):
    if final:
        (q_ref, z_ref, v_ref, lb_ref, masks_ref, s0_ref, of_ref, gate_ref, hnw_ref,
         o_ref, sfin_ref, st_ref) = refs
    else:
        q_ref, z_ref, v_ref, lb_ref, masks_ref, s0_ref, o_ref, sfin_ref, st_ref = refs
    j = pl.program_id(1)
    nj = pl.num_programs(1)
    c = CHUNK
    hd = HEAD_DIM
    nt = (((1,), (1,)), ((), ()))
    tn = (((0,), (0,)), ((), ()))

    @pl.when(j == 0)
    def _():
        st_ref[...] = s0_ref[...]

    mbool = [masks_ref[lv] > 0.0 for lv in range(n_lvl)]

    def chunk(s, _):
        cl = (nsub - 1 - s) if reverse else s
        rows = pl.ds(pl.multiple_of(cl * c, c), c)
        for h in range(heads):
            sl = slice(h * hd, (h + 1) * hd)
            z = z_ref[rows, sl]
            q = q_ref[rows, sl]
            v = v_ref[rows, sl].astype(BF16)
            lb = lb_ref[:, sl]
            f = lb + (1.0 - lb) * jax.nn.sigmoid(z)
            k = 1.0 - f
            g = jnp.log(jnp.maximum(f, F_MIN))
            levels, edp, edn, cq, ck, total = _hgrn_decays(g, reverse)
            qb = q.astype(BF16)
            kb = k.astype(BF16)
            a = jnp.zeros((c, c), F32)
            for lv in range(n_lvl - 1, -1, -1):
                if lv < n_lvl - 1:
                    e = levels[lv].astype(BF16)
                    ql = qb * e
                    kl = kb * e
                else:
                    ql = qb * edp.astype(BF16)
                    kl = kb * edn.astype(BF16)
                sc = lax.dot_general(ql, kl, nt, preferred_element_type=F32)
                a = jnp.where(mbool[lv], sc, a)
            st = st_ref[h]
            o_h = jnp.dot(a.astype(BF16), v, preferred_element_type=F32)
            o_h += lax.dot_general(qb * cq.astype(BF16), st.astype(BF16), nt, preferred_element_type=F32)
            st_ref[h] = st * total + lax.dot_general(v, kb * ck.astype(BF16), tn, preferred_element_type=F32)
            if final:
                o_h = o_h + of_ref[rows, sl]
                ms = jnp.mean(o_h * o_h, axis=-1, keepdims=True)
                o_h = o_h * lax.rsqrt(ms + EPS) * hnw_ref[...] * _silu(gate_ref[rows, sl])
            o_ref[rows, sl] = o_h.astype(o_ref.dtype)
        return 0

    lax.fori_loop(0, nsub, chunk, 0, unroll=2)

    @pl.when(j == nj - 1)
    def _():
        sfin_ref[...] = st_ref[...]


def _hgrn(p, lb, masks, s0, *, direction, width, o_fwd=None, hnw=None, nsub=8):
    b, t, _ = p.shape
    heads = width // HEAD_DIM
    nsub = _tile(t // CHUNK, nsub, 1)
    rows = nsub * CHUNK
    nj = t // rows
    reverse = direction == 1
    final = o_fwd is not None
    n_lvl = masks.shape[0]
    blk = (lambda j: nj - 1 - j) if reverse else (lambda j: j)
    zcol = 2 if reverse else 1
    in_specs = [
        pl.BlockSpec((None, rows, width), lambda bi, j: (bi, blk(j), 0)),
        pl.BlockSpec((None, rows, width), lambda bi, j: (bi, blk(j), zcol)),
        pl.BlockSpec((None, rows, width), lambda bi, j: (bi, blk(j), 3)),
        pl.BlockSpec((1, width), lambda bi, j: (0, 0)),
        pl.BlockSpec(masks.shape, lambda bi, j: (0, 0, 0)),
        pl.BlockSpec((None, heads, HEAD_DIM, HEAD_DIM), lambda bi, j: (bi, 0, 0, 0)),
    ]
    args = [p, p, p, lb.reshape(1, width), masks, s0]
    if final:
        in_specs += [
            pl.BlockSpec((None, rows, width), lambda bi, j: (bi, blk(j), 0)),
            pl.BlockSpec((None, rows, width), lambda bi, j: (bi, blk(j), 4)),
            pl.BlockSpec((1, HEAD_DIM), lambda bi, j: (0, 0)),
        ]
        args += [o_fwd, p, hnw.reshape(1, HEAD_DIM)]
    return pl.pallas_call(
        functools.partial(_hgrn_kernel, nsub=nsub, heads=heads, reverse=reverse, final=final, n_lvl=n_lvl),
        grid=(b, nj),
        in_specs=in_specs,
        out_specs=[
            pl.BlockSpec((None, rows, width), lambda bi, j: (bi, blk(j), 0)),
            pl.BlockSpec((None, heads, HEAD_DIM, HEAD_DIM), lambda bi, j: (bi, 0, 0, 0)),
        ],
        out_shape=[
            jax.ShapeDtypeStruct((b, t, width), F32),
            jax.ShapeDtypeStruct((b, heads, HEAD_DIM, HEAD_DIM), F32),
        ],
        scratch_shapes=[pltpu.VMEM((heads, HEAD_DIM, HEAD_DIM), F32)],
        compiler_params=_params(("arbitrary", "arbitrary")),
        name="hgrn_bwd" if final else "hgrn_fwd",
    )(*args)


def _mixout_kernel(x_ref, y_ref, u_ref, hg_ref, m_ref, dsk_ref, wglu_ref, bglu_ref, wo_ref, o_ref,
                   *, s5w, rb, colmajor):
    yy = y_ref[...] + dsk_ref[...] * u_ref[...]
    g = _gelu_tanh(yy)
    zz = jnp.dot(g.astype(BF16), wglu_ref[...], preferred_element_type=F32) + bglu_ref[...]
    s5 = (g * jax.nn.sigmoid(zz)).astype(BF16)
    if colmajor:
        hg = jnp.concatenate([hg_ref[:, r, :] for r in range(rb)], axis=0).astype(BF16)
    else:
        hg = hg_ref[...].astype(BF16)
    acc = jnp.dot(s5, wo_ref[:s5w, :], preferred_element_type=F32)
    acc += jnp.dot(hg, wo_ref[s5w:, :], preferred_element_type=F32)
    o_ref[...] = x_ref[...] + m_ref[0, 5:6, :] * acc


def _mixout(x3d, y3d, u3d, hg, mods, d_skip, w_glu, b_glu, w_o, *, layer, mod_of_batch, colmajor, rb=8):
    b, t, d = x3d.shape
    s5w = y3d.shape[-1]
    hw = hg.shape[-1]
    if colmajor:
        cols = GRID_W
        rows = t // cols
        rb = _tile(rows, rb, 8)
        tm = rb * cols
        hg_v = hg.reshape(b, cols, rows, hw)
        hg_spec = pl.BlockSpec((None, cols, rb, hw), lambda bi, i: (bi, 0, i, 0))
    else:
        rb = 1
        tm = _tile(t, 512)
        hg_v = hg
        hg_spec = pl.BlockSpec((None, tm, hw), lambda bi, i: (bi, i, 0))
    mod_fn = (lambda bi: bi) if mod_of_batch else (lambda bi: b)
    return pl.pallas_call(
        functools.partial(_mixout_kernel, s5w=s5w, rb=rb, colmajor=colmajor),
        grid=(b, t // tm),
        in_specs=[
            pl.BlockSpec((None, tm, d), lambda bi, i: (bi, i, 0)),
            pl.BlockSpec((None, tm, s5w), lambda bi, i: (bi, i, 0)),
            pl.BlockSpec((None, tm, s5w), lambda bi, i: (bi, i, 0)),
            hg_spec,
            pl.BlockSpec((1, N_MOD, d), lambda bi, i: (mod_fn(bi), 0, 0)),
            pl.BlockSpec((1, s5w), lambda bi, i: (0, 0)),
            pl.BlockSpec((None, s5w, s5w), lambda bi, i: (layer, 0, 0)),
            pl.BlockSpec((1, s5w), lambda bi, i: (0, 0)),
            pl.BlockSpec((None, s5w + hw, d), lambda bi, i: (layer, 0, 0)),
        ],
        out_specs=pl.BlockSpec((None, tm, d), lambda bi, i: (bi, i, 0)),
        out_shape=jax.ShapeDtypeStruct((b, t, d), F32),
        compiler_params=_params(("arbitrary", "arbitrary")),
        name="mixout",
    )(x3d, y3d, u3d, hg_v, mods, d_skip.reshape(1, s5w), w_glu, b_glu.reshape(1, s5w), w_o)


def kernel(x, c, ctx, c_ctx, w_ada, b_ada, norm_w, ffn_w_gate, ffn_w_up, ffn_w_down, w_in, w_out,
           s5_lambda_re, s5_lambda_im, s5_log_step, s5_b_re, s5_b_im, s5_c_re, s5_c_im, s5_d,
           s5_w_glu, s5_b_glu, hgrn_lower_bounds, hgrn_norm_w, final_norm_w):
    batch, seq, d = x.shape
    n_ctx = ctx.shape[1]
    depth = w_ada.shape[0]
    s5w = s5_d.shape[-1]
    hw = hgrn_lower_bounds.shape[-1]
    rows = seq // GRID_W
    assert batch < MOD_ROWS and rows % CHUNK == 0 and n_ctx % CHUNK == 0
    assert seq % S5_BLOCK == 0 and n_ctx % S5_BLOCK == 0 and (s5w // S5_GROUP) % S5_OCT == 0

    cvec = jnp.zeros((MOD_ROWS, d), F32).at[:batch].set(c.astype(F32)).at[batch].set(c_ctx.astype(F32))
    mods_all = _ada(cvec, w_ada, b_ada).reshape(depth, MOD_ROWS, N_MOD, d)

    lb_soft = jax.nn.softmax(hgrn_lower_bounds.astype(F32), axis=0)
    lb_all = jnp.cumsum(lb_soft, axis=0) - lb_soft[0]

    masks_np = _hgrn_masks()
    masks_f = jnp.asarray(masks_np, F32)
    masks_b = jnp.asarray(masks_np[:, ::-1, ::-1].copy(), F32)
    heads = hw // HEAD_DIM
    s_zero = jnp.zeros((batch, heads, HEAD_DIM, HEAD_DIM), F32)

    wg = ffn_w_gate.astype(BF16)
    wu = ffn_w_up.astype(BF16)
    wd = ffn_w_down.astype(BF16)
    w_in_b = w_in.astype(BF16)
    w_o = w_out.astype(BF16)
    w_glu = s5_w_glu.astype(BF16)

    xl = x.astype(F32)
    xc = ctx.astype(F32)
    lat = dict(rows_per_mod=seq, mod_off=0)
    cx = dict(rows_per_mod=batch * n_ctx, mod_off=batch)
    for l in range(depth):
        last = l == depth - 1
        mods = mods_all[l]

        xl = _ffn(xl.reshape(batch * seq, d), mods, norm_w[l, 0], wg, wu, wd, layer=l, half=0, base=0,
                  **lat).reshape(batch, seq, d)
        xc = _ffn(xc.reshape(batch * n_ctx, d), mods, norm_w[l, 0], wg, wu, wd, layer=l, half=0, base=0,
                  **cx).reshape(batch, n_ctx, d)

        u_l = _uproj(xl.reshape(batch * seq, d), mods, norm_w[l, 1], w_in_b, layer=l, s5w=s5w, **lat)
        u_c = _uproj(xc.reshape(batch * n_ctx, d), mods, norm_w[l, 1], w_in_b, layer=l, s5w=s5w, **cx)
        p_lat = _hproj(xl, mods, norm_w[l, 1], w_in_b, layer=l, s5w=s5w, cols=GRID_W, mod_of_batch=True,
                       cb=16, tn=hw)
        p_ctx = _hproj(xc, mods, norm_w[l, 1], w_in_b, layer=l, s5w=s5w, cols=1, mod_of_batch=False,
                       cb=1, tn=hw)

        kmat, we, wy, dec = _s5_prep(s5_lambda_re[l], s5_lambda_im[l], s5_log_step[l], s5_b_re[l],
                                     s5_b_im[l], s5_c_re[l], s5_c_im[l])
        e_l = _s5e(u_l, we)
        e_c = _s5e(u_c, we)
        st_c, st_l = _s5scan(e_c, e_l, dec, nb=batch)
        y_l = _s5y(u_l, st_l, kmat, wy).reshape(batch, seq, s5w)

        lb_f = lb_all[l, 0]
        lb_b = lb_all[l, 1]
        oc_f, sc_f = _hgrn(p_ctx, lb_f, masks_f, s_zero, direction=0, width=hw)
        ol_f, _ = _hgrn(p_lat, lb_f, masks_f, sc_f, direction=0, width=hw)
        hg_c, sc_b = _hgrn(p_ctx, lb_b, masks_b, s_zero, direction=1, width=hw,
                           o_fwd=oc_f, hnw=hgrn_norm_w[l])
        hg_l, _ = _hgrn(p_lat, lb_b, masks_b, sc_b, direction=1, width=hw,
                        o_fwd=ol_f, hnw=hgrn_norm_w[l])

        xl = _mixout(xl, y_l, u_l.reshape(batch, seq, s5w), hg_l, mods, s5_d[l], w_glu, s5_b_glu[l], w_o,
                     layer=l, mod_of_batch=True, colmajor=True)
        if not last:
            y_c = _s5y(u_c, st_c, kmat, wy).reshape(batch, n_ctx, s5w)
            xc = _mixout(xc, y_c, u_c.reshape(batch, n_ctx, s5w), hg_c, mods, s5_d[l], w_glu, s5_b_glu[l],
                         w_o, layer=l, mod_of_batch=False, colmajor=False)

        xl = _ffn(xl.reshape(batch * seq, d), mods, norm_w[l, 2], wg, wu, wd, layer=l, half=1, base=6,
                  final_nw=final_norm_w if last else None, **lat).reshape(batch, seq, d)
        if not last:
            xc = _ffn(xc.reshape(batch * n_ctx, d), mods, norm_w[l, 2], wg, wu, wd, layer=l, half=1, base=6,
                      **cx).reshape(batch, n_ctx, d)
    return xl
```

```python
import functools

import numpy as np
import jax
import jax.numpy as jnp
from jax import lax
from jax.experimental import pallas as pl
from jax.experimental.pallas import tpu as pltpu

F32 = jnp.float32
BF16 = jnp.bfloat16

EPS = 1e-6
F_MIN = 1e-6
LAMBDA_RE_MAX = -1e-4
GRID_W = 64
N_MOD = 9
S5_GROUP = 16
S5_STATE = 64
S5_BLOCK = 8
S5_OCT = 8
HEAD_DIM = 128
CHUNK = 64
SUB = 8
LANE = 128
MOD_ROWS = 8
VMEM_LIMIT = 60 * 1024 * 1024


def _params(sem):
    return pltpu.CompilerParams(dimension_semantics=sem, vmem_limit_bytes=VMEM_LIMIT)


def _tile(n, pref, mult=8):
    t = min(n, pref)
    while t > 0:
        if n % t == 0 and t % mult == 0:
            return t
        t -= 1
    return n


def _norm_mod(x, nw, shift, scale):
    ms = jnp.mean(x * x, axis=-1, keepdims=True)
    y = x * lax.rsqrt(ms + EPS) * nw
    return y * (1.0 + scale) + shift


def _silu(x):
    return x * jax.nn.sigmoid(x)


def _gelu_tanh(x):
    return 0.5 * x * (1.0 + jnp.tanh(0.7978845608028654 * (x + 0.044715 * (x * x * x))))


def _ada_kernel(c_ref, w_ref, b_ref, o_ref):
    a = _silu(c_ref[...])
    a_hi = a.astype(BF16)
    a_lo = (a - a_hi.astype(F32)).astype(BF16)
    w = w_ref[...]
    w_hi = w.astype(BF16)
    w_lo = (w - w_hi.astype(F32)).astype(BF16)
    acc = jnp.dot(a_hi, w_hi, preferred_element_type=F32)
    acc += jnp.dot(a_lo, w_hi, preferred_element_type=F32)
    acc += jnp.dot(a_hi, w_lo, preferred_element_type=F32)
    o_ref[...] = acc + b_ref[...]


def _ada(cvec, w_ada, b_ada):
    depth, d, n = w_ada.shape
    tn = _tile(n, 1024, LANE)
    return pl.pallas_call(
        _ada_kernel,
        grid=(depth, n // tn),
        in_specs=[
            pl.BlockSpec((MOD_ROWS, d), lambda l, j: (0, 0)),
            pl.BlockSpec((None, d, tn), lambda l, j: (l, 0, j)),
            pl.BlockSpec((None, 1, tn), lambda l, j: (l, 0, j)),
        ],
        out_specs=pl.BlockSpec((None, MOD_ROWS, tn), lambda l, j: (l, 0, j)),
        out_shape=jax.ShapeDtypeStruct((depth, MOD_ROWS, n), F32),
        compiler_params=_params(("arbitrary", "arbitrary")),
        name="ada",
    )(cvec, w_ada, b_ada.reshape(depth, 1, n))


def _ffn_kernel(*refs, base, n_f, final, tail):
    if final:
        x_ref, m_ref, nw_ref, wg_ref, wu_ref, wd_ref, fnw_ref, o_ref, h_ref = refs
    else:
        x_ref, m_ref, nw_ref, wg_ref, wu_ref, wd_ref, o_ref, h_ref = refs
    j = pl.program_id(1)

    @pl.when(j == 0)
    def _():
        h = _norm_mod(x_ref[...], nw_ref[...], m_ref[0, base:base + 1, :], m_ref[0, base + 1:base + 2, :])
        h_ref[...] = h.astype(BF16)

    def partial_sum(valid=None):
        h = h_ref[...]
        g = jnp.dot(h, wg_ref[...], preferred_element_type=F32)
        u = jnp.dot(h, wu_ref[...], preferred_element_type=F32)
        a = (_silu(g) * u).astype(BF16)
        wd = wd_ref[...]
        if valid is not None:
            a = jnp.where(lax.broadcasted_iota(jnp.int32, a.shape, 1) < valid, a, jnp.zeros_like(a))
            wd = jnp.where(lax.broadcasted_iota(jnp.int32, wd.shape, 0) < valid, wd, jnp.zeros_like(wd))
        return jnp.dot(a, wd, preferred_element_type=F32)

    @pl.when(j == 0)
    def _():
        o_ref[...] = partial_sum()

    @pl.when((j > 0) & (j < n_f - 1))
    def _():
        o_ref[...] += partial_sum()

    @pl.when(j == n_f - 1)
    def _():
        y = x_ref[...] + (0.5 * m_ref[0, base + 2:base + 3, :]) * (o_ref[...] + partial_sum(tail))
        if final:
            ms = jnp.mean(y * y, axis=-1, keepdims=True)
            y = y * lax.rsqrt(ms + EPS) * fnw_ref[...]
        o_ref[...] = y


def _ffn(x2d, mods, nw, wg, wu, wd, *, layer, half, base, rows_per_mod, mod_off, final_nw=None,
         tm=1024, tf=512):
    n, d = x2d.shape
    fp = wg.shape[-1]
    tm = _tile(min(n, rows_per_mod), tm)
    n_f = pl.cdiv(fp, tf)
    tail = fp - (n_f - 1) * tf if fp % tf else None
    assert n_f >= 2
    final = final_nw is not None
    in_specs = [
        pl.BlockSpec((tm, d), lambda i, j: (i, 0)),
        pl.BlockSpec((1, N_MOD, d), lambda i, j: ((i * tm) // rows_per_mod + mod_off, 0, 0)),
        pl.BlockSpec((1, d), lambda i, j: (0, 0)),
        pl.BlockSpec((None, None, d, tf), lambda i, j: (layer, half, 0, j)),
        pl.BlockSpec((None, None, d, tf), lambda i, j: (layer, half, 0, j)),
        pl.BlockSpec((None, None, tf, d), lambda i, j: (layer, half, j, 0)),
    ]
    args = [x2d, mods, nw.reshape(1, d), wg, wu, wd]
    if final:
        in_specs.append(pl.BlockSpec((1, d), lambda i, j: (0, 0)))
        args.append(final_nw.reshape(1, d))
    return pl.pallas_call(
        functools.partial(_ffn_kernel, base=base, n_f=n_f, final=final, tail=tail),
        grid=(n // tm, n_f),
        in_specs=in_specs,
        out_specs=pl.BlockSpec((tm, d), lambda i, j: (i, 0)),
        out_shape=jax.ShapeDtypeStruct((n, d), F32),
        scratch_shapes=[pltpu.VMEM((tm, d), BF16)],
        compiler_params=_params(("arbitrary", "arbitrary")),
        name="ffn_final" if final else "ffn",
    )(*args)


def _uproj_kernel(x_ref, m_ref, nw_ref, w_ref, u_ref):
    h = _norm_mod(x_ref[...], nw_ref[...], m_ref[0, 3:4, :], m_ref[0, 4:5, :])
    u_ref[...] = jnp.dot(h.astype(BF16), w_ref[...], preferred_element_type=F32)


def _uproj(x2d, mods, nw, w_in, *, layer, s5w, rows_per_mod, mod_off, tm=512):
    n, d = x2d.shape
    tm = _tile(min(n, rows_per_mod), tm)
    return pl.pallas_call(
        _uproj_kernel,
        grid=(n // tm,),
        in_specs=[
            pl.BlockSpec((tm, d), lambda i: (i, 0)),
            pl.BlockSpec((1, N_MOD, d), lambda i: ((i * tm) // rows_per_mod + mod_off, 0, 0)),
            pl.BlockSpec((1, d), lambda i: (0, 0)),
            pl.BlockSpec((None, d, s5w), lambda i: (layer, 0, 0)),
        ],
        out_specs=pl.BlockSpec((tm, s5w), lambda i: (i, 0)),
        out_shape=jax.ShapeDtypeStruct((n, s5w), F32),
        compiler_params=_params(("arbitrary",)),
        name="uproj",
    )(x2d, mods, nw.reshape(1, d), w_in)


def _hproj_kernel(*refs, colmajor):
    if colmajor:
        x_ref, m_ref, nw_ref, w_ref, perm_ref, o_ref, h_ref = refs
    else:
        x_ref, m_ref, nw_ref, w_ref, o_ref, h_ref = refs
    n = pl.program_id(2)

    @pl.when(n == 0)
    def _():
        h = _norm_mod(x_ref[...], nw_ref[...], m_ref[0, 3:4, :], m_ref[0, 4:5, :]).astype(BF16)
        if colmajor:
            h = h.reshape(h_ref.shape)
            h = jnp.dot(perm_ref[...], h, preferred_element_type=F32).astype(BF16)
        h_ref[...] = h

    p = jnp.dot(h_ref[...], w_ref[...], preferred_element_type=F32)

    @pl.when(n == 0)
    def _():
        o_ref[...] = _silu(p)

    @pl.when(n != 0)
    def _():
        o_ref[...] = p


def _hproj(x3d, mods, nw, w_in, *, layer, s5w, cols, mod_of_batch, cb, tn):
    b, t, d = x3d.shape
    rows = t // cols
    nh = w_in.shape[-1] - s5w
    assert s5w % tn == 0
    col0 = s5w // tn
    colmajor = cols > 1
    cb = _tile(cols, cb, 8) if colmajor else 1
    mod_fn = (lambda bi: bi) if mod_of_batch else (lambda bi: b)
    in_specs = [
        None,
        pl.BlockSpec((1, N_MOD, d), lambda bi, ci, n: (mod_fn(bi), 0, 0)),
        pl.BlockSpec((1, d), lambda bi, ci, n: (0, 0)),
        pl.BlockSpec((None, d, tn), lambda bi, ci, n: (layer, 0, col0 + n)),
    ]
    args = [None, mods, nw.reshape(1, d), w_in]
    if colmajor:
        args[0] = x3d.reshape(b, rows, cols, d)
        in_specs[0] = pl.BlockSpec((None, rows, cb, d), lambda bi, ci, n: (bi, 0, ci, 0))
        src = np.arange(rows * cb).reshape(rows, cb).T.reshape(-1)
        perm = np.zeros((rows * cb, rows * cb), np.float32)
        perm[np.arange(rows * cb), src] = 1.0
        args.append(jnp.asarray(perm, BF16))
        in_specs.append(pl.BlockSpec(perm.shape, lambda bi, ci, n: (0, 0)))
    else:
        args[0] = x3d
        in_specs[0] = pl.BlockSpec((None, rows, d), lambda bi, ci, n: (bi, 0, 0))
    return pl.pallas_call(
        functools.partial(_hproj_kernel, colmajor=colmajor),
        grid=(b, cols // cb, nh // tn),
        in_specs=in_specs,
        out_specs=pl.BlockSpec((None, cb * rows, tn), lambda bi, ci, n: (bi, ci, n)),
        out_shape=jax.ShapeDtypeStruct((b, t, nh), F32),
        scratch_shapes=[pltpu.VMEM((cb * rows, d), BF16)],
        compiler_params=_params(("arbitrary", "arbitrary", "arbitrary")),
        name="hproj",
    )(*args)


def _s5_sel_consts():
    t, h, p, o8 = S5_BLOCK, S5_GROUP, S5_STATE, S5_OCT
    colsel = np.zeros((t, 2, t, h, t, o8, h), np.float32)
    for r in range(t):
        for r2 in range(t):
            for hh in range(h):
                if r2 >= r:
                    colsel[r, 0, r2 - r, hh, r2, :, hh] = 1.0
                if r >= r2:
                    colsel[r, 1, r - r2, hh, r2, :, hh] = 1.0
    colsel = colsel.reshape(t, 2 * t * h, t * o8 * h)
    tile_e = np.zeros((4, p, 4, o8, p), np.float32)
    tile_y = np.zeros((4, h, 4, o8, h), np.float32)
    for dp in range(4):
        for i in range(p):
            tile_e[dp, i, dp, :, i] = 1.0
        for i in range(h):
            tile_y[dp, i, dp, :, i] = 1.0
    return colsel, tile_e.reshape(4 * p, 4 * o8 * p), tile_y.reshape(4 * h, 4 * o8 * h)


def _s5w_kernel(uk_ref, pb_ref, ca_ref, colsel_ref, tile_e_ref, tile_y_ref, k_ref, we_ref, wy_ref):
    t, h, p, o8 = S5_BLOCK, S5_GROUP, S5_STATE, S5_OCT
    gh = o8 * h
    gp = o8 * p

    def diag_mask(shape, row_div, col_mod, col_div):
        rg = lax.broadcasted_iota(jnp.int32, shape, 0) // row_div
        cg = (lax.broadcasted_iota(jnp.int32, shape, 1) % col_mod) // col_div
        return rg == cg

    mk = diag_mask((gh, t * gh), h, gh, h)
    me = diag_mask((gh, 4 * gp), h, gp, p)
    my = diag_mask((gp, 4 * gh), p, gh, h)
    uk = uk_ref[...].astype(BF16)
    for r in range(t):
        blk = jnp.dot(uk, colsel_ref[r], preferred_element_type=F32)
        k_ref[r * gh:(r + 1) * gh, :] = jnp.where(mk, blk, 0.0).astype(BF16)
        blk = jnp.dot(pb_ref[r].astype(BF16), tile_e_ref[...], preferred_element_type=F32)
        we_ref[r * gh:(r + 1) * gh, :] = jnp.where(me, blk, 0.0).astype(BF16)
        blk = jnp.where(my, jnp.dot(ca_ref[r].astype(BF16), tile_y_ref[...], preferred_element_type=F32), 0.0)
        for dp in range(4):
            wy_ref[dp * gp:(dp + 1) * gp, r * gh:(r + 1) * gh] = blk[:, dp * gh:(dp + 1) * gh].astype(BF16)


def _s5_compact(lam_re, lam_im, log_step, b_re, b_im, c_re, c_im):
    hp = lax.Precision.HIGHEST
    t = S5_BLOCK
    g, p = lam_re.shape[1:]
    h = b_re.shape[-1]
    n_oct = g // S5_OCT
    lam_re = jnp.minimum(lam_re.astype(F32), LAMBDA_RE_MAX)
    lam_im = lam_im.astype(F32)
    dt = jnp.exp(log_step.astype(F32))[..., None]
    mag = jnp.exp(lam_re * dt)
    lb_re = mag * jnp.cos(lam_im * dt)
    lb_im = mag * jnp.sin(lam_im * dt)
    den = lam_re * lam_re + lam_im * lam_im
    nr = lb_re - 1.0
    ni = lb_im
    cf_re = (nr * lam_re + ni * lam_im) / den
    cf_im = (ni * lam_re - nr * lam_im) / den
    b_re = b_re.astype(F32)
    b_im = b_im.astype(F32)
    br = cf_re[..., None] * b_re - cf_im[..., None] * b_im
    bi = cf_re[..., None] * b_im + cf_im[..., None] * b_re
    cr = c_re.astype(F32)
    ci = c_im.astype(F32)
    j = jnp.arange(t + 1, dtype=F32)[None, :, None, None]
    pmag = jnp.exp(j * (lam_re * dt)[:, None])
    pw_re = pmag * jnp.cos(j * (lam_im * dt)[:, None])
    pw_im = pmag * jnp.sin(j * (lam_im * dt)[:, None])
    pb_re = pw_re[..., None] * br[:, None] - pw_im[..., None] * bi[:, None]
    pb_im = pw_re[..., None] * bi[:, None] + pw_im[..., None] * br[:, None]
    kj = (jnp.einsum('dgep,djgph->djgeh', cr, pb_re[:, :t], precision=hp)
          - jnp.einsum('dgep,djgph->djgeh', ci, pb_im[:, :t], precision=hp))
    kj = kj.at[0, 0].add(kj[1, 0]).at[1, 0].set(0.0)
    uk = jnp.transpose(kj, (2, 4, 0, 1, 3)).reshape(n_oct, S5_OCT * h, 2 * t * h)
    r = np.arange(t)
    pb4 = jnp.stack([pb_re[0][t - 1 - r], pb_im[0][t - 1 - r], pb_re[1][r], pb_im[1][r]])
    pb3 = jnp.transpose(pb4.reshape(4, t, n_oct, S5_OCT, p, h), (2, 1, 3, 5, 0, 4))
    pb3 = pb3.reshape(n_oct, t, S5_OCT * h, 4 * p)

    def c_lam(d_, idx):
        pr = pw_re[d_][idx][:, :, None, :]
        pi = pw_im[d_][idx][:, :, None, :]
        return cr[d_][None] * pr - ci[d_][None] * pi, -(cr[d_][None] * pi + ci[d_][None] * pr)

    ca4 = jnp.stack(c_lam(0, r + 1) + c_lam(1, t - r))
    ca3 = jnp.transpose(ca4.reshape(4, t, n_oct, S5_OCT, h, p), (2, 1, 3, 5, 0, 4))
    ca3 = ca3.reshape(n_oct, t, S5_OCT * p, 4 * h)
    nk2 = S5_OCT * p // LANE
    a_re = jnp.transpose(pw_re[:, t].reshape(2, n_oct, nk2, LANE), (1, 0, 2, 3))
    a_im = jnp.transpose(pw_im[:, t].reshape(2, n_oct, nk2, LANE), (1, 0, 2, 3))
    dec = jnp.stack([jnp.concatenate([a_re, a_re], axis=2), jnp.concatenate([-a_im, a_im], axis=2)], axis=2)
    return uk, pb3, ca3, dec


def _s5_expand(uk, pb3, ca3):
    depth, n_oct = uk.shape[:2]
    colsel, tile_e, tile_y = (jnp.asarray(a, BF16) for a in _s5_sel_consts())
    kdim = S5_BLOCK * S5_OCT * S5_GROUP
    sdim = 4 * S5_OCT * S5_STATE
    return pl.pallas_call(
        _s5w_kernel,
        grid=(depth, n_oct),
        in_specs=[
            pl.BlockSpec((None, None) + uk.shape[2:], lambda l, o: (l, o, 0, 0)),
            pl.BlockSpec((None, None) + pb3.shape[2:], lambda l, o: (l, o, 0, 0, 0)),
            pl.BlockSpec((None, None) + ca3.shape[2:], lambda l, o: (l, o, 0, 0, 0)),
            pl.BlockSpec(colsel.shape, lambda l, o: (0, 0, 0)),
            pl.BlockSpec(tile_e.shape, lambda l, o: (0, 0)),
            pl.BlockSpec(tile_y.shape, lambda l, o: (0, 0)),
        ],
        out_specs=[
            pl.BlockSpec((None, None, kdim, kdim), lambda l, o: (l, o, 0, 0)),
            pl.BlockSpec((None, None, kdim, sdim), lambda l, o: (l, o, 0, 0)),
            pl.BlockSpec((None, None, sdim, kdim), lambda l, o: (l, o, 0, 0)),
        ],
        out_shape=[
            jax.ShapeDtypeStruct((depth, n_oct, kdim, kdim), BF16),
            jax.ShapeDtypeStruct((depth, n_oct, kdim, sdim), BF16),
            jax.ShapeDtypeStruct((depth, n_oct, sdim, kdim), BF16),
        ],
        compiler_params=_params(("arbitrary", "arbitrary")),
        name="s5w",
    )(uk, pb3, ca3, colsel, tile_e, tile_y)


def _block_rows(u_ref, tmr):
    return jnp.concatenate(
        [u_ref[pl.ds(r, tmr, stride=S5_BLOCK), :].astype(BF16) for r in range(S5_BLOCK)], axis=1)


def _s5e_kernel(u_ref, we_ref, e_ref, *, tmr):
    res = jnp.dot(_block_rows(u_ref, tmr), we_ref[...], preferred_element_type=F32)
    nk = res.shape[1] // (2 * LANE)
    for d in range(2):
        for k in range(nk):
            c0 = (d * nk + k) * LANE
            e_ref[d, pl.ds(k, tmr, stride=nk), :] = res[:, c0:c0 + LANE]


def _s5e(u, we, *, layer, tmr=1024):
    n, wid = u.shape
    m = n // S5_BLOCK
    _, n_oct, kdim, ncol = we.shape
    nk = ncol // (2 * LANE)
    tmr = _tile(m, tmr)
    return pl.pallas_call(
        functools.partial(_s5e_kernel, tmr=tmr),
        grid=(n_oct, m // tmr),
        in_specs=[
            pl.BlockSpec((tmr * S5_BLOCK, LANE), lambda o, i: (i, o)),
            pl.BlockSpec((None, None, kdim, ncol), lambda o, i: (layer, o, 0, 0)),
        ],
        out_specs=pl.BlockSpec((None, 2, tmr * nk, LANE), lambda o, i: (o, 0, i, 0)),
        out_shape=jax.ShapeDtypeStruct((n_oct, 2, m * nk, LANE), F32),
        compiler_params=_params(("arbitrary", "arbitrary")),
        name="s5e",
    )(u, we)


def _s5scan_kernel(ec_ref, el_ref, a_ref, sc_ref, sl_ref, *, nb, n_c, n_l, nk):
    rev = pl.program_id(1) == 1
    a1 = a_ref[0]
    a2 = a_ref[1]

    def run(e_ref, s_ref, n, carry):
        def body(s, carry):
            i = jnp.where(rev, n - 1 - s, s)
            new = []
            for b in range(nb):
                x = carry[b]
                rows = pl.ds(pl.multiple_of((b * n + i) * nk, nk), nk)
                s_ref[rows, :] = x
                new.append(a1 * x + a2 * pltpu.roll(x, nk // 2, 0) + e_ref[rows, :])
            return tuple(new)
        return lax.fori_loop(0, n, body, carry)

    carry = tuple(jnp.zeros((nk, LANE), F32) for _ in range(nb))
    carry = run(ec_ref, sc_ref, n_c, carry)
    run(el_ref, sl_ref, n_l, carry)


def _s5scan(e_ctx, e_lat, dec, *, layer, nb):
    n_oct, _, rc, _ = e_ctx.shape
    rl = e_lat.shape[2]
    nk = dec.shape[-2]
    return pl.pallas_call(
        functools.partial(_s5scan_kernel, nb=nb, n_c=rc // (nk * nb), n_l=rl // (nk * nb), nk=nk),
        grid=(n_oct, 2),
        in_specs=[
            pl.BlockSpec((None, None, rc, LANE), lambda o, d: (o, d, 0, 0)),
            pl.BlockSpec((None, None, rl, LANE), lambda o, d: (o, d, 0, 0)),
            pl.BlockSpec((None, None, None, 2, nk, LANE), lambda o, d: (layer, o, d, 0, 0, 0)),
        ],
        out_specs=[
            pl.BlockSpec((None, None, rc, LANE), lambda o, d: (o, d, 0, 0)),
            pl.BlockSpec((None, None, rl, LANE), lambda o, d: (o, d, 0, 0)),
        ],
        out_shape=[jax.ShapeDtypeStruct(e_ctx.shape, F32), jax.ShapeDtypeStruct(e_lat.shape, F32)],
        compiler_params=_params(("arbitrary", "arbitrary")),
        name="s5scan",
    )(e_ctx, e_lat, dec)


def _s5y_kernel(u_ref, s_ref, k_ref, wy_ref, y_ref, *, tmr):
    nk = s_ref.shape[1] // tmr
    st = jnp.concatenate(
        [s_ref[d, pl.ds(k, tmr, stride=nk), :].astype(BF16) for d in range(2) for k in range(nk)], axis=1)
    res = jnp.dot(_block_rows(u_ref, tmr), k_ref[...], preferred_element_type=F32)
    res += jnp.dot(st, wy_ref[...], preferred_element_type=F32)
    for r in range(S5_BLOCK):
        y_ref[pl.ds(r, tmr, stride=S5_BLOCK), :] = res[:, r * LANE:(r + 1) * LANE]


def _s5y(u, s, kmat, wy, *, layer, tmr=512):
    n, wid = u.shape
    m = n // S5_BLOCK
    _, n_oct, kdim, _ = kmat.shape
    sdim = wy.shape[2]
    nk = s.shape[2] // m
    tmr = _tile(m, tmr)
    return pl.pallas_call(
        functools.partial(_s5y_kernel, tmr=tmr),
        grid=(n_oct, m // tmr),
        in_specs=[
            pl.BlockSpec((tmr * S5_BLOCK, LANE), lambda o, i: (i, o)),
            pl.BlockSpec((None, 2, tmr * nk, LANE), lambda o, i: (o, 0, i, 0)),
            pl.BlockSpec((None, None, kdim, kdim), lambda o, i: (layer, o, 0, 0)),
            pl.BlockSpec((None, None, sdim, kdim), lambda o, i: (layer, o, 0, 0)),
        ],
        out_specs=pl.BlockSpec((tmr * S5_BLOCK, LANE), lambda o, i: (i, o)),
        out_shape=jax.ShapeDtypeStruct((n, wid), F32),
        compiler_params=_params(("arbitrary", "arbitrary")),
        name="s5y",
    )(u, s, kmat, wy)


def _hgrn_masks():
    c = CHUNK
    t = np.arange(c)
    masks = []
    half = c // 2
    while half >= SUB:
        par = 2 * half
        second = (t % par) >= half
        same_parent = (t[:, None] // par) == (t[None, :] // par)
        masks.append((same_parent & second[:, None] & (~second)[None, :]).astype(np.float32))
        half //= 2
    masks.append((((t[:, None] // SUB) == (t[None, :] // SUB)) & (t[None, :] <= t[:, None])).astype(np.float32))
    return np.stack(masks)


def _hgrn_decays(g, reverse):
    nb = CHUNK // SUB
    g3 = g.reshape(nb, SUB, g.shape[1])
    r = lax.broadcasted_iota(jnp.int32, g3.shape, 1)
    p = g3
    for k in (1, 2, 4):
        if reverse:
            p = p + jnp.where(r <= SUB - 1 - k, pltpu.roll(p, SUB - k, 1), 0.0)
        else:
            p = p + jnp.where(r >= k, pltpu.roll(p, k, 1), 0.0)
    last = 0 if reverse else SUB - 1
    ref = SUB // 2 if reverse else SUB // 2 - 1
    tot = jnp.broadcast_to(p[:, last:last + 1, :], p.shape)
    ep = jnp.exp(p)
    eq = jnp.exp(tot - p)
    ed = p - jnp.broadcast_to(p[:, ref:ref + 1, :], p.shape)
    edp = jnp.exp(ed)
    edn = jnp.exp(-ed)
    et = jnp.broadcast_to(ep[:, last:last + 1, :], p.shape)
    mem = (lambda i: nb - 1 - i) if reverse else (lambda i: i)
    epb = [ep[mem(i)] for i in range(nb)]
    eqb = [eq[mem(i)] for i in range(nb)]
    etb = [et[mem(i)] for i in range(nb)]

    def assemble(blocks):
        return jnp.concatenate([blocks[mem(i)] for i in range(nb)], axis=0)

    cq = [None] * nb
    acc = None
    for i in range(nb):
        cq[i] = epb[i] if acc is None else epb[i] * acc
        acc = etb[i] if acc is None else acc * etb[i]
    total = acc[0:1, :]
    ck = [None] * nb
    acc = None
    for i in range(nb - 1, -1, -1):
        ck[i] = eqb[i] if acc is None else eqb[i] * acc
        acc = etb[i] if acc is None else acc * etb[i]
    levels = []
    half = nb // 2
    while half >= 1:
        par = 2 * half
        blocks = []
        for i in range(nb):
            j = i % par
            if j >= half:
                f = epb[i]
                for m in range(i - j + half, i):
                    f = f * etb[m]
            else:
                f = eqb[i]
                for m in range(i + 1, i - j + half):
                    f = f * etb[m]
            blocks.append(f)
        levels.append(assemble(blocks))
        half //= 2
    return levels, edp.reshape(g.shape), edn.reshape(g.shape), assemble(cq), assemble(ck), total


def _hgrn_kernel(*refs, nsub, heads, reverse, final, n_lvl):
    if final:
        (q_ref, z_ref, v_ref, lb_ref, masks_ref, s0_ref, of_ref, gate_ref, hnw_ref,
         o_ref, sfin_ref, st_ref) = refs
    else:
        q_ref, z_ref, v_ref, lb_ref, masks_ref, s0_ref, o_ref, sfin_ref, st_ref = refs
    j = pl.program_id(1)
    nj = pl.num_programs(1)
    c = CHUNK
    hd = HEAD_DIM
    nt = (((1,), (1,)), ((), ()))
    tn = (((0,), (0,)), ((), ()))

    @pl.when(j == 0)
    def _():
        st_ref[...] = s0_ref[...]

    mbool = [masks_ref[lv] > 0.0 for lv in range(n_lvl)]

    mpair = [jnp.concatenate([m, m], axis=0) for m in mbool]
    zpad = jnp.zeros((c, hd), BF16)

    def chunk(s, _):
        cl = (nsub - 1 - s) if reverse else s
        rows = pl.ds(pl.multiple_of(cl * c, c), c)
        for hp in range(heads // 2):
            sl = slice(2 * hp * hd, (2 * hp + 2) * hd)
            z = z_ref[rows, sl]
            q = q_ref[rows, sl]
            v = v_ref[rows, sl].astype(BF16)
            lb = lb_ref[:, sl]
            f = lb + (1.0 - lb) * jax.nn.sigmoid(z)
            k = 1.0 - f
            g = jnp.log(jnp.maximum(f, F_MIN))
            levels, edp, edn, cq, ck, total = _hgrn_decays(g, reverse)
            qb = q.astype(BF16)
            kb = k.astype(BF16)
            a = jnp.zeros((2 * c, c), F32)
            for lv in range(n_lvl - 1, -1, -1):
                if lv < n_lvl - 1:
                    e = levels[lv].astype(BF16)
                    ql = qb * e
                    kl = kb * e
                else:
                    ql = qb * edp.astype(BF16)
                    kl = kb * edn.astype(BF16)
                lhs = jnp.concatenate([jnp.concatenate([ql[:, :hd], zpad], axis=1),
                                       jnp.concatenate([zpad, ql[:, hd:]], axis=1)], axis=0)
                sc = lax.dot_general(lhs, kl, nt, preferred_element_type=F32)
                a = jnp.where(mpair[lv], sc, a)
            ab = a.astype(BF16)
            qd = qb * cq.astype(BF16)
            kd = kb * ck.astype(BF16)
            for i in range(2):
                h = 2 * hp + i
                hs = slice(i * hd, (i + 1) * hd)
                st = st_ref[h]
                o_h = jnp.dot(ab[i * c:(i + 1) * c], v[:, hs], preferred_element_type=F32)
                o_h += lax.dot_general(qd[:, hs], st.astype(BF16), nt, preferred_element_type=F32)
                st_ref[h] = st * total[:, hs] + lax.dot_general(v[:, hs], kd[:, hs], tn,
                                                                 preferred_element_type=F32)
                osl = slice(h * hd, (h + 1) * hd)
                if final:
                    o_h = o_h + of_ref[rows, osl]
                    ms = jnp.mean(o_h * o_h, axis=-1, keepdims=True)
                    o_h = o_h * lax.rsqrt(ms + EPS) * hnw_ref[...] * _silu(gate_ref[rows, osl])
                o_ref[rows, osl] = o_h.astype(o_ref.dtype)
        return 0

    lax.fori_loop(0, nsub, chunk, 0, unroll=2)

    @pl.when(j == nj - 1)
    def _():
        sfin_ref[...] = st_ref[...]


def _hgrn(p, lb, masks, s0, *, direction, width, o_fwd=None, hnw=None, nsub=8):
    b, t, _ = p.shape
    heads = width // HEAD_DIM
    nsub = _tile(t // CHUNK, nsub, 1)
    rows = nsub * CHUNK
    nj = t // rows
    reverse = direction == 1
    final = o_fwd is not None
    n_lvl = masks.shape[0]
    blk = (lambda j: nj - 1 - j) if reverse else (lambda j: j)
    zcol = 2 if reverse else 1
    in_specs = [
        pl.BlockSpec((None, rows, width), lambda bi, j: (bi, blk(j), 0)),
        pl.BlockSpec((None, rows, width), lambda bi, j: (bi, blk(j), zcol)),
        pl.BlockSpec((None, rows, width), lambda bi, j: (bi, blk(j), 3)),
        pl.BlockSpec((1, width), lambda bi, j: (0, 0)),
        pl.BlockSpec(masks.shape, lambda bi, j: (0, 0, 0)),
        pl.BlockSpec((None, heads, HEAD_DIM, HEAD_DIM), lambda bi, j: (bi, 0, 0, 0)),
    ]
    args = [p, p, p, lb.reshape(1, width), masks, s0]
    if final:
        in_specs += [
            pl.BlockSpec((None, rows, width), lambda bi, j: (bi, blk(j), 0)),
            pl.BlockSpec((None, rows, width), lambda bi, j: (bi, blk(j), 4)),
            pl.BlockSpec((1, HEAD_DIM), lambda bi, j: (0, 0)),
        ]
        args += [o_fwd, p, hnw.reshape(1, HEAD_DIM)]
    return pl.pallas_call(
        functools.partial(_hgrn_kernel, nsub=nsub, heads=heads, reverse=reverse, final=final, n_lvl=n_lvl),
        grid=(b, nj),
        in_specs=in_specs,
        out_specs=[
            pl.BlockSpec((None, rows, width), lambda bi, j: (bi, blk(j), 0)),
            pl.BlockSpec((None, heads, HEAD_DIM, HEAD_DIM), lambda bi, j: (bi, 0, 0, 0)),
        ],
        out_shape=[
            jax.ShapeDtypeStruct((b, t, width), F32),
            jax.ShapeDtypeStruct((b, heads, HEAD_DIM, HEAD_DIM), F32),
        ],
        scratch_shapes=[pltpu.VMEM((heads, HEAD_DIM, HEAD_DIM), F32)],
        compiler_params=_params(("arbitrary", "arbitrary")),
        name="hgrn_bwd" if final else "hgrn_fwd",
    )(*args)


def _mixout_kernel(x_ref, y_ref, u_ref, hg_ref, m_ref, dsk_ref, wglu_ref, bglu_ref, wo_ref, o_ref,
                   *, s5w, rb, colmajor):
    yy = y_ref[...] + dsk_ref[...] * u_ref[...]
    g = _gelu_tanh(yy)
    zz = jnp.dot(g.astype(BF16), wglu_ref[...], preferred_element_type=F32) + bglu_ref[...]
    s5 = (g * jax.nn.sigmoid(zz)).astype(BF16)
    if colmajor:
        hg = jnp.concatenate([hg_ref[:, r, :] for r in range(rb)], axis=0).astype(BF16)
    else:
        hg = hg_ref[...].astype(BF16)
    acc = jnp.dot(s5, wo_ref[:s5w, :], preferred_element_type=F32)
    acc += jnp.dot(hg, wo_ref[s5w:, :], preferred_element_type=F32)
    o_ref[...] = x_ref[...] + m_ref[0, 5:6, :] * acc


def _mixout(x3d, y3d, u3d, hg, mods, d_skip, w_glu, b_glu, w_o, *, layer, mod_of_batch, colmajor, rb=8):
    b, t, d = x3d.shape
    s5w = y3d.shape[-1]
    hw = hg.shape[-1]
    if colmajor:
        cols = GRID_W
        rows = t // cols
        rb = _tile(rows, rb, 8)
        tm = rb * cols
        hg_v = hg.reshape(b, cols, rows, hw)
        hg_spec = pl.BlockSpec((None, cols, rb, hw), lambda bi, i: (bi, 0, i, 0))
    else:
        rb = 1
        tm = _tile(t, 512)
        hg_v = hg
        hg_spec = pl.BlockSpec((None, tm, hw), lambda bi, i: (bi, i, 0))
    mod_fn = (lambda bi: bi) if mod_of_batch else (lambda bi: b)
    return pl.pallas_call(
        functools.partial(_mixout_kernel, s5w=s5w, rb=rb, colmajor=colmajor),
        grid=(b, t // tm),
        in_specs=[
            pl.BlockSpec((None, tm, d), lambda bi, i: (bi, i, 0)),
            pl.BlockSpec((None, tm, s5w), lambda bi, i: (bi, i, 0)),
            pl.BlockSpec((None, tm, s5w), lambda bi, i: (bi, i, 0)),
            hg_spec,
            pl.BlockSpec((1, N_MOD, d), lambda bi, i: (mod_fn(bi), 0, 0)),
            pl.BlockSpec((1, s5w), lambda bi, i: (0, 0)),
            pl.BlockSpec((None, s5w, s5w), lambda bi, i: (layer, 0, 0)),
            pl.BlockSpec((1, s5w), lambda bi, i: (0, 0)),
            pl.BlockSpec((None, s5w + hw, d), lambda bi, i: (layer, 0, 0)),
        ],
        out_specs=pl.BlockSpec((None, tm, d), lambda bi, i: (bi, i, 0)),
        out_shape=jax.ShapeDtypeStruct((b, t, d), F32),
        compiler_params=_params(("arbitrary", "arbitrary")),
        name="mixout",
    )(x3d, y3d, u3d, hg_v, mods, d_skip.reshape(1, s5w), w_glu, b_glu.reshape(1, s5w), w_o)


def kernel(x, c, ctx, c_ctx, w_ada, b_ada, norm_w, ffn_w_gate, ffn_w_up, ffn_w_down, w_in, w_out,
           s5_lambda_re, s5_lambda_im, s5_log_step, s5_b_re, s5_b_im, s5_c_re, s5_c_im, s5_d,
           s5_w_glu, s5_b_glu, hgrn_lower_bounds, hgrn_norm_w, final_norm_w):
    batch, seq, d = x.shape
    n_ctx = ctx.shape[1]
    depth = w_ada.shape[0]
    s5w = s5_d.shape[-1]
    hw = hgrn_lower_bounds.shape[-1]
    rows = seq // GRID_W
    assert batch < MOD_ROWS and rows % CHUNK == 0 and n_ctx % CHUNK == 0
    assert seq % S5_BLOCK == 0 and n_ctx % S5_BLOCK == 0 and (s5w // S5_GROUP) % S5_OCT == 0

    cvec = jnp.zeros((MOD_ROWS, d), F32).at[:batch].set(c.astype(F32)).at[batch].set(c_ctx.astype(F32))
    mods_all = _ada(cvec, w_ada, b_ada).reshape(depth, MOD_ROWS, N_MOD, d)

    lb_soft = jax.nn.softmax(hgrn_lower_bounds.astype(F32), axis=0)
    lb_all = jnp.cumsum(lb_soft, axis=0) - lb_soft[0]

    masks_np = _hgrn_masks()
    masks_f = jnp.asarray(masks_np, F32)
    masks_b = jnp.asarray(masks_np[:, ::-1, ::-1].copy(), F32)
    heads = hw // HEAD_DIM
    s_zero = jnp.zeros((batch, heads, HEAD_DIM, HEAD_DIM), F32)

    wg = ffn_w_gate.astype(BF16)
    wu = ffn_w_up.astype(BF16)
    wd = ffn_w_down.astype(BF16)
    uk, pb3, ca3, dec = jax.vmap(_s5_compact)(s5_lambda_re, s5_lambda_im, s5_log_step, s5_b_re, s5_b_im,
                                              s5_c_re, s5_c_im)
    kmat, we, wy = _s5_expand(uk, pb3, ca3)
    w_in_b = w_in.astype(BF16)
    w_o = w_out.astype(BF16)
    w_glu = s5_w_glu.astype(BF16)

    xl = x.astype(F32)
    xc = ctx.astype(F32)
    lat = dict(rows_per_mod=seq, mod_off=0)
    cx = dict(rows_per_mod=batch * n_ctx, mod_off=batch)
    for l in range(depth):
        last = l == depth - 1
        mods = mods_all[l]

        xl = _ffn(xl.reshape(batch * seq, d), mods, norm_w[l, 0], wg, wu, wd, layer=l, half=0, base=0,
                  **lat).reshape(batch, seq, d)
        xc = _ffn(xc.reshape(batch * n_ctx, d), mods, norm_w[l, 0], wg, wu, wd, layer=l, half=0, base=0,
                  **cx).reshape(batch, n_ctx, d)

        u_l = _uproj(xl.reshape(batch * seq, d), mods, norm_w[l, 1], w_in_b, layer=l, s5w=s5w, **lat)
        u_c = _uproj(xc.reshape(batch * n_ctx, d), mods, norm_w[l, 1], w_in_b, layer=l, s5w=s5w, **cx)
        p_lat = _hproj(xl, mods, norm_w[l, 1], w_in_b, layer=l, s5w=s5w, cols=GRID_W, mod_of_batch=True,
                       cb=16, tn=hw)
        p_ctx = _hproj(xc, mods, norm_w[l, 1], w_in_b, layer=l, s5w=s5w, cols=1, mod_of_batch=False,
                       cb=1, tn=hw)

        e_l = _s5e(u_l, we, layer=l)
        e_c = _s5e(u_c, we, layer=l)
        st_c, st_l = _s5scan(e_c, e_l, dec, layer=l, nb=batch)
        y_l = _s5y(u_l, st_l, kmat, wy, layer=l).reshape(batch, seq, s5w)

        lb_f = lb_all[l, 0]
        lb_b = lb_all[l, 1]
        oc_f, sc_f = _hgrn(p_ctx, lb_f, masks_f, s_zero, direction=0, width=hw)
        ol_f, _ = _hgrn(p_lat, lb_f, masks_f, sc_f, direction=0, width=hw)
        hg_c, sc_b = _hgrn(p_ctx, lb_b, masks_b, s_zero, direction=1, width=hw,
                           o_fwd=oc_f, hnw=hgrn_norm_w[l])
        hg_l, _ = _hgrn(p_lat, lb_b, masks_b, sc_b, direction=1, width=hw,
                        o_fwd=ol_f, hnw=hgrn_norm_w[l])

        xl = _mixout(xl, y_l, u_l.reshape(batch, seq, s5w), hg_l, mods, s5_d[l], w_glu, s5_b_glu[l], w_o,
                     layer=l, mod_of_batch=True, colmajor=True)
        if not last:
            y_c = _s5y(u_c, st_c, kmat, wy, layer=l).reshape(batch, n_ctx, s5w)
            xc = _mixout(xc, y_c, u_c.reshape(batch, n_ctx, s5w), hg_c, mods, s5_d[l], w_glu, s5_b_glu[l],
                         w_o, layer=l, mod_of_batch=False, colmajor=False)

        xl = _ffn(xl.reshape(batch * seq, d), mods, norm_w[l, 2], wg, wu, wd, layer=l, half=1, base=6,
                  final_nw=final_norm_w if last else None, **lat).reshape(batch, seq, d)
        if not last:
            xc = _ffn(xc.reshape(batch * n_ctx, d), mods, norm_w[l, 2], wg, wu, wd, layer=l, half=1, base=6,
                      **cx).reshape(batch, n_ctx, d)
    return xl
```

```python
import functools

import numpy as np
import jax
import jax.numpy as jnp
from jax import lax
from jax.experimental import pallas as pl
from jax.experimental.pallas import tpu as pltpu

F32 = jnp.float32
BF16 = jnp.bfloat16

EPS = 1e-6
F_MIN = 1e-6
LAMBDA_RE_MAX = -1e-4
GRID_W = 64
N_MOD = 9
S5_GROUP = 16
S5_STATE = 64
S5_BLOCK = 8
S5_OCT = 8
HEAD_DIM = 128
CHUNK = 64
SUB = 8
LANE = 128
MOD_ROWS = 8
VMEM_LIMIT = 60 * 1024 * 1024


def _params(sem):
    return pltpu.CompilerParams(dimension_semantics=sem, vmem_limit_bytes=VMEM_LIMIT)


def _tile(n, pref, mult=8):
    t = min(n, pref)
    while t > 0:
        if n % t == 0 and t % mult == 0:
            return t
        t -= 1
    return n


def _norm_mod(x, nw, shift, scale):
    ms = jnp.mean(x * x, axis=-1, keepdims=True)
    y = x * lax.rsqrt(ms + EPS) * nw
    return y * (1.0 + scale) + shift


def _silu(x):
    return x * jax.nn.sigmoid(x)


def _gelu_tanh(x):
    return 0.5 * x * (1.0 + jnp.tanh(0.7978845608028654 * (x + 0.044715 * (x * x * x))))


def _ada_kernel(c_ref, w_ref, b_ref, o_ref):
    a = _silu(c_ref[...])
    a_hi = a.astype(BF16)
    a_lo = (a - a_hi.astype(F32)).astype(BF16)
    w = w_ref[...]
    w_hi = w.astype(BF16)
    w_lo = (w - w_hi.astype(F32)).astype(BF16)
    acc = jnp.dot(a_hi, w_hi, preferred_element_type=F32)
    acc += jnp.dot(a_lo, w_hi, preferred_element_type=F32)
    acc += jnp.dot(a_hi, w_lo, preferred_element_type=F32)
    o_ref[...] = acc + b_ref[...]


def _ada(cvec, w_ada, b_ada):
    depth, d, n = w_ada.shape
    tn = _tile(n, 2048, LANE)
    return pl.pallas_call(
        _ada_kernel,
        grid=(depth, n // tn),
        in_specs=[
            pl.BlockSpec((MOD_ROWS, d), lambda l, j: (0, 0)),
            pl.BlockSpec((None, d, tn), lambda l, j: (l, 0, j)),
            pl.BlockSpec((None, 1, tn), lambda l, j: (l, 0, j)),
        ],
        out_specs=pl.BlockSpec((None, MOD_ROWS, tn), lambda l, j: (l, 0, j)),
        out_shape=jax.ShapeDtypeStruct((depth, MOD_ROWS, n), F32),
        compiler_params=_params(("arbitrary", "arbitrary")),
        name="ada",
    )(cvec, w_ada, b_ada.reshape(depth, 1, n))


def _ffn_kernel(*refs, base, n_f, final, tail):
    if final:
        x_ref, m_ref, nw_ref, wg_ref, wu_ref, wd_ref, fnw_ref, o_ref, h_ref = refs
    else:
        x_ref, m_ref, nw_ref, wg_ref, wu_ref, wd_ref, o_ref, h_ref = refs
    j = pl.program_id(1)

    @pl.when(j == 0)
    def _():
        h = _norm_mod(x_ref[...], nw_ref[...], m_ref[0, base:base + 1, :], m_ref[0, base + 1:base + 2, :])
        h_ref[...] = h.astype(BF16)

    def partial_sum(valid=None):
        h = h_ref[...]
        g = jnp.dot(h, wg_ref[...], preferred_element_type=F32)
        u = jnp.dot(h, wu_ref[...], preferred_element_type=F32)
        a = (_silu(g) * u).astype(BF16)
        wd = wd_ref[...]
        if valid is not None:
            a = jnp.where(lax.broadcasted_iota(jnp.int32, a.shape, 1) < valid, a, jnp.zeros_like(a))
            wd = jnp.where(lax.broadcasted_iota(jnp.int32, wd.shape, 0) < valid, wd, jnp.zeros_like(wd))
        return jnp.dot(a, wd, preferred_element_type=F32)

    @pl.when(j == 0)
    def _():
        o_ref[...] = partial_sum()

    @pl.when((j > 0) & (j < n_f - 1))
    def _():
        o_ref[...] += partial_sum()

    @pl.when(j == n_f - 1)
    def _():
        y = x_ref[...] + (0.5 * m_ref[0, base + 2:base + 3, :]) * (o_ref[...] + partial_sum(tail))
        if final:
            ms = jnp.mean(y * y, axis=-1, keepdims=True)
            y = y * lax.rsqrt(ms + EPS) * fnw_ref[...]
        o_ref[...] = y


def _ffn(x2d, mods, nw, wg, wu, wd, *, layer, half, base, rows_per_mod, mod_off, final_nw=None,
         tm=1024, tf=512):
    n, d = x2d.shape
    fp = wg.shape[-1]
    tm = _tile(min(n, rows_per_mod), tm)
    n_f = pl.cdiv(fp, tf)
    tail = fp - (n_f - 1) * tf if fp % tf else None
    assert n_f >= 2
    final = final_nw is not None
    in_specs = [
        pl.BlockSpec((tm, d), lambda i, j: (i, 0)),
        pl.BlockSpec((1, N_MOD, d), lambda i, j: ((i * tm) // rows_per_mod + mod_off, 0, 0)),
        pl.BlockSpec((1, d), lambda i, j: (0, 0)),
        pl.BlockSpec((None, None, d, tf), lambda i, j: (layer, half, 0, j)),
        pl.BlockSpec((None, None, d, tf), lambda i, j: (layer, half, 0, j)),
        pl.BlockSpec((None, None, tf, d), lambda i, j: (layer, half, j, 0)),
    ]
    args = [x2d, mods, nw.reshape(1, d), wg, wu, wd]
    if final:
        in_specs.append(pl.BlockSpec((1, d), lambda i, j: (0, 0)))
        args.append(final_nw.reshape(1, d))
    return pl.pallas_call(
        functools.partial(_ffn_kernel, base=base, n_f=n_f, final=final, tail=tail),
        grid=(n // tm, n_f),
        in_specs=in_specs,
        out_specs=pl.BlockSpec((tm, d), lambda i, j: (i, 0)),
        out_shape=jax.ShapeDtypeStruct((n, d), F32),
        scratch_shapes=[pltpu.VMEM((tm, d), BF16)],
        compiler_params=_params(("arbitrary", "arbitrary")),
        name="ffn_final" if final else "ffn",
    )(*args)


def _uproj_kernel(x_ref, m_ref, nw_ref, w_ref, u_ref):
    h = _norm_mod(x_ref[...], nw_ref[...], m_ref[0, 3:4, :], m_ref[0, 4:5, :])
    u_ref[...] = jnp.dot(h.astype(BF16), w_ref[...], preferred_element_type=F32)


def _uproj(x2d, mods, nw, w_in, *, layer, s5w, rows_per_mod, mod_off, tm=1024):
    n, d = x2d.shape
    tm = _tile(min(n, rows_per_mod), tm)
    return pl.pallas_call(
        _uproj_kernel,
        grid=(n // tm,),
        in_specs=[
            pl.BlockSpec((tm, d), lambda i: (i, 0)),
            pl.BlockSpec((1, N_MOD, d), lambda i: ((i * tm) // rows_per_mod + mod_off, 0, 0)),
            pl.BlockSpec((1, d), lambda i: (0, 0)),
            pl.BlockSpec((None, d, s5w), lambda i: (layer, 0, 0)),
        ],
        out_specs=pl.BlockSpec((tm, s5w), lambda i: (i, 0)),
        out_shape=jax.ShapeDtypeStruct((n, s5w), F32),
        compiler_params=_params(("arbitrary",)),
        name="uproj",
    )(x2d, mods, nw.reshape(1, d), w_in)


def _hproj_kernel(*refs, colmajor, hw):
    if colmajor:
        x_ref, m_ref, nw_ref, w_ref, perm_ref, o_ref = refs
    else:
        x_ref, m_ref, nw_ref, w_ref, o_ref = refs
    h = _norm_mod(x_ref[...], nw_ref[...], m_ref[0, 3:4, :], m_ref[0, 4:5, :]).astype(BF16)
    if colmajor:
        h = h.reshape(perm_ref.shape[0], h.shape[-1])
        h = jnp.dot(perm_ref[...], h, preferred_element_type=F32).astype(BF16)
    for nb in range(o_ref.shape[-1] // hw):
        p = jnp.dot(h, w_ref[:, nb * hw:(nb + 1) * hw], preferred_element_type=F32)
        o_ref[:, nb * hw:(nb + 1) * hw] = _silu(p) if nb == 0 else p


def _hproj(x3d, mods, nw, w_h, *, layer, hw, cols, mod_row, cb=8, tr=512):
    b, t, d = x3d.shape
    nh = w_h.shape[-1]
    colmajor = cols > 1
    mod_fn = (lambda bi: bi) if mod_row is None else (lambda bi: mod_row)
    in_specs = [
        None,
        pl.BlockSpec((1, N_MOD, d), lambda bi, i: (mod_fn(bi), 0, 0)),
        pl.BlockSpec((1, d), lambda bi, i: (0, 0)),
        pl.BlockSpec((None, d, nh), lambda bi, i: (layer, 0, 0), pipeline_mode=pl.Buffered(1)),
    ]
    args = [None, mods, nw.reshape(1, d), w_h]
    if colmajor:
        rows = t // cols
        cb = _tile(cols, cb, 8)
        tr = rows * cb
        args[0] = x3d.reshape(b, rows, cols, d)
        in_specs[0] = pl.BlockSpec((None, rows, cb, d), lambda bi, i: (bi, 0, i, 0))
        src = np.arange(tr).reshape(rows, cb).T.reshape(-1)
        perm = np.zeros((tr, tr), np.float32)
        perm[np.arange(tr), src] = 1.0
        args.append(jnp.asarray(perm, BF16))
        in_specs.append(pl.BlockSpec(perm.shape, lambda bi, i: (0, 0)))
    else:
        tr = _tile(t, tr)
        args[0] = x3d
        in_specs[0] = pl.BlockSpec((None, tr, d), lambda bi, i: (bi, i, 0))
    return pl.pallas_call(
        functools.partial(_hproj_kernel, colmajor=colmajor, hw=hw),
        grid=(b, t // tr),
        in_specs=in_specs,
        out_specs=pl.BlockSpec((None, tr, nh), lambda bi, i: (bi, i, 0)),
        out_shape=jax.ShapeDtypeStruct((b, t, nh), F32),
        compiler_params=_params(("arbitrary", "arbitrary")),
        name="hproj",
    )(*args)


def _s5_sel_consts():
    t, h, p, o8 = S5_BLOCK, S5_GROUP, S5_STATE, S5_OCT
    colsel = np.zeros((t, 2, t, h, t, o8, h), np.float32)
    for r in range(t):
        for r2 in range(t):
            for hh in range(h):
                if r2 >= r:
                    colsel[r, 0, r2 - r, hh, r2, :, hh] = 1.0
                if r >= r2:
                    colsel[r, 1, r - r2, hh, r2, :, hh] = 1.0
    colsel = colsel.reshape(t, 2 * t * h, t * o8 * h)
    tile_e = np.zeros((4, p, 4, o8, p), np.float32)
    tile_y = np.zeros((4, h, 4, o8, h), np.float32)
    for dp in range(4):
        for i in range(p):
            tile_e[dp, i, dp, :, i] = 1.0
        for i in range(h):
            tile_y[dp, i, dp, :, i] = 1.0
    return colsel, tile_e.reshape(4 * p, 4 * o8 * p), tile_y.reshape(4 * h, 4 * o8 * h)


def _s5w_kernel(uk_ref, pb_ref, ca_ref, colsel_ref, tile_e_ref, tile_y_ref, k_ref, we_ref, wy_ref):
    t, h, p, o8 = S5_BLOCK, S5_GROUP, S5_STATE, S5_OCT
    gh = o8 * h
    gp = o8 * p

    def diag_mask(shape, row_div, col_mod, col_div):
        rg = lax.broadcasted_iota(jnp.int32, shape, 0) // row_div
        cg = (lax.broadcasted_iota(jnp.int32, shape, 1) % col_mod) // col_div
        return rg == cg

    mk = diag_mask((gh, t * gh), h, gh, h)
    me = diag_mask((gh, 4 * gp), h, gp, p)
    my = diag_mask((gp, 4 * gh), p, gh, h)
    uk = uk_ref[...].astype(BF16)
    for r in range(t):
        blk = jnp.dot(uk, colsel_ref[r], preferred_element_type=F32)
        k_ref[r * gh:(r + 1) * gh, :] = jnp.where(mk, blk, 0.0).astype(BF16)
        blk = jnp.dot(pb_ref[r].astype(BF16), tile_e_ref[...], preferred_element_type=F32)
        we_ref[r * gh:(r + 1) * gh, :] = jnp.where(me, blk, 0.0).astype(BF16)
        blk = jnp.where(my, jnp.dot(ca_ref[r].astype(BF16), tile_y_ref[...], preferred_element_type=F32), 0.0)
        for dp in range(4):
            wy_ref[dp * gp:(dp + 1) * gp, r * gh:(r + 1) * gh] = blk[:, dp * gh:(dp + 1) * gh].astype(BF16)


def _s5_compact(lam_re, lam_im, log_step, b_re, b_im, c_re, c_im):
    hp = lax.Precision.HIGHEST
    t = S5_BLOCK
    g, p = lam_re.shape[1:]
    h = b_re.shape[-1]
    n_oct = g // S5_OCT
    lam_re = jnp.minimum(lam_re.astype(F32), LAMBDA_RE_MAX)
    lam_im = lam_im.astype(F32)
    dt = jnp.exp(log_step.astype(F32))[..., None]
    mag = jnp.exp(lam_re * dt)
    lb_re = mag * jnp.cos(lam_im * dt)
    lb_im = mag * jnp.sin(lam_im * dt)
    den = lam_re * lam_re + lam_im * lam_im
    nr = lb_re - 1.0
    ni = lb_im
    cf_re = (nr * lam_re + ni * lam_im) / den
    cf_im = (ni * lam_re - nr * lam_im) / den
    b_re = b_re.astype(F32)
    b_im = b_im.astype(F32)
    br = cf_re[..., None] * b_re - cf_im[..., None] * b_im
    bi = cf_re[..., None] * b_im + cf_im[..., None] * b_re
    cr = c_re.astype(F32)
    ci = c_im.astype(F32)
    expo = np.concatenate([np.arange(t + 1), np.arange(t - 1, -1, -1), np.arange(t, 0, -1)]).astype(np.float32)
    j = jnp.asarray(expo)[None, :, None, None]
    pmag = jnp.exp(j * (lam_re * dt)[:, None])
    pw_re = pmag * jnp.cos(j * (lam_im * dt)[:, None])
    pw_im = pmag * jnp.sin(j * (lam_im * dt)[:, None])
    asc, desc0, desc1 = slice(0, t), slice(t + 1, 2 * t + 1), slice(2 * t + 1, 3 * t + 1)

    def lam_b(d_, sl):
        pr, pi = pw_re[d_, sl][..., None], pw_im[d_, sl][..., None]
        return pr * br[d_][None] - pi * bi[d_][None], pr * bi[d_][None] + pi * br[d_][None]

    pbf_re, pbf_im = lam_b(0, asc)
    pbb_re, pbb_im = lam_b(1, asc)
    pb_re = jnp.stack([pbf_re, pbb_re])
    pb_im = jnp.stack([pbf_im, pbb_im])
    kj = (jnp.einsum('dgep,djgph->djgeh', cr, pb_re, precision=hp)
          - jnp.einsum('dgep,djgph->djgeh', ci, pb_im, precision=hp))
    kj = kj.at[0, 0].add(kj[1, 0]).at[1, 0].set(0.0)
    uk = jnp.transpose(kj, (2, 4, 0, 1, 3)).reshape(n_oct, S5_OCT * h, 2 * t * h)
    pb4 = jnp.stack(lam_b(0, desc0) + (pbb_re, pbb_im))
    pb3 = jnp.transpose(pb4.reshape(4, t, n_oct, S5_OCT, p, h), (2, 1, 3, 5, 0, 4))
    pb3 = pb3.reshape(n_oct, t, S5_OCT * h, 4 * p)

    def c_lam(d_, sl):
        pr, pi = pw_re[d_, sl][:, :, None, :], pw_im[d_, sl][:, :, None, :]
        return cr[d_][None] * pr - ci[d_][None] * pi, -(cr[d_][None] * pi + ci[d_][None] * pr)

    ca4 = jnp.stack(c_lam(0, slice(1, t + 1)) + c_lam(1, desc1))
    ca3 = jnp.transpose(ca4.reshape(4, t, n_oct, S5_OCT, h, p), (2, 1, 3, 5, 0, 4))
    ca3 = ca3.reshape(n_oct, t, S5_OCT * p, 4 * h)
    nk2 = S5_OCT * p // LANE
    a_re = jnp.transpose(pw_re[:, t].reshape(2, n_oct, nk2, LANE), (1, 0, 2, 3))
    a_im = jnp.transpose(pw_im[:, t].reshape(2, n_oct, nk2, LANE), (1, 0, 2, 3))
    dec = jnp.stack([jnp.concatenate([a_re, a_re], axis=2), jnp.concatenate([-a_im, a_im], axis=2)], axis=2)
    return uk, pb3, ca3, dec


def _s5_expand(uk, pb3, ca3):
    depth, n_oct = uk.shape[:2]
    colsel, tile_e, tile_y = (jnp.asarray(a, BF16) for a in _s5_sel_consts())
    kdim = S5_BLOCK * S5_OCT * S5_GROUP
    sdim = 4 * S5_OCT * S5_STATE
    return pl.pallas_call(
        _s5w_kernel,
        grid=(depth, n_oct),
        in_specs=[
            pl.BlockSpec((None, None) + uk.shape[2:], lambda l, o: (l, o, 0, 0)),
            pl.BlockSpec((None, None) + pb3.shape[2:], lambda l, o: (l, o, 0, 0, 0)),
            pl.BlockSpec((None, None) + ca3.shape[2:], lambda l, o: (l, o, 0, 0, 0)),
            pl.BlockSpec(colsel.shape, lambda l, o: (0, 0, 0)),
            pl.BlockSpec(tile_e.shape, lambda l, o: (0, 0)),
            pl.BlockSpec(tile_y.shape, lambda l, o: (0, 0)),
        ],
        out_specs=[
            pl.BlockSpec((None, None, kdim, kdim), lambda l, o: (l, o, 0, 0)),
            pl.BlockSpec((None, None, kdim, sdim), lambda l, o: (l, o, 0, 0)),
            pl.BlockSpec((None, None, sdim, kdim), lambda l, o: (l, o, 0, 0)),
        ],
        out_shape=[
            jax.ShapeDtypeStruct((depth, n_oct, kdim, kdim), BF16),
            jax.ShapeDtypeStruct((depth, n_oct, kdim, sdim), BF16),
            jax.ShapeDtypeStruct((depth, n_oct, sdim, kdim), BF16),
        ],
        compiler_params=_params(("arbitrary", "arbitrary")),
        name="s5w",
    )(uk, pb3, ca3, colsel, tile_e, tile_y)


def _block_rows(u_ref, tmr):
    return jnp.concatenate(
        [u_ref[pl.ds(r, tmr, stride=S5_BLOCK), :].astype(BF16) for r in range(S5_BLOCK)], axis=1)


def _s5e_kernel(u_ref, we_ref, e_ref, *, tmr):
    res = jnp.dot(_block_rows(u_ref, tmr), we_ref[...], preferred_element_type=F32)
    nk = res.shape[1] // (2 * LANE)
    for d in range(2):
        for k in range(nk):
            c0 = (d * nk + k) * LANE
            e_ref[d, pl.ds(k, tmr, stride=nk), :] = res[:, c0:c0 + LANE]


def _s5e(u, we, *, layer, tmr=1024):
    n, wid = u.shape
    m = n // S5_BLOCK
    _, n_oct, kdim, ncol = we.shape
    nk = ncol // (2 * LANE)
    tmr = _tile(m, tmr)
    return pl.pallas_call(
        functools.partial(_s5e_kernel, tmr=tmr),
        grid=(n_oct, m // tmr),
        in_specs=[
            pl.BlockSpec((tmr * S5_BLOCK, LANE), lambda o, i: (i, o)),
            pl.BlockSpec((None, None, kdim, ncol), lambda o, i: (layer, o, 0, 0)),
        ],
        out_specs=pl.BlockSpec((None, 2, tmr * nk, LANE), lambda o, i: (o, 0, i, 0)),
        out_shape=jax.ShapeDtypeStruct((n_oct, 2, m * nk, LANE), F32),
        compiler_params=_params(("arbitrary", "arbitrary")),
        name="s5e",
    )(u, we)


def _s5scan_kernel(ec_ref, el_ref, a_ref, sc_ref, sl_ref, *, nb, n_c, n_l, nk):
    rev = pl.program_id(1) == 1
    a1 = a_ref[0]
    a2 = a_ref[1]

    def run(e_ref, s_ref, n, carry):
        def body(s, carry):
            i = jnp.where(rev, n - 1 - s, s)
            new = []
            for b in range(nb):
                x = carry[b]
                rows = pl.ds(pl.multiple_of((b * n + i) * nk, nk), nk)
                s_ref[rows, :] = x
                new.append(a1 * x + a2 * pltpu.roll(x, nk // 2, 0) + e_ref[rows, :])
            return tuple(new)
        return lax.fori_loop(0, n, body, carry, unroll=2)

    carry = tuple(jnp.zeros((nk, LANE), F32) for _ in range(nb))
    carry = run(ec_ref, sc_ref, n_c, carry)
    run(el_ref, sl_ref, n_l, carry)


def _s5scan(e_ctx, e_lat, dec, *, layer, nb):
    n_oct, _, rc, _ = e_ctx.shape
    rl = e_lat.shape[2]
    nk = dec.shape[-2]
    return pl.pallas_call(
        functools.partial(_s5scan_kernel, nb=nb, n_c=rc // (nk * nb), n_l=rl // (nk * nb), nk=nk),
        grid=(n_oct, 2),
        in_specs=[
            pl.BlockSpec((None, None, rc, LANE), lambda o, d: (o, d, 0, 0)),
            pl.BlockSpec((None, None, rl, LANE), lambda o, d: (o, d, 0, 0)),
            pl.BlockSpec((None, None, None, 2, nk, LANE), lambda o, d: (layer, o, d, 0, 0, 0)),
        ],
        out_specs=[
            pl.BlockSpec((None, None, rc, LANE), lambda o, d: (o, d, 0, 0)),
            pl.BlockSpec((None, None, rl, LANE), lambda o, d: (o, d, 0, 0)),
        ],
        out_shape=[jax.ShapeDtypeStruct(e_ctx.shape, F32), jax.ShapeDtypeStruct(e_lat.shape, F32)],
        compiler_params=_params(("arbitrary", "arbitrary")),
        name="s5scan",
    )(e_ctx, e_lat, dec)


def _s5y_kernel(u_ref, s_ref, k_ref, wy_ref, y_ref, *, tmr):
    nk = s_ref.shape[1] // tmr
    st = jnp.concatenate(
        [s_ref[d, pl.ds(k, tmr, stride=nk), :].astype(BF16) for d in range(2) for k in range(nk)], axis=1)
    res = jnp.dot(_block_rows(u_ref, tmr), k_ref[...], preferred_element_type=F32)
    res += jnp.dot(st, wy_ref[...], preferred_element_type=F32)
    for r in range(S5_BLOCK):
        y_ref[pl.ds(r, tmr, stride=S5_BLOCK), :] = res[:, r * LANE:(r + 1) * LANE]


def _s5y(u, s, kmat, wy, *, layer, tmr=512):
    n, wid = u.shape
    m = n // S5_BLOCK
    _, n_oct, kdim, _ = kmat.shape
    sdim = wy.shape[2]
    nk = s.shape[2] // m
    tmr = _tile(m, tmr)
    return pl.pallas_call(
        functools.partial(_s5y_kernel, tmr=tmr),
        grid=(n_oct, m // tmr),
        in_specs=[
            pl.BlockSpec((tmr * S5_BLOCK, LANE), lambda o, i: (i, o)),
            pl.BlockSpec((None, 2, tmr * nk, LANE), lambda o, i: (o, 0, i, 0)),
            pl.BlockSpec((None, None, kdim, kdim), lambda o, i: (layer, o, 0, 0)),
            pl.BlockSpec((None, None, sdim, kdim), lambda o, i: (layer, o, 0, 0)),
        ],
        out_specs=pl.BlockSpec((tmr * S5_BLOCK, LANE), lambda o, i: (i, o)),
        out_shape=jax.ShapeDtypeStruct((n, wid), F32),
        compiler_params=_params(("arbitrary", "arbitrary")),
        name="s5y",
    )(u, s, kmat, wy)


def _hgrn_masks():
    c = CHUNK
    t = np.arange(c)
    masks = []
    half = c // 2
    while half >= SUB:
        par = 2 * half
        second = (t % par) >= half
        same_parent = (t[:, None] // par) == (t[None, :] // par)
        masks.append((same_parent & second[:, None] & (~second)[None, :]).astype(np.float32))
        half //= 2
    masks.append((((t[:, None] // SUB) == (t[None, :] // SUB)) & (t[None, :] <= t[:, None])).astype(np.float32))
    return np.stack(masks)


def _hgrn_decays(g, reverse):
    nb = CHUNK // SUB
    g3 = g.reshape(nb, SUB, g.shape[1])
    r = lax.broadcasted_iota(jnp.int32, g3.shape, 1)
    p = g3
    for k in (1, 2, 4):
        if reverse:
            p = p + jnp.where(r <= SUB - 1 - k, pltpu.roll(p, SUB - k, 1), 0.0)
        else:
            p = p + jnp.where(r >= k, pltpu.roll(p, k, 1), 0.0)
    last = 0 if reverse else SUB - 1
    ref = SUB // 2 if reverse else SUB // 2 - 1
    tot = jnp.broadcast_to(p[:, last:last + 1, :], p.shape)
    ep = jnp.exp(p)
    eq = jnp.exp(tot - p)
    ed = p - jnp.broadcast_to(p[:, ref:ref + 1, :], p.shape)
    edp = jnp.exp(ed)
    edn = jnp.exp(-ed)
    et = jnp.broadcast_to(ep[:, last:last + 1, :], p.shape)
    mem = (lambda i: nb - 1 - i) if reverse else (lambda i: i)
    epb = [ep[mem(i)] for i in range(nb)]
    eqb = [eq[mem(i)] for i in range(nb)]
    etb = [et[mem(i)] for i in range(nb)]

    def assemble(blocks):
        return jnp.concatenate([blocks[mem(i)] for i in range(nb)], axis=0)

    cq = [None] * nb
    acc = None
    for i in range(nb):
        cq[i] = epb[i] if acc is None else epb[i] * acc
        acc = etb[i] if acc is None else acc * etb[i]
    total = acc[0:1, :]
    ck = [None] * nb
    acc = None
    for i in range(nb - 1, -1, -1):
        ck[i] = eqb[i] if acc is None else eqb[i] * acc
        acc = etb[i] if acc is None else acc * etb[i]
    levels = []
    half = nb // 2
    while half >= 1:
        par = 2 * half
        blocks = []
        for i in range(nb):
            j = i % par
            if j >= half:
                f = epb[i]
                for m in range(i - j + half, i):
                    f = f * etb[m]
            else:
                f = eqb[i]
                for m in range(i + 1, i - j + half):
                    f = f * etb[m]
            blocks.append(f)
        levels.append(assemble(blocks))
        half //= 2
    return levels, edp.reshape(g.shape), edn.reshape(g.shape), assemble(cq), assemble(ck), total


def _hgrn_kernel(*refs, nsub, heads, reverse, final, n_lvl):
    if final:
        (q_ref, z_ref, v_ref, lb_ref, masks_ref, s0_ref, of_ref, gate_ref, hnw_ref,
         o_ref, sfin_ref, st_ref) = refs
    else:
        q_ref, z_ref, v_ref, lb_ref, masks_ref, s0_ref, o_ref, sfin_ref, st_ref = refs
    j = pl.program_id(1)
    nj = pl.num_programs(1)
    c = CHUNK
    hd = HEAD_DIM
    nt = (((1,), (1,)), ((), ()))
    tn = (((0,), (0,)), ((), ()))

    @pl.when(j == 0)
    def _():
        st_ref[...] = s0_ref[...]

    mbool = [masks_ref[lv] > 0.0 for lv in range(n_lvl)]

    mpair = [jnp.concatenate([m, m], axis=0) for m in mbool]
    zpad = jnp.zeros((c, hd), BF16)

    def chunk(s, _):
        cl = (nsub - 1 - s) if reverse else s
        rows = pl.ds(pl.multiple_of(cl * c, c), c)
        for hp in range(heads // 2):
            sl = slice(2 * hp * hd, (2 * hp + 2) * hd)
            z = z_ref[rows, sl]
            q = q_ref[rows, sl]
            v = v_ref[rows, sl].astype(BF16)
            lb = lb_ref[:, sl]
            f = lb + (1.0 - lb) * jax.nn.sigmoid(z)
            k = 1.0 - f
            g = jnp.log(jnp.maximum(f, F_MIN))
            levels, edp, edn, cq, ck, total = _hgrn_decays(g, reverse)
            qb = q.astype(BF16)
            kb = k.astype(BF16)
            a = jnp.zeros((2 * c, c), F32)
            for lv in range(n_lvl - 1, -1, -1):
                if lv < n_lvl - 1:
                    e = levels[lv].astype(BF16)
                    ql = qb * e
                    kl = kb * e
                else:
                    ql = qb * edp.astype(BF16)
                    kl = kb * edn.astype(BF16)
                lhs = jnp.concatenate([jnp.concatenate([ql[:, :hd], zpad], axis=1),
                                       jnp.concatenate([zpad, ql[:, hd:]], axis=1)], axis=0)
                sc = lax.dot_general(lhs, kl, nt, preferred_element_type=F32)
                a = jnp.where(mpair[lv], sc, a)
            ab = a.astype(BF16)
            qd = qb * cq.astype(BF16)
            kd = kb * ck.astype(BF16)
            for i in range(2):
                h = 2 * hp + i
                hs = slice(i * hd, (i + 1) * hd)
                st = st_ref[h]
                o_h = jnp.dot(ab[i * c:(i + 1) * c], v[:, hs], preferred_element_type=F32)
                o_h += lax.dot_general(qd[:, hs], st.astype(BF16), nt, preferred_element_type=F32)
                st_ref[h] = st * total[:, hs] + lax.dot_general(v[:, hs], kd[:, hs], tn,
                                                                 preferred_element_type=F32)
                osl = slice(h * hd, (h + 1) * hd)
                if final:
                    o_h = o_h + of_ref[rows, osl]
                    ms = jnp.mean(o_h * o_h, axis=-1, keepdims=True)
                    o_h = o_h * lax.rsqrt(ms + EPS) * hnw_ref[...] * _silu(gate_ref[rows, osl])
                o_ref[rows, osl] = o_h.astype(o_ref.dtype)
        return 0

    lax.fori_loop(0, nsub, chunk, 0, unroll=2)

    @pl.when(j == nj - 1)
    def _():
        sfin_ref[...] = st_ref[...]


def _hgrn(p, lb, masks, s0, *, direction, width, o_fwd=None, hnw=None, nsub=8):
    b, t, _ = p.shape
    heads = width // HEAD_DIM
    nsub = _tile(t // CHUNK, nsub, 1)
    rows = nsub * CHUNK
    nj = t // rows
    reverse = direction == 1
    final = o_fwd is not None
    n_lvl = masks.shape[0]
    blk = (lambda j: nj - 1 - j) if reverse else (lambda j: j)
    zcol = 2 if reverse else 1
    in_specs = [
        pl.BlockSpec((None, rows, width), lambda bi, j: (bi, blk(j), 0)),
        pl.BlockSpec((None, rows, width), lambda bi, j: (bi, blk(j), zcol)),
        pl.BlockSpec((None, rows, width), lambda bi, j: (bi, blk(j), 3)),
        pl.BlockSpec((1, width), lambda bi, j: (0, 0)),
        pl.BlockSpec(masks.shape, lambda bi, j: (0, 0, 0)),
        pl.BlockSpec((None, heads, HEAD_DIM, HEAD_DIM), lambda bi, j: (bi, 0, 0, 0)),
    ]
    args = [p, p, p, lb.reshape(1, width), masks, s0]
    if final:
        in_specs += [
            pl.BlockSpec((None, rows, width), lambda bi, j: (bi, blk(j), 0)),
            pl.BlockSpec((None, rows, width), lambda bi, j: (bi, blk(j), 4)),
            pl.BlockSpec((1, HEAD_DIM), lambda bi, j: (0, 0)),
        ]
        args += [o_fwd, p, hnw.reshape(1, HEAD_DIM)]
    return pl.pallas_call(
        functools.partial(_hgrn_kernel, nsub=nsub, heads=heads, reverse=reverse, final=final, n_lvl=n_lvl),
        grid=(b, nj),
        in_specs=in_specs,
        out_specs=[
            pl.BlockSpec((None, rows, width), lambda bi, j: (bi, blk(j), 0)),
            pl.BlockSpec((None, heads, HEAD_DIM, HEAD_DIM), lambda bi, j: (bi, 0, 0, 0)),
        ],
        out_shape=[
            jax.ShapeDtypeStruct((b, t, width), F32),
            jax.ShapeDtypeStruct((b, heads, HEAD_DIM, HEAD_DIM), F32),
        ],
        scratch_shapes=[pltpu.VMEM((heads, HEAD_DIM, HEAD_DIM), F32)],
        compiler_params=_params(("arbitrary", "arbitrary")),
        name="hgrn_bwd" if final else "hgrn_fwd",
    )(*args)


def _mixout_kernel(x_ref, y_ref, u_ref, hg_ref, m_ref, dsk_ref, wglu_ref, bglu_ref, wo_ref, o_ref,
                   *, s5w, rb, colmajor):
    yy = y_ref[...] + dsk_ref[...] * u_ref[...]
    g = _gelu_tanh(yy)
    zz = jnp.dot(g.astype(BF16), wglu_ref[...], preferred_element_type=F32) + bglu_ref[...]
    s5 = (g * jax.nn.sigmoid(zz)).astype(BF16)
    if colmajor:
        hg = jnp.concatenate([hg_ref[:, r, :] for r in range(rb)], axis=0).astype(BF16)
    else:
        hg = hg_ref[...].astype(BF16)
    acc = jnp.dot(s5, wo_ref[:s5w, :], preferred_element_type=F32)
    acc += jnp.dot(hg, wo_ref[s5w:, :], preferred_element_type=F32)
    o_ref[...] = x_ref[...] + m_ref[0, 5:6, :] * acc


def _mixout(x3d, y3d, u3d, hg, mods, d_skip, w_glu, b_glu, w_o, *, layer, mod_of_batch, colmajor, rb=8):
    b, t, d = x3d.shape
    s5w = y3d.shape[-1]
    hw = hg.shape[-1]
    if colmajor:
        cols = GRID_W
        rows = t // cols
        rb = _tile(rows, rb, 8)
        tm = rb * cols
        hg_v = hg.reshape(b, cols, rows, hw)
        hg_spec = pl.BlockSpec((None, cols, rb, hw), lambda bi, i: (bi, 0, i, 0))
    else:
        rb = 1
        tm = _tile(t, 512)
        hg_v = hg
        hg_spec = pl.BlockSpec((None, tm, hw), lambda bi, i: (bi, i, 0))
    mod_fn = (lambda bi: bi) if mod_of_batch else (lambda bi: b)
    return pl.pallas_call(
        functools.partial(_mixout_kernel, s5w=s5w, rb=rb, colmajor=colmajor),
        grid=(b, t // tm),
        in_specs=[
            pl.BlockSpec((None, tm, d), lambda bi, i: (bi, i, 0)),
            pl.BlockSpec((None, tm, s5w), lambda bi, i: (bi, i, 0)),
            pl.BlockSpec((None, tm, s5w), lambda bi, i: (bi, i, 0)),
            hg_spec,
            pl.BlockSpec((1, N_MOD, d), lambda bi, i: (mod_fn(bi), 0, 0)),
            pl.BlockSpec((1, s5w), lambda bi, i: (0, 0)),
            pl.BlockSpec((None, s5w, s5w), lambda bi, i: (layer, 0, 0)),
            pl.BlockSpec((1, s5w), lambda bi, i: (0, 0)),
            pl.BlockSpec((None, s5w + hw, d), lambda bi, i: (layer, 0, 0)),
        ],
        out_specs=pl.BlockSpec((None, tm, d), lambda bi, i: (bi, i, 0)),
        out_shape=jax.ShapeDtypeStruct((b, t, d), F32),
        compiler_params=_params(("arbitrary", "arbitrary")),
        name="mixout",
    )(x3d, y3d, u3d, hg_v, mods, d_skip.reshape(1, s5w), w_glu, b_glu.reshape(1, s5w), w_o)


def kernel(x, c, ctx, c_ctx, w_ada, b_ada, norm_w, ffn_w_gate, ffn_w_up, ffn_w_down, w_in, w_out,
           s5_lambda_re, s5_lambda_im, s5_log_step, s5_b_re, s5_b_im, s5_c_re, s5_c_im, s5_d,
           s5_w_glu, s5_b_glu, hgrn_lower_bounds, hgrn_norm_w, final_norm_w):
    batch, seq, d = x.shape
    n_ctx = ctx.shape[1]
    depth = w_ada.shape[0]
    s5w = s5_d.shape[-1]
    hw = hgrn_lower_bounds.shape[-1]
    rows = seq // GRID_W
    assert batch < MOD_ROWS and rows % CHUNK == 0 and n_ctx % CHUNK == 0
    assert seq % S5_BLOCK == 0 and n_ctx % S5_BLOCK == 0 and (s5w // S5_GROUP) % S5_OCT == 0

    cvec = jnp.zeros((MOD_ROWS, d), F32).at[:batch].set(c.astype(F32)).at[batch].set(c_ctx.astype(F32))
    mods_all = _ada(cvec, w_ada, b_ada).reshape(depth, MOD_ROWS, N_MOD, d)

    lb_soft = jax.nn.softmax(hgrn_lower_bounds.astype(F32), axis=0)
    lb_all = jnp.cumsum(lb_soft, axis=0) - lb_soft[0]

    masks_np = _hgrn_masks()
    masks_f = jnp.asarray(masks_np, F32)
    masks_b = jnp.asarray(masks_np[:, ::-1, ::-1].copy(), F32)
    heads = hw // HEAD_DIM
    s_zero = jnp.zeros((batch, heads, HEAD_DIM, HEAD_DIM), F32)

    wg = ffn_w_gate.astype(BF16)
    wu = ffn_w_up.astype(BF16)
    wd = ffn_w_down.astype(BF16)
    uk, pb3, ca3, dec = jax.vmap(_s5_compact)(s5_lambda_re, s5_lambda_im, s5_log_step, s5_b_re, s5_b_im,
                                              s5_c_re, s5_c_im)
    kmat, we, wy = _s5_expand(uk, pb3, ca3)
    w_in_b = w_in.astype(BF16)
    w_h = w_in_b[:, :, s5w:]
    w_o = w_out.astype(BF16)
    w_glu = s5_w_glu.astype(BF16)

    xl = x.astype(F32)
    xc = ctx.astype(F32)
    lat = dict(rows_per_mod=seq, mod_off=0)
    cx = dict(rows_per_mod=batch * n_ctx, mod_off=batch)
    for l in range(depth):
        last = l == depth - 1
        mods = mods_all[l]

        xl = _ffn(xl.reshape(batch * seq, d), mods, norm_w[l, 0], wg, wu, wd, layer=l, half=0, base=0,
                  **lat).reshape(batch, seq, d)
        xc = _ffn(xc.reshape(batch * n_ctx, d), mods, norm_w[l, 0], wg, wu, wd, layer=l, half=0, base=0,
                  **cx).reshape(batch, n_ctx, d)

        u_l = _uproj(xl.reshape(batch * seq, d), mods, norm_w[l, 1], w_in_b, layer=l, s5w=s5w, **lat)
        u_c = _uproj(xc.reshape(batch * n_ctx, d), mods, norm_w[l, 1], w_in_b, layer=l, s5w=s5w, **cx)
        p_lat = _hproj(xl, mods, norm_w[l, 1], w_h, layer=l, hw=hw, cols=GRID_W, mod_row=None)
        p_ctx = _hproj(xc.reshape(1, batch * n_ctx, d), mods, norm_w[l, 1], w_h, layer=l, hw=hw, cols=1,
                       mod_row=batch).reshape(batch, n_ctx, 5 * hw)

        e_l = _s5e(u_l, we, layer=l)
        e_c = _s5e(u_c, we, layer=l)
        st_c, st_l = _s5scan(e_c, e_l, dec, layer=l, nb=batch)
        y_l = _s5y(u_l, st_l, kmat, wy, layer=l).reshape(batch, seq, s5w)

        lb_f = lb_all[l, 0]
        lb_b = lb_all[l, 1]
        oc_f, sc_f = _hgrn(p_ctx, lb_f, masks_f, s_zero, direction=0, width=hw)
        ol_f, _ = _hgrn(p_lat, lb_f, masks_f, sc_f, direction=0, width=hw)
        hg_c, sc_b = _hgrn(p_ctx, lb_b, masks_b, s_zero, direction=1, width=hw,
                           o_fwd=oc_f, hnw=hgrn_norm_w[l])
        hg_l, _ = _hgrn(p_lat, lb_b, masks_b, sc_b, direction=1, width=hw,
                        o_fwd=ol_f, hnw=hgrn_norm_w[l])

        xl = _mixout(xl, y_l, u_l.reshape(batch, seq, s5w), hg_l, mods, s5_d[l], w_glu, s5_b_glu[l], w_o,
                     layer=l, mod_of_batch=True, colmajor=True)
        if not last:
            y_c = _s5y(u_c, st_c, kmat, wy, layer=l).reshape(batch, n_ctx, s5w)
            xc = _mixout(xc, y_c, u_c.reshape(batch, n_ctx, s5w), hg_c, mods, s5_d[l], w_glu, s5_b_glu[l],
                         w_o, layer=l, mod_of_batch=False, colmajor=False)

        xl = _ffn(xl.reshape(batch * seq, d), mods, norm_w[l, 2], wg, wu, wd, layer=l, half=1, base=6,
                  final_nw=final_norm_w if last else None, **lat).reshape(batch, seq, d)
        if not last:
            xc = _ffn(xc.reshape(batch * n_ctx, d), mods, norm_w[l, 2], wg, wu, wd, layer=l, half=1, base=6,
                      **cx).reshape(batch, n_ctx, d)
    return xl
```

```python
import functools

import numpy as np
import jax
import jax.numpy as jnp
from jax import lax
from jax.experimental import pallas as pl
from jax.experimental.pallas import tpu as pltpu

F32 = jnp.float32
BF16 = jnp.bfloat16

EPS = 1e-6
F_MIN = 1e-6
LAMBDA_RE_MAX = -1e-4
GRID_W = 64
N_MOD = 9
S5_GROUP = 16
S5_STATE = 64
S5_BLOCK = 8
S5_OCT = 8
HEAD_DIM = 128
CHUNK = 64
SUB = 8
LANE = 128
MOD_ROWS = 8
VMEM_LIMIT = 60 * 1024 * 1024


def _params(sem):
    return pltpu.CompilerParams(dimension_semantics=sem, vmem_limit_bytes=VMEM_LIMIT)


def _tile(n, pref, mult=8):
    t = min(n, pref)
    while t > 0:
        if n % t == 0 and t % mult == 0:
            return t
        t -= 1
    return n


def _norm_mod(x, nw, shift, scale):
    ms = jnp.mean(x * x, axis=-1, keepdims=True)
    gain = nw * (1.0 + scale)
    return (x * lax.rsqrt(ms + EPS)) * gain + shift


def _silu(x):
    return x * jax.nn.sigmoid(x)


def _gelu_tanh(x):
    return 0.5 * x * (1.0 + jnp.tanh(0.7978845608028654 * (x + 0.044715 * (x * x * x))))


def _ada_kernel(c_ref, w_ref, b_ref, o_ref):
    a = _silu(c_ref[...])
    a_hi = a.astype(BF16)
    a_lo = (a - a_hi.astype(F32)).astype(BF16)
    w = w_ref[...]
    w_hi = w.astype(BF16)
    w_lo = (w - w_hi.astype(F32)).astype(BF16)
    acc = jnp.dot(a_hi, w_hi, preferred_element_type=F32)
    acc += jnp.dot(a_lo, w_hi, preferred_element_type=F32)
    acc += jnp.dot(a_hi, w_lo, preferred_element_type=F32)
    o_ref[...] = acc + b_ref[...]


def _ada(cvec, w_ada, b_ada):
    depth, d, n = w_ada.shape
    tn = _tile(n, 2048, LANE)
    return pl.pallas_call(
        _ada_kernel,
        grid=(depth, n // tn),
        in_specs=[
            pl.BlockSpec((MOD_ROWS, d), lambda l, j: (0, 0)),
            pl.BlockSpec((None, d, tn), lambda l, j: (l, 0, j)),
            pl.BlockSpec((None, 1, tn), lambda l, j: (l, 0, j)),
        ],
        out_specs=pl.BlockSpec((None, MOD_ROWS, tn), lambda l, j: (l, 0, j)),
        out_shape=jax.ShapeDtypeStruct((depth, MOD_ROWS, n), F32),
        compiler_params=_params(("arbitrary", "arbitrary")),
        name="ada",
    )(cvec, w_ada, b_ada.reshape(depth, 1, n))


def _ffn_kernel(*refs, base, n_f, final, tail):
    if final:
        x_ref, m_ref, nw_ref, wg_ref, wu_ref, wd_ref, fnw_ref, o_ref, h_ref = refs
    else:
        x_ref, m_ref, nw_ref, wg_ref, wu_ref, wd_ref, o_ref, h_ref = refs
    j = pl.program_id(1)

    @pl.when(j == 0)
    def _():
        h = _norm_mod(x_ref[...], nw_ref[...], m_ref[0, base:base + 1, :], m_ref[0, base + 1:base + 2, :])
        h_ref[...] = h.astype(BF16)

    def partial_sum(valid=None):
        h = h_ref[...]
        g = jnp.dot(h, wg_ref[...], preferred_element_type=F32)
        u = jnp.dot(h, wu_ref[...], preferred_element_type=F32)
        a = (_silu(g) * u).astype(BF16)
        wd = wd_ref[...]
        if valid is not None:
            a = jnp.where(lax.broadcasted_iota(jnp.int32, a.shape, 1) < valid, a, jnp.zeros_like(a))
            wd = jnp.where(lax.broadcasted_iota(jnp.int32, wd.shape, 0) < valid, wd, jnp.zeros_like(wd))
        return jnp.dot(a, wd, preferred_element_type=F32)

    @pl.when(j == 0)
    def _():
        o_ref[...] = partial_sum()

    @pl.when((j > 0) & (j < n_f - 1))
    def _():
        o_ref[...] += partial_sum()

    @pl.when(j == n_f - 1)
    def _():
        y = x_ref[...] + (0.5 * m_ref[0, base + 2:base + 3, :]) * (o_ref[...] + partial_sum(tail))
        if final:
            ms = jnp.mean(y * y, axis=-1, keepdims=True)
            y = y * lax.rsqrt(ms + EPS) * fnw_ref[...]
        o_ref[...] = y


def _ffn(x2d, mods, nw, wg, wu, wd, *, layer, half, base, rows_per_mod, mod_off, final_nw=None,
         tm=1024, tf=512):
    n, d = x2d.shape
    fp = wg.shape[-1]
    tm = _tile(min(n, rows_per_mod), tm)
    n_f = pl.cdiv(fp, tf)
    tail = fp - (n_f - 1) * tf if fp % tf else None
    assert n_f >= 2
    final = final_nw is not None
    in_specs = [
        pl.BlockSpec((tm, d), lambda i, j: (i, 0)),
        pl.BlockSpec((1, N_MOD, d), lambda i, j: ((i * tm) // rows_per_mod + mod_off, 0, 0)),
        pl.BlockSpec((1, d), lambda i, j: (0, 0)),
        pl.BlockSpec((None, None, d, tf), lambda i, j: (layer, half, 0, j)),
        pl.BlockSpec((None, None, d, tf), lambda i, j: (layer, half, 0, j)),
        pl.BlockSpec((None, None, tf, d), lambda i, j: (layer, half, j, 0)),
    ]
    args = [x2d, mods, nw.reshape(1, d), wg, wu, wd]
    if final:
        in_specs.append(pl.BlockSpec((1, d), lambda i, j: (0, 0)))
        args.append(final_nw.reshape(1, d))
    return pl.pallas_call(
        functools.partial(_ffn_kernel, base=base, n_f=n_f, final=final, tail=tail),
        grid=(n // tm, n_f),
        in_specs=in_specs,
        out_specs=pl.BlockSpec((tm, d), lambda i, j: (i, 0)),
        out_shape=jax.ShapeDtypeStruct((n, d), F32),
        scratch_shapes=[pltpu.VMEM((tm, d), BF16)],
        compiler_params=_params(("arbitrary", "arbitrary")),
        name="ffn_final" if final else "ffn",
    )(*args)


def _uproj_kernel(x_ref, m_ref, nw_ref, w_ref, u_ref):
    h = _norm_mod(x_ref[...], nw_ref[...], m_ref[0, 3:4, :], m_ref[0, 4:5, :])
    u_ref[...] = jnp.dot(h.astype(BF16), w_ref[...], preferred_element_type=F32)


def _uproj(x2d, mods, nw, w_in, *, layer, s5w, rows_per_mod, mod_off, tm=1024):
    n, d = x2d.shape
    tm = _tile(min(n, rows_per_mod), tm)
    return pl.pallas_call(
        _uproj_kernel,
        grid=(n // tm,),
        in_specs=[
            pl.BlockSpec((tm, d), lambda i: (i, 0)),
            pl.BlockSpec((1, N_MOD, d), lambda i: ((i * tm) // rows_per_mod + mod_off, 0, 0)),
            pl.BlockSpec((1, d), lambda i: (0, 0)),
            pl.BlockSpec((None, d, s5w), lambda i: (layer, 0, 0)),
        ],
        out_specs=pl.BlockSpec((tm, s5w), lambda i: (i, 0)),
        out_shape=jax.ShapeDtypeStruct((n, s5w), F32),
        compiler_params=_params(("arbitrary",)),
        name="uproj",
    )(x2d, mods, nw.reshape(1, d), w_in)


def _hproj_kernel(*refs, colmajor, hw):
    if colmajor:
        x_ref, m_ref, nw_ref, w_ref, perm_ref, o_ref = refs
    else:
        x_ref, m_ref, nw_ref, w_ref, o_ref = refs
    h = _norm_mod(x_ref[...], nw_ref[...], m_ref[0, 3:4, :], m_ref[0, 4:5, :]).astype(BF16)
    if colmajor:
        h = h.reshape(perm_ref.shape[0], h.shape[-1])
        h = jnp.dot(perm_ref[...], h, preferred_element_type=F32).astype(BF16)
    for nb in range(o_ref.shape[-1] // hw):
        p = jnp.dot(h, w_ref[:, nb * hw:(nb + 1) * hw], preferred_element_type=F32)
        o_ref[:, nb * hw:(nb + 1) * hw] = _silu(p) if nb == 0 else p


def _hproj(x3d, mods, nw, w_h, *, layer, hw, cols, mod_row, cb=8, tr=512):
    b, t, d = x3d.shape
    nh = w_h.shape[-1]
    colmajor = cols > 1
    mod_fn = (lambda bi: bi) if mod_row is None else (lambda bi: mod_row)
    in_specs = [
        None,
        pl.BlockSpec((1, N_MOD, d), lambda bi, i: (mod_fn(bi), 0, 0)),
        pl.BlockSpec((1, d), lambda bi, i: (0, 0)),
        pl.BlockSpec((None, d, nh), lambda bi, i: (layer, 0, 0), pipeline_mode=pl.Buffered(1)),
    ]
    args = [None, mods, nw.reshape(1, d), w_h]
    if colmajor:
        rows = t // cols
        cb = _tile(cols, cb, 8)
        tr = rows * cb
        args[0] = x3d.reshape(b, rows, cols, d)
        in_specs[0] = pl.BlockSpec((None, rows, cb, d), lambda bi, i: (bi, 0, i, 0))
        src = np.arange(tr).reshape(rows, cb).T.reshape(-1)
        perm = np.zeros((tr, tr), np.float32)
        perm[np.arange(tr), src] = 1.0
        args.append(jnp.asarray(perm, BF16))
        in_specs.append(pl.BlockSpec(perm.shape, lambda bi, i: (0, 0)))
    else:
        tr = _tile(t, tr)
        args[0] = x3d
        in_specs[0] = pl.BlockSpec((None, tr, d), lambda bi, i: (bi, i, 0))
    return pl.pallas_call(
        functools.partial(_hproj_kernel, colmajor=colmajor, hw=hw),
        grid=(b, t // tr),
        in_specs=in_specs,
        out_specs=pl.BlockSpec((None, tr, nh), lambda bi, i: (bi, i, 0)),
        out_shape=jax.ShapeDtypeStruct((b, t, nh), F32),
        compiler_params=_params(("arbitrary", "arbitrary")),
        name="hproj",
    )(*args)


def _s5_sel_consts():
    t, h, p, o8 = S5_BLOCK, S5_GROUP, S5_STATE, S5_OCT
    colsel = np.zeros((t, 2, t, h, t, o8, h), np.float32)
    for r in range(t):
        for r2 in range(t):
            for hh in range(h):
                if r2 >= r:
                    colsel[r, 0, r2 - r, hh, r2, :, hh] = 1.0
                if r >= r2:
                    colsel[r, 1, r - r2, hh, r2, :, hh] = 1.0
    colsel = colsel.reshape(t, 2 * t * h, t * o8 * h)
    tile_e = np.zeros((4, p, 4, o8, p), np.float32)
    tile_y = np.zeros((4, h, 4, o8, h), np.float32)
    for dp in range(4):
        for i in range(p):
            tile_e[dp, i, dp, :, i] = 1.0
        for i in range(h):
            tile_y[dp, i, dp, :, i] = 1.0
    return colsel, tile_e.reshape(4 * p, 4 * o8 * p), tile_y.reshape(4 * h, 4 * o8 * h)


def _s5w_kernel(uk_ref, pb_ref, ca_ref, colsel_ref, tile_e_ref, tile_y_ref, k_ref, we_ref, wy_ref):
    t, h, p, o8 = S5_BLOCK, S5_GROUP, S5_STATE, S5_OCT
    gh = o8 * h
    gp = o8 * p

    def diag_mask(shape, row_div, col_mod, col_div):
        rg = lax.broadcasted_iota(jnp.int32, shape, 0) // row_div
        cg = (lax.broadcasted_iota(jnp.int32, shape, 1) % col_mod) // col_div
        return rg == cg

    mk = diag_mask((gh, t * gh), h, gh, h)
    me = diag_mask((gh, 4 * gp), h, gp, p)
    my = diag_mask((gp, 4 * gh), p, gh, h)
    uk = uk_ref[...].astype(BF16)
    for r in range(t):
        blk = jnp.dot(uk, colsel_ref[r], preferred_element_type=F32)
        k_ref[r * gh:(r + 1) * gh, :] = jnp.where(mk, blk, 0.0).astype(BF16)
        blk = jnp.dot(pb_ref[r].astype(BF16), tile_e_ref[...], preferred_element_type=F32)
        we_ref[r * gh:(r + 1) * gh, :] = jnp.where(me, blk, 0.0).astype(BF16)
        blk = jnp.where(my, jnp.dot(ca_ref[r].astype(BF16), tile_y_ref[...], preferred_element_type=F32), 0.0)
        for dp in range(4):
            wy_ref[dp * gp:(dp + 1) * gp, r * gh:(r + 1) * gh] = blk[:, dp * gh:(dp + 1) * gh].astype(BF16)


def _s5_compact(lam_re, lam_im, log_step, b_re, b_im, c_re, c_im):
    hp = lax.Precision.HIGHEST
    t = S5_BLOCK
    g, p = lam_re.shape[1:]
    h = b_re.shape[-1]
    n_oct = g // S5_OCT
    lam_re = jnp.minimum(lam_re.astype(F32), LAMBDA_RE_MAX)
    lam_im = lam_im.astype(F32)
    dt = jnp.exp(log_step.astype(F32))[..., None]
    mag = jnp.exp(lam_re * dt)
    lb_re = mag * jnp.cos(lam_im * dt)
    lb_im = mag * jnp.sin(lam_im * dt)
    den = lam_re * lam_re + lam_im * lam_im
    nr = lb_re - 1.0
    ni = lb_im
    cf_re = (nr * lam_re + ni * lam_im) / den
    cf_im = (ni * lam_re - nr * lam_im) / den
    b_re = b_re.astype(F32)
    b_im = b_im.astype(F32)
    br = cf_re[..., None] * b_re - cf_im[..., None] * b_im
    bi = cf_re[..., None] * b_im + cf_im[..., None] * b_re
    cr = c_re.astype(F32)
    ci = c_im.astype(F32)
    expo = np.concatenate([np.arange(t + 1), np.arange(t - 1, -1, -1), np.arange(t, 0, -1)]).astype(np.float32)
    j = jnp.asarray(expo)[None, :, None, None]
    pmag = jnp.exp(j * (lam_re * dt)[:, None])
    pw_re = pmag * jnp.cos(j * (lam_im * dt)[:, None])
    pw_im = pmag * jnp.sin(j * (lam_im * dt)[:, None])
    asc, desc0, desc1 = slice(0, t), slice(t + 1, 2 * t + 1), slice(2 * t + 1, 3 * t + 1)

    def lam_b(d_, sl):
        pr, pi = pw_re[d_, sl][..., None], pw_im[d_, sl][..., None]
        return pr * br[d_] - pi * bi[d_], pr * bi[d_] + pi * br[d_]

    pbf_re, pbf_im = lam_b(0, asc)
    pbb_re, pbb_im = lam_b(1, asc)
    pb_re = jnp.stack([pbf_re, pbb_re])
    pb_im = jnp.stack([pbf_im, pbb_im])
    kj = (jnp.einsum('dgep,djgph->djgeh', cr, pb_re, precision=hp)
          - jnp.einsum('dgep,djgph->djgeh', ci, pb_im, precision=hp))
    kj = jnp.stack([jnp.concatenate([kj[0, :1] + kj[1, :1], kj[0, 1:]], axis=0),
                    jnp.concatenate([jnp.zeros_like(kj[1, :1]), kj[1, 1:]], axis=0)])
    uk = jnp.transpose(kj, (2, 4, 0, 1, 3)).reshape(n_oct, S5_OCT * h, 2 * t * h)
    pb4 = jnp.stack(lam_b(0, desc0) + (pbb_re, pbb_im))
    pb3 = jnp.transpose(pb4.reshape(4, t, n_oct, S5_OCT, p, h), (2, 1, 3, 5, 0, 4))
    pb3 = pb3.reshape(n_oct, t, S5_OCT * h, 4 * p)

    def c_lam(d_, sl):
        pr, pi = pw_re[d_, sl][:, :, None, :], pw_im[d_, sl][:, :, None, :]
        return cr[d_] * pr - ci[d_] * pi, -(cr[d_] * pi + ci[d_] * pr)

    ca4 = jnp.stack(c_lam(0, slice(1, t + 1)) + c_lam(1, desc1))
    ca3 = jnp.transpose(ca4.reshape(4, t, n_oct, S5_OCT, h, p), (2, 1, 3, 5, 0, 4))
    ca3 = ca3.reshape(n_oct, t, S5_OCT * p, 4 * h)
    nk2 = S5_OCT * p // LANE
    a_re = jnp.transpose(pw_re[:, t].reshape(2, n_oct, nk2, LANE), (1, 0, 2, 3))
    a_im = jnp.transpose(pw_im[:, t].reshape(2, n_oct, nk2, LANE), (1, 0, 2, 3))
    dec = jnp.stack([jnp.concatenate([a_re, a_re], axis=2), jnp.concatenate([-a_im, a_im], axis=2)], axis=2)
    return uk, pb3, ca3, dec


def _s5_expand(uk, pb3, ca3):
    depth, n_oct = uk.shape[:2]
    colsel, tile_e, tile_y = (jnp.asarray(a, BF16) for a in _s5_sel_consts())
    kdim = S5_BLOCK * S5_OCT * S5_GROUP
    sdim = 4 * S5_OCT * S5_STATE
    return pl.pallas_call(
        _s5w_kernel,
        grid=(depth, n_oct),
        in_specs=[
            pl.BlockSpec((None, None) + uk.shape[2:], lambda l, o: (l, o, 0, 0)),
            pl.BlockSpec((None, None) + pb3.shape[2:], lambda l, o: (l, o, 0, 0, 0)),
            pl.BlockSpec((None, None) + ca3.shape[2:], lambda l, o: (l, o, 0, 0, 0)),
            pl.BlockSpec(colsel.shape, lambda l, o: (0, 0, 0)),
            pl.BlockSpec(tile_e.shape, lambda l, o: (0, 0)),
            pl.BlockSpec(tile_y.shape, lambda l, o: (0, 0)),
        ],
        out_specs=[
            pl.BlockSpec((None, None, kdim, kdim), lambda l, o: (l, o, 0, 0)),
            pl.BlockSpec((None, None, kdim, sdim), lambda l, o: (l, o, 0, 0)),
            pl.BlockSpec((None, None, sdim, kdim), lambda l, o: (l, o, 0, 0)),
        ],
        out_shape=[
            jax.ShapeDtypeStruct((depth, n_oct, kdim, kdim), BF16),
            jax.ShapeDtypeStruct((depth, n_oct, kdim, sdim), BF16),
            jax.ShapeDtypeStruct((depth, n_oct, sdim, kdim), BF16),
        ],
        compiler_params=_params(("arbitrary", "arbitrary")),
        name="s5w",
    )(uk, pb3, ca3, colsel, tile_e, tile_y)


def _block_rows(u_ref, tmr):
    return jnp.concatenate(
        [u_ref[pl.ds(r, tmr, stride=S5_BLOCK), :].astype(BF16) for r in range(S5_BLOCK)], axis=1)


def _s5e_kernel(u_ref, we_ref, e_ref, *, tmr):
    res = jnp.dot(_block_rows(u_ref, tmr), we_ref[...], preferred_element_type=F32)
    nk = res.shape[1] // (2 * LANE)
    for d in range(2):
        for k in range(nk):
            c0 = (d * nk + k) * LANE
            e_ref[d, pl.ds(k, tmr, stride=nk), :] = res[:, c0:c0 + LANE]


def _s5e(u, we, *, layer, tmr=1024):
    n, wid = u.shape
    m = n // S5_BLOCK
    _, n_oct, kdim, ncol = we.shape
    nk = ncol // (2 * LANE)
    tmr = _tile(m, tmr)
    return pl.pallas_call(
        functools.partial(_s5e_kernel, tmr=tmr),
        grid=(n_oct, m // tmr),
        in_specs=[
            pl.BlockSpec((tmr * S5_BLOCK, LANE), lambda o, i: (i, o)),
            pl.BlockSpec((None, None, kdim, ncol), lambda o, i: (layer, o, 0, 0)),
        ],
        out_specs=pl.BlockSpec((None, 2, tmr * nk, LANE), lambda o, i: (o, 0, i, 0)),
        out_shape=jax.ShapeDtypeStruct((n_oct, 2, m * nk, LANE), F32),
        compiler_params=_params(("arbitrary", "arbitrary")),
        name="s5e",
    )(u, we)


def _s5scan_kernel(ec_ref, el_ref, a_ref, sc_ref, sl_ref, *, nb, n_c, n_l, nk):
    rev = pl.program_id(1) == 1
    a1 = a_ref[0]
    a2 = a_ref[1]

    def run(e_ref, s_ref, n, carry):
        def body(s, carry):
            i = jnp.where(rev, n - 1 - s, s)
            new = []
            for b in range(nb):
                x = carry[b]
                rows = pl.ds(pl.multiple_of((b * n + i) * nk, nk), nk)
                s_ref[rows, :] = x
                new.append(a1 * x + a2 * pltpu.roll(x, nk // 2, 0) + e_ref[rows, :])
            return tuple(new)
        return lax.fori_loop(0, n, body, carry, unroll=2)

    carry = tuple(jnp.zeros((nk, LANE), F32) for _ in range(nb))
    carry = run(ec_ref, sc_ref, n_c, carry)
    run(el_ref, sl_ref, n_l, carry)


def _s5scan(e_ctx, e_lat, dec, *, layer, nb):
    n_oct, _, rc, _ = e_ctx.shape
    rl = e_lat.shape[2]
    nk = dec.shape[-2]
    return pl.pallas_call(
        functools.partial(_s5scan_kernel, nb=nb, n_c=rc // (nk * nb), n_l=rl // (nk * nb), nk=nk),
        grid=(n_oct, 2),
        in_specs=[
            pl.BlockSpec((None, None, rc, LANE), lambda o, d: (o, d, 0, 0)),
            pl.BlockSpec((None, None, rl, LANE), lambda o, d: (o, d, 0, 0)),
            pl.BlockSpec((None, None, None, 2, nk, LANE), lambda o, d: (layer, o, d, 0, 0, 0)),
        ],
        out_specs=[
            pl.BlockSpec((None, None, rc, LANE), lambda o, d: (o, d, 0, 0)),
            pl.BlockSpec((None, None, rl, LANE), lambda o, d: (o, d, 0, 0)),
        ],
        out_shape=[jax.ShapeDtypeStruct(e_ctx.shape, F32), jax.ShapeDtypeStruct(e_lat.shape, F32)],
        compiler_params=_params(("arbitrary", "arbitrary")),
        name="s5scan",
    )(e_ctx, e_lat, dec)


def _s5y_kernel(u_ref, s_ref, k_ref, wy_ref, y_ref, *, tmr):
    nk = s_ref.shape[1] // tmr
    st = jnp.concatenate(
        [s_ref[d, pl.ds(k, tmr, stride=nk), :].astype(BF16) for d in range(2) for k in range(nk)], axis=1)
    res = jnp.dot(_block_rows(u_ref, tmr), k_ref[...], preferred_element_type=F32)
    res += jnp.dot(st, wy_ref[...], preferred_element_type=F32)
    for r in range(S5_BLOCK):
        y_ref[pl.ds(r, tmr, stride=S5_BLOCK), :] = res[:, r * LANE:(r + 1) * LANE]


def _s5y(u, s, kmat, wy, *, layer, tmr=512):
    n, wid = u.shape
    m = n // S5_BLOCK
    _, n_oct, kdim, _ = kmat.shape
    sdim = wy.shape[2]
    nk = s.shape[2] // m
    tmr = _tile(m, tmr)
    return pl.pallas_call(
        functools.partial(_s5y_kernel, tmr=tmr),
        grid=(n_oct, m // tmr),
        in_specs=[
            pl.BlockSpec((tmr * S5_BLOCK, LANE), lambda o, i: (i, o)),
            pl.BlockSpec((None, 2, tmr * nk, LANE), lambda o, i: (o, 0, i, 0)),
            pl.BlockSpec((None, None, kdim, kdim), lambda o, i: (layer, o, 0, 0)),
            pl.BlockSpec((None, None, sdim, kdim), lambda o, i: (layer, o, 0, 0)),
        ],
        out_specs=pl.BlockSpec((tmr * S5_BLOCK, LANE), lambda o, i: (i, o)),
        out_shape=jax.ShapeDtypeStruct((n, wid), F32),
        compiler_params=_params(("arbitrary", "arbitrary")),
        name="s5y",
    )(u, s, kmat, wy)


def _hgrn_masks():
    c = CHUNK
    t = np.arange(c)
    masks = []
    half = c // 2
    while half >= SUB:
        par = 2 * half
        second = (t % par) >= half
        same_parent = (t[:, None] // par) == (t[None, :] // par)
        masks.append((same_parent & second[:, None] & (~second)[None, :]).astype(np.float32))
        half //= 2
    masks.append((((t[:, None] // SUB) == (t[None, :] // SUB)) & (t[None, :] <= t[:, None])).astype(np.float32))
    return np.stack(masks)


def _hgrn_decays(g, reverse):
    nb = CHUNK // SUB
    g3 = g.reshape(nb, SUB, g.shape[1])
    r = lax.broadcasted_iota(jnp.int32, g3.shape, 1)
    p = g3
    for k in (1, 2, 4):
        if reverse:
            p = p + jnp.where(r <= SUB - 1 - k, pltpu.roll(p, SUB - k, 1), 0.0)
        else:
            p = p + jnp.where(r >= k, pltpu.roll(p, k, 1), 0.0)
    last = 0 if reverse else SUB - 1
    ref = SUB // 2 if reverse else SUB // 2 - 1
    tot = jnp.broadcast_to(p[:, last:last + 1, :], p.shape)
    ep = jnp.exp(p)
    eq = jnp.exp(tot - p)
    ed = p - jnp.broadcast_to(p[:, ref:ref + 1, :], p.shape)
    edp = jnp.exp(ed)
    edn = jnp.exp(-ed)
    et = jnp.broadcast_to(ep[:, last:last + 1, :], p.shape)
    mem = (lambda i: nb - 1 - i) if reverse else (lambda i: i)
    epb = [ep[mem(i)] for i in range(nb)]
    eqb = [eq[mem(i)] for i in range(nb)]
    etb = [et[mem(i)] for i in range(nb)]

    def assemble(blocks):
        return jnp.concatenate([blocks[mem(i)] for i in range(nb)], axis=0)

    cq = [None] * nb
    acc = None
    for i in range(nb):
        cq[i] = epb[i] if acc is None else epb[i] * acc
        acc = etb[i] if acc is None else acc * etb[i]
    total = acc[0:1, :]
    ck = [None] * nb
    acc = None
    for i in range(nb - 1, -1, -1):
        ck[i] = eqb[i] if acc is None else eqb[i] * acc
        acc = etb[i] if acc is None else acc * etb[i]
    levels = []
    half = nb // 2
    while half >= 1:
        par = 2 * half
        blocks = []
        for i in range(nb):
            j = i % par
            if j >= half:
                f = epb[i]
                for m in range(i - j + half, i):
                    f = f * etb[m]
            else:
                f = eqb[i]
                for m in range(i + 1, i - j + half):
                    f = f * etb[m]
            blocks.append(f)
        levels.append(assemble(blocks))
        half //= 2
    return levels, edp.reshape(g.shape), edn.reshape(g.shape), assemble(cq), assemble(ck), total


def _hgrn_kernel(*refs, nsub, heads, reverse, final, n_lvl):
    if final:
        (q_ref, z_ref, v_ref, lb_ref, masks_ref, s0_ref, of_ref, gate_ref, hnw_ref,
         o_ref, sfin_ref, st_ref) = refs
    else:
        q_ref, z_ref, v_ref, lb_ref, masks_ref, s0_ref, o_ref, sfin_ref, st_ref = refs
    j = pl.program_id(1)
    nj = pl.num_programs(1)
    c = CHUNK
    hd = HEAD_DIM
    nt = (((1,), (1,)), ((), ()))
    tn = (((0,), (0,)), ((), ()))

    @pl.when(j == 0)
    def _():
        st_ref[...] = s0_ref[...]

    mbool = [masks_ref[lv] > 0.0 for lv in range(n_lvl)]

    mpair = [jnp.concatenate([m, m], axis=0) for m in mbool]
    zpad = jnp.zeros((c, hd), BF16)

    def chunk(s, _):
        cl = (nsub - 1 - s) if reverse else s
        rows = pl.ds(pl.multiple_of(cl * c, c), c)
        for hp in range(heads // 2):
            sl = slice(2 * hp * hd, (2 * hp + 2) * hd)
            z = z_ref[rows, sl]
            q = q_ref[rows, sl]
            v = v_ref[rows, sl].astype(BF16)
            lb = lb_ref[:, sl]
            f = lb + (1.0 - lb) * jax.nn.sigmoid(z)
            k = 1.0 - f
            g = jnp.log(jnp.maximum(f, F_MIN))
            levels, edp, edn, cq, ck, total = _hgrn_decays(g, reverse)
            qb = q.astype(BF16)
            kb = k.astype(BF16)
            a = jnp.zeros((2 * c, c), F32)
            for lv in range(n_lvl - 1, -1, -1):
                if lv < n_lvl - 1:
                    e = levels[lv].astype(BF16)
                    ql = qb * e
                    kl = kb * e
                else:
                    ql = qb * edp.astype(BF16)
                    kl = kb * edn.astype(BF16)
                lhs = jnp.concatenate([jnp.concatenate([ql[:, :hd], zpad], axis=1),
                                       jnp.concatenate([zpad, ql[:, hd:]], axis=1)], axis=0)
                sc = lax.dot_general(lhs, kl, nt, preferred_element_type=F32)
                a = jnp.where(mpair[lv], sc, a)
            ab = a.astype(BF16)
            qd = qb * cq.astype(BF16)
            kd = kb * ck.astype(BF16)
            for i in range(2):
                h = 2 * hp + i
                hs = slice(i * hd, (i + 1) * hd)
                st = st_ref[h]
                o_h = jnp.dot(ab[i * c:(i + 1) * c], v[:, hs], preferred_element_type=F32)
                o_h += lax.dot_general(qd[:, hs], st.astype(BF16), nt, preferred_element_type=F32)
                st_ref[h] = st * total[:, hs] + lax.dot_general(v[:, hs], kd[:, hs], tn,
                                                                 preferred_element_type=F32)
                osl = slice(h * hd, (h + 1) * hd)
                if final:
                    o_h = o_h + of_ref[rows, osl]
                    ms = jnp.mean(o_h * o_h, axis=-1, keepdims=True)
                    o_h = o_h * lax.rsqrt(ms + EPS) * hnw_ref[...] * _silu(gate_ref[rows, osl])
                o_ref[rows, osl] = o_h.astype(o_ref.dtype)
        return 0

    lax.fori_loop(0, nsub, chunk, 0, unroll=4)

    @pl.when(j == nj - 1)
    def _():
        sfin_ref[...] = st_ref[...]


def _hgrn(p, lb, masks, s0, *, direction, width, o_fwd=None, hnw=None, nsub=8):
    b, t, _ = p.shape
    heads = width // HEAD_DIM
    nsub = _tile(t // CHUNK, nsub, 1)
    rows = nsub * CHUNK
    nj = t // rows
    reverse = direction == 1
    final = o_fwd is not None
    n_lvl = masks.shape[0]
    blk = (lambda j: nj - 1 - j) if reverse else (lambda j: j)
    zcol = 2 if reverse else 1
    in_specs = [
        pl.BlockSpec((None, rows, width), lambda bi, j: (bi, blk(j), 0)),
        pl.BlockSpec((None, rows, width), lambda bi, j: (bi, blk(j), zcol)),
        pl.BlockSpec((None, rows, width), lambda bi, j: (bi, blk(j), 3)),
        pl.BlockSpec((1, width), lambda bi, j: (0, 0)),
        pl.BlockSpec(masks.shape, lambda bi, j: (0, 0, 0)),
        pl.BlockSpec((None, heads, HEAD_DIM, HEAD_DIM), lambda bi, j: (bi, 0, 0, 0)),
    ]
    args = [p, p, p, lb.reshape(1, width), masks, s0]
    if final:
        in_specs += [
            pl.BlockSpec((None, rows, width), lambda bi, j: (bi, blk(j), 0)),
            pl.BlockSpec((None, rows, width), lambda bi, j: (bi, blk(j), 4)),
            pl.BlockSpec((1, HEAD_DIM), lambda bi, j: (0, 0)),
        ]
        args += [o_fwd, p, hnw.reshape(1, HEAD_DIM)]
    return pl.pallas_call(
        functools.partial(_hgrn_kernel, nsub=nsub, heads=heads, reverse=reverse, final=final, n_lvl=n_lvl),
        grid=(b, nj),
        in_specs=in_specs,
        out_specs=[
            pl.BlockSpec((None, rows, width), lambda bi, j: (bi, blk(j), 0)),
            pl.BlockSpec((None, heads, HEAD_DIM, HEAD_DIM), lambda bi, j: (bi, 0, 0, 0)),
        ],
        out_shape=[
            jax.ShapeDtypeStruct((b, t, width), F32),
            jax.ShapeDtypeStruct((b, heads, HEAD_DIM, HEAD_DIM), F32),
        ],
        scratch_shapes=[pltpu.VMEM((heads, HEAD_DIM, HEAD_DIM), F32)],
        compiler_params=_params(("arbitrary", "arbitrary")),
        name="hgrn_bwd" if final else "hgrn_fwd",
    )(*args)


def _mixout_kernel(x_ref, y_ref, u_ref, hg_ref, m_ref, dsk_ref, wglu_ref, bglu_ref, wo_ref, o_ref,
                   *, s5w, rb, colmajor):
    yy = y_ref[...] + dsk_ref[...] * u_ref[...]
    g = _gelu_tanh(yy)
    zz = jnp.dot(g.astype(BF16), wglu_ref[...], preferred_element_type=F32) + bglu_ref[...]
    s5 = (g * jax.nn.sigmoid(zz)).astype(BF16)
    if colmajor:
        hg = jnp.concatenate([hg_ref[:, r, :] for r in range(rb)], axis=0).astype(BF16)
    else:
        hg = hg_ref[...].astype(BF16)
    acc = jnp.dot(s5, wo_ref[:s5w, :], preferred_element_type=F32)
    acc += jnp.dot(hg, wo_ref[s5w:, :], preferred_element_type=F32)
    o_ref[...] = x_ref[...] + m_ref[0, 5:6, :] * acc


def _mixout(x3d, y3d, u3d, hg, mods, d_skip, w_glu, b_glu, w_o, *, layer, mod_of_batch, colmajor, rb=8):
    b, t, d = x3d.shape
    s5w = y3d.shape[-1]
    hw = hg.shape[-1]
    if colmajor:
        cols = GRID_W
        rows = t // cols
        rb = _tile(rows, rb, 8)
        tm = rb * cols
        hg_v = hg.reshape(b, cols, rows, hw)
        hg_spec = pl.BlockSpec((None, cols, rb, hw), lambda bi, i: (bi, 0, i, 0))
    else:
        rb = 1
        tm = _tile(t, 512)
        hg_v = hg
        hg_spec = pl.BlockSpec((None, tm, hw), lambda bi, i: (bi, i, 0))
    mod_fn = (lambda bi: bi) if mod_of_batch else (lambda bi: b)
    return pl.pallas_call(
        functools.partial(_mixout_kernel, s5w=s5w, rb=rb, colmajor=colmajor),
        grid=(b, t // tm),
        in_specs=[
            pl.BlockSpec((None, tm, d), lambda bi, i: (bi, i, 0)),
            pl.BlockSpec((None, tm, s5w), lambda bi, i: (bi, i, 0)),
            pl.BlockSpec((None, tm, s5w), lambda bi, i: (bi, i, 0)),
            hg_spec,
            pl.BlockSpec((1, N_MOD, d), lambda bi, i: (mod_fn(bi), 0, 0)),
            pl.BlockSpec((1, s5w), lambda bi, i: (0, 0)),
            pl.BlockSpec((None, s5w, s5w), lambda bi, i: (layer, 0, 0)),
            pl.BlockSpec((1, s5w), lambda bi, i: (0, 0)),
            pl.BlockSpec((None, s5w + hw, d), lambda bi, i: (layer, 0, 0)),
        ],
        out_specs=pl.BlockSpec((None, tm, d), lambda bi, i: (bi, i, 0)),
        out_shape=jax.ShapeDtypeStruct((b, t, d), F32),
        compiler_params=_params(("arbitrary", "arbitrary")),
        name="mixout",
    )(x3d, y3d, u3d, hg_v, mods, d_skip.reshape(1, s5w), w_glu, b_glu.reshape(1, s5w), w_o)


def kernel(x, c, ctx, c_ctx, w_ada, b_ada, norm_w, ffn_w_gate, ffn_w_up, ffn_w_down, w_in, w_out,
           s5_lambda_re, s5_lambda_im, s5_log_step, s5_b_re, s5_b_im, s5_c_re, s5_c_im, s5_d,
           s5_w_glu, s5_b_glu, hgrn_lower_bounds, hgrn_norm_w, final_norm_w):
    batch, seq, d = x.shape
    n_ctx = ctx.shape[1]
    depth = w_ada.shape[0]
    s5w = s5_d.shape[-1]
    hw = hgrn_lower_bounds.shape[-1]
    rows = seq // GRID_W
    assert batch < MOD_ROWS and rows % CHUNK == 0 and n_ctx % CHUNK == 0
    assert seq % S5_BLOCK == 0 and n_ctx % S5_BLOCK == 0 and (s5w // S5_GROUP) % S5_OCT == 0

    cvec = jnp.zeros((MOD_ROWS, d), F32).at[:batch].set(c.astype(F32)).at[batch].set(c_ctx.astype(F32))
    mods_all = _ada(cvec, w_ada, b_ada).reshape(depth, MOD_ROWS, N_MOD, d)

    lb_soft = jax.nn.softmax(hgrn_lower_bounds.astype(F32), axis=0)
    lb_all = jnp.cumsum(lb_soft, axis=0) - lb_soft[0]

    masks_np = _hgrn_masks()
    masks_f = jnp.asarray(masks_np, F32)
    masks_b = jnp.asarray(masks_np[:, ::-1, ::-1].copy(), F32)
    heads = hw // HEAD_DIM
    s_zero = jnp.zeros((batch, heads, HEAD_DIM, HEAD_DIM), F32)

    wg = ffn_w_gate.astype(BF16)
    wu = ffn_w_up.astype(BF16)
    wd = ffn_w_down.astype(BF16)
    uk, pb3, ca3, dec = (jnp.stack(a) for a in zip(*[
        _s5_compact(s5_lambda_re[l], s5_lambda_im[l], s5_log_step[l], s5_b_re[l], s5_b_im[l],
                    s5_c_re[l], s5_c_im[l]) for l in range(depth)]))
    kmat, we, wy = _s5_expand(uk, pb3, ca3)
    w_in_b = w_in.astype(BF16)
    w_h = w_in_b[:, :, s5w:]
    w_o = w_out.astype(BF16)
    w_glu = s5_w_glu.astype(BF16)

    xl = x.astype(F32)
    xc = ctx.astype(F32)
    lat = dict(rows_per_mod=seq, mod_off=0)
    cx = dict(rows_per_mod=batch * n_ctx, mod_off=batch)
    for l in range(depth):
        last = l == depth - 1
        mods = mods_all[l]

        xl = _ffn(xl.reshape(batch * seq, d), mods, norm_w[l, 0], wg, wu, wd, layer=l, half=0, base=0,
                  **lat).reshape(batch, seq, d)
        xc = _ffn(xc.reshape(batch * n_ctx, d), mods, norm_w[l, 0], wg, wu, wd, layer=l, half=0, base=0,
                  **cx).reshape(batch, n_ctx, d)

        u_l = _uproj(xl.reshape(batch * seq, d), mods, norm_w[l, 1], w_in_b, layer=l, s5w=s5w, **lat)
        u_c = _uproj(xc.reshape(batch * n_ctx, d), mods, norm_w[l, 1], w_in_b, layer=l, s5w=s5w, **cx)
        p_lat = _hproj(xl, mods, norm_w[l, 1], w_h, layer=l, hw=hw, cols=GRID_W, mod_row=None)
        p_ctx = _hproj(xc.reshape(1, batch * n_ctx, d), mods, norm_w[l, 1], w_h, layer=l, hw=hw, cols=1,
                       mod_row=batch).reshape(batch, n_ctx, 5 * hw)

        e_l = _s5e(u_l, we, layer=l)
        e_c = _s5e(u_c, we, layer=l)
        st_c, st_l = _s5scan(e_c, e_l, dec, layer=l, nb=batch)
        y_l = _s5y(u_l, st_l, kmat, wy, layer=l).reshape(batch, seq, s5w)

        lb_f = lb_all[l, 0]
        lb_b = lb_all[l, 1]
        oc_f, sc_f = _hgrn(p_ctx, lb_f, masks_f, s_zero, direction=0, width=hw)
        ol_f, _ = _hgrn(p_lat, lb_f, masks_f, sc_f, direction=0, width=hw)
        hg_c, sc_b = _hgrn(p_ctx, lb_b, masks_b, s_zero, direction=1, width=hw,
                           o_fwd=oc_f, hnw=hgrn_norm_w[l])
        hg_l, _ = _hgrn(p_lat, lb_b, masks_b, sc_b, direction=1, width=hw,
                        o_fwd=ol_f, hnw=hgrn_norm_w[l])

        xl = _mixout(xl, y_l, u_l.reshape(batch, seq, s5w), hg_l, mods, s5_d[l], w_glu, s5_b_glu[l], w_o,
                     layer=l, mod_of_batch=True, colmajor=True)
        if not last:
            y_c = _s5y(u_c, st_c, kmat, wy, layer=l).reshape(batch, n_ctx, s5w)
            xc = _mixout(xc, y_c, u_c.reshape(batch, n_ctx, s5w), hg_c, mods, s5_d[l], w_glu, s5_b_glu[l],
                         w_o, layer=l, mod_of_batch=False, colmajor=False)

        xl = _ffn(xl.reshape(batch * seq, d), mods, norm_w[l, 2], wg, wu, wd, layer=l, half=1, base=6,
                  final_nw=final_norm_w if last else None, **lat).reshape(batch, seq, d)
        if not last:
            xc = _ffn(xc.reshape(batch * n_ctx, d), mods, norm_w[l, 2], wg, wu, wd, layer=l, half=1, base=6,
                      **cx).reshape(batch, n_ctx, d)
    return xl
```

```python
import functools

import numpy as np
import jax
import jax.numpy as jnp
from jax import lax
from jax.experimental import pallas as pl
from jax.experimental.pallas import tpu as pltpu

F32 = jnp.float32
BF16 = jnp.bfloat16

EPS = 1e-6
F_MIN = 1e-6
LAMBDA_RE_MAX = -1e-4
GRID_W = 64
N_MOD = 9
S5_GROUP = 16
S5_STATE = 64
S5_BLOCK = 8
S5_OCT = 8
HEAD_DIM = 128
CHUNK = 64
SUB = 8
LANE = 128
MOD_ROWS = 8
VMEM_LIMIT = 60 * 1024 * 1024


def _params(sem):
    return pltpu.CompilerParams(dimension_semantics=sem, vmem_limit_bytes=VMEM_LIMIT)


def _tile(n, pref, mult=8):
    t = min(n, pref)
    while t > 0:
        if n % t == 0 and t % mult == 0:
            return t
        t -= 1
    return n


def _norm_mod(x, nw, shift, scale):
    ms = jnp.mean(x * x, axis=-1, keepdims=True)
    gain = nw * (1.0 + scale)
    return (x * lax.rsqrt(ms + EPS)) * gain + shift


def _silu(x):
    return x * jax.nn.sigmoid(x)


def _gelu_tanh(x):
    return 0.5 * x * (1.0 + jnp.tanh(0.7978845608028654 * (x + 0.044715 * (x * x * x))))


def _ada_kernel(c_ref, w_ref, b_ref, o_ref):
    a = _silu(c_ref[...])
    a_hi = a.astype(BF16)
    a_lo = (a - a_hi.astype(F32)).astype(BF16)
    w = w_ref[...]
    w_hi = w.astype(BF16)
    w_lo = (w - w_hi.astype(F32)).astype(BF16)
    both = jnp.dot(jnp.concatenate([a_hi, a_lo], axis=0), w_hi, preferred_element_type=F32)
    acc = both[:MOD_ROWS] + both[MOD_ROWS:] + jnp.dot(a_hi, w_lo, preferred_element_type=F32)
    o_ref[...] = acc + b_ref[...]


def _ada(cvec, w_ada, b_ada):
    depth, d, n = w_ada.shape
    tn = _tile(n, 2048, LANE)
    return pl.pallas_call(
        _ada_kernel,
        grid=(depth, n // tn),
        in_specs=[
            pl.BlockSpec((MOD_ROWS, d), lambda l, j: (0, 0)),
            pl.BlockSpec((None, d, tn), lambda l, j: (l, 0, j)),
            pl.BlockSpec((None, 1, tn), lambda l, j: (l, 0, j)),
        ],
        out_specs=pl.BlockSpec((None, MOD_ROWS, tn), lambda l, j: (l, 0, j)),
        out_shape=jax.ShapeDtypeStruct((depth, MOD_ROWS, n), F32),
        compiler_params=_params(("arbitrary", "arbitrary")),
        name="ada",
    )(cvec, w_ada, b_ada.reshape(depth, 1, n))


def _ffn_kernel(*refs, base, n_f, final, tail):
    if final:
        x_ref, m_ref, nw_ref, wg_ref, wu_ref, wd_ref, fnw_ref, o_ref, h_ref = refs
    else:
        x_ref, m_ref, nw_ref, wg_ref, wu_ref, wd_ref, o_ref, h_ref = refs
    j = pl.program_id(1)

    @pl.when(j == 0)
    def _():
        h = _norm_mod(x_ref[...], nw_ref[...], m_ref[0, base:base + 1, :], m_ref[0, base + 1:base + 2, :])
        h_ref[...] = h.astype(BF16)

    def partial_sum(valid=None):
        h = h_ref[...]
        g = jnp.dot(h, wg_ref[...], preferred_element_type=F32)
        u = jnp.dot(h, wu_ref[...], preferred_element_type=F32)
        a = (_silu(g) * u).astype(BF16)
        wd = wd_ref[...]
        if valid is not None:
            a = jnp.where(lax.broadcasted_iota(jnp.int32, a.shape, 1) < valid, a, jnp.zeros_like(a))
            wd = jnp.where(lax.broadcasted_iota(jnp.int32, wd.shape, 0) < valid, wd, jnp.zeros_like(wd))
        return jnp.dot(a, wd, preferred_element_type=F32)

    @pl.when(j == 0)
    def _():
        o_ref[...] = partial_sum()

    @pl.when((j > 0) & (j < n_f - 1))
    def _():
        o_ref[...] += partial_sum()

    @pl.when(j == n_f - 1)
    def _():
        y = x_ref[...] + (0.5 * m_ref[0, base + 2:base + 3, :]) * (o_ref[...] + partial_sum(tail))
        if final:
            ms = jnp.mean(y * y, axis=-1, keepdims=True)
            y = y * lax.rsqrt(ms + EPS) * fnw_ref[...]
        o_ref[...] = y


def _ffn(x2d, mods, nw, wg, wu, wd, *, layer, half, base, rows_per_mod, mod_off, final_nw=None,
         tm=1024, tf=512):
    n, d = x2d.shape
    fp = wg.shape[-1]
    tm = _tile(min(n, rows_per_mod), tm)
    n_f = pl.cdiv(fp, tf)
    tail = fp - (n_f - 1) * tf if fp % tf else None
    assert n_f >= 2
    final = final_nw is not None
    in_specs = [
        pl.BlockSpec((tm, d), lambda i, j: (i, 0)),
        pl.BlockSpec((1, N_MOD, d), lambda i, j: ((i * tm) // rows_per_mod + mod_off, 0, 0)),
        pl.BlockSpec((1, d), lambda i, j: (0, 0)),
        pl.BlockSpec((None, None, d, tf), lambda i, j: (layer, half, 0, j)),
        pl.BlockSpec((None, None, d, tf), lambda i, j: (layer, half, 0, j)),
        pl.BlockSpec((None, None, tf, d), lambda i, j: (layer, half, j, 0)),
    ]
    args = [x2d, mods, nw.reshape(1, d), wg, wu, wd]
    if final:
        in_specs.append(pl.BlockSpec((1, d), lambda i, j: (0, 0)))
        args.append(final_nw.reshape(1, d))
    return pl.pallas_call(
        functools.partial(_ffn_kernel, base=base, n_f=n_f, final=final, tail=tail),
        grid=(n // tm, n_f),
        in_specs=in_specs,
        out_specs=pl.BlockSpec((tm, d), lambda i, j: (i, 0)),
        out_shape=jax.ShapeDtypeStruct((n, d), F32),
        scratch_shapes=[pltpu.VMEM((tm, d), BF16)],
        compiler_params=_params(("arbitrary", "arbitrary")),
        name="ffn_final" if final else "ffn",
    )(*args)


def _uproj_kernel(x_ref, m_ref, nw_ref, w_ref, u_ref):
    h = _norm_mod(x_ref[...], nw_ref[...], m_ref[0, 3:4, :], m_ref[0, 4:5, :])
    u_ref[...] = jnp.dot(h.astype(BF16), w_ref[...], preferred_element_type=F32)


def _uproj(x2d, mods, nw, w_in, *, layer, s5w, rows_per_mod, mod_off, tm=1024):
    n, d = x2d.shape
    tm = _tile(min(n, rows_per_mod), tm)
    return pl.pallas_call(
        _uproj_kernel,
        grid=(n // tm,),
        in_specs=[
            pl.BlockSpec((tm, d), lambda i: (i, 0)),
            pl.BlockSpec((1, N_MOD, d), lambda i: ((i * tm) // rows_per_mod + mod_off, 0, 0)),
            pl.BlockSpec((1, d), lambda i: (0, 0)),
            pl.BlockSpec((None, d, s5w), lambda i: (layer, 0, 0)),
        ],
        out_specs=pl.BlockSpec((tm, s5w), lambda i: (i, 0)),
        out_shape=jax.ShapeDtypeStruct((n, s5w), F32),
        compiler_params=_params(("arbitrary",)),
        name="uproj",
    )(x2d, mods, nw.reshape(1, d), w_in)


def _hproj_kernel(*refs, colmajor, hw):
    if colmajor:
        x_ref, m_ref, nw_ref, w_ref, perm_ref, o_ref = refs
    else:
        x_ref, m_ref, nw_ref, w_ref, o_ref = refs
    h = _norm_mod(x_ref[...], nw_ref[...], m_ref[0, 3:4, :], m_ref[0, 4:5, :]).astype(BF16)
    if colmajor:
        h = h.reshape(perm_ref.shape[0], h.shape[-1])
        h = jnp.dot(perm_ref[...], h, preferred_element_type=F32).astype(BF16)
    for nb in range(o_ref.shape[-1] // hw):
        p = jnp.dot(h, w_ref[:, nb * hw:(nb + 1) * hw], preferred_element_type=F32)
        o_ref[:, nb * hw:(nb + 1) * hw] = _silu(p) if nb == 0 else p


def _hproj(x3d, mods, nw, w_h, *, layer, hw, cols, mod_row, cb=8, tr=512):
    b, t, d = x3d.shape
    nh = w_h.shape[-1]
    colmajor = cols > 1
    mod_fn = (lambda bi: bi) if mod_row is None else (lambda bi: mod_row)
    in_specs = [
        None,
        pl.BlockSpec((1, N_MOD, d), lambda bi, i: (mod_fn(bi), 0, 0)),
        pl.BlockSpec((1, d), lambda bi, i: (0, 0)),
        pl.BlockSpec((None, d, nh), lambda bi, i: (layer, 0, 0), pipeline_mode=pl.Buffered(1)),
    ]
    args = [None, mods, nw.reshape(1, d), w_h]
    if colmajor:
        rows = t // cols
        cb = _tile(cols, cb, 8)
        tr = rows * cb
        args[0] = x3d.reshape(b, rows, cols, d)
        in_specs[0] = pl.BlockSpec((None, rows, cb, d), lambda bi, i: (bi, 0, i, 0))
        src = np.arange(tr).reshape(rows, cb).T.reshape(-1)
        perm = np.zeros((tr, tr), np.float32)
        perm[np.arange(tr), src] = 1.0
        args.append(jnp.asarray(perm, BF16))
        in_specs.append(pl.BlockSpec(perm.shape, lambda bi, i: (0, 0)))
    else:
        tr = _tile(t, tr)
        args[0] = x3d
        in_specs[0] = pl.BlockSpec((None, tr, d), lambda bi, i: (bi, i, 0))
    return pl.pallas_call(
        functools.partial(_hproj_kernel, colmajor=colmajor, hw=hw),
        grid=(b, t // tr),
        in_specs=in_specs,
        out_specs=pl.BlockSpec((None, tr, nh), lambda bi, i: (bi, i, 0)),
        out_shape=jax.ShapeDtypeStruct((b, t, nh), F32),
        compiler_params=_params(("arbitrary", "arbitrary")),
        name="hproj",
    )(*args)


def _s5_sel_consts():
    t, h, p, o8 = S5_BLOCK, S5_GROUP, S5_STATE, S5_OCT
    colsel = np.zeros((t, 2, t, h, t, o8, h), np.float32)
    for r in range(t):
        for r2 in range(t):
            for hh in range(h):
                if r2 >= r:
                    colsel[r, 0, r2 - r, hh, r2, :, hh] = 1.0
                if r >= r2:
                    colsel[r, 1, r - r2, hh, r2, :, hh] = 1.0
    colsel = colsel.reshape(t, 2 * t * h, t * o8 * h)
    tile_e = np.zeros((4, p, 4, o8, p), np.float32)
    tile_y = np.zeros((4, h, 4, o8, h), np.float32)
    for dp in range(4):
        for i in range(p):
            tile_e[dp, i, dp, :, i] = 1.0
        for i in range(h):
            tile_y[dp, i, dp, :, i] = 1.0
    return colsel, tile_e.reshape(4 * p, 4 * o8 * p), tile_y.reshape(4 * h, 4 * o8 * h)


def _s5w_kernel(uk_ref, pb_ref, ca_ref, colsel_ref, tile_e_ref, tile_y_ref, k_ref, we_ref, wy_ref):
    t, h, p, o8 = S5_BLOCK, S5_GROUP, S5_STATE, S5_OCT
    gh = o8 * h
    gp = o8 * p

    def diag_mask(shape, row_div, col_mod, col_div):
        rg = lax.broadcasted_iota(jnp.int32, shape, 0) // row_div
        cg = (lax.broadcasted_iota(jnp.int32, shape, 1) % col_mod) // col_div
        return rg == cg

    mk = diag_mask((gh, t * gh), h, gh, h)
    me = diag_mask((gh, 4 * gp), h, gp, p)
    my = diag_mask((gp, 4 * gh), p, gh, h)
    uk = uk_ref[...].astype(BF16)
    for r in range(t):
        blk = jnp.dot(uk, colsel_ref[r], preferred_element_type=F32)
        k_ref[r * gh:(r + 1) * gh, :] = jnp.where(mk, blk, 0.0).astype(BF16)
        blk = jnp.dot(pb_ref[r].astype(BF16), tile_e_ref[...], preferred_element_type=F32)
        we_ref[r * gh:(r + 1) * gh, :] = jnp.where(me, blk, 0.0).astype(BF16)
        blk = jnp.where(my, jnp.dot(ca_ref[r].astype(BF16), tile_y_ref[...], preferred_element_type=F32), 0.0)
        for dp in range(4):
            wy_ref[dp * gp:(dp + 1) * gp, r * gh:(r + 1) * gh] = blk[:, dp * gh:(dp + 1) * gh].astype(BF16)


def _s5_compact(lam_re, lam_im, log_step, b_re, b_im, c_re, c_im):
    hp = lax.Precision.HIGHEST
    t = S5_BLOCK
    g, p = lam_re.shape[1:]
    h = b_re.shape[-1]
    n_oct = g // S5_OCT
    lam_re = jnp.minimum(lam_re.astype(F32), LAMBDA_RE_MAX)
    lam_im = lam_im.astype(F32)
    dt = jnp.exp(log_step.astype(F32))[..., None]
    mag = jnp.exp(lam_re * dt)
    lb_re = mag * jnp.cos(lam_im * dt)
    lb_im = mag * jnp.sin(lam_im * dt)
    den = lam_re * lam_re + lam_im * lam_im
    nr = lb_re - 1.0
    ni = lb_im
    cf_re = (nr * lam_re + ni * lam_im) / den
    cf_im = (ni * lam_re - nr * lam_im) / den
    b_re = b_re.astype(F32)
    b_im = b_im.astype(F32)
    br = cf_re[..., None] * b_re - cf_im[..., None] * b_im
    bi = cf_re[..., None] * b_im + cf_im[..., None] * b_re
    cr = c_re.astype(F32)
    ci = c_im.astype(F32)
    expo = np.concatenate([np.arange(t + 1), np.arange(t - 1, -1, -1), np.arange(t, 0, -1)]).astype(np.float32)
    j = jnp.asarray(expo)[None, :, None, None]
    pmag = jnp.exp(j * (lam_re * dt)[:, None])
    pw_re = pmag * jnp.cos(j * (lam_im * dt)[:, None])
    pw_im = pmag * jnp.sin(j * (lam_im * dt)[:, None])
    asc, desc0, desc1 = slice(0, t), slice(t + 1, 2 * t + 1), slice(2 * t + 1, 3 * t + 1)

    def lam_b(d_, sl):
        pr, pi = pw_re[d_, sl][..., None], pw_im[d_, sl][..., None]
        return pr * br[d_] - pi * bi[d_], pr * bi[d_] + pi * br[d_]

    pbf_re, pbf_im = lam_b(0, asc)
    pbb_re, pbb_im = lam_b(1, asc)
    pb_re = jnp.stack([pbf_re, pbb_re])
    pb_im = jnp.stack([pbf_im, pbb_im])
    kj = (jnp.einsum('dgep,djgph->djgeh', cr, pb_re, precision=hp)
          - jnp.einsum('dgep,djgph->djgeh', ci, pb_im, precision=hp))
    kj = jnp.stack([jnp.concatenate([kj[0, :1] + kj[1, :1], kj[0, 1:]], axis=0),
                    jnp.concatenate([jnp.zeros_like(kj[1, :1]), kj[1, 1:]], axis=0)])
    uk = jnp.transpose(kj, (2, 4, 0, 1, 3)).reshape(n_oct, S5_OCT * h, 2 * t * h)
    pb4 = jnp.stack(lam_b(0, desc0) + (pbb_re, pbb_im))
    pb3 = jnp.transpose(pb4.reshape(4, t, n_oct, S5_OCT, p, h), (2, 1, 3, 5, 0, 4))
    pb3 = pb3.reshape(n_oct, t, S5_OCT * h, 4 * p)

    def c_lam(d_, sl):
        pr, pi = pw_re[d_, sl][:, :, None, :], pw_im[d_, sl][:, :, None, :]
        return cr[d_] * pr - ci[d_] * pi, -(cr[d_] * pi + ci[d_] * pr)

    ca4 = jnp.stack(c_lam(0, slice(1, t + 1)) + c_lam(1, desc1))
    ca3 = jnp.transpose(ca4.reshape(4, t, n_oct, S5_OCT, h, p), (2, 1, 3, 5, 0, 4))
    ca3 = ca3.reshape(n_oct, t, S5_OCT * p, 4 * h)
    nk2 = S5_OCT * p // LANE
    a_re = jnp.transpose(pw_re[:, t].reshape(2, n_oct, nk2, LANE), (1, 0, 2, 3))
    a_im = jnp.transpose(pw_im[:, t].reshape(2, n_oct, nk2, LANE), (1, 0, 2, 3))
    dec = jnp.stack([jnp.concatenate([a_re, a_re], axis=2), jnp.concatenate([-a_im, a_im], axis=2)], axis=2)
    return uk, pb3, ca3, dec


def _s5_expand(uk, pb3, ca3):
    depth, n_oct = uk.shape[:2]
    colsel, tile_e, tile_y = (jnp.asarray(a, BF16) for a in _s5_sel_consts())
    kdim = S5_BLOCK * S5_OCT * S5_GROUP
    sdim = 4 * S5_OCT * S5_STATE
    return pl.pallas_call(
        _s5w_kernel,
        grid=(depth, n_oct),
        in_specs=[
            pl.BlockSpec((None, None) + uk.shape[2:], lambda l, o: (l, o, 0, 0)),
            pl.BlockSpec((None, None) + pb3.shape[2:], lambda l, o: (l, o, 0, 0, 0)),
            pl.BlockSpec((None, None) + ca3.shape[2:], lambda l, o: (l, o, 0, 0, 0)),
            pl.BlockSpec(colsel.shape, lambda l, o: (0, 0, 0)),
            pl.BlockSpec(tile_e.shape, lambda l, o: (0, 0)),
            pl.BlockSpec(tile_y.shape, lambda l, o: (0, 0)),
        ],
        out_specs=[
            pl.BlockSpec((None, None, kdim, kdim), lambda l, o: (l, o, 0, 0)),
            pl.BlockSpec((None, None, kdim, sdim), lambda l, o: (l, o, 0, 0)),
            pl.BlockSpec((None, None, sdim, kdim), lambda l, o: (l, o, 0, 0)),
        ],
        out_shape=[
            jax.ShapeDtypeStruct((depth, n_oct, kdim, kdim), BF16),
            jax.ShapeDtypeStruct((depth, n_oct, kdim, sdim), BF16),
            jax.ShapeDtypeStruct((depth, n_oct, sdim, kdim), BF16),
        ],
        compiler_params=_params(("arbitrary", "arbitrary")),
        name="s5w",
    )(uk, pb3, ca3, colsel, tile_e, tile_y)


def _block_rows(u_ref, tmr):
    return jnp.concatenate(
        [u_ref[pl.ds(r, tmr, stride=S5_BLOCK), :].astype(BF16) for r in range(S5_BLOCK)], axis=1)


def _s5e_kernel(u_ref, we_ref, e_ref, *, tmr):
    res = jnp.dot(_block_rows(u_ref, tmr), we_ref[...], preferred_element_type=F32)
    nk = res.shape[1] // (2 * LANE)
    for d in range(2):
        for k in range(nk):
            c0 = (d * nk + k) * LANE
            e_ref[d, pl.ds(k, tmr, stride=nk), :] = res[:, c0:c0 + LANE]


def _s5e(u, we, *, layer, tmr=1024):
    n, wid = u.shape
    m = n // S5_BLOCK
    _, n_oct, kdim, ncol = we.shape
    nk = ncol // (2 * LANE)
    tmr = _tile(m, tmr)
    return pl.pallas_call(
        functools.partial(_s5e_kernel, tmr=tmr),
        grid=(n_oct, m // tmr),
        in_specs=[
            pl.BlockSpec((tmr * S5_BLOCK, LANE), lambda o, i: (i, o)),
            pl.BlockSpec((None, None, kdim, ncol), lambda o, i: (layer, o, 0, 0)),
        ],
        out_specs=pl.BlockSpec((None, 2, tmr * nk, LANE), lambda o, i: (o, 0, i, 0)),
        out_shape=jax.ShapeDtypeStruct((n_oct, 2, m * nk, LANE), F32),
        compiler_params=_params(("arbitrary", "arbitrary")),
        name="s5e",
    )(u, we)


def _s5scan_kernel(ec_ref, el_ref, a_ref, sc_ref, sl_ref, *, nb, n_c, n_l, nk):
    rev = pl.program_id(1) == 1
    a1 = a_ref[0]
    a2 = a_ref[1]

    def run(e_ref, s_ref, n, carry):
        def body(s, carry):
            i = jnp.where(rev, n - 1 - s, s)
            new = []
            for b in range(nb):
                x = carry[b]
                rows = pl.ds(pl.multiple_of((b * n + i) * nk, nk), nk)
                s_ref[rows, :] = x
                new.append(a1 * x + a2 * pltpu.roll(x, nk // 2, 0) + e_ref[rows, :])
            return tuple(new)
        return lax.fori_loop(0, n, body, carry, unroll=2)

    carry = tuple(jnp.zeros((nk, LANE), F32) for _ in range(nb))
    carry = run(ec_ref, sc_ref, n_c, carry)
    run(el_ref, sl_ref, n_l, carry)


def _s5scan(e_ctx, e_lat, dec, *, layer, nb):
    n_oct, _, rc, _ = e_ctx.shape
    rl = e_lat.shape[2]
    nk = dec.shape[-2]
    return pl.pallas_call(
        functools.partial(_s5scan_kernel, nb=nb, n_c=rc // (nk * nb), n_l=rl // (nk * nb), nk=nk),
        grid=(n_oct, 2),
        in_specs=[
            pl.BlockSpec((None, None, rc, LANE), lambda o, d: (o, d, 0, 0)),
            pl.BlockSpec((None, None, rl, LANE), lambda o, d: (o, d, 0, 0)),
            pl.BlockSpec((None, None, None, 2, nk, LANE), lambda o, d: (layer, o, d, 0, 0, 0)),
        ],
        out_specs=[
            pl.BlockSpec((None, None, rc, LANE), lambda o, d: (o, d, 0, 0)),
            pl.BlockSpec((None, None, rl, LANE), lambda o, d: (o, d, 0, 0)),
        ],
        out_shape=[jax.ShapeDtypeStruct(e_ctx.shape, F32), jax.ShapeDtypeStruct(e_lat.shape, F32)],
        compiler_params=_params(("arbitrary", "arbitrary")),
        name="s5scan",
    )(e_ctx, e_lat, dec)


def _s5y_kernel(u_ref, s_ref, k_ref, wy_ref, y_ref, *, tmr):
    nk = s_ref.shape[1] // tmr
    st = jnp.concatenate(
        [s_ref[d, pl.ds(k, tmr, stride=nk), :].astype(BF16) for d in range(2) for k in range(nk)], axis=1)
    res = jnp.dot(_block_rows(u_ref, tmr), k_ref[...], preferred_element_type=F32)
    res += jnp.dot(st, wy_ref[...], preferred_element_type=F32)
    for r in range(S5_BLOCK):
        y_ref[pl.ds(r, tmr, stride=S5_BLOCK), :] = res[:, r * LANE:(r + 1) * LANE]


def _s5y(u, s, kmat, wy, *, layer, tmr=1024):
    n, wid = u.shape
    m = n // S5_BLOCK
    _, n_oct, kdim, _ = kmat.shape
    sdim = wy.shape[2]
    nk = s.shape[2] // m
    tmr = _tile(m, tmr)
    return pl.pallas_call(
        functools.partial(_s5y_kernel, tmr=tmr),
        grid=(n_oct, m // tmr),
        in_specs=[
            pl.BlockSpec((tmr * S5_BLOCK, LANE), lambda o, i: (i, o)),
            pl.BlockSpec((None, 2, tmr * nk, LANE), lambda o, i: (o, 0, i, 0)),
            pl.BlockSpec((None, None, kdim, kdim), lambda o, i: (layer, o, 0, 0)),
            pl.BlockSpec((None, None, sdim, kdim), lambda o, i: (layer, o, 0, 0)),
        ],
        out_specs=pl.BlockSpec((tmr * S5_BLOCK, LANE), lambda o, i: (i, o)),
        out_shape=jax.ShapeDtypeStruct((n, wid), F32),
        compiler_params=_params(("arbitrary", "arbitrary")),
        name="s5y",
    )(u, s, kmat, wy)


def _hgrn_masks():
    c = CHUNK
    t = np.arange(c)
    masks = []
    half = c // 2
    while half >= SUB:
        par = 2 * half
        second = (t % par) >= half
        same_parent = (t[:, None] // par) == (t[None, :] // par)
        masks.append((same_parent & second[:, None] & (~second)[None, :]).astype(np.float32))
        half //= 2
    masks.append((((t[:, None] // SUB) == (t[None, :] // SUB)) & (t[None, :] <= t[:, None])).astype(np.float32))
    return np.stack(masks)


def _hgrn_decays(g, reverse):
    nb = CHUNK // SUB
    g3 = g.reshape(nb, SUB, g.shape[1])
    r = lax.broadcasted_iota(jnp.int32, g3.shape, 1)
    p = g3
    for k in (1, 2, 4):
        if reverse:
            p = p + jnp.where(r <= SUB - 1 - k, pltpu.roll(p, SUB - k, 1), 0.0)
        else:
            p = p + jnp.where(r >= k, pltpu.roll(p, k, 1), 0.0)
    last = 0 if reverse else SUB - 1
    ref = SUB // 2 if reverse else SUB // 2 - 1
    tot = jnp.broadcast_to(p[:, last:last + 1, :], p.shape)
    ep = jnp.exp(p)
    eq = jnp.exp(tot - p)
    ed = p - jnp.broadcast_to(p[:, ref:ref + 1, :], p.shape)
    edp = jnp.exp(ed)
    edn = jnp.exp(-ed)
    et = jnp.broadcast_to(ep[:, last:last + 1, :], p.shape)
    mem = (lambda i: nb - 1 - i) if reverse else (lambda i: i)
    epb = [ep[mem(i)] for i in range(nb)]
    eqb = [eq[mem(i)] for i in range(nb)]
    etb = [et[mem(i)] for i in range(nb)]

    def assemble(blocks):
        return jnp.concatenate([blocks[mem(i)] for i in range(nb)], axis=0)

    cq = [None] * nb
    acc = None
    for i in range(nb):
        cq[i] = epb[i] if acc is None else epb[i] * acc
        acc = etb[i] if acc is None else acc * etb[i]
    total = acc[0:1, :]
    ck = [None] * nb
    acc = None
    for i in range(nb - 1, -1, -1):
        ck[i] = eqb[i] if acc is None else eqb[i] * acc
        acc = etb[i] if acc is None else acc * etb[i]
    levels = []
    half = nb // 2
    while half >= 1:
        par = 2 * half
        blocks = []
        for i in range(nb):
            j = i % par
            if j >= half:
                f = epb[i]
                for m in range(i - j + half, i):
                    f = f * etb[m]
            else:
                f = eqb[i]
                for m in range(i + 1, i - j + half):
                    f = f * etb[m]
            blocks.append(f)
        levels.append(assemble(blocks))
        half //= 2
    return levels, edp.reshape(g.shape), edn.reshape(g.shape), assemble(cq), assemble(ck), total


def _hgrn_kernel(*refs, nsub, heads, reverse, final, n_lvl):
    if final:
        (q_ref, z_ref, v_ref, lb_ref, masks_ref, s0_ref, of_ref, gate_ref, hnw_ref,
         o_ref, sfin_ref, st_ref) = refs
    else:
        q_ref, z_ref, v_ref, lb_ref, masks_ref, s0_ref, o_ref, sfin_ref, st_ref = refs
    j = pl.program_id(1)
    nj = pl.num_programs(1)
    c = CHUNK
    hd = HEAD_DIM
    nt = (((1,), (1,)), ((), ()))
    tn = (((0,), (0,)), ((), ()))

    @pl.when(j == 0)
    def _():
        st_ref[...] = s0_ref[...]

    mbool = [masks_ref[lv] > 0.0 for lv in range(n_lvl)]

    mpair = [jnp.concatenate([m, m], axis=0) for m in mbool]
    zpad = jnp.zeros((c, hd), BF16)

    def chunk(s, _):
        cl = (nsub - 1 - s) if reverse else s
        rows = pl.ds(pl.multiple_of(cl * c, c), c)
        for hp in range(heads // 2):
            sl = slice(2 * hp * hd, (2 * hp + 2) * hd)
            z = z_ref[rows, sl]
            q = q_ref[rows, sl]
            v = v_ref[rows, sl].astype(BF16)
            lb = lb_ref[:, sl]
            f = lb + (1.0 - lb) * jax.nn.sigmoid(z)
            k = 1.0 - f
            g = jnp.log(jnp.maximum(f, F_MIN))
            levels, edp, edn, cq, ck, total = _hgrn_decays(g, reverse)
            qb = q.astype(BF16)
            kb = k.astype(BF16)
            a = jnp.zeros((2 * c, c), F32)
            for lv in range(n_lvl - 1, -1, -1):
                if lv < n_lvl - 1:
                    e = levels[lv].astype(BF16)
                    ql = qb * e
                    kl = kb * e
                else:
                    ql = qb * edp.astype(BF16)
                    kl = kb * edn.astype(BF16)
                lhs = jnp.concatenate([jnp.concatenate([ql[:, :hd], zpad], axis=1),
                                       jnp.concatenate([zpad, ql[:, hd:]], axis=1)], axis=0)
                sc = lax.dot_general(lhs, kl, nt, preferred_element_type=F32)
                a = jnp.where(mpair[lv], sc, a)
            ab = a.astype(BF16)
            qd = qb * cq.astype(BF16)
            kd = kb * ck.astype(BF16)
            for i in range(2):
                h = 2 * hp + i
                hs = slice(i * hd, (i + 1) * hd)
                st = st_ref[h]
                o_h = jnp.dot(ab[i * c:(i + 1) * c], v[:, hs], preferred_element_type=F32)
                o_h += lax.dot_general(qd[:, hs], st.astype(BF16), nt, preferred_element_type=F32)
                st_ref[h] = st * total[:, hs] + lax.dot_general(v[:, hs], kd[:, hs], tn,
                                                                 preferred_element_type=F32)
                osl = slice(h * hd, (h + 1) * hd)
                if final:
                    o_h = o_h + of_ref[rows, osl]
                    ms = jnp.mean(o_h * o_h, axis=-1, keepdims=True)
                    o_h = o_h * lax.rsqrt(ms + EPS) * hnw_ref[...] * _silu(gate_ref[rows, osl])
                o_ref[rows, osl] = o_h.astype(o_ref.dtype)
        return 0

    lax.fori_loop(0, nsub, chunk, 0, unroll=4)

    @pl.when(j == nj - 1)
    def _():
        sfin_ref[...] = st_ref[...]


def _hgrn(p, lb, masks, s0, *, direction, width, o_fwd=None, hnw=None, nsub=16):
    b, t, _ = p.shape
    heads = width // HEAD_DIM
    nsub = _tile(t // CHUNK, nsub, 1)
    rows = nsub * CHUNK
    nj = t // rows
    reverse = direction == 1
    final = o_fwd is not None
    n_lvl = masks.shape[0]
    blk = (lambda j: nj - 1 - j) if reverse else (lambda j: j)
    zcol = 2 if reverse else 1
    in_specs = [
        pl.BlockSpec((None, rows, width), lambda bi, j: (bi, blk(j), 0)),
        pl.BlockSpec((None, rows, width), lambda bi, j: (bi, blk(j), zcol)),
        pl.BlockSpec((None, rows, width), lambda bi, j: (bi, blk(j), 3)),
        pl.BlockSpec((1, width), lambda bi, j: (0, 0)),
        pl.BlockSpec(masks.shape, lambda bi, j: (0, 0, 0)),
        pl.BlockSpec((None, heads, HEAD_DIM, HEAD_DIM), lambda bi, j: (bi, 0, 0, 0)),
    ]
    args = [p, p, p, lb.reshape(1, width), masks, s0]
    if final:
        in_specs += [
            pl.BlockSpec((None, rows, width), lambda bi, j: (bi, blk(j), 0)),
            pl.BlockSpec((None, rows, width), lambda bi, j: (bi, blk(j), 4)),
            pl.BlockSpec((1, HEAD_DIM), lambda bi, j: (0, 0)),
        ]
        args += [o_fwd, p, hnw.reshape(1, HEAD_DIM)]
    return pl.pallas_call(
        functools.partial(_hgrn_kernel, nsub=nsub, heads=heads, reverse=reverse, final=final, n_lvl=n_lvl),
        grid=(b, nj),
        in_specs=in_specs,
        out_specs=[
            pl.BlockSpec((None, rows, width), lambda bi, j: (bi, blk(j), 0)),
            pl.BlockSpec((None, heads, HEAD_DIM, HEAD_DIM), lambda bi, j: (bi, 0, 0, 0)),
        ],
        out_shape=[
            jax.ShapeDtypeStruct((b, t, width), F32),
            jax.ShapeDtypeStruct((b, heads, HEAD_DIM, HEAD_DIM), F32),
        ],
        scratch_shapes=[pltpu.VMEM((heads, HEAD_DIM, HEAD_DIM), F32)],
        compiler_params=_params(("arbitrary", "arbitrary")),
        name="hgrn_bwd" if final else "hgrn_fwd",
    )(*args)


def _mixout_kernel(x_ref, y_ref, u_ref, hg_ref, m_ref, dsk_ref, wglu_ref, bglu_ref, wo_ref, o_ref,
                   *, s5w, rb, colmajor):
    yy = y_ref[...] + dsk_ref[...] * u_ref[...]
    g = _gelu_tanh(yy)
    zz = jnp.dot(g.astype(BF16), wglu_ref[...], preferred_element_type=F32) + bglu_ref[...]
    s5 = (g * jax.nn.sigmoid(zz)).astype(BF16)
    if colmajor:
        hg = jnp.concatenate([hg_ref[:, r, :] for r in range(rb)], axis=0).astype(BF16)
    else:
        hg = hg_ref[...].astype(BF16)
    acc = jnp.dot(s5, wo_ref[:s5w, :], preferred_element_type=F32)
    acc += jnp.dot(hg, wo_ref[s5w:, :], preferred_element_type=F32)
    o_ref[...] = x_ref[...] + m_ref[0, 5:6, :] * acc


def _mixout(x3d, y3d, u3d, hg, mods, d_skip, w_glu, b_glu, w_o, *, layer, mod_of_batch, colmajor, rb=8):
    b, t, d = x3d.shape
    s5w = y3d.shape[-1]
    hw = hg.shape[-1]
    if colmajor:
        cols = GRID_W
        rows = t // cols
        rb = _tile(rows, rb, 8)
        tm = rb * cols
        hg_v = hg.reshape(b, cols, rows, hw)
        hg_spec = pl.BlockSpec((None, cols, rb, hw), lambda bi, i: (bi, 0, i, 0))
    else:
        rb = 1
        tm = _tile(t, 512)
        hg_v = hg
        hg_spec = pl.BlockSpec((None, tm, hw), lambda bi, i: (bi, i, 0))
    mod_fn = (lambda bi: bi) if mod_of_batch else (lambda bi: b)
    return pl.pallas_call(
        functools.partial(_mixout_kernel, s5w=s5w, rb=rb, colmajor=colmajor),
        grid=(b, t // tm),
        in_specs=[
            pl.BlockSpec((None, tm, d), lambda bi, i: (bi, i, 0)),
            pl.BlockSpec((None, tm, s5w), lambda bi, i: (bi, i, 0)),
            pl.BlockSpec((None, tm, s5w), lambda bi, i: (bi, i, 0)),
            hg_spec,
            pl.BlockSpec((1, N_MOD, d), lambda bi, i: (mod_fn(bi), 0, 0)),
            pl.BlockSpec((1, s5w), lambda bi, i: (0, 0)),
            pl.BlockSpec((None, s5w, s5w), lambda bi, i: (layer, 0, 0)),
            pl.BlockSpec((1, s5w), lambda bi, i: (0, 0)),
            pl.BlockSpec((None, s5w + hw, d), lambda bi, i: (layer, 0, 0)),
        ],
        out_specs=pl.BlockSpec((None, tm, d), lambda bi, i: (bi, i, 0)),
        out_shape=jax.ShapeDtypeStruct((b, t, d), F32),
        compiler_params=_params(("arbitrary", "arbitrary")),
        name="mixout",
    )(x3d, y3d, u3d, hg_v, mods, d_skip.reshape(1, s5w), w_glu, b_glu.reshape(1, s5w), w_o)


def kernel(x, c, ctx, c_ctx, w_ada, b_ada, norm_w, ffn_w_gate, ffn_w_up, ffn_w_down, w_in, w_out,
           s5_lambda_re, s5_lambda_im, s5_log_step, s5_b_re, s5_b_im, s5_c_re, s5_c_im, s5_d,
           s5_w_glu, s5_b_glu, hgrn_lower_bounds, hgrn_norm_w, final_norm_w):
    batch, seq, d = x.shape
    n_ctx = ctx.shape[1]
    depth = w_ada.shape[0]
    s5w = s5_d.shape[-1]
    hw = hgrn_lower_bounds.shape[-1]
    rows = seq // GRID_W
    assert batch < MOD_ROWS and rows % CHUNK == 0 and n_ctx % CHUNK == 0
    assert seq % S5_BLOCK == 0 and n_ctx % S5_BLOCK == 0 and (s5w // S5_GROUP) % S5_OCT == 0

    cvec = jnp.zeros((MOD_ROWS, d), F32).at[:batch].set(c.astype(F32)).at[batch].set(c_ctx.astype(F32))
    mods_all = _ada(cvec, w_ada, b_ada).reshape(depth, MOD_ROWS, N_MOD, d)

    lb_soft = jax.nn.softmax(hgrn_lower_bounds.astype(F32), axis=0)
    lb_all = jnp.cumsum(lb_soft, axis=0) - lb_soft[0]

    masks_np = _hgrn_masks()
    masks_f = jnp.asarray(masks_np, F32)
    masks_b = jnp.asarray(masks_np[:, ::-1, ::-1].copy(), F32)
    heads = hw // HEAD_DIM
    s_zero = jnp.zeros((batch, heads, HEAD_DIM, HEAD_DIM), F32)

    wg = ffn_w_gate.astype(BF16)
    wu = ffn_w_up.astype(BF16)
    wd = ffn_w_down.astype(BF16)
    uk, pb3, ca3, dec = (jnp.stack(a) for a in zip(*[
        _s5_compact(s5_lambda_re[l], s5_lambda_im[l], s5_log_step[l], s5_b_re[l], s5_b_im[l],
                    s5_c_re[l], s5_c_im[l]) for l in range(depth)]))
    kmat, we, wy = _s5_expand(uk, pb3, ca3)
    w_in_b = w_in.astype(BF16)
    w_h = w_in_b[:, :, s5w:]
    w_o = w_out.astype(BF16)
    w_glu = s5_w_glu.astype(BF16)

    xl = x.astype(F32)
    xc = ctx.astype(F32)
    lat = dict(rows_per_mod=seq, mod_off=0)
    cx = dict(rows_per_mod=batch * n_ctx, mod_off=batch)
    for l in range(depth):
        last = l == depth - 1
        mods = mods_all[l]

        xl = _ffn(xl.reshape(batch * seq, d), mods, norm_w[l, 0], wg, wu, wd, layer=l, half=0, base=0,
                  **lat).reshape(batch, seq, d)
        xc = _ffn(xc.reshape(batch * n_ctx, d), mods, norm_w[l, 0], wg, wu, wd, layer=l, half=0, base=0,
                  **cx).reshape(batch, n_ctx, d)

        u_l = _uproj(xl.reshape(batch * seq, d), mods, norm_w[l, 1], w_in_b, layer=l, s5w=s5w, **lat)
        u_c = _uproj(xc.reshape(batch * n_ctx, d), mods, norm_w[l, 1], w_in_b, layer=l, s5w=s5w, **cx)
        p_lat = _hproj(xl, mods, norm_w[l, 1], w_h, layer=l, hw=hw, cols=GRID_W, mod_row=None)
        p_ctx = _hproj(xc.reshape(1, batch * n_ctx, d), mods, norm_w[l, 1], w_h, layer=l, hw=hw, cols=1,
                       mod_row=batch).reshape(batch, n_ctx, 5 * hw)

        e_l = _s5e(u_l, we, layer=l)
        e_c = _s5e(u_c, we, layer=l)
        st_c, st_l = _s5scan(e_c, e_l, dec, layer=l, nb=batch)
        y_l = _s5y(u_l, st_l, kmat, wy, layer=l).reshape(batch, seq, s5w)

        lb_f = lb_all[l, 0]
        lb_b = lb_all[l, 1]
        oc_f, sc_f = _hgrn(p_ctx, lb_f, masks_f, s_zero, direction=0, width=hw)
        ol_f, _ = _hgrn(p_lat, lb_f, masks_f, sc_f, direction=0, width=hw)
        hg_c, sc_b = _hgrn(p_ctx, lb_b, masks_b, s_zero, direction=1, width=hw,
                           o_fwd=oc_f, hnw=hgrn_norm_w[l])
        hg_l, _ = _hgrn(p_lat, lb_b, masks_b, sc_b, direction=1, width=hw,
                        o_fwd=ol_f, hnw=hgrn_norm_w[l])

        xl = _mixout(xl, y_l, u_l.reshape(batch, seq, s5w), hg_l, mods, s5_d[l], w_glu, s5_b_glu[l], w_o,
                     layer=l, mod_of_batch=True, colmajor=True)
        if not last:
            y_c = _s5y(u_c, st_c, kmat, wy, layer=l).reshape(batch, n_ctx, s5w)
            xc = _mixout(xc, y_c, u_c.reshape(batch, n_ctx, s5w), hg_c, mods, s5_d[l], w_glu, s5_b_glu[l],
                         w_o, layer=l, mod_of_batch=False, colmajor=False)

        xl = _ffn(xl.reshape(batch * seq, d), mods, norm_w[l, 2], wg, wu, wd, layer=l, half=1, base=6,
                  final_nw=final_norm_w if last else None, **lat).reshape(batch, seq, d)
        if not last:
            xc = _ffn(xc.reshape(batch * n_ctx, d), mods, norm_w[l, 2], wg, wu, wd, layer=l, half=1, base=6,
                      **cx).reshape(batch, n_ctx, d)
    return xl
```

```python
import functools

import numpy as np
import jax
import jax.numpy as jnp
from jax import lax
from jax.experimental import pallas as pl
from jax.experimental.pallas import tpu as pltpu

F32 = jnp.float32
BF16 = jnp.bfloat16

EPS = 1e-6
F_MIN = 1e-6
LAMBDA_RE_MAX = -1e-4
GRID_W = 64
N_MOD = 9
S5_GROUP = 16
S5_STATE = 64
S5_BLOCK = 8
S5_OCT = 8
HEAD_DIM = 128
CHUNK = 64
SUB = 8
LANE = 128
MOD_ROWS = 8
VMEM_LIMIT = 60 * 1024 * 1024


def _params(sem):
    return pltpu.CompilerParams(dimension_semantics=sem, vmem_limit_bytes=VMEM_LIMIT)


def _tile(n, pref, mult=8):
    t = min(n, pref)
    while t > 0:
        if n % t == 0 and t % mult == 0:
            return t
        t -= 1
    return n


def _norm_mod(x, nw, shift, scale):
    ms = jnp.mean(x * x, axis=-1, keepdims=True)
    gain = nw * (1.0 + scale)
    return (x * lax.rsqrt(ms + EPS)) * gain + shift


def _silu(x):
    return x * jax.nn.sigmoid(x)


def _gelu_tanh(x):
    return 0.5 * x * (1.0 + jnp.tanh(0.7978845608028654 * (x + 0.044715 * (x * x * x))))


def _ada_kernel(c_ref, w_ref, b_ref, o_ref):
    a = _silu(c_ref[...])
    a_hi = a.astype(BF16)
    a_lo = (a - a_hi.astype(F32)).astype(BF16)
    w = w_ref[...]
    w_hi = w.astype(BF16)
    w_lo = (w - w_hi.astype(F32)).astype(BF16)
    both = jnp.dot(jnp.concatenate([a_hi, a_lo], axis=0), w_hi, preferred_element_type=F32)
    acc = both[:MOD_ROWS] + both[MOD_ROWS:] + jnp.dot(a_hi, w_lo, preferred_element_type=F32)
    o_ref[...] = acc + b_ref[...]


def _ada(cvec, w_ada, b_ada):
    depth, d, n = w_ada.shape
    tn = _tile(n, 2048, LANE)
    return pl.pallas_call(
        _ada_kernel,
        grid=(depth, n // tn),
        in_specs=[
            pl.BlockSpec((MOD_ROWS, d), lambda l, j: (0, 0)),
            pl.BlockSpec((None, d, tn), lambda l, j: (l, 0, j)),
            pl.BlockSpec((None, 1, tn), lambda l, j: (l, 0, j)),
        ],
        out_specs=pl.BlockSpec((None, MOD_ROWS, tn), lambda l, j: (l, 0, j)),
        out_shape=jax.ShapeDtypeStruct((depth, MOD_ROWS, n), F32),
        compiler_params=_params(("arbitrary", "arbitrary")),
        name="ada",
    )(cvec, w_ada, b_ada.reshape(depth, 1, n))


def _ffn_kernel(*refs, base, n_f, final, tail):
    if final:
        x_ref, m_ref, nw_ref, wg_ref, wu_ref, wd_ref, fnw_ref, o_ref, h_ref = refs
    else:
        x_ref, m_ref, nw_ref, wg_ref, wu_ref, wd_ref, o_ref, h_ref = refs
    j = pl.program_id(1)

    @pl.when(j == 0)
    def _():
        h = _norm_mod(x_ref[...], nw_ref[...], m_ref[0, base:base + 1, :], m_ref[0, base + 1:base + 2, :])
        h_ref[...] = h.astype(BF16)

    def partial_sum(valid=None):
        h = h_ref[...]
        g = jnp.dot(h, wg_ref[...], preferred_element_type=F32)
        u = jnp.dot(h, wu_ref[...], preferred_element_type=F32)
        a = (_silu(g) * u).astype(BF16)
        wd = wd_ref[...]
        if valid is not None:
            a = jnp.where(lax.broadcasted_iota(jnp.int32, a.shape, 1) < valid, a, jnp.zeros_like(a))
            wd = jnp.where(lax.broadcasted_iota(jnp.int32, wd.shape, 0) < valid, wd, jnp.zeros_like(wd))
        return jnp.dot(a, wd, preferred_element_type=F32)

    @pl.when(j == 0)
    def _():
        o_ref[...] = partial_sum()

    @pl.when((j > 0) & (j < n_f - 1))
    def _():
        o_ref[...] += partial_sum()

    @pl.when(j == n_f - 1)
    def _():
        y = x_ref[...] + (0.5 * m_ref[0, base + 2:base + 3, :]) * (o_ref[...] + partial_sum(tail))
        if final:
            ms = jnp.mean(y * y, axis=-1, keepdims=True)
            y = y * lax.rsqrt(ms + EPS) * fnw_ref[...]
        o_ref[...] = y


def _ffn(x2d, mods, nw, wg, wu, wd, *, layer, half, base, rows_per_mod, mod_off, final_nw=None,
         tm=1024, tf=512):
    n, d = x2d.shape
    fp = wg.shape[-1]
    tm = _tile(min(n, rows_per_mod), tm)
    n_f = pl.cdiv(fp, tf)
    tail = fp - (n_f - 1) * tf if fp % tf else None
    assert n_f >= 2
    final = final_nw is not None
    in_specs = [
        pl.BlockSpec((tm, d), lambda i, j: (i, 0)),
        pl.BlockSpec((1, N_MOD, d), lambda i, j: ((i * tm) // rows_per_mod + mod_off, 0, 0)),
        pl.BlockSpec((1, d), lambda i, j: (0, 0)),
        pl.BlockSpec((None, None, d, tf), lambda i, j: (layer, half, 0, j)),
        pl.BlockSpec((None, None, d, tf), lambda i, j: (layer, half, 0, j)),
        pl.BlockSpec((None, None, tf, d), lambda i, j: (layer, half, j, 0)),
    ]
    args = [x2d, mods, nw.reshape(1, d), wg, wu, wd]
    if final:
        in_specs.append(pl.BlockSpec((1, d), lambda i, j: (0, 0)))
        args.append(final_nw.reshape(1, d))
    return pl.pallas_call(
        functools.partial(_ffn_kernel, base=base, n_f=n_f, final=final, tail=tail),
        grid=(n // tm, n_f),
        in_specs=in_specs,
        out_specs=pl.BlockSpec((tm, d), lambda i, j: (i, 0)),
        out_shape=jax.ShapeDtypeStruct((n, d), F32),
        scratch_shapes=[pltpu.VMEM((tm, d), BF16)],
        compiler_params=_params(("arbitrary", "arbitrary")),
        name="ffn_final" if final else "ffn",
    )(*args)


def _uproj_kernel(x_ref, m_ref, nw_ref, w_ref, u_ref):
    h = _norm_mod(x_ref[...], nw_ref[...], m_ref[0, 3:4, :], m_ref[0, 4:5, :])
    u_ref[...] = jnp.dot(h.astype(BF16), w_ref[...], preferred_element_type=F32)


def _uproj(x2d, mods, nw, w_in, *, layer, s5w, rows_per_mod, mod_off, tm=1024):
    n, d = x2d.shape
    tm = _tile(min(n, rows_per_mod), tm)
    return pl.pallas_call(
        _uproj_kernel,
        grid=(n // tm,),
        in_specs=[
            pl.BlockSpec((tm, d), lambda i: (i, 0)),
            pl.BlockSpec((1, N_MOD, d), lambda i: ((i * tm) // rows_per_mod + mod_off, 0, 0)),
            pl.BlockSpec((1, d), lambda i: (0, 0)),
            pl.BlockSpec((None, d, s5w), lambda i: (layer, 0, 0)),
        ],
        out_specs=pl.BlockSpec((tm, s5w), lambda i: (i, 0)),
        out_shape=jax.ShapeDtypeStruct((n, s5w), F32),
        compiler_params=_params(("arbitrary",)),
        name="uproj",
    )(x2d, mods, nw.reshape(1, d), w_in)


def _hproj_kernel(*refs, colmajor, hw):
    if colmajor:
        x_ref, m_ref, nw_ref, w_ref, perm_ref, o_ref = refs
    else:
        x_ref, m_ref, nw_ref, w_ref, o_ref = refs
    h = _norm_mod(x_ref[...], nw_ref[...], m_ref[0, 3:4, :], m_ref[0, 4:5, :]).astype(BF16)
    if colmajor:
        h = h.reshape(perm_ref.shape[0], h.shape[-1])
        h = jnp.dot(perm_ref[...], h, preferred_element_type=F32).astype(BF16)
    for nb in range(o_ref.shape[-1] // hw):
        p = jnp.dot(h, w_ref[:, nb * hw:(nb + 1) * hw], preferred_element_type=F32)
        o_ref[:, nb * hw:(nb + 1) * hw] = _silu(p) if nb == 0 else p


def _hproj(x3d, mods, nw, w_h, *, layer, hw, cols, mod_row, cb=8, tr=512):
    b, t, d = x3d.shape
    nh = w_h.shape[-1]
    colmajor = cols > 1
    mod_fn = (lambda bi: bi) if mod_row is None else (lambda bi: mod_row)
    in_specs = [
        None,
        pl.BlockSpec((1, N_MOD, d), lambda bi, i: (mod_fn(bi), 0, 0)),
        pl.BlockSpec((1, d), lambda bi, i: (0, 0)),
        pl.BlockSpec((None, d, nh), lambda bi, i: (layer, 0, 0), pipeline_mode=pl.Buffered(1)),
    ]
    args = [None, mods, nw.reshape(1, d), w_h]
    if colmajor:
        rows = t // cols
        cb = _tile(cols, cb, 8)
        tr = rows * cb
        args[0] = x3d.reshape(b, rows, cols, d)
        in_specs[0] = pl.BlockSpec((None, rows, cb, d), lambda bi, i: (bi, 0, i, 0))
        src = np.arange(tr).reshape(rows, cb).T.reshape(-1)
        perm = np.zeros((tr, tr), np.float32)
        perm[np.arange(tr), src] = 1.0
        args.append(jnp.asarray(perm, BF16))
        in_specs.append(pl.BlockSpec(perm.shape, lambda bi, i: (0, 0)))
    else:
        tr = _tile(t, tr)
        args[0] = x3d
        in_specs[0] = pl.BlockSpec((None, tr, d), lambda bi, i: (bi, i, 0))
    return pl.pallas_call(
        functools.partial(_hproj_kernel, colmajor=colmajor, hw=hw),
        grid=(b, t // tr),
        in_specs=in_specs,
        out_specs=pl.BlockSpec((None, tr, nh), lambda bi, i: (bi, i, 0)),
        out_shape=jax.ShapeDtypeStruct((b, t, nh), F32),
        compiler_params=_params(("arbitrary", "arbitrary")),
        name="hproj",
    )(*args)


def _s5_sel_consts():
    t, h, p, o8 = S5_BLOCK, S5_GROUP, S5_STATE, S5_OCT
    colsel = np.zeros((t, 2, t, h, t, o8, h), np.float32)
    for r in range(t):
        for r2 in range(t):
            for hh in range(h):
                if r2 >= r:
                    colsel[r, 0, r2 - r, hh, r2, :, hh] = 1.0
                if r >= r2:
                    colsel[r, 1, r - r2, hh, r2, :, hh] = 1.0
    colsel = colsel.reshape(t, 2 * t * h, t * o8 * h)
    tile_e = np.zeros((4, p, 4, o8, p), np.float32)
    tile_y = np.zeros((4, h, 4, o8, h), np.float32)
    for dp in range(4):
        for i in range(p):
            tile_e[dp, i, dp, :, i] = 1.0
        for i in range(h):
            tile_y[dp, i, dp, :, i] = 1.0
    return colsel, tile_e.reshape(4 * p, 4 * o8 * p), tile_y.reshape(4 * h, 4 * o8 * h)


def _s5w_kernel(uk_ref, pb_ref, caf_ref, cab_ref, colsel_ref, tile_e_ref, tile_y_ref, k_ref, we_ref, wy_ref):
    t, h, p, o8 = S5_BLOCK, S5_GROUP, S5_STATE, S5_OCT
    gh = o8 * h
    gp = o8 * p

    def diag_mask(shape, row_div, col_mod, col_div):
        rg = lax.broadcasted_iota(jnp.int32, shape, 0) // row_div
        cg = (lax.broadcasted_iota(jnp.int32, shape, 1) % col_mod) // col_div
        return rg == cg

    mk = diag_mask((gh, t * gh), h, gh, h)
    me = diag_mask((gh, 4 * gp), h, gp, p)
    my = diag_mask((gp, 4 * gh), p, gh, h)
    uk = uk_ref[...].astype(BF16)
    for r in range(t):
        blk = jnp.dot(uk, colsel_ref[r], preferred_element_type=F32)
        k_ref[r * gh:(r + 1) * gh, :] = jnp.where(mk, blk, 0.0).astype(BF16)
        blk = jnp.dot(pb_ref[r].astype(BF16), tile_e_ref[...], preferred_element_type=F32)
        we_ref[r * gh:(r + 1) * gh, :] = jnp.where(me, blk, 0.0).astype(BF16)
        half = tile_y_ref.shape[0] // 2
        blk = jnp.dot(caf_ref[r].astype(BF16), tile_y_ref[:half, :], preferred_element_type=F32)
        blk += jnp.dot(cab_ref[r].astype(BF16), tile_y_ref[half:, :], preferred_element_type=F32)
        blk = jnp.where(my, blk, 0.0)
        for dp in range(4):
            wy_ref[dp * gp:(dp + 1) * gp, r * gh:(r + 1) * gh] = blk[:, dp * gh:(dp + 1) * gh].astype(BF16)


def _s5_compact(lam_re, lam_im, log_step, b_re, b_im, c_re, c_im):
    hp = lax.Precision.HIGHEST
    t = S5_BLOCK
    g, p = lam_re.shape[1:]
    h = b_re.shape[-1]
    n_oct = g // S5_OCT
    lam_re = jnp.minimum(lam_re.astype(F32), LAMBDA_RE_MAX)
    lam_im = lam_im.astype(F32)
    dt = jnp.exp(log_step.astype(F32))[..., None]
    mag = jnp.exp(lam_re * dt)
    lb_re = mag * jnp.cos(lam_im * dt)
    lb_im = mag * jnp.sin(lam_im * dt)
    den = lam_re * lam_re + lam_im * lam_im
    nr = lb_re - 1.0
    ni = lb_im
    cf_re = (nr * lam_re + ni * lam_im) / den
    cf_im = (ni * lam_re - nr * lam_im) / den
    b_re = b_re.astype(F32)
    b_im = b_im.astype(F32)
    br = cf_re[..., None] * b_re - cf_im[..., None] * b_im
    bi = cf_re[..., None] * b_im + cf_im[..., None] * b_re
    cr = c_re.astype(F32)
    ci = c_im.astype(F32)
    expo = np.concatenate([np.arange(t + 1), np.arange(t - 1, -1, -1), np.arange(t, 0, -1)]).astype(np.float32)
    j = jnp.asarray(expo)[None, :, None, None]
    pmag = jnp.exp(j * (lam_re * dt)[:, None])
    pw_re = pmag * jnp.cos(j * (lam_im * dt)[:, None])
    pw_im = pmag * jnp.sin(j * (lam_im * dt)[:, None])
    asc, desc0, desc1 = slice(0, t), slice(t + 1, 2 * t + 1), slice(2 * t + 1, 3 * t + 1)

    cl_re = cr[:, None] * pw_re[:, asc, :, None, :] - ci[:, None] * pw_im[:, asc, :, None, :]
    cl_im = cr[:, None] * pw_im[:, asc, :, None, :] + ci[:, None] * pw_re[:, asc, :, None, :]
    kj = (jnp.einsum('djgep,dgph->djgeh', cl_re, br, precision=hp)
          - jnp.einsum('djgep,dgph->djgeh', cl_im, bi, precision=hp))
    kj = jnp.stack([jnp.concatenate([kj[0, :1] + kj[1, :1], kj[0, 1:]], axis=0),
                    jnp.concatenate([jnp.zeros_like(kj[1, :1]), kj[1, 1:]], axis=0)])
    uk = jnp.transpose(kj, (2, 4, 0, 1, 3)).reshape(n_oct, S5_OCT * h, 2 * t * h)

    def oct_rows(a):
        return jnp.swapaxes(a.reshape(2, a.shape[1], n_oct, S5_OCT, p), 1, 2)

    def lanes4(f_a, f_b, b_a, b_b):
        return jnp.concatenate([f_a, f_b, b_a, b_b], axis=-1)

    pr_f, pi_f = oct_rows(pw_re[:, desc0])[0], oct_rows(pw_im[:, desc0])[0]
    pr_b, pi_b = oct_rows(pw_re[:, asc])[1], oct_rows(pw_im[:, asc])[1]
    br_t = jnp.swapaxes(br, 2, 3).reshape(2, n_oct, 1, S5_OCT, h, p)
    bi_t = jnp.swapaxes(bi, 2, 3).reshape(2, n_oct, 1, S5_OCT, h, p)
    p1 = lanes4(pr_f, pr_f, pr_b, pr_b)[:, :, :, None, :]
    p2 = lanes4(-pi_f, pi_f, -pi_b, pi_b)[:, :, :, None, :]
    b1 = lanes4(br_t[0], bi_t[0], br_t[1], bi_t[1])
    b2 = lanes4(bi_t[0], br_t[0], bi_t[1], br_t[1])
    pb3 = (p1 * b1 + p2 * b2).reshape(n_oct, t, S5_OCT * h, 4 * p)

    cr_t = jnp.swapaxes(cr, 2, 3).reshape(2, n_oct, 1, S5_OCT, p, h)
    ci_t = jnp.swapaxes(ci, 2, 3).reshape(2, n_oct, 1, S5_OCT, p, h)
    qr_f, qi_f = oct_rows(pw_re[:, 1:t + 1])[0], oct_rows(pw_im[:, 1:t + 1])[0]
    qr_b, qi_b = oct_rows(pw_re[:, desc1])[1], oct_rows(pw_im[:, desc1])[1]

    def c_lam(d_, qr, qi):
        c1 = jnp.concatenate([cr_t[d_], -ci_t[d_]], axis=-1)
        c2 = jnp.concatenate([-ci_t[d_], -cr_t[d_]], axis=-1)
        return (c1 * qr[..., None] + c2 * qi[..., None]).reshape(n_oct, t, S5_OCT * p, 2 * h)

    ca3 = (c_lam(0, qr_f, qi_f), c_lam(1, qr_b, qi_b))
    nk2 = S5_OCT * p // LANE
    a_re = jnp.transpose(pw_re[:, t].reshape(2, n_oct, nk2, LANE), (1, 0, 2, 3))
    a_im = jnp.transpose(pw_im[:, t].reshape(2, n_oct, nk2, LANE), (1, 0, 2, 3))
    dec = jnp.stack([jnp.concatenate([a_re, a_re], axis=2), jnp.concatenate([-a_im, a_im], axis=2)], axis=2)
    return uk, pb3, ca3, dec


def _s5_expand(uk, pb3, ca3):
    depth, n_oct = uk.shape[:2]
    colsel, tile_e, tile_y = (jnp.asarray(a, BF16) for a in _s5_sel_consts())
    kdim = S5_BLOCK * S5_OCT * S5_GROUP
    sdim = 4 * S5_OCT * S5_STATE
    return pl.pallas_call(
        _s5w_kernel,
        grid=(depth, n_oct),
        in_specs=[
            pl.BlockSpec((None, None) + uk.shape[2:], lambda l, o: (l, o, 0, 0)),
            pl.BlockSpec((None, None) + pb3.shape[2:], lambda l, o: (l, o, 0, 0, 0)),
            pl.BlockSpec((None, None) + ca3[0].shape[2:], lambda l, o: (l, o, 0, 0, 0)),
            pl.BlockSpec((None, None) + ca3[1].shape[2:], lambda l, o: (l, o, 0, 0, 0)),
            pl.BlockSpec(colsel.shape, lambda l, o: (0, 0, 0)),
            pl.BlockSpec(tile_e.shape, lambda l, o: (0, 0)),
            pl.BlockSpec(tile_y.shape, lambda l, o: (0, 0)),
        ],
        out_specs=[
            pl.BlockSpec((None, None, kdim, kdim), lambda l, o: (l, o, 0, 0)),
            pl.BlockSpec((None, None, kdim, sdim), lambda l, o: (l, o, 0, 0)),
            pl.BlockSpec((None, None, sdim, kdim), lambda l, o: (l, o, 0, 0)),
        ],
        out_shape=[
            jax.ShapeDtypeStruct((depth, n_oct, kdim, kdim), BF16),
            jax.ShapeDtypeStruct((depth, n_oct, kdim, sdim), BF16),
            jax.ShapeDtypeStruct((depth, n_oct, sdim, kdim), BF16),
        ],
        compiler_params=_params(("arbitrary", "arbitrary")),
        name="s5w",
    )(uk, pb3, ca3[0], ca3[1], colsel, tile_e, tile_y)


def _block_rows(u_ref, tmr):
    return jnp.concatenate(
        [u_ref[pl.ds(r, tmr, stride=S5_BLOCK), :].astype(BF16) for r in range(S5_BLOCK)], axis=1)


def _s5e_kernel(u_ref, we_ref, e_ref, *, tmr):
    res = jnp.dot(_block_rows(u_ref, tmr), we_ref[...], preferred_element_type=F32)
    nk = res.shape[1] // (2 * LANE)
    for d in range(2):
        for k in range(nk):
            c0 = (d * nk + k) * LANE
            e_ref[d, pl.ds(k, tmr, stride=nk), :] = res[:, c0:c0 + LANE]


def _s5e(u, we, *, layer, tmr=1024):
    n, wid = u.shape
    m = n // S5_BLOCK
    _, n_oct, kdim, ncol = we.shape
    nk = ncol // (2 * LANE)
    tmr = _tile(m, tmr)
    return pl.pallas_call(
        functools.partial(_s5e_kernel, tmr=tmr),
        grid=(n_oct, m // tmr),
        in_specs=[
            pl.BlockSpec((tmr * S5_BLOCK, LANE), lambda o, i: (i, o)),
            pl.BlockSpec((None, None, kdim, ncol), lambda o, i: (layer, o, 0, 0)),
        ],
        out_specs=pl.BlockSpec((None, 2, tmr * nk, LANE), lambda o, i: (o, 0, i, 0)),
        out_shape=jax.ShapeDtypeStruct((n_oct, 2, m * nk, LANE), F32),
        compiler_params=_params(("arbitrary", "arbitrary")),
        name="s5e",
    )(u, we)


def _s5scan_kernel(ec_ref, el_ref, a_ref, sc_ref, sl_ref, *, nb, n_c, n_l, nk):
    rev = pl.program_id(1) == 1
    a1 = a_ref[0]
    a2 = a_ref[1]

    def run(e_ref, s_ref, n, carry):
        def body(s, carry):
            i = jnp.where(rev, n - 1 - s, s)
            new = []
            for b in range(nb):
                x = carry[b]
                rows = pl.ds(pl.multiple_of((b * n + i) * nk, nk), nk)
                s_ref[rows, :] = x
                new.append(a1 * x + a2 * pltpu.roll(x, nk // 2, 0) + e_ref[rows, :])
            return tuple(new)
        return lax.fori_loop(0, n, body, carry, unroll=2)

    carry = tuple(jnp.zeros((nk, LANE), F32) for _ in range(nb))
    carry = run(ec_ref, sc_ref, n_c, carry)
    run(el_ref, sl_ref, n_l, carry)


def _s5scan(e_ctx, e_lat, dec, *, layer, nb):
    n_oct, _, rc, _ = e_ctx.shape
    rl = e_lat.shape[2]
    nk = dec.shape[-2]
    return pl.pallas_call(
        functools.partial(_s5scan_kernel, nb=nb, n_c=rc // (nk * nb), n_l=rl // (nk * nb), nk=nk),
        grid=(n_oct, 2),
        in_specs=[
            pl.BlockSpec((None, None, rc, LANE), lambda o, d: (o, d, 0, 0)),
            pl.BlockSpec((None, None, rl, LANE), lambda o, d: (o, d, 0, 0)),
            pl.BlockSpec((None, None, None, 2, nk, LANE), lambda o, d: (layer, o, d, 0, 0, 0)),
        ],
        out_specs=[
            pl.BlockSpec((None, None, rc, LANE), lambda o, d: (o, d, 0, 0)),
            pl.BlockSpec((None, None, rl, LANE), lambda o, d: (o, d, 0, 0)),
        ],
        out_shape=[jax.ShapeDtypeStruct(e_ctx.shape, F32), jax.ShapeDtypeStruct(e_lat.shape, F32)],
        compiler_params=_params(("arbitrary", "arbitrary")),
        name="s5scan",
    )(e_ctx, e_lat, dec)


def _s5y_kernel(u_ref, s_ref, k_ref, wy_ref, y_ref, *, tmr):
    nk = s_ref.shape[1] // tmr
    st = jnp.concatenate(
        [s_ref[d, pl.ds(k, tmr, stride=nk), :].astype(BF16) for d in range(2) for k in range(nk)], axis=1)
    res = jnp.dot(_block_rows(u_ref, tmr), k_ref[...], preferred_element_type=F32)
    res += jnp.dot(st, wy_ref[...], preferred_element_type=F32)
    for r in range(S5_BLOCK):
        y_ref[pl.ds(r, tmr, stride=S5_BLOCK), :] = res[:, r * LANE:(r + 1) * LANE]


def _s5y(u, s, kmat, wy, *, layer, tmr=1024):
    n, wid = u.shape
    m = n // S5_BLOCK
    _, n_oct, kdim, _ = kmat.shape
    sdim = wy.shape[2]
    nk = s.shape[2] // m
    tmr = _tile(m, tmr)
    return pl.pallas_call(
        functools.partial(_s5y_kernel, tmr=tmr),
        grid=(n_oct, m // tmr),
        in_specs=[
            pl.BlockSpec((tmr * S5_BLOCK, LANE), lambda o, i: (i, o)),
            pl.BlockSpec((None, 2, tmr * nk, LANE), lambda o, i: (o, 0, i, 0)),
            pl.BlockSpec((None, None, kdim, kdim), lambda o, i: (layer, o, 0, 0)),
            pl.BlockSpec((None, None, sdim, kdim), lambda o, i: (layer, o, 0, 0)),
        ],
        out_specs=pl.BlockSpec((tmr * S5_BLOCK, LANE), lambda o, i: (i, o)),
        out_shape=jax.ShapeDtypeStruct((n, wid), F32),
        compiler_params=_params(("arbitrary", "arbitrary")),
        name="s5y",
    )(u, s, kmat, wy)


def _hgrn_masks():
    c = CHUNK
    t = np.arange(c)
    masks = []
    half = c // 2
    while half >= SUB:
        par = 2 * half
        second = (t % par) >= half
        same_parent = (t[:, None] // par) == (t[None, :] // par)
        masks.append((same_parent & second[:, None] & (~second)[None, :]).astype(np.float32))
        half //= 2
    masks.append((((t[:, None] // SUB) == (t[None, :] // SUB)) & (t[None, :] <= t[:, None])).astype(np.float32))
    return np.stack(masks)


def _hgrn_decays(g, reverse):
    nb = CHUNK // SUB
    g3 = g.reshape(nb, SUB, g.shape[1])
    r = lax.broadcasted_iota(jnp.int32, g3.shape, 1)
    p = g3
    for k in (1, 2, 4):
        if reverse:
            p = p + jnp.where(r <= SUB - 1 - k, pltpu.roll(p, SUB - k, 1), 0.0)
        else:
            p = p + jnp.where(r >= k, pltpu.roll(p, k, 1), 0.0)
    last = 0 if reverse else SUB - 1
    ref = SUB // 2 if reverse else SUB // 2 - 1
    tot = jnp.broadcast_to(p[:, last:last + 1, :], p.shape)
    ep = jnp.exp(p)
    eq = jnp.exp(tot - p)
    ed = p - jnp.broadcast_to(p[:, ref:ref + 1, :], p.shape)
    edp = jnp.exp(ed)
    edn = jnp.exp(-ed)
    et = jnp.broadcast_to(ep[:, last:last + 1, :], p.shape)
    mem = (lambda i: nb - 1 - i) if reverse else (lambda i: i)
    epb = [ep[mem(i)] for i in range(nb)]
    eqb = [eq[mem(i)] for i in range(nb)]
    etb = [et[mem(i)] for i in range(nb)]

    def assemble(blocks):
        return jnp.concatenate([blocks[mem(i)] for i in range(nb)], axis=0)

    cq = [None] * nb
    acc = None
    for i in range(nb):
        cq[i] = epb[i] if acc is None else epb[i] * acc
        acc = etb[i] if acc is None else acc * etb[i]
    total = acc[0:1, :]
    ck = [None] * nb
    acc = None
    for i in range(nb - 1, -1, -1):
        ck[i] = eqb[i] if acc is None else eqb[i] * acc
        acc = etb[i] if acc is None else acc * etb[i]
    levels = []
    half = nb // 2
    while half >= 1:
        par = 2 * half
        blocks = []
        for i in range(nb):
            j = i % par
            if j >= half:
                f = epb[i]
                for m in range(i - j + half, i):
                    f = f * etb[m]
            else:
                f = eqb[i]
                for m in range(i + 1, i - j + half):
                    f = f * etb[m]
            blocks.append(f)
        levels.append(assemble(blocks))
        half //= 2
    return levels, edp.reshape(g.shape), edn.reshape(g.shape), assemble(cq), assemble(ck), total


def _hgrn_kernel(*refs, nsub, heads, reverse, final, n_lvl):
    if final:
        (q_ref, z_ref, v_ref, lb_ref, masks_ref, s0_ref, of_ref, gate_ref, hnw_ref,
         o_ref, sfin_ref, st_ref) = refs
    else:
        q_ref, z_ref, v_ref, lb_ref, masks_ref, s0_ref, o_ref, sfin_ref, st_ref = refs
    j = pl.program_id(1)
    nj = pl.num_programs(1)
    c = CHUNK
    hd = HEAD_DIM
    nt = (((1,), (1,)), ((), ()))
    tn = (((0,), (0,)), ((), ()))

    @pl.when(j == 0)
    def _():
        st_ref[...] = s0_ref[...]

    mbool = [masks_ref[lv] > 0.0 for lv in range(n_lvl)]

    mpair = [jnp.concatenate([m, m], axis=0) for m in mbool]
    zpad = jnp.zeros((c, hd), BF16)

    def chunk(s, _):
        cl = (nsub - 1 - s) if reverse else s
        rows = pl.ds(pl.multiple_of(cl * c, c), c)
        for hp in range(heads // 2):
            sl = slice(2 * hp * hd, (2 * hp + 2) * hd)
            z = z_ref[rows, sl]
            q = q_ref[rows, sl]
            v = v_ref[rows, sl].astype(BF16)
            lb = lb_ref[:, sl]
            f = lb + (1.0 - lb) * jax.nn.sigmoid(z)
            k = 1.0 - f
            g = jnp.log(jnp.maximum(f, F_MIN))
            levels, edp, edn, cq, ck, total = _hgrn_decays(g, reverse)
            qb = q.astype(BF16)
            kb = k.astype(BF16)
            a = jnp.zeros((2 * c, c), F32)
            for lv in range(n_lvl - 1, -1, -1):
                if lv < n_lvl - 1:
                    e = levels[lv].astype(BF16)
                    ql = qb * e
                    kl = kb * e
                else:
                    ql = qb * edp.astype(BF16)
                    kl = kb * edn.astype(BF16)
                lhs = jnp.concatenate([jnp.concatenate([ql[:, :hd], zpad], axis=1),
                                       jnp.concatenate([zpad, ql[:, hd:]], axis=1)], axis=0)
                sc = lax.dot_general(lhs, kl, nt, preferred_element_type=F32)
                a = jnp.where(mpair[lv], sc, a)
            ab = a.astype(BF16)
            qd = qb * cq.astype(BF16)
            kd = kb * ck.astype(BF16)
            for i in range(2):
                h = 2 * hp + i
                hs = slice(i * hd, (i + 1) * hd)
                st = st_ref[h]
                o_h = jnp.dot(ab[i * c:(i + 1) * c], v[:, hs], preferred_element_type=F32)
                o_h += lax.dot_general(qd[:, hs], st.astype(BF16), nt, preferred_element_type=F32)
                st_ref[h] = st * total[:, hs] + lax.dot_general(v[:, hs], kd[:, hs], tn,
                                                                 preferred_element_type=F32)
                osl = slice(h * hd, (h + 1) * hd)
                if final:
                    o_h = o_h + of_ref[rows, osl]
                    ms = jnp.mean(o_h * o_h, axis=-1, keepdims=True)
                    o_h = o_h * lax.rsqrt(ms + EPS) * hnw_ref[...] * _silu(gate_ref[rows, osl])
                o_ref[rows, osl] = o_h.astype(o_ref.dtype)
        return 0

    lax.fori_loop(0, nsub, chunk, 0, unroll=4)

    @pl.when(j == nj - 1)
    def _():
        sfin_ref[...] = st_ref[...]


def _hgrn(p, lb, masks, s0, *, direction, width, o_fwd=None, hnw=None, nsub=8):
    b, t, _ = p.shape
    heads = width // HEAD_DIM
    nsub = _tile(t // CHUNK, nsub, 1)
    rows = nsub * CHUNK
    nj = t // rows
    reverse = direction == 1
    final = o_fwd is not None
    n_lvl = masks.shape[0]
    blk = (lambda j: nj - 1 - j) if reverse else (lambda j: j)
    zcol = 2 if reverse else 1
    in_specs = [
        pl.BlockSpec((None, rows, width), lambda bi, j: (bi, blk(j), 0)),
        pl.BlockSpec((None, rows, width), lambda bi, j: (bi, blk(j), zcol)),
        pl.BlockSpec((None, rows, width), lambda bi, j: (bi, blk(j), 3)),
        pl.BlockSpec((1, width), lambda bi, j: (0, 0)),
        pl.BlockSpec(masks.shape, lambda bi, j: (0, 0, 0)),
        pl.BlockSpec((None, heads, HEAD_DIM, HEAD_DIM), lambda bi, j: (bi, 0, 0, 0)),
    ]
    args = [p, p, p, lb.reshape(1, width), masks, s0]
    if final:
        in_specs += [
            pl.BlockSpec((None, rows, width), lambda bi, j: (bi, blk(j), 0)),
            pl.BlockSpec((None, rows, width), lambda bi, j: (bi, blk(j), 4)),
            pl.BlockSpec((1, HEAD_DIM), lambda bi, j: (0, 0)),
        ]
        args += [o_fwd, p, hnw.reshape(1, HEAD_DIM)]
    return pl.pallas_call(
        functools.partial(_hgrn_kernel, nsub=nsub, heads=heads, reverse=reverse, final=final, n_lvl=n_lvl),
        grid=(b, nj),
        in_specs=in_specs,
        out_specs=[
            pl.BlockSpec((None, rows, width), lambda bi, j: (bi, blk(j), 0)),
            pl.BlockSpec((None, heads, HEAD_DIM, HEAD_DIM), lambda bi, j: (bi, 0, 0, 0)),
        ],
        out_shape=[
            jax.ShapeDtypeStruct((b, t, width), F32),
            jax.ShapeDtypeStruct((b, heads, HEAD_DIM, HEAD_DIM), F32),
        ],
        scratch_shapes=[pltpu.VMEM((heads, HEAD_DIM, HEAD_DIM), F32)],
        compiler_params=_params(("arbitrary", "arbitrary")),
        name="hgrn_bwd" if final else "hgrn_fwd",
    )(*args)


def _mixout_kernel(x_ref, y_ref, u_ref, hg_ref, m_ref, dsk_ref, wglu_ref, bglu_ref, wo_ref, o_ref,
                   *, s5w, rb, colmajor):
    yy = y_ref[...] + dsk_ref[...] * u_ref[...]
    g = _gelu_tanh(yy)
    zz = jnp.dot(g.astype(BF16), wglu_ref[...], preferred_element_type=F32) + bglu_ref[...]
    s5 = (g * jax.nn.sigmoid(zz)).astype(BF16)
    if colmajor:
        hg = jnp.concatenate([hg_ref[:, r, :] for r in range(rb)], axis=0).astype(BF16)
    else:
        hg = hg_ref[...].astype(BF16)
    acc = jnp.dot(s5, wo_ref[:s5w, :], preferred_element_type=F32)
    acc += jnp.dot(hg, wo_ref[s5w:, :], preferred_element_type=F32)
    o_ref[...] = x_ref[...] + m_ref[0, 5:6, :] * acc


def _mixout(x3d, y3d, u3d, hg, mods, d_skip, w_glu, b_glu, w_o, *, layer, mod_of_batch, colmajor, rb=8):
    b, t, d = x3d.shape
    s5w = y3d.shape[-1]
    hw = hg.shape[-1]
    if colmajor:
        cols = GRID_W
        rows = t // cols
        rb = _tile(rows, rb, 8)
        tm = rb * cols
        hg_v = hg.reshape(b, cols, rows, hw)
        hg_spec = pl.BlockSpec((None, cols, rb, hw), lambda bi, i: (bi, 0, i, 0))
    else:
        rb = 1
        tm = _tile(t, 512)
        hg_v = hg
        hg_spec = pl.BlockSpec((None, tm, hw), lambda bi, i: (bi, i, 0))
    mod_fn = (lambda bi: bi) if mod_of_batch else (lambda bi: b)
    return pl.pallas_call(
        functools.partial(_mixout_kernel, s5w=s5w, rb=rb, colmajor=colmajor),
        grid=(b, t // tm),
        in_specs=[
            pl.BlockSpec((None, tm, d), lambda bi, i: (bi, i, 0)),
            pl.BlockSpec((None, tm, s5w), lambda bi, i: (bi, i, 0)),
            pl.BlockSpec((None, tm, s5w), lambda bi, i: (bi, i, 0)),
            hg_spec,
            pl.BlockSpec((1, N_MOD, d), lambda bi, i: (mod_fn(bi), 0, 0)),
            pl.BlockSpec((1, s5w), lambda bi, i: (0, 0)),
            pl.BlockSpec((None, s5w, s5w), lambda bi, i: (layer, 0, 0)),
            pl.BlockSpec((1, s5w), lambda bi, i: (0, 0)),
            pl.BlockSpec((None, s5w + hw, d), lambda bi, i: (layer, 0, 0)),
        ],
        out_specs=pl.BlockSpec((None, tm, d), lambda bi, i: (bi, i, 0)),
        out_shape=jax.ShapeDtypeStruct((b, t, d), F32),
        compiler_params=_params(("arbitrary", "arbitrary")),
        name="mixout",
    )(x3d, y3d, u3d, hg_v, mods, d_skip.reshape(1, s5w), w_glu, b_glu.reshape(1, s5w), w_o)


def kernel(x, c, ctx, c_ctx, w_ada, b_ada, norm_w, ffn_w_gate, ffn_w_up, ffn_w_down, w_in, w_out,
           s5_lambda_re, s5_lambda_im, s5_log_step, s5_b_re, s5_b_im, s5_c_re, s5_c_im, s5_d,
           s5_w_glu, s5_b_glu, hgrn_lower_bounds, hgrn_norm_w, final_norm_w):
    batch, seq, d = x.shape
    n_ctx = ctx.shape[1]
    depth = w_ada.shape[0]
    s5w = s5_d.shape[-1]
    hw = hgrn_lower_bounds.shape[-1]
    rows = seq // GRID_W
    assert batch < MOD_ROWS and rows % CHUNK == 0 and n_ctx % CHUNK == 0
    assert seq % S5_BLOCK == 0 and n_ctx % S5_BLOCK == 0 and (s5w // S5_GROUP) % S5_OCT == 0

    cvec = jnp.zeros((MOD_ROWS, d), F32).at[:batch].set(c.astype(F32)).at[batch].set(c_ctx.astype(F32))
    mods_all = _ada(cvec, w_ada, b_ada).reshape(depth, MOD_ROWS, N_MOD, d)

    lb_soft = jax.nn.softmax(hgrn_lower_bounds.astype(F32), axis=0)
    lb_all = jnp.cumsum(lb_soft, axis=0) - lb_soft[0]

    masks_np = _hgrn_masks()
    masks_f = jnp.asarray(masks_np, F32)
    masks_b = jnp.asarray(masks_np[:, ::-1, ::-1].copy(), F32)
    heads = hw // HEAD_DIM
    s_zero = jnp.zeros((batch, heads, HEAD_DIM, HEAD_DIM), F32)

    wg = ffn_w_gate.astype(BF16)
    wu = ffn_w_up.astype(BF16)
    wd = ffn_w_down.astype(BF16)
    uk, pb3, ca3, dec = jax.tree.map(lambda *a: jnp.stack(a), *[
        _s5_compact(s5_lambda_re[l], s5_lambda_im[l], s5_log_step[l], s5_b_re[l], s5_b_im[l],
                    s5_c_re[l], s5_c_im[l]) for l in range(depth)])
    kmat, we, wy = _s5_expand(uk, pb3, ca3)
    w_in_b = w_in.astype(BF16)
    w_h = w_in_b[:, :, s5w:]
    w_o = w_out.astype(BF16)
    w_glu = s5_w_glu.astype(BF16)

    xl = x.astype(F32)
    xc = ctx.astype(F32)
    lat = dict(rows_per_mod=seq, mod_off=0)
    cx = dict(rows_per_mod=batch * n_ctx, mod_off=batch)
    for l in range(depth):
        last = l == depth - 1
        mods = mods_all[l]

        xl = _ffn(xl.reshape(batch * seq, d), mods, norm_w[l, 0], wg, wu, wd, layer=l, half=0, base=0,
                  **lat).reshape(batch, seq, d)
        xc = _ffn(xc.reshape(batch * n_ctx, d), mods, norm_w[l, 0], wg, wu, wd, layer=l, half=0, base=0,
                  **cx).reshape(batch, n_ctx, d)

        u_l = _uproj(xl.reshape(batch * seq, d), mods, norm_w[l, 1], w_in_b, layer=l, s5w=s5w, **lat)
        u_c = _uproj(xc.reshape(batch * n_ctx, d), mods, norm_w[l, 1], w_in_b, layer=l, s5w=s5w, **cx)
        p_lat = _hproj(xl, mods, norm_w[l, 1], w_h, layer=l, hw=hw, cols=GRID_W, mod_row=None)
        p_ctx = _hproj(xc.reshape(1, batch * n_ctx, d), mods, norm_w[l, 1], w_h, layer=l, hw=hw, cols=1,
                       mod_row=batch).reshape(batch, n_ctx, 5 * hw)

        e_l = _s5e(u_l, we, layer=l)
        e_c = _s5e(u_c, we, layer=l)
        st_c, st_l = _s5scan(e_c, e_l, dec, layer=l, nb=batch)
        y_l = _s5y(u_l, st_l, kmat, wy, layer=l).reshape(batch, seq, s5w)

        lb_f = lb_all[l, 0]
        lb_b = lb_all[l, 1]
        oc_f, sc_f = _hgrn(p_ctx, lb_f, masks_f, s_zero, direction=0, width=hw)
        ol_f, _ = _hgrn(p_lat, lb_f, masks_f, sc_f, direction=0, width=hw)
        hg_c, sc_b = _hgrn(p_ctx, lb_b, masks_b, s_zero, direction=1, width=hw,
                           o_fwd=oc_f, hnw=hgrn_norm_w[l])
        hg_l, _ = _hgrn(p_lat, lb_b, masks_b, sc_b, direction=1, width=hw,
                        o_fwd=ol_f, hnw=hgrn_norm_w[l])

        xl = _mixout(xl, y_l, u_l.reshape(batch, seq, s5w), hg_l, mods, s5_d[l], w_glu, s5_b_glu[l], w_o,
                     layer=l, mod_of_batch=True, colmajor=True)
        if not last:
            y_c = _s5y(u_c, st_c, kmat, wy, layer=l).reshape(batch, n_ctx, s5w)
            xc = _mixout(xc, y_c, u_c.reshape(batch, n_ctx, s5w), hg_c, mods, s5_d[l], w_glu, s5_b_glu[l],
                         w_o, layer=l, mod_of_batch=False, colmajor=False)

        xl = _ffn(xl.reshape(batch * seq, d), mods, norm_w[l, 2], wg, wu, wd, layer=l, half=1, base=6,
                  final_nw=final_norm_w if last else None, **lat).reshape(batch, seq, d)
        if not last:
            xc = _ffn(xc.reshape(batch * n_ctx, d), mods, norm_w[l, 2], wg, wu, wd, layer=l, half=1, base=6,
                      **cx).reshape(batch, n_ctx, d)
    return xl
```

```python
import functools

import numpy as np
import jax
import jax.numpy as jnp
from jax import lax
from jax.experimental import pallas as pl
from jax.experimental.pallas import tpu as pltpu

F32 = jnp.float32
BF16 = jnp.bfloat16

EPS = 1e-6
F_MIN = 1e-6
LAMBDA_RE_MAX = -1e-4
GRID_W = 64
N_MOD = 9
S5_GROUP = 16
S5_STATE = 64
S5_BLOCK = 8
S5_OCT = 8
HEAD_DIM = 128
CHUNK = 64
SUB = 8
LANE = 128
MOD_ROWS = 8
VMEM_LIMIT = 60 * 1024 * 1024


def _params(sem):
    return pltpu.CompilerParams(dimension_semantics=sem, vmem_limit_bytes=VMEM_LIMIT)


def _tile(n, pref, mult=8):
    t = min(n, pref)
    while t > 0:
        if n % t == 0 and t % mult == 0:
            return t
        t -= 1
    return n


def _norm_mod(x, nw, shift, scale):
    ms = jnp.mean(x * x, axis=-1, keepdims=True)
    gain = nw * (1.0 + scale)
    return (x * lax.rsqrt(ms + EPS)) * gain + shift


def _silu(x):
    return x * jax.nn.sigmoid(x)


def _gelu_tanh(x):
    return 0.5 * x * (1.0 + jnp.tanh(0.7978845608028654 * (x + 0.044715 * (x * x * x))))


def _ada_kernel(c_ref, w_ref, b_ref, o_ref):
    a = _silu(c_ref[...])
    a_hi = a.astype(BF16)
    a_lo = (a - a_hi.astype(F32)).astype(BF16)
    w = w_ref[...]
    w_hi = w.astype(BF16)
    w_lo = (w - w_hi.astype(F32)).astype(BF16)
    both = jnp.dot(jnp.concatenate([a_hi, a_lo], axis=0), w_hi, preferred_element_type=F32)
    acc = both[:MOD_ROWS] + both[MOD_ROWS:] + jnp.dot(a_hi, w_lo, preferred_element_type=F32)
    o_ref[...] = acc + b_ref[...]


def _ada(cvec, w_ada, b_ada):
    depth, d, n = w_ada.shape
    tn = _tile(n, 2048, LANE)
    return pl.pallas_call(
        _ada_kernel,
        grid=(depth, n // tn),
        in_specs=[
            pl.BlockSpec((MOD_ROWS, d), lambda l, j: (0, 0)),
            pl.BlockSpec((None, d, tn), lambda l, j: (l, 0, j)),
            pl.BlockSpec((None, 1, tn), lambda l, j: (l, 0, j)),
        ],
        out_specs=pl.BlockSpec((None, MOD_ROWS, tn), lambda l, j: (l, 0, j)),
        out_shape=jax.ShapeDtypeStruct((depth, MOD_ROWS, n), F32),
        compiler_params=_params(("arbitrary", "arbitrary")),
        name="ada",
    )(cvec, w_ada, b_ada.reshape(depth, 1, n))


def _ffn_kernel(*refs, base, n_f, final, tail):
    if final:
        x_ref, m_ref, nw_ref, wg_ref, wu_ref, wd_ref, fnw_ref, o_ref, h_ref = refs
    else:
        x_ref, m_ref, nw_ref, wg_ref, wu_ref, wd_ref, o_ref, h_ref = refs
    j = pl.program_id(1)

    @pl.when(j == 0)
    def _():
        h = _norm_mod(x_ref[...], nw_ref[...], m_ref[0, base:base + 1, :], m_ref[0, base + 1:base + 2, :])
        h_ref[...] = h.astype(BF16)

    def partial_sum(valid=None):
        h = h_ref[...]
        g = jnp.dot(h, wg_ref[...], preferred_element_type=F32)
        u = jnp.dot(h, wu_ref[...], preferred_element_type=F32)
        a = (_silu(g) * u).astype(BF16)
        wd = wd_ref[...]
        if valid is not None:
            a = jnp.where(lax.broadcasted_iota(jnp.int32, a.shape, 1) < valid, a, jnp.zeros_like(a))
            wd = jnp.where(lax.broadcasted_iota(jnp.int32, wd.shape, 0) < valid, wd, jnp.zeros_like(wd))
        return jnp.dot(a, wd, preferred_element_type=F32)

    @pl.when(j == 0)
    def _():
        o_ref[...] = partial_sum()

    @pl.when((j > 0) & (j < n_f - 1))
    def _():
        o_ref[...] += partial_sum()

    @pl.when(j == n_f - 1)
    def _():
        y = x_ref[...] + (0.5 * m_ref[0, base + 2:base + 3, :]) * (o_ref[...] + partial_sum(tail))
        if final:
            ms = jnp.mean(y * y, axis=-1, keepdims=True)
            y = y * lax.rsqrt(ms + EPS) * fnw_ref[...]
        o_ref[...] = y


def _ffn(x2d, mods, nw, wg, wu, wd, *, layer, half, base, rows_per_mod, mod_off, final_nw=None,
         tm=1024, tf=512):
    n, d = x2d.shape
    fp = wg.shape[-1]
    tm = _tile(min(n, rows_per_mod), tm)
    n_f = pl.cdiv(fp, tf)
    tail = fp - (n_f - 1) * tf if fp % tf else None
    assert n_f >= 2
    final = final_nw is not None
    in_specs = [
        pl.BlockSpec((tm, d), lambda i, j: (i, 0)),
        pl.BlockSpec((1, N_MOD, d), lambda i, j: ((i * tm) // rows_per_mod + mod_off, 0, 0)),
        pl.BlockSpec((1, d), lambda i, j: (0, 0)),
        pl.BlockSpec((None, None, d, tf), lambda i, j: (layer, half, 0, j)),
        pl.BlockSpec((None, None, d, tf), lambda i, j: (layer, half, 0, j)),
        pl.BlockSpec((None, None, tf, d), lambda i, j: (layer, half, j, 0)),
    ]
    args = [x2d, mods, nw.reshape(1, d), wg, wu, wd]
    if final:
        in_specs.append(pl.BlockSpec((1, d), lambda i, j: (0, 0)))
        args.append(final_nw.reshape(1, d))
    return pl.pallas_call(
        functools.partial(_ffn_kernel, base=base, n_f=n_f, final=final, tail=tail),
        grid=(n // tm, n_f),
        in_specs=in_specs,
        out_specs=pl.BlockSpec((tm, d), lambda i, j: (i, 0)),
        out_shape=jax.ShapeDtypeStruct((n, d), F32),
        scratch_shapes=[pltpu.VMEM((tm, d), BF16)],
        compiler_params=_params(("arbitrary", "arbitrary")),
        name="ffn_final" if final else "ffn",
    )(*args)


def _uproj_kernel(x_ref, m_ref, nw_ref, w_ref, u_ref):
    h = _norm_mod(x_ref[...], nw_ref[...], m_ref[0, 3:4, :], m_ref[0, 4:5, :])
    u_ref[...] = jnp.dot(h.astype(BF16), w_ref[...], preferred_element_type=F32)


def _uproj(x2d, mods, nw, w_in, *, layer, s5w, rows_per_mod, mod_off, tm=1024):
    n, d = x2d.shape
    tm = _tile(min(n, rows_per_mod), tm)
    return pl.pallas_call(
        _uproj_kernel,
        grid=(n // tm,),
        in_specs=[
            pl.BlockSpec((tm, d), lambda i: (i, 0)),
            pl.BlockSpec((1, N_MOD, d), lambda i: ((i * tm) // rows_per_mod + mod_off, 0, 0)),
            pl.BlockSpec((1, d), lambda i: (0, 0)),
            pl.BlockSpec((None, d, s5w), lambda i: (layer, 0, 0)),
        ],
        out_specs=pl.BlockSpec((tm, s5w), lambda i: (i, 0)),
        out_shape=jax.ShapeDtypeStruct((n, s5w), F32),
        compiler_params=_params(("arbitrary",)),
        name="uproj",
    )(x2d, mods, nw.reshape(1, d), w_in)


def _hproj_kernel(*refs, colmajor, hw):
    if colmajor:
        x_ref, m_ref, nw_ref, w_ref, perm_ref, o_ref = refs
    else:
        x_ref, m_ref, nw_ref, w_ref, o_ref = refs
    h = _norm_mod(x_ref[...], nw_ref[...], m_ref[0, 3:4, :], m_ref[0, 4:5, :]).astype(BF16)
    if colmajor:
        h = h.reshape(perm_ref.shape[0], h.shape[-1])
        h = jnp.dot(perm_ref[...], h, preferred_element_type=F32).astype(BF16)
    for nb in range(o_ref.shape[-1] // hw):
        p = jnp.dot(h, w_ref[:, nb * hw:(nb + 1) * hw], preferred_element_type=F32)
        o_ref[:, nb * hw:(nb + 1) * hw] = _silu(p) if nb == 0 else p


def _hproj(x3d, mods, nw, w_h, *, layer, hw, cols, mod_row, cb=8, tr=512):
    b, t, d = x3d.shape
    nh = w_h.shape[-1]
    colmajor = cols > 1
    mod_fn = (lambda bi: bi) if mod_row is None else (lambda bi: mod_row)
    in_specs = [
        None,
        pl.BlockSpec((1, N_MOD, d), lambda bi, i: (mod_fn(bi), 0, 0)),
        pl.BlockSpec((1, d), lambda bi, i: (0, 0)),
        pl.BlockSpec((None, d, nh), lambda bi, i: (layer, 0, 0), pipeline_mode=pl.Buffered(1)),
    ]
    args = [None, mods, nw.reshape(1, d), w_h]
    if colmajor:
        rows = t // cols
        cb = _tile(cols, cb, 8)
        tr = rows * cb
        args[0] = x3d.reshape(b, rows, cols, d)
        in_specs[0] = pl.BlockSpec((None, rows, cb, d), lambda bi, i: (bi, 0, i, 0))
        src = np.arange(tr).reshape(rows, cb).T.reshape(-1)
        perm = np.zeros((tr, tr), np.float32)
        perm[np.arange(tr), src] = 1.0
        args.append(jnp.asarray(perm, BF16))
        in_specs.append(pl.BlockSpec(perm.shape, lambda bi, i: (0, 0)))
    else:
        tr = _tile(t, tr)
        args[0] = x3d
        in_specs[0] = pl.BlockSpec((None, tr, d), lambda bi, i: (bi, i, 0))
    return pl.pallas_call(
        functools.partial(_hproj_kernel, colmajor=colmajor, hw=hw),
        grid=(b, t // tr),
        in_specs=in_specs,
        out_specs=pl.BlockSpec((None, tr, nh), lambda bi, i: (bi, i, 0)),
        out_shape=jax.ShapeDtypeStruct((b, t, nh), F32),
        compiler_params=_params(("arbitrary", "arbitrary")),
        name="hproj",
    )(*args)


def _s5_sel_consts():
    t, h, p, o8 = S5_BLOCK, S5_GROUP, S5_STATE, S5_OCT
    colsel = np.zeros((t, 2, t, h, t, o8, h), np.float32)
    for r in range(t):
        for r2 in range(t):
            for hh in range(h):
                if r2 >= r:
                    colsel[r, 0, r2 - r, hh, r2, :, hh] = 1.0
                if r >= r2:
                    colsel[r, 1, r - r2, hh, r2, :, hh] = 1.0
    colsel = colsel.reshape(t, 2 * t * h, t * o8 * h)
    tile_e = np.zeros((4, p, 4, o8, p), np.float32)
    tile_y = np.zeros((4, h, 4, o8, h), np.float32)
    for dp in range(4):
        for i in range(p):
            tile_e[dp, i, dp, :, i] = 1.0
        for i in range(h):
            tile_y[dp, i, dp, :, i] = 1.0
    return colsel, tile_e.reshape(4 * p, 4 * o8 * p), tile_y.reshape(4 * h, 4 * o8 * h)


def _s5w_kernel(uk_ref, pb_ref, ca_ref, colsel_ref, tile_e_ref, tile_y_ref, k_ref, we_ref, wy_ref):
    t, h, p, o8 = S5_BLOCK, S5_GROUP, S5_STATE, S5_OCT
    gh = o8 * h
    gp = o8 * p

    def diag_mask(shape, row_div, col_mod, col_div):
        rg = lax.broadcasted_iota(jnp.int32, shape, 0) // row_div
        cg = (lax.broadcasted_iota(jnp.int32, shape, 1) % col_mod) // col_div
        return rg == cg

    mk = diag_mask((gh, t * gh), h, gh, h)
    me = diag_mask((gh, 4 * gp), h, gp, p)
    my = diag_mask((gp, 4 * gh), p, gh, h)
    uk = uk_ref[...].astype(BF16)
    for r in range(t):
        blk = jnp.dot(uk, colsel_ref[r], preferred_element_type=F32)
        k_ref[r * gh:(r + 1) * gh, :] = jnp.where(mk, blk, 0.0).astype(BF16)
        blk = jnp.dot(pb_ref[r].astype(BF16), tile_e_ref[...], preferred_element_type=F32)
        we_ref[r * gh:(r + 1) * gh, :] = jnp.where(me, blk, 0.0).astype(BF16)
        blk = jnp.where(my, jnp.dot(ca_ref[r].astype(BF16), tile_y_ref[...], preferred_element_type=F32), 0.0)
        for dp in range(4):
            wy_ref[dp * gp:(dp + 1) * gp, r * gh:(r + 1) * gh] = blk[:, dp * gh:(dp + 1) * gh].astype(BF16)


def _s5_compact(lam_re, lam_im, log_step, b_re, b_im, c_re, c_im):
    hp = lax.Precision.HIGHEST
    t = S5_BLOCK
    g, p = lam_re.shape[1:]
    h = b_re.shape[-1]
    n_oct = g // S5_OCT
    lam_re = jnp.minimum(lam_re.astype(F32), LAMBDA_RE_MAX)
    lam_im = lam_im.astype(F32)
    dt = jnp.exp(log_step.astype(F32))[..., None]
    mag = jnp.exp(lam_re * dt)
    lb_re = mag * jnp.cos(lam_im * dt)
    lb_im = mag * jnp.sin(lam_im * dt)
    den = lam_re * lam_re + lam_im * lam_im
    nr = lb_re - 1.0
    ni = lb_im
    cf_re = (nr * lam_re + ni * lam_im) / den
    cf_im = (ni * lam_re - nr * lam_im) / den
    b_re = b_re.astype(F32)
    b_im = b_im.astype(F32)
    br = cf_re[..., None] * b_re - cf_im[..., None] * b_im
    bi = cf_re[..., None] * b_im + cf_im[..., None] * b_re
    cr = c_re.astype(F32)
    ci = c_im.astype(F32)
    expo = np.concatenate([np.arange(t + 1), np.arange(t - 1, -1, -1), np.arange(t, 0, -1)]).astype(np.float32)
    j = jnp.asarray(expo)[None, :, None, None]
    pmag = jnp.exp(j * (lam_re * dt)[:, None])
    pw_re = pmag * jnp.cos(j * (lam_im * dt)[:, None])
    pw_im = pmag * jnp.sin(j * (lam_im * dt)[:, None])
    asc, desc0, desc1 = slice(0, t), slice(t + 1, 2 * t + 1), slice(2 * t + 1, 3 * t + 1)


    def lam_b(d_, sl):
        pr, pi = pw_re[d_, sl][..., None], pw_im[d_, sl][..., None]
        return pr * br[d_] - pi * bi[d_], pr * bi[d_] + pi * br[d_]

    pbf_re, pbf_im = lam_b(0, asc)
    pbb_re, pbb_im = lam_b(1, asc)
    pb_re = jnp.stack([pbf_re, pbb_re])
    pb_im = jnp.stack([pbf_im, pbb_im])
    kj = (jnp.einsum('dgep,djgph->djgeh', cr, pb_re, precision=hp)
          - jnp.einsum('dgep,djgph->djgeh', ci, pb_im, precision=hp))
    kj = jnp.stack([jnp.concatenate([kj[0, :1] + kj[1, :1], kj[0, 1:]], axis=0),
                    jnp.concatenate([jnp.zeros_like(kj[1, :1]), kj[1, 1:]], axis=0)])
    uk = jnp.transpose(kj, (2, 4, 0, 1, 3)).reshape(n_oct, S5_OCT * h, 2 * t * h)
    pb4 = jnp.stack(lam_b(0, desc0) + (pbb_re, pbb_im))
    pb3 = jnp.transpose(pb4.reshape(4, t, n_oct, S5_OCT, p, h), (2, 1, 3, 5, 0, 4))
    pb3 = pb3.reshape(n_oct, t, S5_OCT * h, 4 * p)

    def c_lam(d_, sl):
        pr, pi = pw_re[d_, sl][:, :, None, :], pw_im[d_, sl][:, :, None, :]
        return cr[d_] * pr - ci[d_] * pi, -(cr[d_] * pi + ci[d_] * pr)

    ca4 = jnp.stack(c_lam(0, slice(1, t + 1)) + c_lam(1, desc1))
    ca3 = jnp.transpose(ca4.reshape(4, t, n_oct, S5_OCT, h, p), (2, 1, 3, 5, 0, 4))
    ca3 = ca3.reshape(n_oct, t, S5_OCT * p, 4 * h)
    nk2 = S5_OCT * p // LANE
    a_re = jnp.transpose(pw_re[:, t].reshape(2, n_oct, nk2, LANE), (1, 0, 2, 3))
    a_im = jnp.transpose(pw_im[:, t].reshape(2, n_oct, nk2, LANE), (1, 0, 2, 3))
    dec = jnp.stack([jnp.concatenate([a_re, a_re], axis=2), jnp.concatenate([-a_im, a_im], axis=2)], axis=2)
    return uk, pb3, ca3, dec


def _s5_expand(uk, pb3, ca3):
    depth, n_oct = uk.shape[:2]
    colsel, tile_e, tile_y = (jnp.asarray(a, BF16) for a in _s5_sel_consts())
    kdim = S5_BLOCK * S5_OCT * S5_GROUP
    sdim = 4 * S5_OCT * S5_STATE
    return pl.pallas_call(
        _s5w_kernel,
        grid=(depth, n_oct),
        in_specs=[
            pl.BlockSpec((None, None) + uk.shape[2:], lambda l, o: (l, o, 0, 0)),
            pl.BlockSpec((None, None) + pb3.shape[2:], lambda l, o: (l, o, 0, 0, 0)),
            pl.BlockSpec((None, None) + ca3.shape[2:], lambda l, o: (l, o, 0, 0, 0)),
            pl.BlockSpec(colsel.shape, lambda l, o: (0, 0, 0)),
            pl.BlockSpec(tile_e.shape, lambda l, o: (0, 0)),
            pl.BlockSpec(tile_y.shape, lambda l, o: (0, 0)),
        ],
        out_specs=[
            pl.BlockSpec((None, None, kdim, kdim), lambda l, o: (l, o, 0, 0)),
            pl.BlockSpec((None, None, kdim, sdim), lambda l, o: (l, o, 0, 0)),
            pl.BlockSpec((None, None, sdim, kdim), lambda l, o: (l, o, 0, 0)),
        ],
        out_shape=[
            jax.ShapeDtypeStruct((depth, n_oct, kdim, kdim), BF16),
            jax.ShapeDtypeStruct((depth, n_oct, kdim, sdim), BF16),
            jax.ShapeDtypeStruct((depth, n_oct, sdim, kdim), BF16),
        ],
        compiler_params=_params(("arbitrary", "arbitrary")),
        name="s5w",
    )(uk, pb3, ca3, colsel, tile_e, tile_y)


def _block_rows(u_ref, tmr):
    return jnp.concatenate(
        [u_ref[pl.ds(r, tmr, stride=S5_BLOCK), :].astype(BF16) for r in range(S5_BLOCK)], axis=1)


def _s5e_kernel(u_ref, we_ref, e_ref, *, tmr):
    res = jnp.dot(_block_rows(u_ref, tmr), we_ref[...], preferred_element_type=F32)
    nk = res.shape[1] // (2 * LANE)
    for d in range(2):
        for k in range(nk):
            c0 = (d * nk + k) * LANE
            e_ref[d, pl.ds(k, tmr, stride=nk), :] = res[:, c0:c0 + LANE]


def _s5e(u, we, *, layer, tmr=1024):
    n, wid = u.shape
    m = n // S5_BLOCK
    _, n_oct, kdim, ncol = we.shape
    nk = ncol // (2 * LANE)
    tmr = _tile(m, tmr)
    return pl.pallas_call(
        functools.partial(_s5e_kernel, tmr=tmr),
        grid=(n_oct, m // tmr),
        in_specs=[
            pl.BlockSpec((tmr * S5_BLOCK, LANE), lambda o, i: (i, o)),
            pl.BlockSpec((None, None, kdim, ncol), lambda o, i: (layer, o, 0, 0)),
        ],
        out_specs=pl.BlockSpec((None, 2, tmr * nk, LANE), lambda o, i: (o, 0, i, 0)),
        out_shape=jax.ShapeDtypeStruct((n_oct, 2, m * nk, LANE), F32),
        compiler_params=_params(("arbitrary", "arbitrary")),
        name="s5e",
    )(u, we)


def _s5scan_kernel(ec_ref, el_ref, a_ref, sc_ref, sl_ref, *, nb, n_c, n_l, nk):
    rev = pl.program_id(1) == 1
    a1 = a_ref[0]
    a2 = a_ref[1]

    def run(e_ref, s_ref, n, carry):
        def body(s, carry):
            i = jnp.where(rev, n - 1 - s, s)
            new = []
            for b in range(nb):
                x = carry[b]
                rows = pl.ds(pl.multiple_of((b * n + i) * nk, nk), nk)
                s_ref[rows, :] = x
                new.append(a1 * x + a2 * pltpu.roll(x, nk // 2, 0) + e_ref[rows, :])
            return tuple(new)
        return lax.fori_loop(0, n, body, carry, unroll=2)

    carry = tuple(jnp.zeros((nk, LANE), F32) for _ in range(nb))
    carry = run(ec_ref, sc_ref, n_c, carry)
    run(el_ref, sl_ref, n_l, carry)


def _s5scan(e_ctx, e_lat, dec, *, layer, nb):
    n_oct, _, rc, _ = e_ctx.shape
    rl = e_lat.shape[2]
    nk = dec.shape[-2]
    return pl.pallas_call(
        functools.partial(_s5scan_kernel, nb=nb, n_c=rc // (nk * nb), n_l=rl // (nk * nb), nk=nk),
        grid=(n_oct, 2),
        in_specs=[
            pl.BlockSpec((None, None, rc, LANE), lambda o, d: (o, d, 0, 0)),
            pl.BlockSpec((None, None, rl, LANE), lambda o, d: (o, d, 0, 0)),
            pl.BlockSpec((None, None, None, 2, nk, LANE), lambda o, d: (layer, o, d, 0, 0, 0)),
        ],
        out_specs=[
            pl.BlockSpec((None, None, rc, LANE), lambda o, d: (o, d, 0, 0)),
            pl.BlockSpec((None, None, rl, LANE), lambda o, d: (o, d, 0, 0)),
        ],
        out_shape=[jax.ShapeDtypeStruct(e_ctx.shape, F32), jax.ShapeDtypeStruct(e_lat.shape, F32)],
        compiler_params=_params(("arbitrary", "arbitrary")),
        name="s5scan",
    )(e_ctx, e_lat, dec)


def _s5y_kernel(u_ref, s_ref, k_ref, wy_ref, y_ref, *, tmr):
    nk = s_ref.shape[1] // tmr
    st = jnp.concatenate(
        [s_ref[d, pl.ds(k, tmr, stride=nk), :].astype(BF16) for d in range(2) for k in range(nk)], axis=1)
    res = jnp.dot(_block_rows(u_ref, tmr), k_ref[...], preferred_element_type=F32)
    res += jnp.dot(st, wy_ref[...], preferred_element_type=F32)
    for r in range(S5_BLOCK):
        y_ref[pl.ds(r, tmr, stride=S5_BLOCK), :] = res[:, r * LANE:(r + 1) * LANE]


def _s5y(u, s, kmat, wy, *, layer, tmr=1024):
    n, wid = u.shape
    m = n // S5_BLOCK
    _, n_oct, kdim, _ = kmat.shape
    sdim = wy.shape[2]
    nk = s.shape[2] // m
    tmr = _tile(m, tmr)
    return pl.pallas_call(
        functools.partial(_s5y_kernel, tmr=tmr),
        grid=(n_oct, m // tmr),
        in_specs=[
            pl.BlockSpec((tmr * S5_BLOCK, LANE), lambda o, i: (i, o)),
            pl.BlockSpec((None, 2, tmr * nk, LANE), lambda o, i: (o, 0, i, 0)),
            pl.BlockSpec((None, None, kdim, kdim), lambda o, i: (layer, o, 0, 0)),
            pl.BlockSpec((None, None, sdim, kdim), lambda o, i: (layer, o, 0, 0)),
        ],
        out_specs=pl.BlockSpec((tmr * S5_BLOCK, LANE), lambda o, i: (i, o)),
        out_shape=jax.ShapeDtypeStruct((n, wid), F32),
        compiler_params=_params(("arbitrary", "arbitrary")),
        name="s5y",
    )(u, s, kmat, wy)


def _hgrn_masks():
    c = CHUNK
    t = np.arange(c)
    masks = []
    half = c // 2
    while half >= SUB:
        par = 2 * half
        second = (t % par) >= half
        same_parent = (t[:, None] // par) == (t[None, :] // par)
        masks.append((same_parent & second[:, None] & (~second)[None, :]).astype(np.float32))
        half //= 2
    masks.append((((t[:, None] // SUB) == (t[None, :] // SUB)) & (t[None, :] <= t[:, None])).astype(np.float32))
    return np.stack(masks)


def _hgrn_decays(g, reverse):
    nb = CHUNK // SUB
    g3 = g.reshape(nb, SUB, g.shape[1])
    r = lax.broadcasted_iota(jnp.int32, g3.shape, 1)
    p = g3
    for k in (1, 2, 4):
        if reverse:
            p = p + jnp.where(r <= SUB - 1 - k, pltpu.roll(p, SUB - k, 1), 0.0)
        else:
            p = p + jnp.where(r >= k, pltpu.roll(p, k, 1), 0.0)
    last = 0 if reverse else SUB - 1
    ref = SUB // 2 if reverse else SUB // 2 - 1
    tot = jnp.broadcast_to(p[:, last:last + 1, :], p.shape)
    ep = jnp.exp(p)
    eq = jnp.exp(tot - p)
    ed = p - jnp.broadcast_to(p[:, ref:ref + 1, :], p.shape)
    edp = jnp.exp(ed)
    edn = jnp.exp(-ed)
    et = jnp.broadcast_to(ep[:, last:last + 1, :], p.shape)
    mem = (lambda i: nb - 1 - i) if reverse else (lambda i: i)
    epb = [ep[mem(i)] for i in range(nb)]
    eqb = [eq[mem(i)] for i in range(nb)]
    etb = [et[mem(i)] for i in range(nb)]

    def assemble(blocks):
        return jnp.concatenate([blocks[mem(i)] for i in range(nb)], axis=0)

    cq = [None] * nb
    acc = None
    for i in range(nb):
        cq[i] = epb[i] if acc is None else epb[i] * acc
        acc = etb[i] if acc is None else acc * etb[i]
    total = acc[0:1, :]
    ck = [None] * nb
    acc = None
    for i in range(nb - 1, -1, -1):
        ck[i] = eqb[i] if acc is None else eqb[i] * acc
        acc = etb[i] if acc is None else acc * etb[i]
    levels = []
    half = nb // 2
    while half >= 1:
        par = 2 * half
        blocks = []
        for i in range(nb):
            j = i % par
            if j >= half:
                f = epb[i]
                for m in range(i - j + half, i):
                    f = f * etb[m]
            else:
                f = eqb[i]
                for m in range(i + 1, i - j + half):
                    f = f * etb[m]
            blocks.append(f)
        levels.append(assemble(blocks))
        half //= 2
    return levels, edp.reshape(g.shape), edn.reshape(g.shape), assemble(cq), assemble(ck), total


def _hgrn_kernel(*refs, nsub, heads, reverse, final, n_lvl):
    if final:
        (q_ref, z_ref, v_ref, lb_ref, masks_ref, s0_ref, of_ref, gate_ref, hnw_ref,
         o_ref, sfin_ref, st_ref) = refs
    else:
        q_ref, z_ref, v_ref, lb_ref, masks_ref, s0_ref, o_ref, sfin_ref, st_ref = refs
    j = pl.program_id(1)
    nj = pl.num_programs(1)
    c = CHUNK
    hd = HEAD_DIM
    nt = (((1,), (1,)), ((), ()))
    tn = (((0,), (0,)), ((), ()))

    @pl.when(j == 0)
    def _():
        st_ref[...] = s0_ref[...]

    mbool = [masks_ref[lv] > 0.0 for lv in range(n_lvl)]

    mpair = [jnp.concatenate([m, m], axis=0) for m in mbool]
    zpad = jnp.zeros((c, hd), BF16)

    def chunk(s, _):
        cl = (nsub - 1 - s) if reverse else s
        rows = pl.ds(pl.multiple_of(cl * c, c), c)
        for hp in range(heads // 2):
            sl = slice(2 * hp * hd, (2 * hp + 2) * hd)
            z = z_ref[rows, sl]
            q = q_ref[rows, sl]
            v = v_ref[rows, sl].astype(BF16)
            lb = lb_ref[:, sl]
            f = lb + (1.0 - lb) * jax.nn.sigmoid(z)
            k = 1.0 - f
            g = jnp.log(jnp.maximum(f, F_MIN))
            levels, edp, edn, cq, ck, total = _hgrn_decays(g, reverse)
            qb = q.astype(BF16)
            kb = k.astype(BF16)
            a = jnp.zeros((2 * c, c), F32)
            for lv in range(n_lvl - 1, -1, -1):
                if lv < n_lvl - 1:
                    e = levels[lv].astype(BF16)
                    ql = qb * e
                    kl = kb * e
                else:
                    ql = qb * edp.astype(BF16)
                    kl = kb * edn.astype(BF16)
                lhs = jnp.concatenate([jnp.concatenate([ql[:, :hd], zpad], axis=1),
                                       jnp.concatenate([zpad, ql[:, hd:]], axis=1)], axis=0)
                sc = lax.dot_general(lhs, kl, nt, preferred_element_type=F32)
                a = jnp.where(mpair[lv], sc, a)
            ab = a.astype(BF16)
            qd = qb * cq.astype(BF16)
            kd = kb * ck.astype(BF16)
            for i in range(2):
                h = 2 * hp + i
                hs = slice(i * hd, (i + 1) * hd)
                st = st_ref[h]
                o_h = jnp.dot(ab[i * c:(i + 1) * c], v[:, hs], preferred_element_type=F32)
                o_h += lax.dot_general(qd[:, hs], st.astype(BF16), nt, preferred_element_type=F32)
                st_ref[h] = st * total[:, hs] + lax.dot_general(v[:, hs], kd[:, hs], tn,
                                                                 preferred_element_type=F32)
                osl = slice(h * hd, (h + 1) * hd)
                if final:
                    o_h = o_h + of_ref[rows, osl]
                    ms = jnp.mean(o_h * o_h, axis=-1, keepdims=True)
                    o_h = o_h * lax.rsqrt(ms + EPS) * hnw_ref[...] * _silu(gate_ref[rows, osl])
                o_ref[rows, osl] = o_h.astype(o_ref.dtype)
        return 0

    lax.fori_loop(0, nsub, chunk, 0, unroll=8)

    @pl.when(j == nj - 1)
    def _():
        sfin_ref[...] = st_ref[...]


def _hgrn(p, lb, masks, s0, *, direction, width, o_fwd=None, hnw=None, nsub=8):
    b, t, _ = p.shape
    heads = width // HEAD_DIM
    nsub = _tile(t // CHUNK, nsub, 1)
    rows = nsub * CHUNK
    nj = t // rows
    reverse = direction == 1
    final = o_fwd is not None
    n_lvl = masks.shape[0]
    blk = (lambda j: nj - 1 - j) if reverse else (lambda j: j)
    zcol = 2 if reverse else 1
    in_specs = [
        pl.BlockSpec((None, rows, width), lambda bi, j: (bi, blk(j), 0)),
        pl.BlockSpec((None, rows, width), lambda bi, j: (bi, blk(j), zcol)),
        pl.BlockSpec((None, rows, width), lambda bi, j: (bi, blk(j), 3)),
        pl.BlockSpec((1, width), lambda bi, j: (0, 0)),
        pl.BlockSpec(masks.shape, lambda bi, j: (0, 0, 0)),
        pl.BlockSpec((None, heads, HEAD_DIM, HEAD_DIM), lambda bi, j: (bi, 0, 0, 0)),
    ]
    args = [p, p, p, lb.reshape(1, width), masks, s0]
    if final:
        in_specs += [
            pl.BlockSpec((None, rows, width), lambda bi, j: (bi, blk(j), 0)),
            pl.BlockSpec((None, rows, width), lambda bi, j: (bi, blk(j), 4)),
            pl.BlockSpec((1, HEAD_DIM), lambda bi, j: (0, 0)),
        ]
        args += [o_fwd, p, hnw.reshape(1, HEAD_DIM)]
    return pl.pallas_call(
        functools.partial(_hgrn_kernel, nsub=nsub, heads=heads, reverse=reverse, final=final, n_lvl=n_lvl),
        grid=(b, nj),
        in_specs=in_specs,
        out_specs=[
            pl.BlockSpec((None, rows, width), lambda bi, j: (bi, blk(j), 0)),
            pl.BlockSpec((None, heads, HEAD_DIM, HEAD_DIM), lambda bi, j: (bi, 0, 0, 0)),
        ],
        out_shape=[
            jax.ShapeDtypeStruct((b, t, width), F32),
            jax.ShapeDtypeStruct((b, heads, HEAD_DIM, HEAD_DIM), F32),
        ],
        scratch_shapes=[pltpu.VMEM((heads, HEAD_DIM, HEAD_DIM), F32)],
        compiler_params=_params(("arbitrary", "arbitrary")),
        name="hgrn_bwd" if final else "hgrn_fwd",
    )(*args)


def _mixout_kernel(x_ref, y_ref, u_ref, hg_ref, m_ref, dsk_ref, wglu_ref, bglu_ref, wo_ref, o_ref,
                   *, s5w, rb, colmajor):
    yy = y_ref[...] + dsk_ref[...] * u_ref[...]
    g = _gelu_tanh(yy)
    zz = jnp.dot(g.astype(BF16), wglu_ref[...], preferred_element_type=F32) + bglu_ref[...]
    s5 = (g * jax.nn.sigmoid(zz)).astype(BF16)
    if colmajor:
        hg = jnp.concatenate([hg_ref[:, r, :] for r in range(rb)], axis=0).astype(BF16)
    else:
        hg = hg_ref[...].astype(BF16)
    acc = jnp.dot(s5, wo_ref[:s5w, :], preferred_element_type=F32)
    acc += jnp.dot(hg, wo_ref[s5w:, :], preferred_element_type=F32)
    o_ref[...] = x_ref[...] + m_ref[0, 5:6, :] * acc


def _mixout(x3d, y3d, u3d, hg, mods, d_skip, w_glu, b_glu, w_o, *, layer, mod_of_batch, colmajor, rb=8):
    b, t, d = x3d.shape
    s5w = y3d.shape[-1]
    hw = hg.shape[-1]
    if colmajor:
        cols = GRID_W
        rows = t // cols
        rb = _tile(rows, rb, 8)
        tm = rb * cols
        hg_v = hg.reshape(b, cols, rows, hw)
        hg_spec = pl.BlockSpec((None, cols, rb, hw), lambda bi, i: (bi, 0, i, 0))
    else:
        rb = 1
        tm = _tile(t, 512)
        hg_v = hg
        hg_spec = pl.BlockSpec((None, tm, hw), lambda bi, i: (bi, i, 0))
    mod_fn = (lambda bi: bi) if mod_of_batch else (lambda bi: b)
    return pl.pallas_call(
        functools.partial(_mixout_kernel, s5w=s5w, rb=rb, colmajor=colmajor),
        grid=(b, t // tm),
        in_specs=[
            pl.BlockSpec((None, tm, d), lambda bi, i: (bi, i, 0)),
            pl.BlockSpec((None, tm, s5w), lambda bi, i: (bi, i, 0)),
            pl.BlockSpec((None, tm, s5w), lambda bi, i: (bi, i, 0)),
            hg_spec,
            pl.BlockSpec((1, N_MOD, d), lambda bi, i: (mod_fn(bi), 0, 0)),
            pl.BlockSpec((1, s5w), lambda bi, i: (0, 0)),
            pl.BlockSpec((None, s5w, s5w), lambda bi, i: (layer, 0, 0)),
            pl.BlockSpec((1, s5w), lambda bi, i: (0, 0)),
            pl.BlockSpec((None, s5w + hw, d), lambda bi, i: (layer, 0, 0)),
        ],
        out_specs=pl.BlockSpec((None, tm, d), lambda bi, i: (bi, i, 0)),
        out_shape=jax.ShapeDtypeStruct((b, t, d), F32),
        compiler_params=_params(("arbitrary", "arbitrary")),
        name="mixout",
    )(x3d, y3d, u3d, hg_v, mods, d_skip.reshape(1, s5w), w_glu, b_glu.reshape(1, s5w), w_o)


def kernel(x, c, ctx, c_ctx, w_ada, b_ada, norm_w, ffn_w_gate, ffn_w_up, ffn_w_down, w_in, w_out,
           s5_lambda_re, s5_lambda_im, s5_log_step, s5_b_re, s5_b_im, s5_c_re, s5_c_im, s5_d,
           s5_w_glu, s5_b_glu, hgrn_lower_bounds, hgrn_norm_w, final_norm_w):
    batch, seq, d = x.shape
    n_ctx = ctx.shape[1]
    depth = w_ada.shape[0]
    s5w = s5_d.shape[-1]
    hw = hgrn_lower_bounds.shape[-1]
    rows = seq // GRID_W
    assert batch < MOD_ROWS and rows % CHUNK == 0 and n_ctx % CHUNK == 0
    assert seq % S5_BLOCK == 0 and n_ctx % S5_BLOCK == 0 and (s5w // S5_GROUP) % S5_OCT == 0

    cvec = jnp.zeros((MOD_ROWS, d), F32).at[:batch].set(c.astype(F32)).at[batch].set(c_ctx.astype(F32))
    mods_all = _ada(cvec, w_ada, b_ada).reshape(depth, MOD_ROWS, N_MOD, d)

    lb_soft = jax.nn.softmax(hgrn_lower_bounds.astype(F32), axis=0)
    lb_all = jnp.cumsum(lb_soft, axis=0) - lb_soft[0]

    masks_np = _hgrn_masks()
    masks_f = jnp.asarray(masks_np, F32)
    masks_b = jnp.asarray(masks_np[:, ::-1, ::-1].copy(), F32)
    heads = hw // HEAD_DIM
    s_zero = jnp.zeros((batch, heads, HEAD_DIM, HEAD_DIM), F32)

    wg = ffn_w_gate.astype(BF16)
    wu = ffn_w_up.astype(BF16)
    wd = ffn_w_down.astype(BF16)
    uk, pb3, ca3, dec = (jnp.stack(a) for a in zip(*[
        _s5_compact(s5_lambda_re[l], s5_lambda_im[l], s5_log_step[l], s5_b_re[l], s5_b_im[l],
                    s5_c_re[l], s5_c_im[l]) for l in range(depth)]))
    kmat, we, wy = _s5_expand(uk, pb3, ca3)
    w_in_b = w_in.astype(BF16)
    w_h = w_in_b[:, :, s5w:]
    w_o = w_out.astype(BF16)
    w_glu = s5_w_glu.astype(BF16)

    xl = x.astype(F32)
    xc = ctx.astype(F32)
    lat = dict(rows_per_mod=seq, mod_off=0)
    cx = dict(rows_per_mod=batch * n_ctx, mod_off=batch)
    for l in range(depth):
        last = l == depth - 1
        mods = mods_all[l]

        xl = _ffn(xl.reshape(batch * seq, d), mods, norm_w[l, 0], wg, wu, wd, layer=l, half=0, base=0,
                  **lat).reshape(batch, seq, d)
        xc = _ffn(xc.reshape(batch * n_ctx, d), mods, norm_w[l, 0], wg, wu, wd, layer=l, half=0, base=0,
                  **cx).reshape(batch, n_ctx, d)

        u_l = _uproj(xl.reshape(batch * seq, d), mods, norm_w[l, 1], w_in_b, layer=l, s5w=s5w, **lat)
        u_c = _uproj(xc.reshape(batch * n_ctx, d), mods, norm_w[l, 1], w_in_b, layer=l, s5w=s5w, **cx)
        p_lat = _hproj(xl, mods, norm_w[l, 1], w_h, layer=l, hw=hw, cols=GRID_W, mod_row=None)
        p_ctx = _hproj(xc.reshape(1, batch * n_ctx, d), mods, norm_w[l, 1], w_h, layer=l, hw=hw, cols=1,
                       mod_row=batch).reshape(batch, n_ctx, 5 * hw)

        e_l = _s5e(u_l, we, layer=l)
        e_c = _s5e(u_c, we, layer=l)
        st_c, st_l = _s5scan(e_c, e_l, dec, layer=l, nb=batch)
        y_l = _s5y(u_l, st_l, kmat, wy, layer=l).reshape(batch, seq, s5w)

        lb_f = lb_all[l, 0]
        lb_b = lb_all[l, 1]
        oc_f, sc_f = _hgrn(p_ctx, lb_f, masks_f, s_zero, direction=0, width=hw)
        ol_f, _ = _hgrn(p_lat, lb_f, masks_f, sc_f, direction=0, width=hw)
        hg_c, sc_b = _hgrn(p_ctx, lb_b, masks_b, s_zero, direction=1, width=hw,
                           o_fwd=oc_f, hnw=hgrn_norm_w[l])
        hg_l, _ = _hgrn(p_lat, lb_b, masks_b, sc_b, direction=1, width=hw,
                        o_fwd=ol_f, hnw=hgrn_norm_w[l])

        xl = _mixout(xl, y_l, u_l.reshape(batch, seq, s5w), hg_l, mods, s5_d[l], w_glu, s5_b_glu[l], w_o,
                     layer=l, mod_of_batch=True, colmajor=True)
        if not last:
            y_c = _s5y(u_c, st_c, kmat, wy, layer=l).reshape(batch, n_ctx, s5w)
            xc = _mixout(xc, y_c, u_c.reshape(batch, n_ctx, s5w), hg_c, mods, s5_d[l], w_glu, s5_b_glu[l],
                         w_o, layer=l, mod_of_batch=False, colmajor=False)

        xl = _ffn(xl.reshape(batch * seq, d), mods, norm_w[l, 2], wg, wu, wd, layer=l, half=1, base=6,
                  final_nw=final_norm_w if last else None, **lat).reshape(batch, seq, d)
        if not last:
            xc = _ffn(xc.reshape(batch * n_ctx, d), mods, norm_w[l, 2], wg, wu, wd, layer=l, half=1, base=6,
                      **cx).reshape(batch, n_ctx, d)
    return xl
```

```python
import functools

import numpy as np
import jax
import jax.numpy as jnp
from jax import lax
from jax.experimental import pallas as pl
from jax.experimental.pallas import tpu as pltpu

F32 = jnp.float32
BF16 = jnp.bfloat16

EPS = 1e-6
F_MIN = 1e-6
LAMBDA_RE_MAX = -1e-4
GRID_W = 64
N_MOD = 9
S5_GROUP = 16
S5_STATE = 64
S5_BLOCK = 8
S5_OCT = 8
HEAD_DIM = 128
CHUNK = 64
SUB = 8
LANE = 128
MOD_ROWS = 8
VMEM_LIMIT = 60 * 1024 * 1024


def _params(sem):
    return pltpu.CompilerParams(dimension_semantics=sem, vmem_limit_bytes=VMEM_LIMIT)


def _tile(n, pref, mult=8):
    t = min(n, pref)
    while t > 0:
        if n % t == 0 and t % mult == 0:
            return t
        t -= 1
    return n


def _norm_mod(x, nw, shift, scale):
    ms = jnp.mean(x * x, axis=-1, keepdims=True)
    gain = nw * (1.0 + scale)
    return (x * lax.rsqrt(ms + EPS)) * gain + shift


def _silu(x):
    return x * jax.nn.sigmoid(x)


def _gelu_tanh(x):
    return 0.5 * x * (1.0 + jnp.tanh(0.7978845608028654 * (x + 0.044715 * (x * x * x))))


def _ada_kernel(c_ref, w_ref, b_ref, o_ref):
    a = _silu(c_ref[...])
    a_hi = a.astype(BF16)
    a_lo = (a - a_hi.astype(F32)).astype(BF16)
    w = w_ref[...]
    w_hi = w.astype(BF16)
    w_lo = (w - w_hi.astype(F32)).astype(BF16)
    both = jnp.dot(jnp.concatenate([a_hi, a_lo], axis=0), w_hi, preferred_element_type=F32)
    acc = both[:MOD_ROWS] + both[MOD_ROWS:] + jnp.dot(a_hi, w_lo, preferred_element_type=F32)
    o_ref[...] = acc + b_ref[...]


def _ada(cvec, w_ada, b_ada):
    depth, d, n = w_ada.shape
    tn = _tile(n, 2048, LANE)
    return pl.pallas_call(
        _ada_kernel,
        grid=(depth, n // tn),
        in_specs=[
            pl.BlockSpec((MOD_ROWS, d), lambda l, j: (0, 0)),
            pl.BlockSpec((None, d, tn), lambda l, j: (l, 0, j)),
            pl.BlockSpec((None, 1, tn), lambda l, j: (l, 0, j)),
        ],
        out_specs=pl.BlockSpec((None, MOD_ROWS, tn), lambda l, j: (l, 0, j)),
        out_shape=jax.ShapeDtypeStruct((depth, MOD_ROWS, n), F32),
        compiler_params=_params(("arbitrary", "arbitrary")),
        name="ada",
    )(cvec, w_ada, b_ada.reshape(depth, 1, n))


def _ffn_kernel(*refs, base, n_f, final, tail):
    if final:
        x_ref, m_ref, nw_ref, wg_ref, wu_ref, wd_ref, fnw_ref, o_ref, h_ref = refs
    else:
        x_ref, m_ref, nw_ref, wg_ref, wu_ref, wd_ref, o_ref, h_ref = refs
    j = pl.program_id(1)

    @pl.when(j == 0)
    def _():
        h = _norm_mod(x_ref[...], nw_ref[...], m_ref[0, base:base + 1, :], m_ref[0, base + 1:base + 2, :])
        h_ref[...] = h.astype(BF16)

    def partial_sum(valid=None):
        h = h_ref[...]
        g = jnp.dot(h, wg_ref[...], preferred_element_type=F32)
        u = jnp.dot(h, wu_ref[...], preferred_element_type=F32)
        a = (_silu(g) * u).astype(BF16)
        wd = wd_ref[...]
        if valid is not None:
            a = jnp.where(lax.broadcasted_iota(jnp.int32, a.shape, 1) < valid, a, jnp.zeros_like(a))
            wd = jnp.where(lax.broadcasted_iota(jnp.int32, wd.shape, 0) < valid, wd, jnp.zeros_like(wd))
        return jnp.dot(a, wd, preferred_element_type=F32)

    @pl.when(j == 0)
    def _():
        o_ref[...] = partial_sum()

    @pl.when((j > 0) & (j < n_f - 1))
    def _():
        o_ref[...] += partial_sum()

    @pl.when(j == n_f - 1)
    def _():
        y = x_ref[...] + (0.5 * m_ref[0, base + 2:base + 3, :]) * (o_ref[...] + partial_sum(tail))
        if final:
            ms = jnp.mean(y * y, axis=-1, keepdims=True)
            y = y * lax.rsqrt(ms + EPS) * fnw_ref[...]
        o_ref[...] = y


def _ffn(x2d, mods, nw, wg, wu, wd, *, layer, half, base, rows_per_mod, mod_off, final_nw=None,
         tm=1024, tf=512):
    n, d = x2d.shape
    fp = wg.shape[-1]
    tm = _tile(min(n, rows_per_mod), tm)
    n_f = pl.cdiv(fp, tf)
    tail = fp - (n_f - 1) * tf if fp % tf else None
    assert n_f >= 2
    final = final_nw is not None
    in_specs = [
        pl.BlockSpec((tm, d), lambda i, j: (i, 0)),
        pl.BlockSpec((1, N_MOD, d), lambda i, j: ((i * tm) // rows_per_mod + mod_off, 0, 0)),
        pl.BlockSpec((1, d), lambda i, j: (0, 0)),
        pl.BlockSpec((None, None, d, tf), lambda i, j: (layer, half, 0, j)),
        pl.BlockSpec((None, None, d, tf), lambda i, j: (layer, half, 0, j)),
        pl.BlockSpec((None, None, tf, d), lambda i, j: (layer, half, j, 0)),
    ]
    args = [x2d, mods, nw.reshape(1, d), wg, wu, wd]
    if final:
        in_specs.append(pl.BlockSpec((1, d), lambda i, j: (0, 0)))
        args.append(final_nw.reshape(1, d))
    return pl.pallas_call(
        functools.partial(_ffn_kernel, base=base, n_f=n_f, final=final, tail=tail),
        grid=(n // tm, n_f),
        in_specs=in_specs,
        out_specs=pl.BlockSpec((tm, d), lambda i, j: (i, 0)),
        out_shape=jax.ShapeDtypeStruct((n, d), F32),
        scratch_shapes=[pltpu.VMEM((tm, d), BF16)],
        compiler_params=_params(("arbitrary", "arbitrary")),
        name="ffn_final" if final else "ffn",
    )(*args)


def _uproj_kernel(x_ref, m_ref, nw_ref, w_ref, u_ref):
    h = _norm_mod(x_ref[...], nw_ref[...], m_ref[0, 3:4, :], m_ref[0, 4:5, :])
    u_ref[...] = jnp.dot(h.astype(BF16), w_ref[...], preferred_element_type=F32)


def _uproj(x2d, mods, nw, w_in, *, layer, s5w, rows_per_mod, mod_off, tm=1024):
    n, d = x2d.shape
    tm = _tile(min(n, rows_per_mod), tm)
    return pl.pallas_call(
        _uproj_kernel,
        grid=(n // tm,),
        in_specs=[
            pl.BlockSpec((tm, d), lambda i: (i, 0)),
            pl.BlockSpec((1, N_MOD, d), lambda i: ((i * tm) // rows_per_mod + mod_off, 0, 0)),
            pl.BlockSpec((1, d), lambda i: (0, 0)),
            pl.BlockSpec((None, d, s5w), lambda i: (layer, 0, 0)),
        ],
        out_specs=pl.BlockSpec((tm, s5w), lambda i: (i, 0)),
        out_shape=jax.ShapeDtypeStruct((n, s5w), F32),
        compiler_params=_params(("arbitrary",)),
        name="uproj",
    )(x2d, mods, nw.reshape(1, d), w_in)


def _hproj_kernel(*refs, colmajor, hw):
    if colmajor:
        x_ref, m_ref, nw_ref, w_ref, perm_ref, o_ref = refs
    else:
        x_ref, m_ref, nw_ref, w_ref, o_ref = refs
    h = _norm_mod(x_ref[...], nw_ref[...], m_ref[0, 3:4, :], m_ref[0, 4:5, :]).astype(BF16)
    if colmajor:
        h = h.reshape(perm_ref.shape[0], h.shape[-1])
        h = jnp.dot(perm_ref[...], h, preferred_element_type=F32).astype(BF16)
    for nb in range(o_ref.shape[-1] // hw):
        p = jnp.dot(h, w_ref[:, nb * hw:(nb + 1) * hw], preferred_element_type=F32)
        o_ref[:, nb * hw:(nb + 1) * hw] = _silu(p) if nb == 0 else p


def _hproj(x3d, mods, nw, w_h, *, layer, hw, cols, mod_row, cb=8, tr=512):
    b, t, d = x3d.shape
    nh = w_h.shape[-1]
    colmajor = cols > 1
    mod_fn = (lambda bi: bi) if mod_row is None else (lambda bi: mod_row)
    in_specs = [
        None,
        pl.BlockSpec((1, N_MOD, d), lambda bi, i: (mod_fn(bi), 0, 0)),
        pl.BlockSpec((1, d), lambda bi, i: (0, 0)),
        pl.BlockSpec((None, d, nh), lambda bi, i: (layer, 0, 0), pipeline_mode=pl.Buffered(1)),
    ]
    args = [None, mods, nw.reshape(1, d), w_h]
    if colmajor:
        rows = t // cols
        cb = _tile(cols, cb, 8)
        tr = rows * cb
        args[0] = x3d.reshape(b, rows, cols, d)
        in_specs[0] = pl.BlockSpec((None, rows, cb, d), lambda bi, i: (bi, 0, i, 0))
        src = np.arange(tr).reshape(rows, cb).T.reshape(-1)
        perm = np.zeros((tr, tr), np.float32)
        perm[np.arange(tr), src] = 1.0
        args.append(jnp.asarray(perm, BF16))
        in_specs.append(pl.BlockSpec(perm.shape, lambda bi, i: (0, 0)))
    else:
        tr = _tile(t, tr)
        args[0] = x3d
        in_specs[0] = pl.BlockSpec((None, tr, d), lambda bi, i: (bi, i, 0))
    return pl.pallas_call(
        functools.partial(_hproj_kernel, colmajor=colmajor, hw=hw),
        grid=(b, t // tr),
        in_specs=in_specs,
        out_specs=pl.BlockSpec((None, tr, nh), lambda bi, i: (bi, i, 0)),
        out_shape=jax.ShapeDtypeStruct((b, t, nh), F32),
        compiler_params=_params(("arbitrary", "arbitrary")),
        name="hproj",
    )(*args)


def _s5_sel_consts():
    t, h, p, o8 = S5_BLOCK, S5_GROUP, S5_STATE, S5_OCT
    colsel = np.zeros((t, 2, t, h, t, o8, h), np.float32)
    for r in range(t):
        for r2 in range(t):
            for hh in range(h):
                if r2 >= r:
                    colsel[r, 0, r2 - r, hh, r2, :, hh] = 1.0
                if r >= r2:
                    colsel[r, 1, r - r2, hh, r2, :, hh] = 1.0
    colsel = colsel.reshape(t, 2 * t * h, t * o8 * h)
    tile_e = np.zeros((4, p, 4, o8, p), np.float32)
    tile_y = np.zeros((4, h, 4, o8, h), np.float32)
    for dp in range(4):
        for i in range(p):
            tile_e[dp, i, dp, :, i] = 1.0
        for i in range(h):
            tile_y[dp, i, dp, :, i] = 1.0
    return colsel, tile_e.reshape(4 * p, 4 * o8 * p), tile_y.reshape(4 * h, 4 * o8 * h)


def _s5w_kernel(uk_ref, pb_ref, ca_ref, colsel_ref, tile_e_ref, tile_y_ref, k_ref, we_ref, wy_ref):
    t, h, p, o8 = S5_BLOCK, S5_GROUP, S5_STATE, S5_OCT
    gh = o8 * h
    gp = o8 * p

    def diag_mask(shape, row_div, col_mod, col_div):
        rg = lax.broadcasted_iota(jnp.int32, shape, 0) // row_div
        cg = (lax.broadcasted_iota(jnp.int32, shape, 1) % col_mod) // col_div
        return rg == cg

    mk = diag_mask((gh, t * gh), h, gh, h)
    me = diag_mask((gh, 4 * gp), h, gp, p)
    my = diag_mask((gp, 4 * gh), p, gh, h)
    uk = uk_ref[...].astype(BF16)
    for r in range(t):
        blk = jnp.dot(uk, colsel_ref[r], preferred_element_type=F32)
        k_ref[r * gh:(r + 1) * gh, :] = jnp.where(mk, blk, 0.0).astype(BF16)
        blk = jnp.dot(pb_ref[r].astype(BF16), tile_e_ref[...], preferred_element_type=F32)
        we_ref[r * gh:(r + 1) * gh, :] = jnp.where(me, blk, 0.0).astype(BF16)
        blk = jnp.where(my, jnp.dot(ca_ref[r].astype(BF16), tile_y_ref[...], preferred_element_type=F32), 0.0)
        for dp in range(4):
            wy_ref[dp * gp:(dp + 1) * gp, r * gh:(r + 1) * gh] = blk[:, dp * gh:(dp + 1) * gh].astype(BF16)


def _s5_compact(lam_re, lam_im, log_step, b_re, b_im, c_re, c_im):
    hp = lax.Precision.HIGHEST
    t = S5_BLOCK
    g, p = lam_re.shape[1:]
    h = b_re.shape[-1]
    n_oct = g // S5_OCT
    lam_re = jnp.minimum(lam_re.astype(F32), LAMBDA_RE_MAX)
    lam_im = lam_im.astype(F32)
    dt = jnp.exp(log_step.astype(F32))[..., None]
    mag = jnp.exp(lam_re * dt)
    lb_re = mag * jnp.cos(lam_im * dt)
    lb_im = mag * jnp.sin(lam_im * dt)
    den = lam_re * lam_re + lam_im * lam_im
    nr = lb_re - 1.0
    ni = lb_im
    cf_re = (nr * lam_re + ni * lam_im) / den
    cf_im = (ni * lam_re - nr * lam_im) / den
    b_re = b_re.astype(F32)
    b_im = b_im.astype(F32)
    br = cf_re[..., None] * b_re - cf_im[..., None] * b_im
    bi = cf_re[..., None] * b_im + cf_im[..., None] * b_re
    cr = c_re.astype(F32)
    ci = c_im.astype(F32)
    expo = np.concatenate([np.arange(t + 1), np.arange(t - 1, -1, -1), np.arange(t, 0, -1)]).astype(np.float32)
    j = jnp.asarray(expo)[None, :, None, None]
    pmag = jnp.exp(j * (lam_re * dt)[:, None])
    pw_re = pmag * jnp.cos(j * (lam_im * dt)[:, None])
    pw_im = pmag * jnp.sin(j * (lam_im * dt)[:, None])
    asc, desc0, desc1 = slice(0, t), slice(t + 1, 2 * t + 1), slice(2 * t + 1, 3 * t + 1)


    def lam_b(d_, sl):
        pr, pi = pw_re[d_, sl][..., None], pw_im[d_, sl][..., None]
        return pr * br[d_] - pi * bi[d_], pr * bi[d_] + pi * br[d_]

    pbf_re, pbf_im = lam_b(0, asc)
    pbb_re, pbb_im = lam_b(1, asc)
    pb_re = jnp.stack([pbf_re, pbb_re])
    pb_im = jnp.stack([pbf_im, pbb_im])
    kj = (jnp.einsum('dgep,djgph->djgeh', cr, pb_re, precision=hp)
          - jnp.einsum('dgep,djgph->djgeh', ci, pb_im, precision=hp))
    kj = jnp.stack([jnp.concatenate([kj[0, :1] + kj[1, :1], kj[0, 1:]], axis=0),
                    jnp.concatenate([jnp.zeros_like(kj[1, :1]), kj[1, 1:]], axis=0)])
    uk = jnp.transpose(kj, (2, 4, 0, 1, 3)).reshape(n_oct, S5_OCT * h, 2 * t * h)
    pb4 = jnp.stack(lam_b(0, desc0) + (pbb_re, pbb_im))
    pb3 = jnp.transpose(pb4.reshape(4, t, n_oct, S5_OCT, p, h), (2, 1, 3, 5, 0, 4))
    pb3 = pb3.reshape(n_oct, t, S5_OCT * h, 4 * p)

    def c_lam(d_, sl):
        pr, pi = pw_re[d_, sl][:, :, None, :], pw_im[d_, sl][:, :, None, :]
        return cr[d_] * pr - ci[d_] * pi, -(cr[d_] * pi + ci[d_] * pr)

    ca4 = jnp.stack(c_lam(0, slice(1, t + 1)) + c_lam(1, desc1))
    ca3 = jnp.transpose(ca4.reshape(4, t, n_oct, S5_OCT, h, p), (2, 1, 3, 5, 0, 4))
    ca3 = ca3.reshape(n_oct, t, S5_OCT * p, 4 * h)
    nk2 = S5_OCT * p // LANE
    a_re = jnp.transpose(pw_re[:, t].reshape(2, n_oct, nk2, LANE), (1, 0, 2, 3))
    a_im = jnp.transpose(pw_im[:, t].reshape(2, n_oct, nk2, LANE), (1, 0, 2, 3))
    dec = jnp.stack([jnp.concatenate([a_re, a_re], axis=2), jnp.concatenate([-a_im, a_im], axis=2)], axis=2)
    return uk, pb3, ca3, dec


def _s5_expand(uk, pb3, ca3):
    depth, n_oct = uk.shape[:2]
    colsel, tile_e, tile_y = (jnp.asarray(a, BF16) for a in _s5_sel_consts())
    kdim = S5_BLOCK * S5_OCT * S5_GROUP
    sdim = 4 * S5_OCT * S5_STATE
    return pl.pallas_call(
        _s5w_kernel,
        grid=(depth, n_oct),
        in_specs=[
            pl.BlockSpec((None, None) + uk.shape[2:], lambda l, o: (l, o, 0, 0)),
            pl.BlockSpec((None, None) + pb3.shape[2:], lambda l, o: (l, o, 0, 0, 0)),
            pl.BlockSpec((None, None) + ca3.shape[2:], lambda l, o: (l, o, 0, 0, 0)),
            pl.BlockSpec(colsel.shape, lambda l, o: (0, 0, 0)),
            pl.BlockSpec(tile_e.shape, lambda l, o: (0, 0)),
            pl.BlockSpec(tile_y.shape, lambda l, o: (0, 0)),
        ],
        out_specs=[
            pl.BlockSpec((None, None, kdim, kdim), lambda l, o: (l, o, 0, 0)),
            pl.BlockSpec((None, None, kdim, sdim), lambda l, o: (l, o, 0, 0)),
            pl.BlockSpec((None, None, sdim, kdim), lambda l, o: (l, o, 0, 0)),
        ],
        out_shape=[
            jax.ShapeDtypeStruct((depth, n_oct, kdim, kdim), BF16),
            jax.ShapeDtypeStruct((depth, n_oct, kdim, sdim), BF16),
            jax.ShapeDtypeStruct((depth, n_oct, sdim, kdim), BF16),
        ],
        compiler_params=_params(("arbitrary", "arbitrary")),
        name="s5w",
    )(uk, pb3, ca3, colsel, tile_e, tile_y)


def _block_rows(u_ref, tmr):
    return jnp.concatenate(
        [u_ref[pl.ds(r, tmr, stride=S5_BLOCK), :].astype(BF16) for r in range(S5_BLOCK)], axis=1)


def _s5e_kernel(u_ref, we_ref, e_ref, *, tmr):
    res = jnp.dot(_block_rows(u_ref, tmr), we_ref[...], preferred_element_type=F32)
    nk = res.shape[1] // (2 * LANE)
    for d in range(2):
        for k in range(nk):
            c0 = (d * nk + k) * LANE
            e_ref[d, pl.ds(k, tmr, stride=nk), :] = res[:, c0:c0 + LANE]


def _s5e(u, we, *, layer, tmr=1024):
    n, wid = u.shape
    m = n // S5_BLOCK
    _, n_oct, kdim, ncol = we.shape
    nk = ncol // (2 * LANE)
    tmr = _tile(m, tmr)
    return pl.pallas_call(
        functools.partial(_s5e_kernel, tmr=tmr),
        grid=(n_oct, m // tmr),
        in_specs=[
            pl.BlockSpec((tmr * S5_BLOCK, LANE), lambda o, i: (i, o)),
            pl.BlockSpec((None, None, kdim, ncol), lambda o, i: (layer, o, 0, 0)),
        ],
        out_specs=pl.BlockSpec((None, 2, tmr * nk, LANE), lambda o, i: (o, 0, i, 0)),
        out_shape=jax.ShapeDtypeStruct((n_oct, 2, m * nk, LANE), F32),
        compiler_params=_params(("arbitrary", "arbitrary")),
        name="s5e",
    )(u, we)


def _s5scan_kernel(ec_ref, el_ref, a_ref, sc_ref, sl_ref, *, nb, n_c, n_l, nk):
    rev = pl.program_id(1) == 1
    a1 = a_ref[0]
    a2 = a_ref[1]

    def run(e_ref, s_ref, n, carry):
        def body(s, carry):
            i = jnp.where(rev, n - 1 - s, s)
            new = []
            for b in range(nb):
                x = carry[b]
                rows = pl.ds(pl.multiple_of((b * n + i) * nk, nk), nk)
                s_ref[rows, :] = x
                new.append(a1 * x + a2 * pltpu.roll(x, nk // 2, 0) + e_ref[rows, :])
            return tuple(new)
        return lax.fori_loop(0, n, body, carry, unroll=2)

    carry = tuple(jnp.zeros((nk, LANE), F32) for _ in range(nb))
    carry = run(ec_ref, sc_ref, n_c, carry)
    run(el_ref, sl_ref, n_l, carry)


def _s5scan(e_ctx, e_lat, dec, *, layer, nb):
    n_oct, _, rc, _ = e_ctx.shape
    rl = e_lat.shape[2]
    nk = dec.shape[-2]
    return pl.pallas_call(
        functools.partial(_s5scan_kernel, nb=nb, n_c=rc // (nk * nb), n_l=rl // (nk * nb), nk=nk),
        grid=(n_oct, 2),
        in_specs=[
            pl.BlockSpec((None, None, rc, LANE), lambda o, d: (o, d, 0, 0)),
            pl.BlockSpec((None, None, rl, LANE), lambda o, d: (o, d, 0, 0)),
            pl.BlockSpec((None, None, None, 2, nk, LANE), lambda o, d: (layer, o, d, 0, 0, 0)),
        ],
        out_specs=[
            pl.BlockSpec((None, None, rc, LANE), lambda o, d: (o, d, 0, 0)),
            pl.BlockSpec((None, None, rl, LANE), lambda o, d: (o, d, 0, 0)),
        ],
        out_shape=[jax.ShapeDtypeStruct(e_ctx.shape, F32), jax.ShapeDtypeStruct(e_lat.shape, F32)],
        compiler_params=_params(("arbitrary", "arbitrary")),
        name="s5scan",
    )(e_ctx, e_lat, dec)


def _s5y_kernel(u_ref, s_ref, k_ref, wy_ref, y_ref, *, tmr):
    nk = s_ref.shape[1] // tmr
    st = jnp.concatenate(
        [s_ref[d, pl.ds(k, tmr, stride=nk), :].astype(BF16) for d in range(2) for k in range(nk)], axis=1)
    res = jnp.dot(_block_rows(u_ref, tmr), k_ref[...], preferred_element_type=F32)
    res += jnp.dot(st, wy_ref[...], preferred_element_type=F32)
    for r in range(S5_BLOCK):
        y_ref[pl.ds(r, tmr, stride=S5_BLOCK), :] = res[:, r * LANE:(r + 1) * LANE]


def _s5y(u, s, kmat, wy, *, layer, tmr=1024):
    n, wid = u.shape
    m = n // S5_BLOCK
    _, n_oct, kdim, _ = kmat.shape
    sdim = wy.shape[2]
    nk = s.shape[2] // m
    tmr = _tile(m, tmr)
    return pl.pallas_call(
        functools.partial(_s5y_kernel, tmr=tmr),
        grid=(n_oct, m // tmr),
        in_specs=[
            pl.BlockSpec((tmr * S5_BLOCK, LANE), lambda o, i: (i, o)),
            pl.BlockSpec((None, 2, tmr * nk, LANE), lambda o, i: (o, 0, i, 0)),
            pl.BlockSpec((None, None, kdim, kdim), lambda o, i: (layer, o, 0, 0)),
            pl.BlockSpec((None, None, sdim, kdim), lambda o, i: (layer, o, 0, 0)),
        ],
        out_specs=pl.BlockSpec((tmr * S5_BLOCK, LANE), lambda o, i: (i, o)),
        out_shape=jax.ShapeDtypeStruct((n, wid), F32),
        compiler_params=_params(("arbitrary", "arbitrary")),
        name="s5y",
    )(u, s, kmat, wy)


def _hgrn_masks():
    c = CHUNK
    t = np.arange(c)
    masks = []
    half = c // 2
    while half >= SUB:
        par = 2 * half
        second = (t % par) >= half
        same_parent = (t[:, None] // par) == (t[None, :] // par)
        masks.append((same_parent & second[:, None] & (~second)[None, :]).astype(np.float32))
        half //= 2
    masks.append((((t[:, None] // SUB) == (t[None, :] // SUB)) & (t[None, :] <= t[:, None])).astype(np.float32))
    return np.stack(masks)


def _hgrn_decays(g, reverse):
    nb = CHUNK // SUB
    g3 = g.reshape(nb, SUB, g.shape[1])
    r = lax.broadcasted_iota(jnp.int32, g3.shape, 1)
    p = g3
    for k in (1, 2, 4):
        if reverse:
            p = p + jnp.where(r <= SUB - 1 - k, pltpu.roll(p, SUB - k, 1), 0.0)
        else:
            p = p + jnp.where(r >= k, pltpu.roll(p, k, 1), 0.0)
    last = 0 if reverse else SUB - 1
    ref = SUB // 2 if reverse else SUB // 2 - 1
    tot = jnp.broadcast_to(p[:, last:last + 1, :], p.shape)
    ep = jnp.exp(p)
    eq = jnp.exp(tot - p)
    ed = p - jnp.broadcast_to(p[:, ref:ref + 1, :], p.shape)
    edp = jnp.exp(ed)
    edn = jnp.exp(-ed)
    et = jnp.broadcast_to(ep[:, last:last + 1, :], p.shape)
    mem = (lambda i: nb - 1 - i) if reverse else (lambda i: i)
    epb = [ep[mem(i)] for i in range(nb)]
    eqb = [eq[mem(i)] for i in range(nb)]
    etb = [et[mem(i)] for i in range(nb)]

    def assemble(blocks):
        return jnp.concatenate([blocks[mem(i)] for i in range(nb)], axis=0)

    cq = [None] * nb
    acc = None
    for i in range(nb):
        cq[i] = epb[i] if acc is None else epb[i] * acc
        acc = etb[i] if acc is None else acc * etb[i]
    total = acc[0:1, :]
    ck = [None] * nb
    acc = None
    for i in range(nb - 1, -1, -1):
        ck[i] = eqb[i] if acc is None else eqb[i] * acc
        acc = etb[i] if acc is None else acc * etb[i]
    levels = []
    half = nb // 2
    while half >= 1:
        par = 2 * half
        blocks = []
        for i in range(nb):
            j = i % par
            if j >= half:
                f = epb[i]
                for m in range(i - j + half, i):
                    f = f * etb[m]
            else:
                f = eqb[i]
                for m in range(i + 1, i - j + half):
                    f = f * etb[m]
            blocks.append(f)
        levels.append(assemble(blocks))
        half //= 2
    return levels, edp.reshape(g.shape), edn.reshape(g.shape), assemble(cq), assemble(ck), total


def _hgrn_kernel(*refs, nsub, heads, reverse, final, n_lvl):
    if final:
        (q_ref, z_ref, v_ref, lb_ref, masks_ref, s0_ref, of_ref, gate_ref, hnw_ref,
         o_ref, sfin_ref, st_ref) = refs
    else:
        q_ref, z_ref, v_ref, lb_ref, masks_ref, s0_ref, o_ref, sfin_ref, st_ref = refs
    j = pl.program_id(1)
    nj = pl.num_programs(1)
    c = CHUNK
    hd = HEAD_DIM
    nt = (((1,), (1,)), ((), ()))
    tn = (((0,), (0,)), ((), ()))

    @pl.when(j == 0)
    def _():
        st_ref[...] = s0_ref[...]

    mbool = [masks_ref[lv] > 0.0 for lv in range(n_lvl)]

    mpair = [jnp.concatenate([m, m], axis=0) for m in mbool]
    zpad = jnp.zeros((c, hd), BF16)

    def chunk(s, _):
        cl = (nsub - 1 - s) if reverse else s
        rows = pl.ds(pl.multiple_of(cl * c, c), c)
        for hp in range(heads // 2):
            sl = slice(2 * hp * hd, (2 * hp + 2) * hd)
            z = z_ref[rows, sl]
            q = q_ref[rows, sl]
            v = v_ref[rows, sl].astype(BF16)
            lb = lb_ref[:, sl]
            f = lb + (1.0 - lb) * jax.nn.sigmoid(z)
            k = 1.0 - f
            g = jnp.log(jnp.maximum(f, F_MIN))
            levels, edp, edn, cq, ck, total = _hgrn_decays(g, reverse)
            qb = q.astype(BF16)
            kb = k.astype(BF16)
            a = jnp.zeros((2 * c, c), F32)
            for lv in range(n_lvl - 1, -1, -1):
                if lv < n_lvl - 1:
                    e = levels[lv].astype(BF16)
                    ql = qb * e
                    kl = kb * e
                else:
                    ql = qb * edp.astype(BF16)
                    kl = kb * edn.astype(BF16)
                lhs = jnp.concatenate([jnp.concatenate([ql[:, :hd], zpad], axis=1),
                                       jnp.concatenate([zpad, ql[:, hd:]], axis=1)], axis=0)
                sc = lax.dot_general(lhs, kl, nt, preferred_element_type=F32)
                a = jnp.where(mpair[lv], sc, a)
            ab = a.astype(BF16)
            qd = qb * cq.astype(BF16)
            kd = kb * ck.astype(BF16)
            for i in range(2):
                h = 2 * hp + i
                hs = slice(i * hd, (i + 1) * hd)
                st = st_ref[h]
                o_h = jnp.dot(ab[i * c:(i + 1) * c], v[:, hs], preferred_element_type=F32)
                o_h += lax.dot_general(qd[:, hs], st.astype(BF16), nt, preferred_element_type=F32)
                st_ref[h] = st * total[:, hs] + lax.dot_general(v[:, hs], kd[:, hs], tn,
                                                                 preferred_element_type=F32)
                osl = slice(h * hd, (h + 1) * hd)
                if final:
                    o_h = o_h + of_ref[rows, osl]
                    ms = jnp.mean(o_h * o_h, axis=-1, keepdims=True)
                    o_h = o_h * lax.rsqrt(ms + EPS) * hnw_ref[...] * _silu(gate_ref[rows, osl])
                o_ref[rows, osl] = o_h.astype(o_ref.dtype)
        return 0

    lax.fori_loop(0, nsub, chunk, 0, unroll=8)

    @pl.when(j == nj - 1)
    def _():
        sfin_ref[...] = st_ref[...]


def _hgrn(p, lb, masks, s0, *, direction, width, o_fwd=None, hnw=None, nsub=8):
    b, t, _ = p.shape
    heads = width // HEAD_DIM
    nsub = _tile(t // CHUNK, nsub, 1)
    rows = nsub * CHUNK
    nj = t // rows
    reverse = direction == 1
    final = o_fwd is not None
    n_lvl = masks.shape[0]
    blk = (lambda j: nj - 1 - j) if reverse else (lambda j: j)
    zcol = 2 if reverse else 1
    in_specs = [
        pl.BlockSpec((None, rows, width), lambda bi, j: (bi, blk(j), 0)),
        pl.BlockSpec((None, rows, width), lambda bi, j: (bi, blk(j), zcol)),
        pl.BlockSpec((None, rows, width), lambda bi, j: (bi, blk(j), 3)),
        pl.BlockSpec((1, width), lambda bi, j: (0, 0)),
        pl.BlockSpec(masks.shape, lambda bi, j: (0, 0, 0)),
        pl.BlockSpec((None, heads, HEAD_DIM, HEAD_DIM), lambda bi, j: (bi, 0, 0, 0)),
    ]
    args = [p, p, p, lb.reshape(1, width), masks, s0]
    if final:
        in_specs += [
            pl.BlockSpec((None, rows, width), lambda bi, j: (bi, blk(j), 0)),
            pl.BlockSpec((None, rows, width), lambda bi, j: (bi, blk(j), 4)),
            pl.BlockSpec((1, HEAD_DIM), lambda bi, j: (0, 0)),
        ]
        args += [o_fwd, p, hnw.reshape(1, HEAD_DIM)]
    return pl.pallas_call(
        functools.partial(_hgrn_kernel, nsub=nsub, heads=heads, reverse=reverse, final=final, n_lvl=n_lvl),
        grid=(b, nj),
        in_specs=in_specs,
        out_specs=[
            pl.BlockSpec((None, rows, width), lambda bi, j: (bi, blk(j), 0)),
            pl.BlockSpec((None, heads, HEAD_DIM, HEAD_DIM), lambda bi, j: (bi, 0, 0, 0)),
        ],
        out_shape=[
            jax.ShapeDtypeStruct((b, t, width), F32),
            jax.ShapeDtypeStruct((b, heads, HEAD_DIM, HEAD_DIM), F32),
        ],
        scratch_shapes=[pltpu.VMEM((heads, HEAD_DIM, HEAD_DIM), F32)],
        compiler_params=_params(("arbitrary", "arbitrary")),
        name="hgrn_bwd" if final else "hgrn_fwd",
    )(*args)


def _mixout_kernel(x_ref, y_ref, u_ref, hg_ref, m_ref, dsk_ref, wglu_ref, bglu_ref, wo_ref, o_ref,
                   *, s5w, rb, colmajor):
    yy = y_ref[...] + dsk_ref[...] * u_ref[...]
    g = _gelu_tanh(yy)
    zz = jnp.dot(g.astype(BF16), wglu_ref[...], preferred_element_type=F32) + bglu_ref[...]
    s5 = (g * jax.nn.sigmoid(zz)).astype(BF16)
    if colmajor:
        hg = jnp.concatenate([hg_ref[:, r, :] for r in range(rb)], axis=0).astype(BF16)
    else:
        hg = hg_ref[...].astype(BF16)
    acc = jnp.dot(s5, wo_ref[:s5w, :], preferred_element_type=F32)
    acc += jnp.dot(hg, wo_ref[s5w:, :], preferred_element_type=F32)
    o_ref[...] = x_ref[...] + m_ref[0, 5:6, :] * acc


def _mixout(x3d, y3d, u3d, hg, mods, d_skip, w_glu, b_glu, w_o, *, layer, mod_of_batch, colmajor, rb=8):
    b, t, d = x3d.shape
    s5w = y3d.shape[-1]
    hw = hg.shape[-1]
    if colmajor:
        cols = GRID_W
        rows = t // cols
        rb = _tile(rows, rb, 8)
        tm = rb * cols
        hg_v = hg.reshape(b, cols, rows, hw)
        hg_spec = pl.BlockSpec((None, cols, rb, hw), lambda bi, i: (bi, 0, i, 0))
    else:
        rb = 1
        tm = _tile(t, 512)
        hg_v = hg
        hg_spec = pl.BlockSpec((None, tm, hw), lambda bi, i: (bi, i, 0))
    mod_fn = (lambda bi: bi) if mod_of_batch else (lambda bi: b)
    return pl.pallas_call(
        functools.partial(_mixout_kernel, s5w=s5w, rb=rb, colmajor=colmajor),
        grid=(b, t // tm),
        in_specs=[
            pl.BlockSpec((None, tm, d), lambda bi, i: (bi, i, 0)),
            pl.BlockSpec((None, tm, s5w), lambda bi, i: (bi, i, 0)),
            pl.BlockSpec((None, tm, s5w), lambda bi, i: (bi, i, 0)),
            hg_spec,
            pl.BlockSpec((1, N_MOD, d), lambda bi, i: (mod_fn(bi), 0, 0)),
            pl.BlockSpec((1, s5w), lambda bi, i: (0, 0)),
            pl.BlockSpec((None, s5w, s5w), lambda bi, i: (layer, 0, 0)),
            pl.BlockSpec((1, s5w), lambda bi, i: (0, 0)),
            pl.BlockSpec((None, s5w + hw, d), lambda bi, i: (layer, 0, 0)),
        ],
        out_specs=pl.BlockSpec((None, tm, d), lambda bi, i: (bi, i, 0)),
        out_shape=jax.ShapeDtypeStruct((b, t, d), F32),
        compiler_params=_params(("arbitrary", "arbitrary")),
        name="mixout",
    )(x3d, y3d, u3d, hg_v, mods, d_skip.reshape(1, s5w), w_glu, b_glu.reshape(1, s5w), w_o)


def kernel(x, c, ctx, c_ctx, w_ada, b_ada, norm_w, ffn_w_gate, ffn_w_up, ffn_w_down, w_in, w_out,
           s5_lambda_re, s5_lambda_im, s5_log_step, s5_b_re, s5_b_im, s5_c_re, s5_c_im, s5_d,
           s5_w_glu, s5_b_glu, hgrn_lower_bounds, hgrn_norm_w, final_norm_w):
    batch, seq, d = x.shape
    n_ctx = ctx.shape[1]
    depth = w_ada.shape[0]
    s5w = s5_d.shape[-1]
    hw = hgrn_lower_bounds.shape[-1]
    rows = seq // GRID_W
    assert batch < MOD_ROWS and rows % CHUNK == 0 and n_ctx % CHUNK == 0
    assert seq % S5_BLOCK == 0 and n_ctx % S5_BLOCK == 0 and (s5w // S5_GROUP) % S5_OCT == 0

    cvec = jnp.zeros((MOD_ROWS, d), F32).at[:batch].set(c.astype(F32)).at[batch].set(c_ctx.astype(F32))
    mods_all = _ada(cvec, w_ada, b_ada).reshape(depth, MOD_ROWS, N_MOD, d)

    lb_soft = jax.nn.softmax(hgrn_lower_bounds.astype(F32), axis=0)
    lb_all = jnp.cumsum(lb_soft, axis=0) - lb_soft[0]

    masks_np = _hgrn_masks()
    masks_f = jnp.asarray(masks_np, F32)
    masks_b = jnp.asarray(masks_np[:, ::-1, ::-1].copy(), F32)
    heads = hw // HEAD_DIM
    s_zero = jnp.zeros((batch, heads, HEAD_DIM, HEAD_DIM), F32)

    wg = ffn_w_gate.astype(BF16)
    wu = ffn_w_up.astype(BF16)
    wd = ffn_w_down.astype(BF16)
    uk, pb3, ca3, dec = (jnp.stack(a) for a in zip(*[
        _s5_compact(s5_lambda_re[l], s5_lambda_im[l], s5_log_step[l], s5_b_re[l], s5_b_im[l],
                    s5_c_re[l], s5_c_im[l]) for l in range(depth)]))
    kmat, we, wy = _s5_expand(uk, pb3, ca3)
    w_in_b = w_in[:, :, :s5w].astype(BF16)
    w_h = w_in[:, :, s5w:].astype(BF16)
    w_o = w_out.astype(BF16)
    w_glu = s5_w_glu.astype(BF16)

    xl = x.astype(F32)
    xc = ctx.astype(F32)
    lat = dict(rows_per_mod=seq, mod_off=0)
    cx = dict(rows_per_mod=batch * n_ctx, mod_off=batch)
    for l in range(depth):
        last = l == depth - 1
        mods = mods_all[l]

        xl = _ffn(xl.reshape(batch * seq, d), mods, norm_w[l, 0], wg, wu, wd, layer=l, half=0, base=0,
                  **lat).reshape(batch, seq, d)
        xc = _ffn(xc.reshape(batch * n_ctx, d), mods, norm_w[l, 0], wg, wu, wd, layer=l, half=0, base=0,
                  **cx).reshape(batch, n_ctx, d)

        u_l = _uproj(xl.reshape(batch * seq, d), mods, norm_w[l, 1], w_in_b, layer=l, s5w=s5w, **lat)
        u_c = _uproj(xc.reshape(batch * n_ctx, d), mods, norm_w[l, 1], w_in_b, layer=l, s5w=s5w, **cx)
        p_lat = _hproj(xl, mods, norm_w[l, 1], w_h, layer=l, hw=hw, cols=GRID_W, mod_row=None)
        p_ctx = _hproj(xc.reshape(1, batch * n_ctx, d), mods, norm_w[l, 1], w_h, layer=l, hw=hw, cols=1,
                       mod_row=batch).reshape(batch, n_ctx, 5 * hw)

        e_l = _s5e(u_l, we, layer=l)
        e_c = _s5e(u_c, we, layer=l)
        st_c, st_l = _s5scan(e_c, e_l, dec, layer=l, nb=batch)
        y_l = _s5y(u_l, st_l, kmat, wy, layer=l).reshape(batch, seq, s5w)

        lb_f = lb_all[l, 0]
        lb_b = lb_all[l, 1]
        oc_f, sc_f = _hgrn(p_ctx, lb_f, masks_f, s_zero, direction=0, width=hw)
        ol_f, _ = _hgrn(p_lat, lb_f, masks_f, sc_f, direction=0, width=hw)
        hg_c, sc_b = _hgrn(p_ctx, lb_b, masks_b, s_zero, direction=1, width=hw,
                           o_fwd=oc_f, hnw=hgrn_norm_w[l])
        hg_l, _ = _hgrn(p_lat, lb_b, masks_b, sc_b, direction=1, width=hw,
                        o_fwd=ol_f, hnw=hgrn_norm_w[l])

        xl = _mixout(xl, y_l, u_l.reshape(batch, seq, s5w), hg_l, mods, s5_d[l], w_glu, s5_b_glu[l], w_o,
                     layer=l, mod_of_batch=True, colmajor=True)
        if not last:
            y_c = _s5y(u_c, st_c, kmat, wy, layer=l).reshape(batch, n_ctx, s5w)
            xc = _mixout(xc, y_c, u_c.reshape(batch, n_ctx, s5w), hg_c, mods, s5_d[l], w_glu, s5_b_glu[l],
                         w_o, layer=l, mod_of_batch=False, colmajor=False)

        xl = _ffn(xl.reshape(batch * seq, d), mods, norm_w[l, 2], wg, wu, wd, layer=l, half=1, base=6,
                  final_nw=final_norm_w if last else None, **lat).reshape(batch, seq, d)
        if not last:
            xc = _ffn(xc.reshape(batch * n_ctx, d), mods, norm_w[l, 2], wg, wu, wd, layer=l, half=1, base=6,
                      **cx).reshape(batch, n_ctx, d)
    return xl
```

```python
import functools

import numpy as np
import jax
import jax.numpy as jnp
from jax import lax
from jax.experimental import pallas as pl
from jax.experimental.pallas import tpu as pltpu

F32 = jnp.float32
BF16 = jnp.bfloat16

EPS = 1e-6
F_MIN = 1e-6
LAMBDA_RE_MAX = -1e-4
GRID_W = 64
N_MOD = 9
S5_GROUP = 16
S5_STATE = 64
S5_BLOCK = 8
S5_OCT = 8
HEAD_DIM = 128
CHUNK = 64
SUB = 8
LANE = 128
MOD_ROWS = 8
VMEM_LIMIT = 60 * 1024 * 1024


def _params(sem):
    return pltpu.CompilerParams(dimension_semantics=sem, vmem_limit_bytes=VMEM_LIMIT)


def _tile(n, pref, mult=8):
    t = min(n, pref)
    while t > 0:
        if n % t == 0 and t % mult == 0:
            return t
        t -= 1
    return n


def _norm_mod(x, nw, shift, scale):
    ms = jnp.mean(x * x, axis=-1, keepdims=True)
    gain = nw * (1.0 + scale)
    return (x * lax.rsqrt(ms + EPS)) * gain + shift


def _silu(x):
    return x * jax.nn.sigmoid(x)


def _gelu_tanh(x):
    return 0.5 * x * (1.0 + jnp.tanh(0.7978845608028654 * (x + 0.044715 * (x * x * x))))


def _ada_kernel(c_ref, w_ref, b_ref, o_ref):
    a = _silu(c_ref[...])
    a_hi = a.astype(BF16)
    a_lo = (a - a_hi.astype(F32)).astype(BF16)
    w = w_ref[...]
    w_hi = w.astype(BF16)
    w_lo = (w - w_hi.astype(F32)).astype(BF16)
    both = jnp.dot(jnp.concatenate([a_hi, a_lo], axis=0), w_hi, preferred_element_type=F32)
    acc = both[:MOD_ROWS] + both[MOD_ROWS:] + jnp.dot(a_hi, w_lo, preferred_element_type=F32)
    o_ref[...] = acc + b_ref[...]


def _ada(cvec, w_ada, b_ada):
    depth, d, n = w_ada.shape
    tn = _tile(n, 2048, LANE)
    return pl.pallas_call(
        _ada_kernel,
        grid=(depth, n // tn),
        in_specs=[
            pl.BlockSpec((MOD_ROWS, d), lambda l, j: (0, 0)),
            pl.BlockSpec((None, d, tn), lambda l, j: (l, 0, j)),
            pl.BlockSpec((None, 1, tn), lambda l, j: (l, 0, j)),
        ],
        out_specs=pl.BlockSpec((None, MOD_ROWS, tn), lambda l, j: (l, 0, j)),
        out_shape=jax.ShapeDtypeStruct((depth, MOD_ROWS, n), F32),
        compiler_params=_params(("arbitrary", "arbitrary")),
        name="ada",
    )(cvec, w_ada, b_ada.reshape(depth, 1, n))


def _ffn_kernel(*refs, base, n_f, final, tail):
    if final:
        x_ref, m_ref, nw_ref, wg_ref, wu_ref, wd_ref, fnw_ref, o_ref, h_ref = refs
    else:
        x_ref, m_ref, nw_ref, wg_ref, wu_ref, wd_ref, o_ref, h_ref = refs
    j = pl.program_id(1)

    @pl.when(j == 0)
    def _():
        h = _norm_mod(x_ref[...], nw_ref[...], m_ref[0, base:base + 1, :], m_ref[0, base + 1:base + 2, :])
        h_ref[...] = h.astype(BF16)

    def partial_sum(valid=None):
        h = h_ref[...]
        g = jnp.dot(h, wg_ref[...], preferred_element_type=F32)
        u = jnp.dot(h, wu_ref[...], preferred_element_type=F32)
        a = (_silu(g) * u).astype(BF16)
        wd = wd_ref[...]
        if valid is not None:
            a = jnp.where(lax.broadcasted_iota(jnp.int32, a.shape, 1) < valid, a, jnp.zeros_like(a))
            wd = jnp.where(lax.broadcasted_iota(jnp.int32, wd.shape, 0) < valid, wd, jnp.zeros_like(wd))
        return jnp.dot(a, wd, preferred_element_type=F32)

    @pl.when(j == 0)
    def _():
        o_ref[...] = partial_sum()

    @pl.when((j > 0) & (j < n_f - 1))
    def _():
        o_ref[...] += partial_sum()

    @pl.when(j == n_f - 1)
    def _():
        y = x_ref[...] + (0.5 * m_ref[0, base + 2:base + 3, :]) * (o_ref[...] + partial_sum(tail))
        if final:
            ms = jnp.mean(y * y, axis=-1, keepdims=True)
            y = y * lax.rsqrt(ms + EPS) * fnw_ref[...]
        o_ref[...] = y


def _ffn(x2d, mods, nw, wg, wu, wd, *, layer, half, base, rows_per_mod, mod_off, final_nw=None,
         tm=1024, tf=512):
    n, d = x2d.shape
    fp = wg.shape[-1]
    tm = _tile(min(n, rows_per_mod), tm)
    n_f = pl.cdiv(fp, tf)
    tail = fp - (n_f - 1) * tf if fp % tf else None
    assert n_f >= 2
    final = final_nw is not None
    in_specs = [
        pl.BlockSpec((tm, d), lambda i, j: (i, 0)),
        pl.BlockSpec((1, N_MOD, d), lambda i, j: ((i * tm) // rows_per_mod + mod_off, 0, 0)),
        pl.BlockSpec((1, d), lambda i, j: (0, 0)),
        pl.BlockSpec((None, None, d, tf), lambda i, j: (layer, half, 0, j)),
        pl.BlockSpec((None, None, d, tf), lambda i, j: (layer, half, 0, j)),
        pl.BlockSpec((None, None, tf, d), lambda i, j: (layer, half, j, 0)),
    ]
    args = [x2d, mods, nw.reshape(1, d), wg, wu, wd]
    if final:
        in_specs.append(pl.BlockSpec((1, d), lambda i, j: (0, 0)))
        args.append(final_nw.reshape(1, d))
    return pl.pallas_call(
        functools.partial(_ffn_kernel, base=base, n_f=n_f, final=final, tail=tail),
        grid=(n // tm, n_f),
        in_specs=in_specs,
        out_specs=pl.BlockSpec((tm, d), lambda i, j: (i, 0)),
        out_shape=jax.ShapeDtypeStruct((n, d), F32),
        scratch_shapes=[pltpu.VMEM((tm, d), BF16)],
        compiler_params=_params(("arbitrary", "arbitrary")),
        name="ffn_final" if final else "ffn",
    )(*args)


def _uproj_kernel(x_ref, m_ref, nw_ref, w_ref, u_ref):
    h = _norm_mod(x_ref[...], nw_ref[...], m_ref[0, 3:4, :], m_ref[0, 4:5, :])
    u_ref[...] = jnp.dot(h.astype(BF16), w_ref[...], preferred_element_type=F32)


def _uproj(x2d, mods, nw, w_in, *, layer, s5w, rows_per_mod, mod_off, tm=1024):
    n, d = x2d.shape
    tm = _tile(min(n, rows_per_mod), tm)
    return pl.pallas_call(
        _uproj_kernel,
        grid=(n // tm,),
        in_specs=[
            pl.BlockSpec((tm, d), lambda i: (i, 0)),
            pl.BlockSpec((1, N_MOD, d), lambda i: ((i * tm) // rows_per_mod + mod_off, 0, 0)),
            pl.BlockSpec((1, d), lambda i: (0, 0)),
            pl.BlockSpec((None, d, s5w), lambda i: (layer, 0, 0)),
        ],
        out_specs=pl.BlockSpec((tm, s5w), lambda i: (i, 0)),
        out_shape=jax.ShapeDtypeStruct((n, s5w), F32),
        compiler_params=_params(("arbitrary",)),
        name="uproj",
    )(x2d, mods, nw.reshape(1, d), w_in)


def _hproj_kernel(*refs, colmajor, hw):
    if colmajor:
        x_ref, m_ref, nw_ref, w_ref, perm_ref, o_ref = refs
    else:
        x_ref, m_ref, nw_ref, w_ref, o_ref = refs
    h = _norm_mod(x_ref[...], nw_ref[...], m_ref[0, 3:4, :], m_ref[0, 4:5, :]).astype(BF16)
    if colmajor:
        h = h.reshape(perm_ref.shape[0], h.shape[-1])
        h = jnp.dot(perm_ref[...], h, preferred_element_type=F32).astype(BF16)
    for nb in range(o_ref.shape[-1] // hw):
        p = jnp.dot(h, w_ref[:, nb * hw:(nb + 1) * hw], preferred_element_type=F32)
        o_ref[:, nb * hw:(nb + 1) * hw] = _silu(p) if nb == 0 else p


def _hproj(x3d, mods, nw, w_h, *, layer, hw, cols, mod_row, cb=8, tr=512):
    b, t, d = x3d.shape
    nh = w_h.shape[-1]
    colmajor = cols > 1
    mod_fn = (lambda bi: bi) if mod_row is None else (lambda bi: mod_row)
    in_specs = [
        None,
        pl.BlockSpec((1, N_MOD, d), lambda bi, i: (mod_fn(bi), 0, 0)),
        pl.BlockSpec((1, d), lambda bi, i: (0, 0)),
        pl.BlockSpec((None, d, nh), lambda bi, i: (layer, 0, 0), pipeline_mode=pl.Buffered(1)),
    ]
    args = [None, mods, nw.reshape(1, d), w_h]
    if colmajor:
        rows = t // cols
        cb = _tile(cols, cb, 8)
        tr = rows * cb
        args[0] = x3d.reshape(b, rows, cols, d)
        in_specs[0] = pl.BlockSpec((None, rows, cb, d), lambda bi, i: (bi, 0, i, 0))
        src = np.arange(tr).reshape(rows, cb).T.reshape(-1)
        perm = np.zeros((tr, tr), np.float32)
        perm[np.arange(tr), src] = 1.0
        args.append(jnp.asarray(perm, BF16))
        in_specs.append(pl.BlockSpec(perm.shape, lambda bi, i: (0, 0)))
    else:
        tr = _tile(t, tr)
        args[0] = x3d
        in_specs[0] = pl.BlockSpec((None, tr, d), lambda bi, i: (bi, i, 0))
    return pl.pallas_call(
        functools.partial(_hproj_kernel, colmajor=colmajor, hw=hw),
        grid=(b, t // tr),
        in_specs=in_specs,
        out_specs=pl.BlockSpec((None, tr, nh), lambda bi, i: (bi, i, 0)),
        out_shape=jax.ShapeDtypeStruct((b, t, nh), F32),
        compiler_params=_params(("arbitrary", "arbitrary")),
        name="hproj",
    )(*args)


def _s5_sel_consts():
    t, h, p, o8 = S5_BLOCK, S5_GROUP, S5_STATE, S5_OCT
    colsel = np.zeros((t, 2, t, h, t, o8, h), np.float32)
    for r in range(t):
        for r2 in range(t):
            for hh in range(h):
                if r2 >= r:
                    colsel[r, 0, r2 - r, hh, r2, :, hh] = 1.0
                if r >= r2:
                    colsel[r, 1, r - r2, hh, r2, :, hh] = 1.0
    colsel = colsel.reshape(t, 2 * t * h, t * o8 * h)
    tile_e = np.zeros((4, p, 4, o8, p), np.float32)
    tile_y = np.zeros((4, h, 4, o8, h), np.float32)
    for dp in range(4):
        for i in range(p):
            tile_e[dp, i, dp, :, i] = 1.0
        for i in range(h):
            tile_y[dp, i, dp, :, i] = 1.0
    return colsel, tile_e.reshape(4 * p, 4 * o8 * p), tile_y.reshape(4 * h, 4 * o8 * h)


def _s5w_kernel(uk_ref, pb_ref, ca_ref, colsel_ref, tile_e_ref, tile_y_ref, k_ref, we_ref, wy_ref):
    t, h, p, o8 = S5_BLOCK, S5_GROUP, S5_STATE, S5_OCT
    gh = o8 * h
    gp = o8 * p

    def diag_mask(shape, row_div, col_mod, col_div):
        rg = lax.broadcasted_iota(jnp.int32, shape, 0) // row_div
        cg = (lax.broadcasted_iota(jnp.int32, shape, 1) % col_mod) // col_div
        return rg == cg

    mk = diag_mask((gh, t * gh), h, gh, h)
    me = diag_mask((gh, 4 * gp), h, gp, p)
    my = diag_mask((gp, 4 * gh), p, gh, h)
    uk = uk_ref[...].astype(BF16)
    for r in range(t):
        blk = jnp.dot(uk, colsel_ref[r], preferred_element_type=F32)
        k_ref[r * gh:(r + 1) * gh, :] = jnp.where(mk, blk, 0.0).astype(BF16)
        blk = jnp.dot(pb_ref[r].astype(BF16), tile_e_ref[...], preferred_element_type=F32)
        we_ref[r * gh:(r + 1) * gh, :] = jnp.where(me, blk, 0.0).astype(BF16)
        blk = jnp.where(my, jnp.dot(ca_ref[r].astype(BF16), tile_y_ref[...], preferred_element_type=F32), 0.0)
        for dp in range(4):
            wy_ref[dp * gp:(dp + 1) * gp, r * gh:(r + 1) * gh] = blk[:, dp * gh:(dp + 1) * gh].astype(BF16)


def _s5_compact(lam_re, lam_im, log_step, b_re, b_im, c_re, c_im):
    hp = lax.Precision.HIGHEST
    t = S5_BLOCK
    depth, _, g, p = lam_re.shape
    h = b_re.shape[-1]
    n_oct = g // S5_OCT
    lam_re = jnp.minimum(lam_re.astype(F32), LAMBDA_RE_MAX)
    lam_im = lam_im.astype(F32)
    dt = jnp.exp(log_step.astype(F32))[..., None]
    mag = jnp.exp(lam_re * dt)
    lb_re = mag * jnp.cos(lam_im * dt)
    lb_im = mag * jnp.sin(lam_im * dt)
    den = lam_re * lam_re + lam_im * lam_im
    nr = lb_re - 1.0
    ni = lb_im
    cf_re = (nr * lam_re + ni * lam_im) / den
    cf_im = (ni * lam_re - nr * lam_im) / den
    b_re = b_re.astype(F32)
    b_im = b_im.astype(F32)
    br = cf_re[..., None] * b_re - cf_im[..., None] * b_im
    bi = cf_re[..., None] * b_im + cf_im[..., None] * b_re
    cr = c_re.astype(F32)
    ci = c_im.astype(F32)
    expo = np.concatenate([np.arange(t + 1), np.arange(t - 1, -1, -1), np.arange(t, 0, -1)]).astype(np.float32)
    j = jnp.asarray(expo)[None, None, :, None, None]
    pmag = jnp.exp(j * (lam_re * dt)[:, :, None])
    pw_re = pmag * jnp.cos(j * (lam_im * dt)[:, :, None])
    pw_im = pmag * jnp.sin(j * (lam_im * dt)[:, :, None])
    asc, desc0, desc1 = slice(0, t), slice(t + 1, 2 * t + 1), slice(2 * t + 1, 3 * t + 1)

    def lam_b(d_, sl):
        pr, pi = pw_re[:, d_, sl][..., None], pw_im[:, d_, sl][..., None]
        b_r, b_i = br[:, d_][:, None], bi[:, d_][:, None]
        return pr * b_r - pi * b_i, pr * b_i + pi * b_r

    pbf_re, pbf_im = lam_b(0, asc)
    pbb_re, pbb_im = lam_b(1, asc)
    pb_re = jnp.stack([pbf_re, pbb_re], axis=1)
    pb_im = jnp.stack([pbf_im, pbb_im], axis=1)
    kj = (jnp.einsum('ldgep,ldjgph->ldjgeh', cr, pb_re, precision=hp)
          - jnp.einsum('ldgep,ldjgph->ldjgeh', ci, pb_im, precision=hp))
    kj = jnp.stack([jnp.concatenate([kj[:, 0, :1] + kj[:, 1, :1], kj[:, 0, 1:]], axis=1),
                    jnp.concatenate([jnp.zeros_like(kj[:, 1, :1]), kj[:, 1, 1:]], axis=1)], axis=1)
    uk = jnp.transpose(kj, (0, 3, 5, 1, 2, 4)).reshape(depth, n_oct, S5_OCT * h, 2 * t * h)
    pb4 = jnp.stack(lam_b(0, desc0) + (pbb_re, pbb_im), axis=1)
    pb3 = jnp.transpose(pb4.reshape(depth, 4, t, n_oct, S5_OCT, p, h), (0, 3, 2, 4, 6, 1, 5))
    pb3 = pb3.reshape(depth, n_oct, t, S5_OCT * h, 4 * p)

    def c_lam(d_, sl):
        pr, pi = pw_re[:, d_, sl][:, :, :, None, :], pw_im[:, d_, sl][:, :, :, None, :]
        c_r, c_i = cr[:, d_][:, None], ci[:, d_][:, None]
        return c_r * pr - c_i * pi, -(c_r * pi + c_i * pr)

    ca4 = jnp.stack(c_lam(0, slice(1, t + 1)) + c_lam(1, desc1), axis=1)
    ca3 = jnp.transpose(ca4.reshape(depth, 4, t, n_oct, S5_OCT, h, p), (0, 3, 2, 4, 6, 1, 5))
    ca3 = ca3.reshape(depth, n_oct, t, S5_OCT * p, 4 * h)
    nk2 = S5_OCT * p // LANE
    a_re = jnp.transpose(pw_re[:, :, t].reshape(depth, 2, n_oct, nk2, LANE), (0, 2, 1, 3, 4))
    a_im = jnp.transpose(pw_im[:, :, t].reshape(depth, 2, n_oct, nk2, LANE), (0, 2, 1, 3, 4))
    dec = jnp.stack([jnp.concatenate([a_re, a_re], axis=3), jnp.concatenate([-a_im, a_im], axis=3)], axis=3)
    return uk, pb3, ca3, dec


def _s5_expand(uk, pb3, ca3):
    depth, n_oct = uk.shape[:2]
    colsel, tile_e, tile_y = (jnp.asarray(a, BF16) for a in _s5_sel_consts())
    kdim = S5_BLOCK * S5_OCT * S5_GROUP
    sdim = 4 * S5_OCT * S5_STATE
    return pl.pallas_call(
        _s5w_kernel,
        grid=(depth, n_oct),
        in_specs=[
            pl.BlockSpec((None, None) + uk.shape[2:], lambda l, o: (l, o, 0, 0)),
            pl.BlockSpec((None, None) + pb3.shape[2:], lambda l, o: (l, o, 0, 0, 0)),
            pl.BlockSpec((None, None) + ca3.shape[2:], lambda l, o: (l, o, 0, 0, 0)),
            pl.BlockSpec(colsel.shape, lambda l, o: (0, 0, 0)),
            pl.BlockSpec(tile_e.shape, lambda l, o: (0, 0)),
            pl.BlockSpec(tile_y.shape, lambda l, o: (0, 0)),
        ],
        out_specs=[
            pl.BlockSpec((None, None, kdim, kdim), lambda l, o: (l, o, 0, 0)),
            pl.BlockSpec((None, None, kdim, sdim), lambda l, o: (l, o, 0, 0)),
            pl.BlockSpec((None, None, sdim, kdim), lambda l, o: (l, o, 0, 0)),
        ],
        out_shape=[
            jax.ShapeDtypeStruct((depth, n_oct, kdim, kdim), BF16),
            jax.ShapeDtypeStruct((depth, n_oct, kdim, sdim), BF16),
            jax.ShapeDtypeStruct((depth, n_oct, sdim, kdim), BF16),
        ],
        compiler_params=_params(("arbitrary", "arbitrary")),
        name="s5w",
    )(uk, pb3, ca3, colsel, tile_e, tile_y)


def _block_rows(u_ref, tmr):
    return jnp.concatenate(
        [u_ref[pl.ds(r, tmr, stride=S5_BLOCK), :].astype(BF16) for r in range(S5_BLOCK)], axis=1)


def _s5e_kernel(u_ref, we_ref, e_ref, *, tmr):
    res = jnp.dot(_block_rows(u_ref, tmr), we_ref[...], preferred_element_type=F32)
    nk = res.shape[1] // (2 * LANE)
    for d in range(2):
        for k in range(nk):
            c0 = (d * nk + k) * LANE
            e_ref[d, pl.ds(k, tmr, stride=nk), :] = res[:, c0:c0 + LANE]


def _s5e(u, we, *, layer, tmr=1024):
    n, wid = u.shape
    m = n // S5_BLOCK
    _, n_oct, kdim, ncol = we.shape
    nk = ncol // (2 * LANE)
    tmr = _tile(m, tmr)
    return pl.pallas_call(
        functools.partial(_s5e_kernel, tmr=tmr),
        grid=(n_oct, m // tmr),
        in_specs=[
            pl.BlockSpec((tmr * S5_BLOCK, LANE), lambda o, i: (i, o)),
            pl.BlockSpec((None, None, kdim, ncol), lambda o, i: (layer, o, 0, 0)),
        ],
        out_specs=pl.BlockSpec((None, 2, tmr * nk, LANE), lambda o, i: (o, 0, i, 0)),
        out_shape=jax.ShapeDtypeStruct((n_oct, 2, m * nk, LANE), F32),
        compiler_params=_params(("arbitrary", "arbitrary")),
        name="s5e",
    )(u, we)


def _s5scan_kernel(ec_ref, el_ref, a_ref, sc_ref, sl_ref, *, nb, n_c, n_l, nk):
    rev = pl.program_id(1) == 1
    a1 = a_ref[0]
    a2 = a_ref[1]

    def run(e_ref, s_ref, n, carry):
        def body(s, carry):
            i = jnp.where(rev, n - 1 - s, s)
            new = []
            for b in range(nb):
                x = carry[b]
                rows = pl.ds(pl.multiple_of((b * n + i) * nk, nk), nk)
                s_ref[rows, :] = x
                new.append(a1 * x + a2 * pltpu.roll(x, nk // 2, 0) + e_ref[rows, :])
            return tuple(new)
        return lax.fori_loop(0, n, body, carry, unroll=2)

    carry = tuple(jnp.zeros((nk, LANE), F32) for _ in range(nb))
    carry = run(ec_ref, sc_ref, n_c, carry)
    run(el_ref, sl_ref, n_l, carry)


def _s5scan(e_ctx, e_lat, dec, *, layer, nb):
    n_oct, _, rc, _ = e_ctx.shape
    rl = e_lat.shape[2]
    nk = dec.shape[-2]
    return pl.pallas_call(
        functools.partial(_s5scan_kernel, nb=nb, n_c=rc // (nk * nb), n_l=rl // (nk * nb), nk=nk),
        grid=(n_oct, 2),
        in_specs=[
            pl.BlockSpec((None, None, rc, LANE), lambda o, d: (o, d, 0, 0)),
            pl.BlockSpec((None, None, rl, LANE), lambda o, d: (o, d, 0, 0)),
            pl.BlockSpec((None, None, None, 2, nk, LANE), lambda o, d: (layer, o, d, 0, 0, 0)),
        ],
        out_specs=[
            pl.BlockSpec((None, None, rc, LANE), lambda o, d: (o, d, 0, 0)),
            pl.BlockSpec((None, None, rl, LANE), lambda o, d: (o, d, 0, 0)),
        ],
        out_shape=[jax.ShapeDtypeStruct(e_ctx.shape, F32), jax.ShapeDtypeStruct(e_lat.shape, F32)],
        compiler_params=_params(("arbitrary", "arbitrary")),
        name="s5scan",
    )(e_ctx, e_lat, dec)


def _s5y_kernel(u_ref, s_ref, k_ref, wy_ref, y_ref, *, tmr):
    nk = s_ref.shape[1] // tmr
    st = jnp.concatenate(
        [s_ref[d, pl.ds(k, tmr, stride=nk), :].astype(BF16) for d in range(2) for k in range(nk)], axis=1)
    res = jnp.dot(_block_rows(u_ref, tmr), k_ref[...], preferred_element_type=F32)
    res += jnp.dot(st, wy_ref[...], preferred_element_type=F32)
    for r in range(S5_BLOCK):
        y_ref[pl.ds(r, tmr, stride=S5_BLOCK), :] = res[:, r * LANE:(r + 1) * LANE]


def _s5y(u, s, kmat, wy, *, layer, tmr=1024):
    n, wid = u.shape
    m = n // S5_BLOCK
    _, n_oct, kdim, _ = kmat.shape
    sdim = wy.shape[2]
    nk = s.shape[2] // m
    tmr = _tile(m, tmr)
    return pl.pallas_call(
        functools.partial(_s5y_kernel, tmr=tmr),
        grid=(n_oct, m // tmr),
        in_specs=[
            pl.BlockSpec((tmr * S5_BLOCK, LANE), lambda o, i: (i, o)),
            pl.BlockSpec((None, 2, tmr * nk, LANE), lambda o, i: (o, 0, i, 0)),
            pl.BlockSpec((None, None, kdim, kdim), lambda o, i: (layer, o, 0, 0)),
            pl.BlockSpec((None, None, sdim, kdim), lambda o, i: (layer, o, 0, 0)),
        ],
        out_specs=pl.BlockSpec((tmr * S5_BLOCK, LANE), lambda o, i: (i, o)),
        out_shape=jax.ShapeDtypeStruct((n, wid), F32),
        compiler_params=_params(("arbitrary", "arbitrary")),
        name="s5y",
    )(u, s, kmat, wy)


def _hgrn_masks():
    c = CHUNK
    t = np.arange(c)
    masks = []
    half = c // 2
    while half >= SUB:
        par = 2 * half
        second = (t % par) >= half
        same_parent = (t[:, None] // par) == (t[None, :] // par)
        masks.append((same_parent & second[:, None] & (~second)[None, :]).astype(np.float32))
        half //= 2
    masks.append((((t[:, None] // SUB) == (t[None, :] // SUB)) & (t[None, :] <= t[:, None])).astype(np.float32))
    return np.stack(masks)


def _hgrn_decays(g, reverse):
    nb = CHUNK // SUB
    g3 = g.reshape(nb, SUB, g.shape[1])
    r = lax.broadcasted_iota(jnp.int32, g3.shape, 1)
    p = g3
    for k in (1, 2, 4):
        if reverse:
            p = p + jnp.where(r <= SUB - 1 - k, pltpu.roll(p, SUB - k, 1), 0.0)
        else:
            p = p + jnp.where(r >= k, pltpu.roll(p, k, 1), 0.0)
    last = 0 if reverse else SUB - 1
    ref = SUB // 2 if reverse else SUB // 2 - 1
    tot = jnp.broadcast_to(p[:, last:last + 1, :], p.shape)
    ep = jnp.exp(p)
    eq = jnp.exp(tot - p)
    ed = p - jnp.broadcast_to(p[:, ref:ref + 1, :], p.shape)
    edp = jnp.exp(ed)
    edn = jnp.exp(-ed)
    et = jnp.broadcast_to(ep[:, last:last + 1, :], p.shape)
    mem = (lambda i: nb - 1 - i) if reverse else (lambda i: i)
    epb = [ep[mem(i)] for i in range(nb)]
    eqb = [eq[mem(i)] for i in range(nb)]
    etb = [et[mem(i)] for i in range(nb)]

    def assemble(blocks):
        return jnp.concatenate([blocks[mem(i)] for i in range(nb)], axis=0)

    cq = [None] * nb
    acc = None
    for i in range(nb):
        cq[i] = epb[i] if acc is None else epb[i] * acc
        acc = etb[i] if acc is None else acc * etb[i]
    total = acc[0:1, :]
    ck = [None] * nb
    acc = None
    for i in range(nb - 1, -1, -1):
        ck[i] = eqb[i] if acc is None else eqb[i] * acc
        acc = etb[i] if acc is None else acc * etb[i]
    levels = []
    half = nb // 2
    while half >= 1:
        par = 2 * half
        blocks = []
        for i in range(nb):
            j = i % par
            if j >= half:
                f = epb[i]
                for m in range(i - j + half, i):
                    f = f * etb[m]
            else:
                f = eqb[i]
                for m in range(i + 1, i - j + half):
                    f = f * etb[m]
            blocks.append(f)
        levels.append(assemble(blocks))
        half //= 2
    return levels, edp.reshape(g.shape), edn.reshape(g.shape), assemble(cq), assemble(ck), total


def _hgrn_kernel(*refs, nsub, heads, reverse, final, n_lvl):
    if final:
        (q_ref, z_ref, v_ref, lb_ref, masks_ref, s0_ref, of_ref, gate_ref, hnw_ref,
         o_ref, sfin_ref, st_ref) = refs
    else:
        q_ref, z_ref, v_ref, lb_ref, masks_ref, s0_ref, o_ref, sfin_ref, st_ref = refs
    j = pl.program_id(1)
    nj = pl.num_programs(1)
    c = CHUNK
    hd = HEAD_DIM
    nt = (((1,), (1,)), ((), ()))
    tn = (((0,), (0,)), ((), ()))

    @pl.when(j == 0)
    def _():
        st_ref[...] = s0_ref[...]

    mbool = [masks_ref[lv] > 0.0 for lv in range(n_lvl)]

    mpair = [jnp.concatenate([m, m], axis=0) for m in mbool]
    zpad = jnp.zeros((c, hd), BF16)

    def chunk(s, _):
        cl = (nsub - 1 - s) if reverse else s
        rows = pl.ds(pl.multiple_of(cl * c, c), c)
        for hp in range(heads // 2):
            sl = slice(2 * hp * hd, (2 * hp + 2) * hd)
            z = z_ref[rows, sl]
            q = q_ref[rows, sl]
            v = v_ref[rows, sl].astype(BF16)
            lb = lb_ref[:, sl]
            f = lb + (1.0 - lb) * jax.nn.sigmoid(z)
            k = 1.0 - f
            g = jnp.log(jnp.maximum(f, F_MIN))
            levels, edp, edn, cq, ck, total = _hgrn_decays(g, reverse)
            qb = q.astype(BF16)
            kb = k.astype(BF16)
            a = jnp.zeros((2 * c, c), F32)
            for lv in range(n_lvl - 1, -1, -1):
                if lv < n_lvl - 1:
                    e = levels[lv].astype(BF16)
                    ql = qb * e
                    kl = kb * e
                else:
                    ql = qb * edp.astype(BF16)
                    kl = kb * edn.astype(BF16)
                lhs = jnp.concatenate([jnp.concatenate([ql[:, :hd], zpad], axis=1),
                                       jnp.concatenate([zpad, ql[:, hd:]], axis=1)], axis=0)
                sc = lax.dot_general(lhs, kl, nt, preferred_element_type=F32)
                a = jnp.where(mpair[lv], sc, a)
            ab = a.astype(BF16)
            qd = qb * cq.astype(BF16)
            kd = kb * ck.astype(BF16)
            for i in range(2):
                h = 2 * hp + i
                hs = slice(i * hd, (i + 1) * hd)
                st = st_ref[h]
                o_h = jnp.dot(ab[i * c:(i + 1) * c], v[:, hs], preferred_element_type=F32)
                o_h += lax.dot_general(qd[:, hs], st.astype(BF16), nt, preferred_element_type=F32)
                st_ref[h] = st * total[:, hs] + lax.dot_general(v[:, hs], kd[:, hs], tn,
                                                                 preferred_element_type=F32)
                osl = slice(h * hd, (h + 1) * hd)
                if final:
                    o_h = o_h + of_ref[rows, osl]
                    ms = jnp.mean(o_h * o_h, axis=-1, keepdims=True)
                    o_h = o_h * lax.rsqrt(ms + EPS) * hnw_ref[...] * _silu(gate_ref[rows, osl])
                o_ref[rows, osl] = o_h.astype(o_ref.dtype)
        return 0

    lax.fori_loop(0, nsub, chunk, 0, unroll=8)

    @pl.when(j == nj - 1)
    def _():
        sfin_ref[...] = st_ref[...]


def _hgrn(p, lb, masks, s0, *, direction, width, o_fwd=None, hnw=None, nsub=8):
    b, t, _ = p.shape
    heads = width // HEAD_DIM
    nsub = _tile(t // CHUNK, nsub, 1)
    rows = nsub * CHUNK
    nj = t // rows
    reverse = direction == 1
    final = o_fwd is not None
    n_lvl = masks.shape[0]
    blk = (lambda j: nj - 1 - j) if reverse else (lambda j: j)
    zcol = 2 if reverse else 1
    in_specs = [
        pl.BlockSpec((None, rows, width), lambda bi, j: (bi, blk(j), 0)),
        pl.BlockSpec((None, rows, width), lambda bi, j: (bi, blk(j), zcol)),
        pl.BlockSpec((None, rows, width), lambda bi, j: (bi, blk(j), 3)),
        pl.BlockSpec((1, width), lambda bi, j: (0, 0)),
        pl.BlockSpec(masks.shape, lambda bi, j: (0, 0, 0)),
        pl.BlockSpec((None, heads, HEAD_DIM, HEAD_DIM), lambda bi, j: (bi, 0, 0, 0)),
    ]
    args = [p, p, p, lb.reshape(1, width), masks, s0]
    if final:
        in_specs += [
            pl.BlockSpec((None, rows, width), lambda bi, j: (bi, blk(j), 0)),
            pl.BlockSpec((None, rows, width), lambda bi, j: (bi, blk(j), 4)),
            pl.BlockSpec((1, HEAD_DIM), lambda bi, j: (0, 0)),
        ]
        args += [o_fwd, p, hnw.reshape(1, HEAD_DIM)]
    return pl.pallas_call(
        functools.partial(_hgrn_kernel, nsub=nsub, heads=heads, reverse=reverse, final=final, n_lvl=n_lvl),
        grid=(b, nj),
        in_specs=in_specs,
        out_specs=[
            pl.BlockSpec((None, rows, width), lambda bi, j: (bi, blk(j), 0)),
            pl.BlockSpec((None, heads, HEAD_DIM, HEAD_DIM), lambda bi, j: (bi, 0, 0, 0)),
        ],
        out_shape=[
            jax.ShapeDtypeStruct((b, t, width), F32),
            jax.ShapeDtypeStruct((b, heads, HEAD_DIM, HEAD_DIM), F32),
        ],
        scratch_shapes=[pltpu.VMEM((heads, HEAD_DIM, HEAD_DIM), F32)],
        compiler_params=_params(("arbitrary", "arbitrary")),
        name="hgrn_bwd" if final else "hgrn_fwd",
    )(*args)


def _mixout_kernel(x_ref, y_ref, u_ref, hg_ref, m_ref, dsk_ref, wglu_ref, bglu_ref, wo_ref, o_ref,
                   *, s5w, rb, colmajor):
    yy = y_ref[...] + dsk_ref[...] * u_ref[...]
    g = _gelu_tanh(yy)
    zz = jnp.dot(g.astype(BF16), wglu_ref[...], preferred_element_type=F32) + bglu_ref[...]
    s5 = (g * jax.nn.sigmoid(zz)).astype(BF16)
    if colmajor:
        hg = jnp.concatenate([hg_ref[:, r, :] for r in range(rb)], axis=0).astype(BF16)
    else:
        hg = hg_ref[...].astype(BF16)
    acc = jnp.dot(s5, wo_ref[:s5w, :], preferred_element_type=F32)
    acc += jnp.dot(hg, wo_ref[s5w:, :], preferred_element_type=F32)
    o_ref[...] = x_ref[...] + m_ref[0, 5:6, :] * acc


def _mixout(x3d, y3d, u3d, hg, mods, d_skip, w_glu, b_glu, w_o, *, layer, mod_of_batch, colmajor, rb=8):
    b, t, d = x3d.shape
    s5w = y3d.shape[-1]
    hw = hg.shape[-1]
    if colmajor:
        cols = GRID_W
        rows = t // cols
        rb = _tile(rows, rb, 8)
        tm = rb * cols
        hg_v = hg.reshape(b, cols, rows, hw)
        hg_spec = pl.BlockSpec((None, cols, rb, hw), lambda bi, i: (bi, 0, i, 0))
    else:
        rb = 1
        tm = _tile(t, 512)
        hg_v = hg
        hg_spec = pl.BlockSpec((None, tm, hw), lambda bi, i: (bi, i, 0))
    mod_fn = (lambda bi: bi) if mod_of_batch else (lambda bi: b)
    return pl.pallas_call(
        functools.partial(_mixout_kernel, s5w=s5w, rb=rb, colmajor=colmajor),
        grid=(b, t // tm),
        in_specs=[
            pl.BlockSpec((None, tm, d), lambda bi, i: (bi, i, 0)),
            pl.BlockSpec((None, tm, s5w), lambda bi, i: (bi, i, 0)),
            pl.BlockSpec((None, tm, s5w), lambda bi, i: (bi, i, 0)),
            hg_spec,
            pl.BlockSpec((1, N_MOD, d), lambda bi, i: (mod_fn(bi), 0, 0)),
            pl.BlockSpec((1, s5w), lambda bi, i: (0, 0)),
            pl.BlockSpec((None, s5w, s5w), lambda bi, i: (layer, 0, 0)),
            pl.BlockSpec((1, s5w), lambda bi, i: (0, 0)),
            pl.BlockSpec((None, s5w + hw, d), lambda bi, i: (layer, 0, 0)),
        ],
        out_specs=pl.BlockSpec((None, tm, d), lambda bi, i: (bi, i, 0)),
        out_shape=jax.ShapeDtypeStruct((b, t, d), F32),
        compiler_params=_params(("arbitrary", "arbitrary")),
        name="mixout",
    )(x3d, y3d, u3d, hg_v, mods, d_skip.reshape(1, s5w), w_glu, b_glu.reshape(1, s5w), w_o)


def kernel(x, c, ctx, c_ctx, w_ada, b_ada, norm_w, ffn_w_gate, ffn_w_up, ffn_w_down, w_in, w_out,
           s5_lambda_re, s5_lambda_im, s5_log_step, s5_b_re, s5_b_im, s5_c_re, s5_c_im, s5_d,
           s5_w_glu, s5_b_glu, hgrn_lower_bounds, hgrn_norm_w, final_norm_w):
    batch, seq, d = x.shape
    n_ctx = ctx.shape[1]
    depth = w_ada.shape[0]
    s5w = s5_d.shape[-1]
    hw = hgrn_lower_bounds.shape[-1]
    rows = seq // GRID_W
    assert batch < MOD_ROWS and rows % CHUNK == 0 and n_ctx % CHUNK == 0
    assert seq % S5_BLOCK == 0 and n_ctx % S5_BLOCK == 0 and (s5w // S5_GROUP) % S5_OCT == 0

    cvec = jnp.zeros((MOD_ROWS, d), F32).at[:batch].set(c.astype(F32)).at[batch].set(c_ctx.astype(F32))
    mods_all = _ada(cvec, w_ada, b_ada).reshape(depth, MOD_ROWS, N_MOD, d)

    lb_soft = jax.nn.softmax(hgrn_lower_bounds.astype(F32), axis=0)
    lb_all = jnp.cumsum(lb_soft, axis=0) - lb_soft[0]

    masks_np = _hgrn_masks()
    masks_f = jnp.asarray(masks_np, F32)
    masks_b = jnp.asarray(masks_np[:, ::-1, ::-1].copy(), F32)
    heads = hw // HEAD_DIM
    s_zero = jnp.zeros((batch, heads, HEAD_DIM, HEAD_DIM), F32)

    wg = ffn_w_gate.astype(BF16)
    wu = ffn_w_up.astype(BF16)
    wd = ffn_w_down.astype(BF16)
    uk, pb3, ca3, dec = _s5_compact(s5_lambda_re, s5_lambda_im, s5_log_step, s5_b_re, s5_b_im, s5_c_re, s5_c_im)
    kmat, we, wy = _s5_expand(uk, pb3, ca3)
    w_in_b = w_in.astype(BF16)
    w_h = w_in_b[:, :, s5w:]
    w_o = w_out.astype(BF16)
    w_glu = s5_w_glu.astype(BF16)

    xl = x.astype(F32)
    xc = ctx.astype(F32)
    lat = dict(rows_per_mod=seq, mod_off=0)
    cx = dict(rows_per_mod=batch * n_ctx, mod_off=batch)
    for l in range(depth):
        last = l == depth - 1
        mods = mods_all[l]

        xl = _ffn(xl.reshape(batch * seq, d), mods, norm_w[l, 0], wg, wu, wd, layer=l, half=0, base=0,
                  **lat).reshape(batch, seq, d)
        xc = _ffn(xc.reshape(batch * n_ctx, d), mods, norm_w[l, 0], wg, wu, wd, layer=l, half=0, base=0,
                  **cx).reshape(batch, n_ctx, d)

        u_l = _uproj(xl.reshape(batch * seq, d), mods, norm_w[l, 1], w_in_b, layer=l, s5w=s5w, **lat)
        u_c = _uproj(xc.reshape(batch * n_ctx, d), mods, norm_w[l, 1], w_in_b, layer=l, s5w=s5w, **cx)
        p_lat = _hproj(xl, mods, norm_w[l, 1], w_h, layer=l, hw=hw, cols=GRID_W, mod_row=None)
        p_ctx = _hproj(xc.reshape(1, batch * n_ctx, d), mods, norm_w[l, 1], w_h, layer=l, hw=hw, cols=1,
                       mod_row=batch).reshape(batch, n_ctx, 5 * hw)

        e_l = _s5e(u_l, we, layer=l)
        e_c = _s5e(u_c, we, layer=l)
        st_c, st_l = _s5scan(e_c, e_l, dec, layer=l, nb=batch)
        y_l = _s5y(u_l, st_l, kmat, wy, layer=l).reshape(batch, seq, s5w)

        lb_f = lb_all[l, 0]
        lb_b = lb_all[l, 1]
        oc_f, sc_f = _hgrn(p_ctx, lb_f, masks_f, s_zero, direction=0, width=hw)
        ol_f, _ = _hgrn(p_lat, lb_f, masks_f, sc_f, direction=0, width=hw)
        hg_c, sc_b = _hgrn(p_ctx, lb_b, masks_b, s_zero, direction=1, width=hw,
                           o_fwd=oc_f, hnw=hgrn_norm_w[l])
        hg_l, _ = _hgrn(p_lat, lb_b, masks_b, sc_b, direction=1, width=hw,
                        o_fwd=ol_f, hnw=hgrn_norm_w[l])

        xl = _mixout(xl, y_l, u_l.reshape(batch, seq, s5w), hg_l, mods, s5_d[l], w_glu, s5_b_glu[l], w_o,
                     layer=l, mod_of_batch=True, colmajor=True)
        if not last:
            y_c = _s5y(u_c, st_c, kmat, wy, layer=l).reshape(batch, n_ctx, s5w)
            xc = _mixout(xc, y_c, u_c.reshape(batch, n_ctx, s5w), hg_c, mods, s5_d[l], w_glu, s5_b_glu[l],
                         w_o, layer=l, mod_of_batch=False, colmajor=False)

        xl = _ffn(xl.reshape(batch * seq, d), mods, norm_w[l, 2], wg, wu, wd, layer=l, half=1, base=6,
                  final_nw=final_norm_w if last else None, **lat).reshape(batch, seq, d)
        if not last:
            xc = _ffn(xc.reshape(batch * n_ctx, d), mods, norm_w[l, 2], wg, wu, wd, layer=l, half=1, base=6,
                      **cx).reshape(batch, n_ctx, d)
    return xl
```

```python
import functools

import numpy as np
import jax
import jax.numpy as jnp
from jax import lax
from jax.experimental import pallas as pl
from jax.experimental.pallas import tpu as pltpu

F32 = jnp.float32
BF16 = jnp.bfloat16

EPS = 1e-6
F_MIN = 1e-6
LAMBDA_RE_MAX = -1e-4
GRID_W = 64
N_MOD = 9
S5_GROUP = 16
S5_STATE = 64
S5_BLOCK = 8
S5_OCT = 8
HEAD_DIM = 128
CHUNK = 64
SUB = 8
LANE = 128
MOD_ROWS = 8
VMEM_LIMIT = 60 * 1024 * 1024


def _params(sem):
    return pltpu.CompilerParams(dimension_semantics=sem, vmem_limit_bytes=VMEM_LIMIT)


def _tile(n, pref, mult=8):
    t = min(n, pref)
    while t > 0:
        if n % t == 0 and t % mult == 0:
            return t
        t -= 1
    return n


def _norm_mod(x, nw, shift, scale):
    ms = jnp.mean(x * x, axis=-1, keepdims=True)
    gain = nw * (1.0 + scale)
    return (x * lax.rsqrt(ms + EPS)) * gain + shift


def _silu(x):
    return x * jax.nn.sigmoid(x)


def _gelu_tanh(x):
    return 0.5 * x * (1.0 + jnp.tanh(0.7978845608028654 * (x + 0.044715 * (x * x * x))))


def _ada_kernel(c_ref, w_ref, b_ref, o_ref):
    a = _silu(c_ref[...])
    a_hi = a.astype(BF16)
    a_lo = (a - a_hi.astype(F32)).astype(BF16)
    w = w_ref[...]
    w_hi = w.astype(BF16)
    w_lo = (w - w_hi.astype(F32)).astype(BF16)
    both = jnp.dot(jnp.concatenate([a_hi, a_lo], axis=0), w_hi, preferred_element_type=F32)
    acc = both[:MOD_ROWS] + both[MOD_ROWS:] + jnp.dot(a_hi, w_lo, preferred_element_type=F32)
    o_ref[...] = acc + b_ref[...]


def _ada(cvec, w_ada, b_ada):
    depth, d, n = w_ada.shape
    tn = _tile(n, 2048, LANE)
    return pl.pallas_call(
        _ada_kernel,
        grid=(depth, n // tn),
        in_specs=[
            pl.BlockSpec((MOD_ROWS, d), lambda l, j: (0, 0)),
            pl.BlockSpec((None, d, tn), lambda l, j: (l, 0, j)),
            pl.BlockSpec((None, 1, tn), lambda l, j: (l, 0, j)),
        ],
        out_specs=pl.BlockSpec((None, MOD_ROWS, tn), lambda l, j: (l, 0, j)),
        out_shape=jax.ShapeDtypeStruct((depth, MOD_ROWS, n), F32),
        compiler_params=_params(("arbitrary", "arbitrary")),
        name="ada",
    )(cvec, w_ada, b_ada.reshape(depth, 1, n))


def _ffn_kernel(*refs, base, n_f, final, tail):
    if final:
        x_ref, m_ref, nw_ref, wg_ref, wu_ref, wd_ref, fnw_ref, o_ref, h_ref = refs
    else:
        x_ref, m_ref, nw_ref, wg_ref, wu_ref, wd_ref, o_ref, h_ref = refs
    j = pl.program_id(1)

    @pl.when(j == 0)
    def _():
        h = _norm_mod(x_ref[...], nw_ref[...], m_ref[0, base:base + 1, :], m_ref[0, base + 1:base + 2, :])
        h_ref[...] = h.astype(BF16)

    def partial_sum(valid=None):
        h = h_ref[...]
        g = jnp.dot(h, wg_ref[...], preferred_element_type=F32)
        u = jnp.dot(h, wu_ref[...], preferred_element_type=F32)
        a = (_silu(g) * u).astype(BF16)
        wd = wd_ref[...]
        if valid is not None:
            a = jnp.where(lax.broadcasted_iota(jnp.int32, a.shape, 1) < valid, a, jnp.zeros_like(a))
            wd = jnp.where(lax.broadcasted_iota(jnp.int32, wd.shape, 0) < valid, wd, jnp.zeros_like(wd))
        return jnp.dot(a, wd, preferred_element_type=F32)

    @pl.when(j == 0)
    def _():
        o_ref[...] = partial_sum()

    @pl.when((j > 0) & (j < n_f - 1))
    def _():
        o_ref[...] += partial_sum()

    @pl.when(j == n_f - 1)
    def _():
        y = x_ref[...] + (0.5 * m_ref[0, base + 2:base + 3, :]) * (o_ref[...] + partial_sum(tail))
        if final:
            ms = jnp.mean(y * y, axis=-1, keepdims=True)
            y = y * lax.rsqrt(ms + EPS) * fnw_ref[...]
        o_ref[...] = y


def _ffn(x2d, mods, nw, wg, wu, wd, *, layer, half, base, rows_per_mod, mod_off, final_nw=None,
         tm=1024, tf=512):
    n, d = x2d.shape
    fp = wg.shape[-1]
    tm = _tile(min(n, rows_per_mod), tm)
    n_f = pl.cdiv(fp, tf)
    tail = fp - (n_f - 1) * tf if fp % tf else None
    assert n_f >= 2
    final = final_nw is not None
    in_specs = [
        pl.BlockSpec((tm, d), lambda i, j: (i, 0)),
        pl.BlockSpec((1, N_MOD, d), lambda i, j: ((i * tm) // rows_per_mod + mod_off, 0, 0)),
        pl.BlockSpec((1, d), lambda i, j: (0, 0)),
        pl.BlockSpec((None, None, d, tf), lambda i, j: (layer, half, 0, j)),
        pl.BlockSpec((None, None, d, tf), lambda i, j: (layer, half, 0, j)),
        pl.BlockSpec((None, None, tf, d), lambda i, j: (layer, half, j, 0)),
    ]
    args = [x2d, mods, nw.reshape(1, d), wg, wu, wd]
    if final:
        in_specs.append(pl.BlockSpec((1, d), lambda i, j: (0, 0)))
        args.append(final_nw.reshape(1, d))
    return pl.pallas_call(
        functools.partial(_ffn_kernel, base=base, n_f=n_f, final=final, tail=tail),
        grid=(n // tm, n_f),
        in_specs=in_specs,
        out_specs=pl.BlockSpec((tm, d), lambda i, j: (i, 0)),
        out_shape=jax.ShapeDtypeStruct((n, d), F32),
        scratch_shapes=[pltpu.VMEM((tm, d), BF16)],
        compiler_params=_params(("arbitrary", "arbitrary")),
        name="ffn_final" if final else "ffn",
    )(*args)


def _uproj_kernel(x_ref, m_ref, nw_ref, w_ref, u_ref):
    h = _norm_mod(x_ref[...], nw_ref[...], m_ref[0, 3:4, :], m_ref[0, 4:5, :])
    u_ref[...] = jnp.dot(h.astype(BF16), w_ref[...], preferred_element_type=F32)


def _uproj(x2d, mods, nw, w_in, *, layer, s5w, rows_per_mod, mod_off, tm=1024):
    n, d = x2d.shape
    tm = _tile(min(n, rows_per_mod), tm)
    return pl.pallas_call(
        _uproj_kernel,
        grid=(n // tm,),
        in_specs=[
            pl.BlockSpec((tm, d), lambda i: (i, 0)),
            pl.BlockSpec((1, N_MOD, d), lambda i: ((i * tm) // rows_per_mod + mod_off, 0, 0)),
            pl.BlockSpec((1, d), lambda i: (0, 0)),
            pl.BlockSpec((None, d, s5w), lambda i: (layer, 0, 0)),
        ],
        out_specs=pl.BlockSpec((tm, s5w), lambda i: (i, 0)),
        out_shape=jax.ShapeDtypeStruct((n, s5w), F32),
        compiler_params=_params(("arbitrary",)),
        name="uproj",
    )(x2d, mods, nw.reshape(1, d), w_in)


def _hproj_kernel(*refs, colmajor, hw):
    if colmajor:
        x_ref, m_ref, nw_ref, w_ref, perm_ref, o_ref = refs
    else:
        x_ref, m_ref, nw_ref, w_ref, o_ref = refs
    h = _norm_mod(x_ref[...], nw_ref[...], m_ref[0, 3:4, :], m_ref[0, 4:5, :]).astype(BF16)
    if colmajor:
        h = h.reshape(perm_ref.shape[0], h.shape[-1])
        h = jnp.dot(perm_ref[...], h, preferred_element_type=F32).astype(BF16)
    for nb in range(o_ref.shape[-1] // hw):
        p = jnp.dot(h, w_ref[:, nb * hw:(nb + 1) * hw], preferred_element_type=F32)
        o_ref[:, nb * hw:(nb + 1) * hw] = _silu(p) if nb == 0 else p


def _hproj(x3d, mods, nw, w_h, *, layer, hw, cols, mod_row, cb=8, tr=512):
    b, t, d = x3d.shape
    nh = w_h.shape[-1]
    colmajor = cols > 1
    mod_fn = (lambda bi: bi) if mod_row is None else (lambda bi: mod_row)
    in_specs = [
        None,
        pl.BlockSpec((1, N_MOD, d), lambda bi, i: (mod_fn(bi), 0, 0)),
        pl.BlockSpec((1, d), lambda bi, i: (0, 0)),
        pl.BlockSpec((None, d, nh), lambda bi, i: (layer, 0, 0), pipeline_mode=pl.Buffered(1)),
    ]
    args = [None, mods, nw.reshape(1, d), w_h]
    if colmajor:
        rows = t // cols
        cb = _tile(cols, cb, 8)
        tr = rows * cb
        args[0] = x3d.reshape(b, rows, cols, d)
        in_specs[0] = pl.BlockSpec((None, rows, cb, d), lambda bi, i: (bi, 0, i, 0))
        src = np.arange(tr).reshape(rows, cb).T.reshape(-1)
        perm = np.zeros((tr, tr), np.float32)
        perm[np.arange(tr), src] = 1.0
        args.append(jnp.asarray(perm, BF16))
        in_specs.append(pl.BlockSpec(perm.shape, lambda bi, i: (0, 0)))
    else:
        tr = _tile(t, tr)
        args[0] = x3d
        in_specs[0] = pl.BlockSpec((None, tr, d), lambda bi, i: (bi, i, 0))
    return pl.pallas_call(
        functools.partial(_hproj_kernel, colmajor=colmajor, hw=hw),
        grid=(b, t // tr),
        in_specs=in_specs,
        out_specs=pl.BlockSpec((None, tr, nh), lambda bi, i: (bi, i, 0)),
        out_shape=jax.ShapeDtypeStruct((b, t, nh), F32),
        compiler_params=_params(("arbitrary", "arbitrary")),
        name="hproj",
    )(*args)


def _s5_sel_consts():
    t, h, p, o8 = S5_BLOCK, S5_GROUP, S5_STATE, S5_OCT
    colsel = np.zeros((t, 2, t, h, t, o8, h), np.float32)
    for r in range(t):
        for r2 in range(t):
            for hh in range(h):
                if r2 >= r:
                    colsel[r, 0, r2 - r, hh, r2, :, hh] = 1.0
                if r >= r2:
                    colsel[r, 1, r - r2, hh, r2, :, hh] = 1.0
    colsel = colsel.reshape(t, 2 * t * h, t * o8 * h)
    tile_e = np.zeros((4, p, 4, o8, p), np.float32)
    tile_y = np.zeros((4, h, 4, o8, h), np.float32)
    for dp in range(4):
        for i in range(p):
            tile_e[dp, i, dp, :, i] = 1.0
        for i in range(h):
            tile_y[dp, i, dp, :, i] = 1.0
    return colsel, tile_e.reshape(4 * p, 4 * o8 * p), tile_y.reshape(4 * h, 4 * o8 * h)


def _s5w_kernel(uk_ref, pb_ref, ca_ref, colsel_ref, tile_e_ref, tile_y_ref, k_ref, we_ref, wy_ref):
    t, h, p, o8 = S5_BLOCK, S5_GROUP, S5_STATE, S5_OCT
    gh = o8 * h
    gp = o8 * p

    def diag_mask(shape, row_div, col_mod, col_div):
        rg = lax.broadcasted_iota(jnp.int32, shape, 0) // row_div
        cg = (lax.broadcasted_iota(jnp.int32, shape, 1) % col_mod) // col_div
        return rg == cg

    mk = diag_mask((gh, t * gh), h, gh, h)
    me = diag_mask((gh, 4 * gp), h, gp, p)
    my = diag_mask((gp, 4 * gh), p, gh, h)
    uk = uk_ref[...].astype(BF16)
    for r in range(t):
        blk = jnp.dot(uk, colsel_ref[r], preferred_element_type=F32)
        k_ref[r * gh:(r + 1) * gh, :] = jnp.where(mk, blk, 0.0).astype(BF16)
        blk = jnp.dot(pb_ref[r].astype(BF16), tile_e_ref[...], preferred_element_type=F32)
        we_ref[r * gh:(r + 1) * gh, :] = jnp.where(me, blk, 0.0).astype(BF16)
        blk = jnp.where(my, jnp.dot(ca_ref[r].astype(BF16), tile_y_ref[...], preferred_element_type=F32), 0.0)
        for dp in range(4):
            wy_ref[dp * gp:(dp + 1) * gp, r * gh:(r + 1) * gh] = blk[:, dp * gh:(dp + 1) * gh].astype(BF16)


def _s5_compact(lam_re, lam_im, log_step, b_re, b_im, c_re, c_im):
    hp = lax.Precision.HIGHEST
    t = S5_BLOCK
    depth, _, g, p = lam_re.shape
    h = b_re.shape[-1]
    n_oct = g // S5_OCT
    lam_re = jnp.minimum(lam_re.astype(F32), LAMBDA_RE_MAX)
    lam_im = lam_im.astype(F32)
    dt = jnp.exp(log_step.astype(F32))[..., None]
    mag = jnp.exp(lam_re * dt)
    lb_re = mag * jnp.cos(lam_im * dt)
    lb_im = mag * jnp.sin(lam_im * dt)
    den = lam_re * lam_re + lam_im * lam_im
    nr = lb_re - 1.0
    ni = lb_im
    cf_re = (nr * lam_re + ni * lam_im) / den
    cf_im = (ni * lam_re - nr * lam_im) / den
    b_re = b_re.astype(F32)
    b_im = b_im.astype(F32)
    br = cf_re[..., None] * b_re - cf_im[..., None] * b_im
    bi = cf_re[..., None] * b_im + cf_im[..., None] * b_re
    cr = c_re.astype(F32)
    ci = c_im.astype(F32)
    expo = np.concatenate([np.arange(t + 1), np.arange(t - 1, -1, -1), np.arange(t, 0, -1)]).astype(np.float32)
    j = jnp.asarray(expo)[None, None, :, None, None]
    pmag = jnp.exp(j * (lam_re * dt)[:, :, None])
    pw_re = pmag * jnp.cos(j * (lam_im * dt)[:, :, None])
    pw_im = pmag * jnp.sin(j * (lam_im * dt)[:, :, None])
    asc, desc0, desc1 = slice(0, t), slice(t + 1, 2 * t + 1), slice(2 * t + 1, 3 * t + 1)

    def lam_b(d_, sl):
        pr, pi = pw_re[:, d_, sl][..., None], pw_im[:, d_, sl][..., None]
        b_r, b_i = br[:, d_][:, None], bi[:, d_][:, None]
        return pr * b_r - pi * b_i, pr * b_i + pi * b_r

    pbf_re, pbf_im = lam_b(0, asc)
    pbb_re, pbb_im = lam_b(1, asc)
    pb_re = jnp.stack([pbf_re, pbb_re], axis=1)
    pb_im = jnp.stack([pbf_im, pbb_im], axis=1)
    kj = (jnp.einsum('ldgep,ldjgph->ldjgeh', cr, pb_re, precision=hp)
          - jnp.einsum('ldgep,ldjgph->ldjgeh', ci, pb_im, precision=hp))
    kj = jnp.stack([jnp.concatenate([kj[:, 0, :1] + kj[:, 1, :1], kj[:, 0, 1:]], axis=1),
                    jnp.concatenate([jnp.zeros_like(kj[:, 1, :1]), kj[:, 1, 1:]], axis=1)], axis=1)
    uk = jnp.transpose(kj, (0, 3, 5, 1, 2, 4)).reshape(depth, n_oct, S5_OCT * h, 2 * t * h)
    pb4 = jnp.stack(lam_b(0, desc0) + (pbb_re, pbb_im), axis=1)
    pb3 = jnp.transpose(pb4.reshape(depth, 4, t, n_oct, S5_OCT, p, h), (0, 3, 2, 4, 6, 1, 5))
    pb3 = pb3.reshape(depth, n_oct, t, S5_OCT * h, 4 * p)

    def c_lam(d_, sl):
        pr, pi = pw_re[:, d_, sl][:, :, :, None, :], pw_im[:, d_, sl][:, :, :, None, :]
        c_r, c_i = cr[:, d_][:, None], ci[:, d_][:, None]
        return c_r * pr - c_i * pi, -(c_r * pi + c_i * pr)

    ca4 = jnp.stack(c_lam(0, slice(1, t + 1)) + c_lam(1, desc1), axis=1)
    ca3 = jnp.transpose(ca4.reshape(depth, 4, t, n_oct, S5_OCT, h, p), (0, 3, 2, 4, 6, 1, 5))
    ca3 = ca3.reshape(depth, n_oct, t, S5_OCT * p, 4 * h)
    nk2 = S5_OCT * p // LANE
    a_re = jnp.transpose(pw_re[:, :, t].reshape(depth, 2, n_oct, nk2, LANE), (0, 2, 1, 3, 4))
    a_im = jnp.transpose(pw_im[:, :, t].reshape(depth, 2, n_oct, nk2, LANE), (0, 2, 1, 3, 4))
    dec = jnp.stack([jnp.concatenate([a_re, a_re], axis=3), jnp.concatenate([-a_im, a_im], axis=3)], axis=3)
    return uk, pb3, ca3, dec


def _s5_expand(uk, pb3, ca3):
    depth, n_oct = uk.shape[:2]
    colsel, tile_e, tile_y = (jnp.asarray(a, BF16) for a in _s5_sel_consts())
    kdim = S5_BLOCK * S5_OCT * S5_GROUP
    sdim = 4 * S5_OCT * S5_STATE
    return pl.pallas_call(
        _s5w_kernel,
        grid=(depth, n_oct),
        in_specs=[
            pl.BlockSpec((None, None) + uk.shape[2:], lambda l, o: (l, o, 0, 0)),
            pl.BlockSpec((None, None) + pb3.shape[2:], lambda l, o: (l, o, 0, 0, 0)),
            pl.BlockSpec((None, None) + ca3.shape[2:], lambda l, o: (l, o, 0, 0, 0)),
            pl.BlockSpec(colsel.shape, lambda l, o: (0, 0, 0)),
            pl.BlockSpec(tile_e.shape, lambda l, o: (0, 0)),
            pl.BlockSpec(tile_y.shape, lambda l, o: (0, 0)),
        ],
        out_specs=[
            pl.BlockSpec((None, None, kdim, kdim), lambda l, o: (l, o, 0, 0)),
            pl.BlockSpec((None, None, kdim, sdim), lambda l, o: (l, o, 0, 0)),
            pl.BlockSpec((None, None, sdim, kdim), lambda l, o: (l, o, 0, 0)),
        ],
        out_shape=[
            jax.ShapeDtypeStruct((depth, n_oct, kdim, kdim), BF16),
            jax.ShapeDtypeStruct((depth, n_oct, kdim, sdim), BF16),
            jax.ShapeDtypeStruct((depth, n_oct, sdim, kdim), BF16),
        ],
        compiler_params=_params(("arbitrary", "arbitrary")),
        name="s5w",
    )(uk, pb3, ca3, colsel, tile_e, tile_y)


def _block_rows(u_ref, tmr):
    return jnp.concatenate(
        [u_ref[pl.ds(r, tmr, stride=S5_BLOCK), :].astype(BF16) for r in range(S5_BLOCK)], axis=1)


def _s5e_kernel(u_ref, we_ref, e_ref, *, tmr):
    res = jnp.dot(_block_rows(u_ref, tmr), we_ref[...], preferred_element_type=F32)
    nk = res.shape[1] // (2 * LANE)
    for d in range(2):
        for k in range(nk):
            c0 = (d * nk + k) * LANE
            e_ref[d, pl.ds(k, tmr, stride=nk), :] = res[:, c0:c0 + LANE]


def _s5e(u, we, *, layer, tmr=1024):
    n, wid = u.shape
    m = n // S5_BLOCK
    _, n_oct, kdim, ncol = we.shape
    nk = ncol // (2 * LANE)
    tmr = _tile(m, tmr)
    return pl.pallas_call(
        functools.partial(_s5e_kernel, tmr=tmr),
        grid=(n_oct, m // tmr),
        in_specs=[
            pl.BlockSpec((tmr * S5_BLOCK, LANE), lambda o, i: (i, o)),
            pl.BlockSpec((None, None, kdim, ncol), lambda o, i: (layer, o, 0, 0)),
        ],
        out_specs=pl.BlockSpec((None, 2, tmr * nk, LANE), lambda o, i: (o, 0, i, 0)),
        out_shape=jax.ShapeDtypeStruct((n_oct, 2, m * nk, LANE), F32),
        compiler_params=_params(("arbitrary", "arbitrary")),
        name="s5e",
    )(u, we)


def _s5scan_kernel(ec_ref, el_ref, a_ref, sc_ref, sl_ref, *, nb, n_c, n_l, nk):
    rev = pl.program_id(1) == 1
    a1 = a_ref[0]
    a2 = a_ref[1]

    def run(e_ref, s_ref, n, carry):
        def body(s, carry):
            i = jnp.where(rev, n - 1 - s, s)
            new = []
            for b in range(nb):
                x = carry[b]
                rows = pl.ds(pl.multiple_of((b * n + i) * nk, nk), nk)
                s_ref[rows, :] = x
                new.append(a1 * x + a2 * pltpu.roll(x, nk // 2, 0) + e_ref[rows, :])
            return tuple(new)
        return lax.fori_loop(0, n, body, carry, unroll=2)

    carry = tuple(jnp.zeros((nk, LANE), F32) for _ in range(nb))
    carry = run(ec_ref, sc_ref, n_c, carry)
    run(el_ref, sl_ref, n_l, carry)


def _s5scan(e_ctx, e_lat, dec, *, layer, nb):
    n_oct, _, rc, _ = e_ctx.shape
    rl = e_lat.shape[2]
    nk = dec.shape[-2]
    return pl.pallas_call(
        functools.partial(_s5scan_kernel, nb=nb, n_c=rc // (nk * nb), n_l=rl // (nk * nb), nk=nk),
        grid=(n_oct, 2),
        in_specs=[
            pl.BlockSpec((None, None, rc, LANE), lambda o, d: (o, d, 0, 0)),
            pl.BlockSpec((None, None, rl, LANE), lambda o, d: (o, d, 0, 0)),
            pl.BlockSpec((None, None, None, 2, nk, LANE), lambda o, d: (layer, o, d, 0, 0, 0)),
        ],
        out_specs=[
            pl.BlockSpec((None, None, rc, LANE), lambda o, d: (o, d, 0, 0)),
            pl.BlockSpec((None, None, rl, LANE), lambda o, d: (o, d, 0, 0)),
        ],
        out_shape=[jax.ShapeDtypeStruct(e_ctx.shape, F32), jax.ShapeDtypeStruct(e_lat.shape, F32)],
        compiler_params=_params(("arbitrary", "arbitrary")),
        name="s5scan",
    )(e_ctx, e_lat, dec)


def _s5y_kernel(u_ref, s_ref, k_ref, wy_ref, y_ref, *, tmr):
    nk = s_ref.shape[1] // tmr
    st = jnp.concatenate(
        [s_ref[d, pl.ds(k, tmr, stride=nk), :].astype(BF16) for d in range(2) for k in range(nk)], axis=1)
    res = jnp.dot(_block_rows(u_ref, tmr), k_ref[...], preferred_element_type=F32)
    res += jnp.dot(st, wy_ref[...], preferred_element_type=F32)
    for r in range(S5_BLOCK):
        y_ref[pl.ds(r, tmr, stride=S5_BLOCK), :] = res[:, r * LANE:(r + 1) * LANE]


def _s5y(u, s, kmat, wy, *, layer, tmr=1024):
    n, wid = u.shape
    m = n // S5_BLOCK
    _, n_oct, kdim, _ = kmat.shape
    sdim = wy.shape[2]
    nk = s.shape[2] // m
    tmr = _tile(m, tmr)
    return pl.pallas_call(
        functools.partial(_s5y_kernel, tmr=tmr),
        grid=(n_oct, m // tmr),
        in_specs=[
            pl.BlockSpec((tmr * S5_BLOCK, LANE), lambda o, i: (i, o)),
            pl.BlockSpec((None, 2, tmr * nk, LANE), lambda o, i: (o, 0, i, 0)),
            pl.BlockSpec((None, None, kdim, kdim), lambda o, i: (layer, o, 0, 0)),
            pl.BlockSpec((None, None, sdim, kdim), lambda o, i: (layer, o, 0, 0)),
        ],
        out_specs=pl.BlockSpec((tmr * S5_BLOCK, LANE), lambda o, i: (i, o)),
        out_shape=jax.ShapeDtypeStruct((n, wid), F32),
        compiler_params=_params(("arbitrary", "arbitrary")),
        name="s5y",
    )(u, s, kmat, wy)


def _hgrn_masks():
    c = CHUNK
    t = np.arange(c)
    masks = []
    half = c // 2
    while half >= SUB:
        par = 2 * half
        second = (t % par) >= half
        same_parent = (t[:, None] // par) == (t[None, :] // par)
        masks.append((same_parent & second[:, None] & (~second)[None, :]).astype(np.float32))
        half //= 2
    masks.append((((t[:, None] // SUB) == (t[None, :] // SUB)) & (t[None, :] <= t[:, None])).astype(np.float32))
    return np.stack(masks)


def _hgrn_decays(g, reverse):
    nb = CHUNK // SUB
    g3 = g.reshape(nb, SUB, g.shape[1])
    r = lax.broadcasted_iota(jnp.int32, g3.shape, 1)
    p = g3
    for k in (1, 2, 4):
        if reverse:
            p = p + jnp.where(r <= SUB - 1 - k, pltpu.roll(p, SUB - k, 1), 0.0)
        else:
            p = p + jnp.where(r >= k, pltpu.roll(p, k, 1), 0.0)
    last = 0 if reverse else SUB - 1
    ref = SUB // 2 if reverse else SUB // 2 - 1
    tot = jnp.broadcast_to(p[:, last:last + 1, :], p.shape)
    ep = jnp.exp(p)
    eq = jnp.exp(tot - p)
    ed = p - jnp.broadcast_to(p[:, ref:ref + 1, :], p.shape)
    edp = jnp.exp(ed)
    edn = jnp.exp(-ed)
    et = jnp.broadcast_to(ep[:, last:last + 1, :], p.shape)
    mem = (lambda i: nb - 1 - i) if reverse else (lambda i: i)
    epb = [ep[mem(i)] for i in range(nb)]
    eqb = [eq[mem(i)] for i in range(nb)]
    etb = [et[mem(i)] for i in range(nb)]

    def assemble(blocks):
        return jnp.concatenate([blocks[mem(i)] for i in range(nb)], axis=0)

    cq = [None] * nb
    acc = None
    for i in range(nb):
        cq[i] = epb[i] if acc is None else epb[i] * acc
        acc = etb[i] if acc is None else acc * etb[i]
    total = acc[0:1, :]
    ck = [None] * nb
    acc = None
    for i in range(nb - 1, -1, -1):
        ck[i] = eqb[i] if acc is None else eqb[i] * acc
        acc = etb[i] if acc is None else acc * etb[i]
    levels = []
    half = nb // 2
    while half >= 1:
        par = 2 * half
        blocks = []
        for i in range(nb):
            j = i % par
            if j >= half:
                f = epb[i]
                for m in range(i - j + half, i):
                    f = f * etb[m]
            else:
                f = eqb[i]
                for m in range(i + 1, i - j + half):
                    f = f * etb[m]
            blocks.append(f)
        levels.append(assemble(blocks))
        half //= 2
    return levels, edp.reshape(g.shape), edn.reshape(g.shape), assemble(cq), assemble(ck), total


def _hgrn_kernel(*refs, nsub, heads, reverse, final, n_lvl):
    if final:
        (q_ref, z_ref, v_ref, lb_ref, masks_ref, s0_ref, of_ref, gate_ref, hnw_ref,
         o_ref, sfin_ref, st_ref) = refs
    else:
        q_ref, z_ref, v_ref, lb_ref, masks_ref, s0_ref, o_ref, sfin_ref, st_ref = refs
    j = pl.program_id(1)
    nj = pl.num_programs(1)
    c = CHUNK
    hd = HEAD_DIM
    nt = (((1,), (1,)), ((), ()))
    tn = (((0,), (0,)), ((), ()))

    @pl.when(j == 0)
    def _():
        st_ref[...] = s0_ref[...]

    mbool = [masks_ref[lv] > 0.0 for lv in range(n_lvl)]

    mpair = [jnp.concatenate([m, m], axis=0) for m in mbool]
    zpad = jnp.zeros((c, hd), BF16)

    def chunk(s, _):
        cl = (nsub - 1 - s) if reverse else s
        rows = pl.ds(pl.multiple_of(cl * c, c), c)
        for hp in range(heads // 2):
            sl = slice(2 * hp * hd, (2 * hp + 2) * hd)
            z = z_ref[rows, sl]
            q = q_ref[rows, sl]
            v = v_ref[rows, sl].astype(BF16)
            lb = lb_ref[:, sl]
            f = lb + (1.0 - lb) * jax.nn.sigmoid(z)
            k = 1.0 - f
            g = jnp.log(jnp.maximum(f, F_MIN))
            levels, edp, edn, cq, ck, total = _hgrn_decays(g, reverse)
            qb = q.astype(BF16)
            kb = k.astype(BF16)
            a = jnp.zeros((2 * c, c), F32)
            for lv in range(n_lvl - 1, -1, -1):
                if lv < n_lvl - 1:
                    e = levels[lv].astype(BF16)
                    ql = qb * e
                    kl = kb * e
                else:
                    ql = qb * edp.astype(BF16)
                    kl = kb * edn.astype(BF16)
                lhs = jnp.concatenate([jnp.concatenate([ql[:, :hd], zpad], axis=1),
                                       jnp.concatenate([zpad, ql[:, hd:]], axis=1)], axis=0)
                sc = lax.dot_general(lhs, kl, nt, preferred_element_type=F32)
                a = jnp.where(mpair[lv], sc, a)
            ab = a.astype(BF16)
            qd = qb * cq.astype(BF16)
            kd = kb * ck.astype(BF16)
            for i in range(2):
                h = 2 * hp + i
                hs = slice(i * hd, (i + 1) * hd)
                st = st_ref[h]
                o_h = jnp.dot(ab[i * c:(i + 1) * c], v[:, hs], preferred_element_type=F32)
                o_h += lax.dot_general(qd[:, hs], st.astype(BF16), nt, preferred_element_type=F32)
                st_ref[h] = st * total[:, hs] + lax.dot_general(v[:, hs], kd[:, hs], tn,
                                                                 preferred_element_type=F32)
                osl = slice(h * hd, (h + 1) * hd)
                if final:
                    o_h = o_h + of_ref[rows, osl]
                    ms = jnp.mean(o_h * o_h, axis=-1, keepdims=True)
                    o_h = o_h * lax.rsqrt(ms + EPS) * hnw_ref[...] * _silu(gate_ref[rows, osl])
                o_ref[rows, osl] = o_h.astype(o_ref.dtype)
        return 0

    lax.fori_loop(0, nsub, chunk, 0, unroll=8)

    @pl.when(j == nj - 1)
    def _():
        sfin_ref[...] = st_ref[...]


def _hgrn(p, lb, masks, s0, *, direction, width, o_fwd=None, hnw=None, nsub=8):
    b, t, _ = p.shape
    heads = width // HEAD_DIM
    assert heads % 2 == 0, "heads are processed in pairs"
    nsub = _tile(t // CHUNK, nsub, 1)
    rows = nsub * CHUNK
    nj = t // rows
    reverse = direction == 1
    final = o_fwd is not None
    n_lvl = masks.shape[0]
    blk = (lambda j: nj - 1 - j) if reverse else (lambda j: j)
    zcol = 2 if reverse else 1
    in_specs = [
        pl.BlockSpec((None, rows, width), lambda bi, j: (bi, blk(j), 0)),
        pl.BlockSpec((None, rows, width), lambda bi, j: (bi, blk(j), zcol)),
        pl.BlockSpec((None, rows, width), lambda bi, j: (bi, blk(j), 3)),
        pl.BlockSpec((1, width), lambda bi, j: (0, 0)),
        pl.BlockSpec(masks.shape, lambda bi, j: (0, 0, 0)),
        pl.BlockSpec((None, heads, HEAD_DIM, HEAD_DIM), lambda bi, j: (bi, 0, 0, 0)),
    ]
    args = [p, p, p, lb.reshape(1, width), masks, s0]
    if final:
        in_specs += [
            pl.BlockSpec((None, rows, width), lambda bi, j: (bi, blk(j), 0)),
            pl.BlockSpec((None, rows, width), lambda bi, j: (bi, blk(j), 4)),
            pl.BlockSpec((1, HEAD_DIM), lambda bi, j: (0, 0)),
        ]
        args += [o_fwd, p, hnw.reshape(1, HEAD_DIM)]
    return pl.pallas_call(
        functools.partial(_hgrn_kernel, nsub=nsub, heads=heads, reverse=reverse, final=final, n_lvl=n_lvl),
        grid=(b, nj),
        in_specs=in_specs,
        out_specs=[
            pl.BlockSpec((None, rows, width), lambda bi, j: (bi, blk(j), 0)),
            pl.BlockSpec((None, heads, HEAD_DIM, HEAD_DIM), lambda bi, j: (bi, 0, 0, 0)),
        ],
        out_shape=[
            jax.ShapeDtypeStruct((b, t, width), F32),
            jax.ShapeDtypeStruct((b, heads, HEAD_DIM, HEAD_DIM), F32),
        ],
        scratch_shapes=[pltpu.VMEM((heads, HEAD_DIM, HEAD_DIM), F32)],
        compiler_params=_params(("arbitrary", "arbitrary")),
        name="hgrn_bwd" if final else "hgrn_fwd",
    )(*args)


def _mixout_kernel(x_ref, y_ref, u_ref, hg_ref, m_ref, dsk_ref, wglu_ref, bglu_ref, wo_ref, o_ref,
                   *, s5w, rb, colmajor):
    yy = y_ref[...] + dsk_ref[...] * u_ref[...]
    g = _gelu_tanh(yy)
    zz = jnp.dot(g.astype(BF16), wglu_ref[...], preferred_element_type=F32) + bglu_ref[...]
    s5 = (g * jax.nn.sigmoid(zz)).astype(BF16)
    if colmajor:
        hg = jnp.concatenate([hg_ref[:, r, :] for r in range(rb)], axis=0).astype(BF16)
    else:
        hg = hg_ref[...].astype(BF16)
    acc = jnp.dot(s5, wo_ref[:s5w, :], preferred_element_type=F32)
    acc += jnp.dot(hg, wo_ref[s5w:, :], preferred_element_type=F32)
    o_ref[...] = x_ref[...] + m_ref[0, 5:6, :] * acc


def _mixout(x3d, y3d, u3d, hg, mods, d_skip, w_glu, b_glu, w_o, *, layer, mod_of_batch, colmajor, rb=8):
    b, t, d = x3d.shape
    s5w = y3d.shape[-1]
    hw = hg.shape[-1]
    if colmajor:
        cols = GRID_W
        rows = t // cols
        rb = _tile(rows, rb, 8)
        tm = rb * cols
        hg_v = hg.reshape(b, cols, rows, hw)
        hg_spec = pl.BlockSpec((None, cols, rb, hw), lambda bi, i: (bi, 0, i, 0))
    else:
        rb = 1
        tm = _tile(t, 512)
        hg_v = hg
        hg_spec = pl.BlockSpec((None, tm, hw), lambda bi, i: (bi, i, 0))
    mod_fn = (lambda bi: bi) if mod_of_batch else (lambda bi: b)
    return pl.pallas_call(
        functools.partial(_mixout_kernel, s5w=s5w, rb=rb, colmajor=colmajor),
        grid=(b, t // tm),
        in_specs=[
            pl.BlockSpec((None, tm, d), lambda bi, i: (bi, i, 0)),
            pl.BlockSpec((None, tm, s5w), lambda bi, i: (bi, i, 0)),
            pl.BlockSpec((None, tm, s5w), lambda bi, i: (bi, i, 0)),
            hg_spec,
            pl.BlockSpec((1, N_MOD, d), lambda bi, i: (mod_fn(bi), 0, 0)),
            pl.BlockSpec((1, s5w), lambda bi, i: (0, 0)),
            pl.BlockSpec((None, s5w, s5w), lambda bi, i: (layer, 0, 0)),
            pl.BlockSpec((1, s5w), lambda bi, i: (0, 0)),
            pl.BlockSpec((None, s5w + hw, d), lambda bi, i: (layer, 0, 0)),
        ],
        out_specs=pl.BlockSpec((None, tm, d), lambda bi, i: (bi, i, 0)),
        out_shape=jax.ShapeDtypeStruct((b, t, d), F32),
        compiler_params=_params(("arbitrary", "arbitrary")),
        name="mixout",
    )(x3d, y3d, u3d, hg_v, mods, d_skip.reshape(1, s5w), w_glu, b_glu.reshape(1, s5w), w_o)


def kernel(x, c, ctx, c_ctx, w_ada, b_ada, norm_w, ffn_w_gate, ffn_w_up, ffn_w_down, w_in, w_out,
           s5_lambda_re, s5_lambda_im, s5_log_step, s5_b_re, s5_b_im, s5_c_re, s5_c_im, s5_d,
           s5_w_glu, s5_b_glu, hgrn_lower_bounds, hgrn_norm_w, final_norm_w):
    batch, seq, d = x.shape
    n_ctx = ctx.shape[1]
    depth = w_ada.shape[0]
    s5w = s5_d.shape[-1]
    hw = hgrn_lower_bounds.shape[-1]
    rows = seq // GRID_W
    assert batch < MOD_ROWS and rows % CHUNK == 0 and n_ctx % CHUNK == 0
    assert seq % S5_BLOCK == 0 and n_ctx % S5_BLOCK == 0 and (s5w // S5_GROUP) % S5_OCT == 0

    cvec = jnp.zeros((MOD_ROWS, d), F32).at[:batch].set(c.astype(F32)).at[batch].set(c_ctx.astype(F32))
    mods_all = _ada(cvec, w_ada, b_ada).reshape(depth, MOD_ROWS, N_MOD, d)

    lb_soft = jax.nn.softmax(hgrn_lower_bounds.astype(F32), axis=0)
    lb_all = jnp.cumsum(lb_soft, axis=0) - lb_soft[0]

    masks_np = _hgrn_masks()
    masks_f = jnp.asarray(masks_np, F32)
    masks_b = jnp.asarray(masks_np[:, ::-1, ::-1].copy(), F32)
    heads = hw // HEAD_DIM
    s_zero = jnp.zeros((batch, heads, HEAD_DIM, HEAD_DIM), F32)

    wg = ffn_w_gate.astype(BF16)
    wu = ffn_w_up.astype(BF16)
    wd = ffn_w_down.astype(BF16)
    uk, pb3, ca3, dec = _s5_compact(s5_lambda_re, s5_lambda_im, s5_log_step, s5_b_re, s5_b_im, s5_c_re, s5_c_im)
    kmat, we, wy = _s5_expand(uk, pb3, ca3)
    w_in_b = w_in.astype(BF16)
    w_h = w_in_b[:, :, s5w:]
    w_o = w_out.astype(BF16)
    w_glu = s5_w_glu.astype(BF16)

    xl = x.astype(F32)
    xc = ctx.astype(F32)
    lat = dict(rows_per_mod=seq, mod_off=0)
    cx = dict(rows_per_mod=batch * n_ctx, mod_off=batch)
    for l in range(depth):
        last = l == depth - 1
        mods = mods_all[l]

        xl = _ffn(xl.reshape(batch * seq, d), mods, norm_w[l, 0], wg, wu, wd, layer=l, half=0, base=0,
                  **lat).reshape(batch, seq, d)
        xc = _ffn(xc.reshape(batch * n_ctx, d), mods, norm_w[l, 0], wg, wu, wd, layer=l, half=0, base=0,
                  **cx).reshape(batch, n_ctx, d)

        u_l = _uproj(xl.reshape(batch * seq, d), mods, norm_w[l, 1], w_in_b, layer=l, s5w=s5w, **lat)
        u_c = _uproj(xc.reshape(batch * n_ctx, d), mods, norm_w[l, 1], w_in_b, layer=l, s5w=s5w, **cx)
        p_lat = _hproj(xl, mods, norm_w[l, 1], w_h, layer=l, hw=hw, cols=GRID_W, mod_row=None)
        p_ctx = _hproj(xc.reshape(1, batch * n_ctx, d), mods, norm_w[l, 1], w_h, layer=l, hw=hw, cols=1,
                       mod_row=batch).reshape(batch, n_ctx, 5 * hw)

        e_l = _s5e(u_l, we, layer=l)
        e_c = _s5e(u_c, we, layer=l)
        st_c, st_l = _s5scan(e_c, e_l, dec, layer=l, nb=batch)
        y_l = _s5y(u_l, st_l, kmat, wy, layer=l).reshape(batch, seq, s5w)

        lb_f = lb_all[l, 0]
        lb_b = lb_all[l, 1]
        oc_f, sc_f = _hgrn(p_ctx, lb_f, masks_f, s_zero, direction=0, width=hw)
        ol_f, _ = _hgrn(p_lat, lb_f, masks_f, sc_f, direction=0, width=hw)
        hg_c, sc_b = _hgrn(p_ctx, lb_b, masks_b, s_zero, direction=1, width=hw,
                           o_fwd=oc_f, hnw=hgrn_norm_w[l])
        hg_l, _ = _hgrn(p_lat, lb_b, masks_b, sc_b, direction=1, width=hw,
                        o_fwd=ol_f, hnw=hgrn_norm_w[l])

        xl = _mixout(xl, y_l, u_l.reshape(batch, seq, s5w), hg_l, mods, s5_d[l], w_glu, s5_b_glu[l], w_o,
                     layer=l, mod_of_batch=True, colmajor=True)
        if not last:
            y_c = _s5y(u_c, st_c, kmat, wy, layer=l).reshape(batch, n_ctx, s5w)
            xc = _mixout(xc, y_c, u_c.reshape(batch, n_ctx, s5w), hg_c, mods, s5_d[l], w_glu, s5_b_glu[l],
                         w_o, layer=l, mod_of_batch=False, colmajor=False)

        xl = _ffn(xl.reshape(batch * seq, d), mods, norm_w[l, 2], wg, wu, wd, layer=l, half=1, base=6,
                  final_nw=final_norm_w if last else None, **lat).reshape(batch, seq, d)
        if not last:
            xc = _ffn(xc.reshape(batch * n_ctx, d), mods, norm_w[l, 2], wg, wu, wd, layer=l, half=1, base=6,
                      **cx).reshape(batch, n_ctx, d)
    return xl
```

```python
import functools

import numpy as np
import jax
import jax.numpy as jnp
from jax import lax
from jax.experimental import pallas as pl
from jax.experimental.pallas import tpu as pltpu

F32 = jnp.float32
BF16 = jnp.bfloat16

EPS = 1e-6
F_MIN = 1e-6
LAMBDA_RE_MAX = -1e-4
GRID_W = 64
N_MOD = 9
S5_GROUP = 16
S5_STATE = 64
S5_BLOCK = 8
S5_OCT = 8
HEAD_DIM = 128
CHUNK = 64
SUB = 8
LANE = 128
MOD_ROWS = 8
VMEM_LIMIT = 60 * 1024 * 1024


def _params(sem):
    return pltpu.CompilerParams(dimension_semantics=sem, vmem_limit_bytes=VMEM_LIMIT)


def _tile(n, pref, mult=8):
    t = min(n, pref)
    while t > 0:
        if n % t == 0 and t % mult == 0:
            return t
        t -= 1
    return n


def _norm_mod(x, nw, shift, scale):
    ms = jnp.mean(x * x, axis=-1, keepdims=True)
    gain = nw * (1.0 + scale)
    return (x * lax.rsqrt(ms + EPS)) * gain + shift


def _silu(x):
    return x * jax.nn.sigmoid(x)


def _gelu_tanh(x):
    return 0.5 * x * (1.0 + jnp.tanh(0.7978845608028654 * (x + 0.044715 * (x * x * x))))


def _ada_kernel(c_ref, w_ref, b_ref, o_ref):
    a = _silu(c_ref[...])
    a_hi = a.astype(BF16)
    a_lo = (a - a_hi.astype(F32)).astype(BF16)
    w = w_ref[...]
    w_hi = w.astype(BF16)
    w_lo = (w - w_hi.astype(F32)).astype(BF16)
    both = jnp.dot(jnp.concatenate([a_hi, a_lo], axis=0), w_hi, preferred_element_type=F32)
    acc = both[:MOD_ROWS] + both[MOD_ROWS:] + jnp.dot(a_hi, w_lo, preferred_element_type=F32)
    o_ref[...] = acc + b_ref[...]


def _ada(cvec, w_ada, b_ada):
    depth, d, n = w_ada.shape
    tn = _tile(n, 2048, LANE)
    return pl.pallas_call(
        _ada_kernel,
        grid=(depth, n // tn),
        in_specs=[
            pl.BlockSpec((MOD_ROWS, d), lambda l, j: (0, 0)),
            pl.BlockSpec((None, d, tn), lambda l, j: (l, 0, j)),
            pl.BlockSpec((None, 1, tn), lambda l, j: (l, 0, j)),
        ],
        out_specs=pl.BlockSpec((None, MOD_ROWS, tn), lambda l, j: (l, 0, j)),
        out_shape=jax.ShapeDtypeStruct((depth, MOD_ROWS, n), F32),
        compiler_params=_params(("arbitrary", "arbitrary")),
        name="ada",
    )(cvec, w_ada, b_ada.reshape(depth, 1, n))


def _ffn_kernel(*refs, base, n_f, final, tail):
    if final:
        x_ref, m_ref, nw_ref, wg_ref, wu_ref, wd_ref, fnw_ref, o_ref, h_ref = refs
    else:
        x_ref, m_ref, nw_ref, wg_ref, wu_ref, wd_ref, o_ref, h_ref = refs
    j = pl.program_id(1)

    @pl.when(j == 0)
    def _():
        h = _norm_mod(x_ref[...], nw_ref[...], m_ref[0, base:base + 1, :], m_ref[0, base + 1:base + 2, :])
        h_ref[...] = h.astype(BF16)

    def partial_sum(valid=None):
        h = h_ref[...]
        g = jnp.dot(h, wg_ref[...], preferred_element_type=F32)
        u = jnp.dot(h, wu_ref[...], preferred_element_type=F32)
        a = (_silu(g) * u).astype(BF16)
        wd = wd_ref[...]
        if valid is not None:
            a = jnp.where(lax.broadcasted_iota(jnp.int32, a.shape, 1) < valid, a, jnp.zeros_like(a))
            wd = jnp.where(lax.broadcasted_iota(jnp.int32, wd.shape, 0) < valid, wd, jnp.zeros_like(wd))
        return jnp.dot(a, wd, preferred_element_type=F32)

    @pl.when(j == 0)
    def _():
        o_ref[...] = partial_sum()

    @pl.when((j > 0) & (j < n_f - 1))
    def _():
        o_ref[...] += partial_sum()

    @pl.when(j == n_f - 1)
    def _():
        y = x_ref[...] + (0.5 * m_ref[0, base + 2:base + 3, :]) * (o_ref[...] + partial_sum(tail))
        if final:
            ms = jnp.mean(y * y, axis=-1, keepdims=True)
            y = y * lax.rsqrt(ms + EPS) * fnw_ref[...]
        o_ref[...] = y


def _ffn(x2d, mods, nw, wg, wu, wd, *, layer, half, base, rows_per_mod, mod_off, final_nw=None,
         tm=1024, tf=512):
    n, d = x2d.shape
    fp = wg.shape[-1]
    tm = _tile(min(n, rows_per_mod), tm)
    n_f = pl.cdiv(fp, tf)
    tail = fp - (n_f - 1) * tf if fp % tf else None
    assert n_f >= 2
    final = final_nw is not None
    in_specs = [
        pl.BlockSpec((tm, d), lambda i, j: (i, 0)),
        pl.BlockSpec((1, N_MOD, d), lambda i, j: ((i * tm) // rows_per_mod + mod_off, 0, 0)),
        pl.BlockSpec((1, d), lambda i, j: (0, 0)),
        pl.BlockSpec((None, None, d, tf), lambda i, j: (layer, half, 0, j)),
        pl.BlockSpec((None, None, d, tf), lambda i, j: (layer, half, 0, j)),
        pl.BlockSpec((None, None, tf, d), lambda i, j: (layer, half, j, 0)),
    ]
    args = [x2d, mods, nw.reshape(1, d), wg, wu, wd]
    if final:
        in_specs.append(pl.BlockSpec((1, d), lambda i, j: (0, 0)))
        args.append(final_nw.reshape(1, d))
    return pl.pallas_call(
        functools.partial(_ffn_kernel, base=base, n_f=n_f, final=final, tail=tail),
        grid=(n // tm, n_f),
        in_specs=in_specs,
        out_specs=pl.BlockSpec((tm, d), lambda i, j: (i, 0)),
        out_shape=jax.ShapeDtypeStruct((n, d), F32),
        scratch_shapes=[pltpu.VMEM((tm, d), BF16)],
        compiler_params=_params(("arbitrary", "arbitrary")),
        name="ffn_final" if final else "ffn",
    )(*args)


def _hproj_kernel(*refs, colmajor, hw):
    if colmajor:
        x_ref, m_ref, nw_ref, w_ref, wu_ref, perm_ref, o_ref, u_ref = refs
    else:
        x_ref, m_ref, nw_ref, w_ref, wu_ref, o_ref, u_ref = refs
    h = _norm_mod(x_ref[...], nw_ref[...], m_ref[0, 3:4, :], m_ref[0, 4:5, :]).astype(BF16)
    if colmajor:
        h = h.reshape(perm_ref.shape[0], h.shape[-1])
        u_ref[...] = jnp.dot(h, wu_ref[...], preferred_element_type=F32).reshape(u_ref.shape)
        h = jnp.dot(perm_ref[...], h, preferred_element_type=F32).astype(BF16)
    else:
        u_ref[...] = jnp.dot(h, wu_ref[...], preferred_element_type=F32)
    for nb in range(o_ref.shape[-1] // hw):
        p = jnp.dot(h, w_ref[:, nb * hw:(nb + 1) * hw], preferred_element_type=F32)
        o_ref[:, nb * hw:(nb + 1) * hw] = _silu(p) if nb == 0 else p


def _hproj(x3d, mods, nw, w_h, w_u, *, layer, hw, cols, mod_row, cb=8, tr=512):
    b, t, d = x3d.shape
    nh = w_h.shape[-1]
    s5w = w_u.shape[-1]
    colmajor = cols > 1
    mod_fn = (lambda bi: bi) if mod_row is None else (lambda bi: mod_row)
    in_specs = [
        None,
        pl.BlockSpec((1, N_MOD, d), lambda bi, i: (mod_fn(bi), 0, 0)),
        pl.BlockSpec((1, d), lambda bi, i: (0, 0)),
        pl.BlockSpec((None, d, nh), lambda bi, i: (layer, 0, 0), pipeline_mode=pl.Buffered(1)),
        pl.BlockSpec((None, d, s5w), lambda bi, i: (layer, 0, 0), pipeline_mode=pl.Buffered(1)),
    ]
    args = [None, mods, nw.reshape(1, d), w_h, w_u]
    if colmajor:
        rows = t // cols
        cb = _tile(cols, cb, 8)
        tr = rows * cb
        args[0] = x3d.reshape(b, rows, cols, d)
        in_specs[0] = pl.BlockSpec((None, rows, cb, d), lambda bi, i: (bi, 0, i, 0))
        src = np.arange(tr).reshape(rows, cb).T.reshape(-1)
        perm = np.zeros((tr, tr), np.float32)
        perm[np.arange(tr), src] = 1.0
        args.append(jnp.asarray(perm, BF16))
        in_specs.append(pl.BlockSpec(perm.shape, lambda bi, i: (0, 0)))
        u_shape = (b, rows, cols, s5w)
        u_spec = pl.BlockSpec((None, rows, cb, s5w), lambda bi, i: (bi, 0, i, 0))
    else:
        tr = _tile(t, tr)
        args[0] = x3d
        in_specs[0] = pl.BlockSpec((None, tr, d), lambda bi, i: (bi, i, 0))
        u_shape = (b, t, s5w)
        u_spec = pl.BlockSpec((None, tr, s5w), lambda bi, i: (bi, i, 0))
    p_out, u_out = pl.pallas_call(
        functools.partial(_hproj_kernel, colmajor=colmajor, hw=hw),
        grid=(b, t // tr),
        in_specs=in_specs,
        out_specs=[pl.BlockSpec((None, tr, nh), lambda bi, i: (bi, i, 0)), u_spec],
        out_shape=[jax.ShapeDtypeStruct((b, t, nh), F32), jax.ShapeDtypeStruct(u_shape, F32)],
        compiler_params=_params(("arbitrary", "arbitrary")),
        name="hproj",
    )(*args)
    return p_out, u_out.reshape(b, t, s5w)


def _s5_sel_consts():
    t, h, p, o8 = S5_BLOCK, S5_GROUP, S5_STATE, S5_OCT
    colsel = np.zeros((t, 2, t, h, t, o8, h), np.float32)
    for r in range(t):
        for r2 in range(t):
            for hh in range(h):
                if r2 >= r:
                    colsel[r, 0, r2 - r, hh, r2, :, hh] = 1.0
                if r >= r2:
                    colsel[r, 1, r - r2, hh, r2, :, hh] = 1.0
    colsel = colsel.reshape(t, 2 * t * h, t * o8 * h)
    tile_e = np.zeros((4, p, 4, o8, p), np.float32)
    tile_y = np.zeros((4, h, 4, o8, h), np.float32)
    for dp in range(4):
        for i in range(p):
            tile_e[dp, i, dp, :, i] = 1.0
        for i in range(h):
            tile_y[dp, i, dp, :, i] = 1.0
    return colsel, tile_e.reshape(4 * p, 4 * o8 * p), tile_y.reshape(4 * h, 4 * o8 * h)


def _s5w_kernel(uk_ref, pb_ref, ca_ref, colsel_ref, tile_e_ref, tile_y_ref, k_ref, we_ref, wy_ref):
    t, h, p, o8 = S5_BLOCK, S5_GROUP, S5_STATE, S5_OCT
    gh = o8 * h
    gp = o8 * p

    def diag_mask(shape, row_div, col_mod, col_div):
        rg = lax.broadcasted_iota(jnp.int32, shape, 0) // row_div
        cg = (lax.broadcasted_iota(jnp.int32, shape, 1) % col_mod) // col_div
        return rg == cg

    mk = diag_mask((gh, t * gh), h, gh, h)
    me = diag_mask((gh, 4 * gp), h, gp, p)
    my = diag_mask((gp, 4 * gh), p, gh, h)
    uk = uk_ref[...].astype(BF16)
    for r in range(t):
        blk = jnp.dot(uk, colsel_ref[r], preferred_element_type=F32)
        k_ref[r * gh:(r + 1) * gh, :] = jnp.where(mk, blk, 0.0).astype(BF16)
        blk = jnp.dot(pb_ref[r].astype(BF16), tile_e_ref[...], preferred_element_type=F32)
        we_ref[r * gh:(r + 1) * gh, :] = jnp.where(me, blk, 0.0).astype(BF16)
        blk = jnp.where(my, jnp.dot(ca_ref[r].astype(BF16), tile_y_ref[...], preferred_element_type=F32), 0.0)
        for dp in range(4):
            wy_ref[dp * gp:(dp + 1) * gp, r * gh:(r + 1) * gh] = blk[:, dp * gh:(dp + 1) * gh].astype(BF16)


def _s5_compact(lam_re, lam_im, log_step, b_re, b_im, c_re, c_im):
    hp = lax.Precision.HIGHEST
    t = S5_BLOCK
    depth, _, g, p = lam_re.shape
    h = b_re.shape[-1]
    n_oct = g // S5_OCT
    lam_re = jnp.minimum(lam_re.astype(F32), LAMBDA_RE_MAX)
    lam_im = lam_im.astype(F32)
    dt = jnp.exp(log_step.astype(F32))[..., None]
    mag = jnp.exp(lam_re * dt)
    lb_re = mag * jnp.cos(lam_im * dt)
    lb_im = mag * jnp.sin(lam_im * dt)
    den = lam_re * lam_re + lam_im * lam_im
    nr = lb_re - 1.0
    ni = lb_im
    cf_re = (nr * lam_re + ni * lam_im) / den
    cf_im = (ni * lam_re - nr * lam_im) / den
    b_re = b_re.astype(F32)
    b_im = b_im.astype(F32)
    br = cf_re[..., None] * b_re - cf_im[..., None] * b_im
    bi = cf_re[..., None] * b_im + cf_im[..., None] * b_re
    cr = c_re.astype(F32)
    ci = c_im.astype(F32)
    expo = np.concatenate([np.arange(t + 1), np.arange(t - 1, -1, -1), np.arange(t, 0, -1)]).astype(np.float32)
    j = jnp.asarray(expo)[None, None, :, None, None]
    pmag = jnp.exp(j * (lam_re * dt)[:, :, None])
    pw_re = pmag * jnp.cos(j * (lam_im * dt)[:, :, None])
    pw_im = pmag * jnp.sin(j * (lam_im * dt)[:, :, None])
    asc, desc0, desc1 = slice(0, t), slice(t + 1, 2 * t + 1), slice(2 * t + 1, 3 * t + 1)

    def lam_b(d_, sl):
        pr, pi = pw_re[:, d_, sl][..., None], pw_im[:, d_, sl][..., None]
        b_r, b_i = br[:, d_][:, None], bi[:, d_][:, None]
        return pr * b_r - pi * b_i, pr * b_i + pi * b_r

    pbf_re, pbf_im = lam_b(0, asc)
    pbb_re, pbb_im = lam_b(1, asc)
    pb_re = jnp.stack([pbf_re, pbb_re], axis=1)
    pb_im = jnp.stack([pbf_im, pbb_im], axis=1)
    kj = (jnp.einsum('ldgep,ldjgph->ldjgeh', cr, pb_re, precision=hp)
          - jnp.einsum('ldgep,ldjgph->ldjgeh', ci, pb_im, precision=hp))
    kj = jnp.stack([jnp.concatenate([kj[:, 0, :1] + kj[:, 1, :1], kj[:, 0, 1:]], axis=1),
                    jnp.concatenate([jnp.zeros_like(kj[:, 1, :1]), kj[:, 1, 1:]], axis=1)], axis=1)
    uk = jnp.transpose(kj, (0, 3, 5, 1, 2, 4)).reshape(depth, n_oct, S5_OCT * h, 2 * t * h)
    pb4 = jnp.stack(lam_b(0, desc0) + (pbb_re, pbb_im), axis=1)
    pb3 = jnp.transpose(pb4.reshape(depth, 4, t, n_oct, S5_OCT, p, h), (0, 3, 2, 4, 6, 1, 5))
    pb3 = pb3.reshape(depth, n_oct, t, S5_OCT * h, 4 * p)

    def c_lam(d_, sl):
        pr, pi = pw_re[:, d_, sl][:, :, :, None, :], pw_im[:, d_, sl][:, :, :, None, :]
        c_r, c_i = cr[:, d_][:, None], ci[:, d_][:, None]
        return c_r * pr - c_i * pi, -(c_r * pi + c_i * pr)

    ca4 = jnp.stack(c_lam(0, slice(1, t + 1)) + c_lam(1, desc1), axis=1)
    ca3 = jnp.transpose(ca4.reshape(depth, 4, t, n_oct, S5_OCT, h, p), (0, 3, 2, 4, 6, 1, 5))
    ca3 = ca3.reshape(depth, n_oct, t, S5_OCT * p, 4 * h)
    nk2 = S5_OCT * p // LANE
    a_re = jnp.transpose(pw_re[:, :, t].reshape(depth, 2, n_oct, nk2, LANE), (0, 2, 1, 3, 4))
    a_im = jnp.transpose(pw_im[:, :, t].reshape(depth, 2, n_oct, nk2, LANE), (0, 2, 1, 3, 4))
    dec = jnp.stack([jnp.concatenate([a_re, a_re], axis=3), jnp.concatenate([-a_im, a_im], axis=3)], axis=3)
    return uk, pb3, ca3, dec


def _s5_expand(uk, pb3, ca3):
    depth, n_oct = uk.shape[:2]
    colsel, tile_e, tile_y = (jnp.asarray(a, BF16) for a in _s5_sel_consts())
    kdim = S5_BLOCK * S5_OCT * S5_GROUP
    sdim = 4 * S5_OCT * S5_STATE
    return pl.pallas_call(
        _s5w_kernel,
        grid=(depth, n_oct),
        in_specs=[
            pl.BlockSpec((None, None) + uk.shape[2:], lambda l, o: (l, o, 0, 0)),
            pl.BlockSpec((None, None) + pb3.shape[2:], lambda l, o: (l, o, 0, 0, 0)),
            pl.BlockSpec((None, None) + ca3.shape[2:], lambda l, o: (l, o, 0, 0, 0)),
            pl.BlockSpec(colsel.shape, lambda l, o: (0, 0, 0)),
            pl.BlockSpec(tile_e.shape, lambda l, o: (0, 0)),
            pl.BlockSpec(tile_y.shape, lambda l, o: (0, 0)),
        ],
        out_specs=[
            pl.BlockSpec((None, None, kdim, kdim), lambda l, o: (l, o, 0, 0)),
            pl.BlockSpec((None, None, kdim, sdim), lambda l, o: (l, o, 0, 0)),
            pl.BlockSpec((None, None, sdim, kdim), lambda l, o: (l, o, 0, 0)),
        ],
        out_shape=[
            jax.ShapeDtypeStruct((depth, n_oct, kdim, kdim), BF16),
            jax.ShapeDtypeStruct((depth, n_oct, kdim, sdim), BF16),
            jax.ShapeDtypeStruct((depth, n_oct, sdim, kdim), BF16),
        ],
        compiler_params=_params(("arbitrary", "arbitrary")),
        name="s5w",
    )(uk, pb3, ca3, colsel, tile_e, tile_y)


def _block_rows(u_ref, tmr):
    return jnp.concatenate(
        [u_ref[pl.ds(r, tmr, stride=S5_BLOCK), :].astype(BF16) for r in range(S5_BLOCK)], axis=1)


def _s5e_kernel(u_ref, we_ref, e_ref, *, tmr):
    res = jnp.dot(_block_rows(u_ref, tmr), we_ref[...], preferred_element_type=F32)
    nk = res.shape[1] // (2 * LANE)
    for d in range(2):
        for k in range(nk):
            c0 = (d * nk + k) * LANE
            e_ref[d, pl.ds(k, tmr, stride=nk), :] = res[:, c0:c0 + LANE]


def _s5e(u, we, *, layer, tmr=1024):
    n, wid = u.shape
    m = n // S5_BLOCK
    _, n_oct, kdim, ncol = we.shape
    nk = ncol // (2 * LANE)
    tmr = _tile(m, tmr)
    return pl.pallas_call(
        functools.partial(_s5e_kernel, tmr=tmr),
        grid=(n_oct, m // tmr),
        in_specs=[
            pl.BlockSpec((tmr * S5_BLOCK, LANE), lambda o, i: (i, o)),
            pl.BlockSpec((None, None, kdim, ncol), lambda o, i: (layer, o, 0, 0)),
        ],
        out_specs=pl.BlockSpec((None, 2, tmr * nk, LANE), lambda o, i: (o, 0, i, 0)),
        out_shape=jax.ShapeDtypeStruct((n_oct, 2, m * nk, LANE), F32),
        compiler_params=_params(("arbitrary", "arbitrary")),
        name="s5e",
    )(u, we)


def _s5scan_kernel(ec_ref, el_ref, a_ref, sc_ref, sl_ref, *, nb, n_c, n_l, nk):
    rev = pl.program_id(1) == 1
    a1 = a_ref[0]
    a2 = a_ref[1]

    def run(e_ref, s_ref, n, carry):
        def body(s, carry):
            i = jnp.where(rev, n - 1 - s, s)
            new = []
            for b in range(nb):
                x = carry[b]
                rows = pl.ds(pl.multiple_of((b * n + i) * nk, nk), nk)
                s_ref[rows, :] = x
                new.append(a1 * x + a2 * pltpu.roll(x, nk // 2, 0) + e_ref[rows, :])
            return tuple(new)
        return lax.fori_loop(0, n, body, carry, unroll=2)

    carry = tuple(jnp.zeros((nk, LANE), F32) for _ in range(nb))
    carry = run(ec_ref, sc_ref, n_c, carry)
    run(el_ref, sl_ref, n_l, carry)


def _s5scan(e_ctx, e_lat, dec, *, layer, nb):
    n_oct, _, rc, _ = e_ctx.shape
    rl = e_lat.shape[2]
    nk = dec.shape[-2]
    return pl.pallas_call(
        functools.partial(_s5scan_kernel, nb=nb, n_c=rc // (nk * nb), n_l=rl // (nk * nb), nk=nk),
        grid=(n_oct, 2),
        in_specs=[
            pl.BlockSpec((None, None, rc, LANE), lambda o, d: (o, d, 0, 0)),
            pl.BlockSpec((None, None, rl, LANE), lambda o, d: (o, d, 0, 0)),
            pl.BlockSpec((None, None, None, 2, nk, LANE), lambda o, d: (layer, o, d, 0, 0, 0)),
        ],
        out_specs=[
            pl.BlockSpec((None, None, rc, LANE), lambda o, d: (o, d, 0, 0)),
            pl.BlockSpec((None, None, rl, LANE), lambda o, d: (o, d, 0, 0)),
        ],
        out_shape=[jax.ShapeDtypeStruct(e_ctx.shape, F32), jax.ShapeDtypeStruct(e_lat.shape, F32)],
        compiler_params=_params(("arbitrary", "arbitrary")),
        name="s5scan",
    )(e_ctx, e_lat, dec)


def _s5y_kernel(u_ref, s_ref, k_ref, wy_ref, y_ref, *, tmr):
    nk = s_ref.shape[1] // tmr
    st = jnp.concatenate(
        [s_ref[d, pl.ds(k, tmr, stride=nk), :].astype(BF16) for d in range(2) for k in range(nk)], axis=1)
    res = jnp.dot(_block_rows(u_ref, tmr), k_ref[...], preferred_element_type=F32)
    res += jnp.dot(st, wy_ref[...], preferred_element_type=F32)
    for r in range(S5_BLOCK):
        y_ref[pl.ds(r, tmr, stride=S5_BLOCK), :] = res[:, r * LANE:(r + 1) * LANE]


def _s5y(u, s, kmat, wy, *, layer, tmr=1024):
    n, wid = u.shape
    m = n // S5_BLOCK
    _, n_oct, kdim, _ = kmat.shape
    sdim = wy.shape[2]
    nk = s.shape[2] // m
    tmr = _tile(m, tmr)
    return pl.pallas_call(
        functools.partial(_s5y_kernel, tmr=tmr),
        grid=(n_oct, m // tmr),
        in_specs=[
            pl.BlockSpec((tmr * S5_BLOCK, LANE), lambda o, i: (i, o)),
            pl.BlockSpec((None, 2, tmr * nk, LANE), lambda o, i: (o, 0, i, 0)),
            pl.BlockSpec((None, None, kdim, kdim), lambda o, i: (layer, o, 0, 0)),
            pl.BlockSpec((None, None, sdim, kdim), lambda o, i: (layer, o, 0, 0)),
        ],
        out_specs=pl.BlockSpec((tmr * S5_BLOCK, LANE), lambda o, i: (i, o)),
        out_shape=jax.ShapeDtypeStruct((n, wid), F32),
        compiler_params=_params(("arbitrary", "arbitrary")),
        name="s5y",
    )(u, s, kmat, wy)


def _hgrn_masks():
    c = CHUNK
    t = np.arange(c)
    masks = []
    half = c // 2
    while half >= SUB:
        par = 2 * half
        second = (t % par) >= half
        same_parent = (t[:, None] // par) == (t[None, :] // par)
        masks.append((same_parent & second[:, None] & (~second)[None, :]).astype(np.float32))
        half //= 2
    masks.append((((t[:, None] // SUB) == (t[None, :] // SUB)) & (t[None, :] <= t[:, None])).astype(np.float32))
    return np.stack(masks)


def _hgrn_decays(g, reverse):
    nb = CHUNK // SUB
    g3 = g.reshape(nb, SUB, g.shape[1])
    r = lax.broadcasted_iota(jnp.int32, g3.shape, 1)
    p = g3
    for k in (1, 2, 4):
        if reverse:
            p = p + jnp.where(r <= SUB - 1 - k, pltpu.roll(p, SUB - k, 1), 0.0)
        else:
            p = p + jnp.where(r >= k, pltpu.roll(p, k, 1), 0.0)
    last = 0 if reverse else SUB - 1
    ref = SUB // 2 if reverse else SUB // 2 - 1
    tot = jnp.broadcast_to(p[:, last:last + 1, :], p.shape)
    ep = jnp.exp(p)
    eq = jnp.exp(tot - p)
    ed = p - jnp.broadcast_to(p[:, ref:ref + 1, :], p.shape)
    edp = jnp.exp(ed)
    edn = jnp.exp(-ed)
    et = jnp.broadcast_to(ep[:, last:last + 1, :], p.shape)
    mem = (lambda i: nb - 1 - i) if reverse else (lambda i: i)
    epb = [ep[mem(i)] for i in range(nb)]
    eqb = [eq[mem(i)] for i in range(nb)]
    etb = [et[mem(i)] for i in range(nb)]

    def assemble(blocks):
        return jnp.concatenate([blocks[mem(i)] for i in range(nb)], axis=0)

    cq = [None] * nb
    acc = None
    for i in range(nb):
        cq[i] = epb[i] if acc is None else epb[i] * acc
        acc = etb[i] if acc is None else acc * etb[i]
    total = acc[0:1, :]
    ck = [None] * nb
    acc = None
    for i in range(nb - 1, -1, -1):
        ck[i] = eqb[i] if acc is None else eqb[i] * acc
        acc = etb[i] if acc is None else acc * etb[i]
    levels = []
    half = nb // 2
    while half >= 1:
        par = 2 * half
        blocks = []
        for i in range(nb):
            j = i % par
            if j >= half:
                f = epb[i]
                for m in range(i - j + half, i):
                    f = f * etb[m]
            else:
                f = eqb[i]
                for m in range(i + 1, i - j + half):
                    f = f * etb[m]
            blocks.append(f)
        levels.append(assemble(blocks))
        half //= 2
    return levels, edp.reshape(g.shape), edn.reshape(g.shape), assemble(cq), assemble(ck), total


def _hgrn_kernel(*refs, nsub, heads, reverse, final, n_lvl):
    if final:
        (q_ref, z_ref, v_ref, lb_ref, masks_ref, s0_ref, of_ref, gate_ref, hnw_ref,
         o_ref, sfin_ref, st_ref) = refs
    else:
        q_ref, z_ref, v_ref, lb_ref, masks_ref, s0_ref, o_ref, sfin_ref, st_ref = refs
    j = pl.program_id(1)
    nj = pl.num_programs(1)
    c = CHUNK
    hd = HEAD_DIM
    nt = (((1,), (1,)), ((), ()))
    tn = (((0,), (0,)), ((), ()))

    @pl.when(j == 0)
    def _():
        st_ref[...] = s0_ref[...]

    mbool = [masks_ref[lv] > 0.0 for lv in range(n_lvl)]

    mpair = [jnp.concatenate([m, m], axis=0) for m in mbool]
    zpad = jnp.zeros((c, hd), BF16)

    def chunk(s, _):
        cl = (nsub - 1 - s) if reverse else s
        rows = pl.ds(pl.multiple_of(cl * c, c), c)
        for hp in range(heads // 2):
            sl = slice(2 * hp * hd, (2 * hp + 2) * hd)
            z = z_ref[rows, sl]
            q = q_ref[rows, sl]
            v = v_ref[rows, sl].astype(BF16)
            lb = lb_ref[:, sl]
            f = lb + (1.0 - lb) * jax.nn.sigmoid(z)
            k = 1.0 - f
            g = jnp.log(jnp.maximum(f, F_MIN))
            levels, edp, edn, cq, ck, total = _hgrn_decays(g, reverse)
            qb = q.astype(BF16)
            kb = k.astype(BF16)
            a = jnp.zeros((2 * c, c), F32)
            for lv in range(n_lvl - 1, -1, -1):
                if lv < n_lvl - 1:
                    e = levels[lv].astype(BF16)
                    ql = qb * e
                    kl = kb * e
                else:
                    ql = qb * edp.astype(BF16)
                    kl = kb * edn.astype(BF16)
                lhs = jnp.concatenate([jnp.concatenate([ql[:, :hd], zpad], axis=1),
                                       jnp.concatenate([zpad, ql[:, hd:]], axis=1)], axis=0)
                sc = lax.dot_general(lhs, kl, nt, preferred_element_type=F32)
                a = jnp.where(mpair[lv], sc, a)
            ab = a.astype(BF16)
            qd = qb * cq.astype(BF16)
            kd = kb * ck.astype(BF16)
            for i in range(2):
                h = 2 * hp + i
                hs = slice(i * hd, (i + 1) * hd)
                st = st_ref[h]
                o_h = jnp.dot(ab[i * c:(i + 1) * c], v[:, hs], preferred_element_type=F32)
                o_h += lax.dot_general(qd[:, hs], st.astype(BF16), nt, preferred_element_type=F32)
                st_ref[h] = st * total[:, hs] + lax.dot_general(v[:, hs], kd[:, hs], tn,
                                                                 preferred_element_type=F32)
                osl = slice(h * hd, (h + 1) * hd)
                if final:
                    o_h = o_h + of_ref[rows, osl]
                    ms = jnp.mean(o_h * o_h, axis=-1, keepdims=True)
                    o_h = o_h * lax.rsqrt(ms + EPS) * hnw_ref[...] * _silu(gate_ref[rows, osl])
                o_ref[rows, osl] = o_h.astype(o_ref.dtype)
        return 0

    lax.fori_loop(0, nsub, chunk, 0, unroll=8)

    @pl.when(j == nj - 1)
    def _():
        sfin_ref[...] = st_ref[...]


def _hgrn(p, lb, masks, s0, *, direction, width, o_fwd=None, hnw=None, nsub=8):
    b, t, _ = p.shape
    heads = width // HEAD_DIM
    assert heads % 2 == 0, "heads are processed in pairs"
    nsub = _tile(t // CHUNK, nsub, 1)
    rows = nsub * CHUNK
    nj = t // rows
    reverse = direction == 1
    final = o_fwd is not None
    n_lvl = masks.shape[0]
    blk = (lambda j: nj - 1 - j) if reverse else (lambda j: j)
    zcol = 2 if reverse else 1
    in_specs = [
        pl.BlockSpec((None, rows, width), lambda bi, j: (bi, blk(j), 0)),
        pl.BlockSpec((None, rows, width), lambda bi, j: (bi, blk(j), zcol)),
        pl.BlockSpec((None, rows, width), lambda bi, j: (bi, blk(j), 3)),
        pl.BlockSpec((1, width), lambda bi, j: (0, 0)),
        pl.BlockSpec(masks.shape, lambda bi, j: (0, 0, 0)),
        pl.BlockSpec((None, heads, HEAD_DIM, HEAD_DIM), lambda bi, j: (bi, 0, 0, 0)),
    ]
    args = [p, p, p, lb.reshape(1, width), masks, s0]
    if final:
        in_specs += [
            pl.BlockSpec((None, rows, width), lambda bi, j: (bi, blk(j), 0)),
            pl.BlockSpec((None, rows, width), lambda bi, j: (bi, blk(j), 4)),
            pl.BlockSpec((1, HEAD_DIM), lambda bi, j: (0, 0)),
        ]
        args += [o_fwd, p, hnw.reshape(1, HEAD_DIM)]
    return pl.pallas_call(
        functools.partial(_hgrn_kernel, nsub=nsub, heads=heads, reverse=reverse, final=final, n_lvl=n_lvl),
        grid=(b, nj),
        in_specs=in_specs,
        out_specs=[
            pl.BlockSpec((None, rows, width), lambda bi, j: (bi, blk(j), 0)),
            pl.BlockSpec((None, heads, HEAD_DIM, HEAD_DIM), lambda bi, j: (bi, 0, 0, 0)),
        ],
        out_shape=[
            jax.ShapeDtypeStruct((b, t, width), F32),
            jax.ShapeDtypeStruct((b, heads, HEAD_DIM, HEAD_DIM), F32),
        ],
        scratch_shapes=[pltpu.VMEM((heads, HEAD_DIM, HEAD_DIM), F32)],
        compiler_params=_params(("arbitrary", "arbitrary")),
        name="hgrn_bwd" if final else "hgrn_fwd",
    )(*args)


def _mixout_kernel(x_ref, y_ref, u_ref, hg_ref, m_ref, dsk_ref, wglu_ref, bglu_ref, wo_ref, o_ref,
                   *, s5w, rb, colmajor):
    yy = y_ref[...] + dsk_ref[...] * u_ref[...]
    g = _gelu_tanh(yy)
    zz = jnp.dot(g.astype(BF16), wglu_ref[...], preferred_element_type=F32) + bglu_ref[...]
    s5 = (g * jax.nn.sigmoid(zz)).astype(BF16)
    if colmajor:
        hg = jnp.concatenate([hg_ref[:, r, :] for r in range(rb)], axis=0).astype(BF16)
    else:
        hg = hg_ref[...].astype(BF16)
    acc = jnp.dot(s5, wo_ref[:s5w, :], preferred_element_type=F32)
    acc += jnp.dot(hg, wo_ref[s5w:, :], preferred_element_type=F32)
    o_ref[...] = x_ref[...] + m_ref[0, 5:6, :] * acc


def _mixout(x3d, y3d, u3d, hg, mods, d_skip, w_glu, b_glu, w_o, *, layer, mod_of_batch, colmajor, rb=8):
    b, t, d = x3d.shape
    s5w = y3d.shape[-1]
    hw = hg.shape[-1]
    if colmajor:
        cols = GRID_W
        rows = t // cols
        rb = _tile(rows, rb, 8)
        tm = rb * cols
        hg_v = hg.reshape(b, cols, rows, hw)
        hg_spec = pl.BlockSpec((None, cols, rb, hw), lambda bi, i: (bi, 0, i, 0))
    else:
        rb = 1
        tm = _tile(t, 512)
        hg_v = hg
        hg_spec = pl.BlockSpec((None, tm, hw), lambda bi, i: (bi, i, 0))
    mod_fn = (lambda bi: bi) if mod_of_batch else (lambda bi: b)
    return pl.pallas_call(
        functools.partial(_mixout_kernel, s5w=s5w, rb=rb, colmajor=colmajor),
        grid=(b, t // tm),
        in_specs=[
            pl.BlockSpec((None, tm, d), lambda bi, i: (bi, i, 0)),
            pl.BlockSpec((None, tm, s5w), lambda bi, i: (bi, i, 0)),
            pl.BlockSpec((None, tm, s5w), lambda bi, i: (bi, i, 0)),
            hg_spec,
            pl.BlockSpec((1, N_MOD, d), lambda bi, i: (mod_fn(bi), 0, 0)),
            pl.BlockSpec((1, s5w), lambda bi, i: (0, 0)),
            pl.BlockSpec((None, s5w, s5w), lambda bi, i: (layer, 0, 0)),
            pl.BlockSpec((1, s5w), lambda bi, i: (0, 0)),
            pl.BlockSpec((None, s5w + hw, d), lambda bi, i: (layer, 0, 0)),
        ],
        out_specs=pl.BlockSpec((None, tm, d), lambda bi, i: (bi, i, 0)),
        out_shape=jax.ShapeDtypeStruct((b, t, d), F32),
        compiler_params=_params(("arbitrary", "arbitrary")),
        name="mixout",
    )(x3d, y3d, u3d, hg_v, mods, d_skip.reshape(1, s5w), w_glu, b_glu.reshape(1, s5w), w_o)


def kernel(x, c, ctx, c_ctx, w_ada, b_ada, norm_w, ffn_w_gate, ffn_w_up, ffn_w_down, w_in, w_out,
           s5_lambda_re, s5_lambda_im, s5_log_step, s5_b_re, s5_b_im, s5_c_re, s5_c_im, s5_d,
           s5_w_glu, s5_b_glu, hgrn_lower_bounds, hgrn_norm_w, final_norm_w):
    batch, seq, d = x.shape
    n_ctx = ctx.shape[1]
    depth = w_ada.shape[0]
    s5w = s5_d.shape[-1]
    hw = hgrn_lower_bounds.shape[-1]
    rows = seq // GRID_W
    assert batch < MOD_ROWS and rows % CHUNK == 0 and n_ctx % CHUNK == 0
    assert seq % S5_BLOCK == 0 and n_ctx % S5_BLOCK == 0 and (s5w // S5_GROUP) % S5_OCT == 0

    cvec = jnp.zeros((MOD_ROWS, d), F32).at[:batch].set(c.astype(F32)).at[batch].set(c_ctx.astype(F32))
    mods_all = _ada(cvec, w_ada, b_ada).reshape(depth, MOD_ROWS, N_MOD, d)

    lb_soft = jax.nn.softmax(hgrn_lower_bounds.astype(F32), axis=0)
    lb_all = jnp.cumsum(lb_soft, axis=0) - lb_soft[0]

    masks_np = _hgrn_masks()
    masks_f = jnp.asarray(masks_np, F32)
    masks_b = jnp.asarray(masks_np[:, ::-1, ::-1].copy(), F32)
    heads = hw // HEAD_DIM
    s_zero = jnp.zeros((batch, heads, HEAD_DIM, HEAD_DIM), F32)

    wg = ffn_w_gate.astype(BF16)
    wu = ffn_w_up.astype(BF16)
    wd = ffn_w_down.astype(BF16)
    uk, pb3, ca3, dec = _s5_compact(s5_lambda_re, s5_lambda_im, s5_log_step, s5_b_re, s5_b_im, s5_c_re, s5_c_im)
    kmat, we, wy = _s5_expand(uk, pb3, ca3)
    w_in_b = w_in.astype(BF16)
    w_h = w_in_b[:, :, s5w:]
    w_u = w_in_b[:, :, :s5w]
    w_o = w_out.astype(BF16)
    w_glu = s5_w_glu.astype(BF16)

    xl = x.astype(F32)
    xc = ctx.astype(F32)
    lat = dict(rows_per_mod=seq, mod_off=0)
    cx = dict(rows_per_mod=batch * n_ctx, mod_off=batch)
    for l in range(depth):
        last = l == depth - 1
        mods = mods_all[l]

        xl = _ffn(xl.reshape(batch * seq, d), mods, norm_w[l, 0], wg, wu, wd, layer=l, half=0, base=0,
                  **lat).reshape(batch, seq, d)
        xc = _ffn(xc.reshape(batch * n_ctx, d), mods, norm_w[l, 0], wg, wu, wd, layer=l, half=0, base=0,
                  **cx).reshape(batch, n_ctx, d)

        p_lat, u_l = _hproj(xl, mods, norm_w[l, 1], w_h, w_u, layer=l, hw=hw, cols=GRID_W, mod_row=None)
        p_ctx, u_c = _hproj(xc.reshape(1, batch * n_ctx, d), mods, norm_w[l, 1], w_h, w_u, layer=l, hw=hw,
                            cols=1, mod_row=batch)
        p_ctx = p_ctx.reshape(batch, n_ctx, 5 * hw)
        u_l = u_l.reshape(batch * seq, s5w)
        u_c = u_c.reshape(batch * n_ctx, s5w)

        e_l = _s5e(u_l, we, layer=l)
        e_c = _s5e(u_c, we, layer=l)
        st_c, st_l = _s5scan(e_c, e_l, dec, layer=l, nb=batch)
        y_l = _s5y(u_l, st_l, kmat, wy, layer=l).reshape(batch, seq, s5w)

        lb_f = lb_all[l, 0]
        lb_b = lb_all[l, 1]
        oc_f, sc_f = _hgrn(p_ctx, lb_f, masks_f, s_zero, direction=0, width=hw)
        ol_f, _ = _hgrn(p_lat, lb_f, masks_f, sc_f, direction=0, width=hw)
        hg_c, sc_b = _hgrn(p_ctx, lb_b, masks_b, s_zero, direction=1, width=hw,
                           o_fwd=oc_f, hnw=hgrn_norm_w[l])
        hg_l, _ = _hgrn(p_lat, lb_b, masks_b, sc_b, direction=1, width=hw,
                        o_fwd=ol_f, hnw=hgrn_norm_w[l])

        xl = _mixout(xl, y_l, u_l.reshape(batch, seq, s5w), hg_l, mods, s5_d[l], w_glu, s5_b_glu[l], w_o,
                     layer=l, mod_of_batch=True, colmajor=True)
        if not last:
            y_c = _s5y(u_c, st_c, kmat, wy, layer=l).reshape(batch, n_ctx, s5w)
            xc = _mixout(xc, y_c, u_c.reshape(batch, n_ctx, s5w), hg_c, mods, s5_d[l], w_glu, s5_b_glu[l],
                         w_o, layer=l, mod_of_batch=False, colmajor=False)

        xl = _ffn(xl.reshape(batch * seq, d), mods, norm_w[l, 2], wg, wu, wd, layer=l, half=1, base=6,
                  final_nw=final_norm_w if last else None, **lat).reshape(batch, seq, d)
        if not last:
            xc = _ffn(xc.reshape(batch * n_ctx, d), mods, norm_w[l, 2], wg, wu, wd, layer=l, half=1, base=6,
                      **cx).reshape(batch, n_ctx, d)
    return xl
```
